```python
import math
import jax
import jax.numpy as jnp
from jax import lax
import numpy as np

D_MODEL = 1024
BATCH = 8
SEQ = 8192
DEPTH = 2

DA_HEAD_DIM = 64
DA_V_DIM = 2 * DA_HEAD_DIM
DA_WIDTH = D_MODEL // 2
DA_HEADS = DA_WIDTH // DA_V_DIM
POOL_WIDTH = D_MODEL // 4
POOL_WINDOWS = (2, 4, 8, 16)
POOL_GROUPS = len(POOL_WINDOWS)
POOL_GROUP_DIM = POOL_WIDTH // POOL_GROUPS
RET_WIDTH = D_MODEL // 4
RET_HEAD_DIM = 64
RET_HEADS = RET_WIDTH // RET_HEAD_DIM
RET_CHUNK = 128
MIX_WIDTH = DA_WIDTH + POOL_WIDTH + RET_WIDTH
IN_SECTIONS = (DA_WIDTH, DA_WIDTH, DA_WIDTH, POOL_WIDTH, RET_WIDTH, RET_WIDTH, RET_WIDTH, RET_WIDTH)
IN_WIDTH = sum(IN_SECTIONS)
IN_SPLITS = tuple(int(s) for s in np.cumsum(IN_SECTIONS)[:-1])
BLOCK_Q = 128
REL_BUCKETS = 32
REL_MAX_DIST = 128
FFN_DIM = 2816
N_EXPERTS = 8
TOP_K = 2
EXPERT_DIM = 3584
MOE_BLOCK = 512
ALPHA = (2 * DEPTH) ** 0.25
BETA = (8 * DEPTH) ** -0.25
LN_EPS = 1e-5
NORM_EPS = 1e-6

kernel_name = 'hymba_style_diffattn_pool_retention_moe'


def layer_norm(x, g, b):
    xf = x.astype(jnp.float32)
    mu = jnp.mean(xf, axis=-1, keepdims=True)
    var = jnp.mean(jnp.square(xf - mu), axis=-1, keepdims=True)
    return ((xf - mu) * lax.rsqrt(var + LN_EPS) * g + b).astype(x.dtype)


def t5_bucket(dist):
    n = jnp.maximum(dist, 0)
    max_exact = REL_BUCKETS // 2
    nf = jnp.maximum(n, 1).astype(jnp.float32)
    large = max_exact + (jnp.log(nf / max_exact) / math.log(REL_MAX_DIST / max_exact)
                         * (REL_BUCKETS - max_exact)).astype(jnp.int32)
    large = jnp.minimum(large, REL_BUCKETS - 1)
    return jnp.where(n < max_exact, n, large)


def diff_attention(q, k, v, rel_bias, lam, lam_init, subln_g):
    B, S = q.shape[0], q.shape[1]
    nqb = S // BLOCK_Q
    scale = DA_HEAD_DIM ** -0.5
    kh = k.transpose(0, 2, 3, 1, 4)
    vh = v.transpose(0, 2, 1, 3)
    q_blocks = q.transpose(0, 2, 3, 1, 4).reshape(B, DA_HEADS, 2, nqb, BLOCK_Q, DA_HEAD_DIM)
    q_blocks = q_blocks.transpose(3, 0, 1, 2, 4, 5)
    table = rel_bias.astype(jnp.float32).reshape(REL_BUCKETS, DA_HEADS, 2)
    k_pos = jnp.arange(S)

    def one_block(args):
        qb, b_idx = args
        q_pos = b_idx * BLOCK_Q + jnp.arange(BLOCK_Q)
        dist = q_pos[:, None] - k_pos[None, :]
        bias = table[t5_bucket(dist)].transpose(2, 3, 0, 1)
        logits = jnp.einsum('bhmqd,bhmkd->bhmqk', qb, kh).astype(jnp.float32) * scale + bias
        logits = jnp.where(dist >= 0, logits, -1e30)
        p = jax.nn.softmax(logits, axis=-1)
        a = p[:, :, 0] - lam * p[:, :, 1]
        return jnp.einsum('bhqk,bhke->bhqe', a.astype(vh.dtype), vh)

    out = lax.map(one_block, (q_blocks, jnp.arange(nqb)))
    out = out.transpose(1, 0, 3, 2, 4).reshape(B, S, DA_HEADS, DA_V_DIM).astype(jnp.float32)
    out = out * lax.rsqrt(jnp.mean(jnp.square(out), axis=-1, keepdims=True) + NORM_EPS) * subln_g
    return (out * (1.0 - lam_init)).reshape(B, S, DA_WIDTH).astype(q.dtype)


def pool_mixer(p, pool_w, pool_scale):
    B, S, _ = p.shape
    pf = p.astype(jnp.float32).reshape(B, S, POOL_GROUPS, POOL_GROUP_DIM)
    csum = jnp.concatenate([jnp.zeros((B, 1, POOL_GROUPS, POOL_GROUP_DIM), jnp.float32),
                            jnp.cumsum(pf, axis=1)], axis=1)
    t = jnp.arange(S)
    outs = []
    for gi, w in enumerate(POOL_WINDOWS):
        lo = jnp.maximum(t + 1 - w, 0)
        cnt = jnp.minimum(t + 1, w).astype(jnp.float32)
        mean = (csum[:, 1:, gi] - csum[:, lo, gi]) / cnt[None, :, None]
        outs.append(mean - pf[:, :, gi])
    pooled = jnp.stack(outs, axis=2)
    mixed = jnp.einsum('bsgc,gcd->bsgd', pooled, pool_w.astype(jnp.float32))
    return (mixed.reshape(B, S, POOL_WIDTH) * pool_scale).astype(p.dtype)


def rotate(t, pos):
    half = t.shape[-1] // 2
    inv = 10000.0 ** (-jnp.linspace(0.0, 1.0, half, dtype=jnp.float32))
    ang = pos[:, None].astype(jnp.float32) * inv[None, :]
    cos = jnp.cos(ang)[None, :, None, :]
    sin = jnp.sin(ang)[None, :, None, :]
    t1 = t[..., :half].astype(jnp.float32)
    t2 = t[..., half:].astype(jnp.float32)
    return jnp.concatenate([t1 * cos - t2 * sin, t1 * sin + t2 * cos], axis=-1)


def retention(q, k, v, g, gn_g):
    dtype = q.dtype
    B, S, H, d = q.shape
    pos = jnp.arange(S)
    q = rotate(q, pos)
    k = rotate(k, pos) * (d ** -0.5)
    log_gamma = jnp.log(1.0 - 2.0 ** (-5.0 - jnp.arange(H, dtype=jnp.float32)))
    C = RET_CHUNK
    nc = S // C
    idx = jnp.arange(C, dtype=jnp.float32)
    rel = idx[:, None] - idx[None, :]
    intra = jnp.where(rel >= 0, jnp.exp(log_gamma[:, None, None] * jnp.maximum(rel, 0.0)), 0.0)
    q_decay = jnp.exp(log_gamma[:, None] * (idx + 1.0))[..., None]
    k_decay = jnp.exp(log_gamma[:, None] * (C - 1.0 - idx))[..., None]
    chunk_decay = jnp.exp(log_gamma * C)[:, None, None]

    def to_chunks(t):
        return t.astype(jnp.float32).reshape(B, nc, C, H, d).transpose(1, 0, 3, 2, 4)

    def step(state, inp):
        qc, kc, vc = inp
        inner = jnp.einsum('bhid,bhjd->bhij', qc, kc) * intra
        y = (jnp.einsum('bhij,bhje->bhie', inner, vc)
             + jnp.einsum('bhid,bhde->bhie', qc * q_decay, state))
        state = state * chunk_decay + jnp.einsum('bhjd,bhje->bhde', kc * k_decay, vc)
        return state, y

    state0 = jnp.zeros((B, H, d, d), jnp.float32)
    _, y = lax.scan(step, state0, (to_chunks(q), to_chunks(k), to_chunks(v)))
    y = y.transpose(1, 0, 3, 2, 4).reshape(B, S, H, d)
    mu = jnp.mean(y, axis=-1, keepdims=True)
    var = jnp.mean(jnp.square(y - mu), axis=-1, keepdims=True)
    y = ((y - mu) * lax.rsqrt(var + NORM_EPS)).reshape(B, S, H * d) * gn_g
    return (jax.nn.silu(g.reshape(B, S, H * d).astype(jnp.float32)) * y).astype(dtype)


def swiglu(h, w_gate, w_up, w_down):
    return jnp.dot(jax.nn.silu(jnp.dot(h, w_gate)) * jnp.dot(h, w_up), w_down)


def moe_swiglu(h, router_w, w_gate, w_up, w_down):
    B, S, D = h.shape
    n_tok = B * S
    n_assign = n_tok * TOP_K
    hf = h.reshape(n_tok, D)
    logits = jnp.dot(hf, router_w).astype(jnp.float32)
    top_vals, top_idx = lax.top_k(logits, TOP_K)
    gates = jax.nn.softmax(top_vals, axis=-1)
    flat_e = top_idx.reshape(-1)
    order = jnp.argsort(flat_e, stable=True)
    sorted_e = flat_e[order]
    tok_sorted = (order // TOP_K).astype(jnp.int32)
    counts = jnp.bincount(flat_e, length=N_EXPERTS)
    padded = (counts + MOE_BLOCK - 1) // MOE_BLOCK * MOE_BLOCK
    pad_end = jnp.cumsum(padded)
    pad_start = pad_end - padded
    start = jnp.cumsum(counts) - counts
    dest = pad_start[sorted_e] + jnp.arange(n_assign) - start[sorted_e]
    n_slots = (n_assign + MOE_BLOCK - 1) // MOE_BLOCK * MOE_BLOCK + N_EXPERTS * MOE_BLOCK
    n_blocks = n_slots // MOE_BLOCK
    buf_tok = jnp.full((n_slots,), n_tok, jnp.int32).at[dest].set(tok_sorted)
    block_e = jnp.minimum(jnp.searchsorted(pad_end, jnp.arange(n_blocks) * MOE_BLOCK, side='right'),
                          N_EXPERTS - 1)
    h_pad = jnp.concatenate([hf, jnp.zeros((1, D), hf.dtype)], axis=0)
    xb = h_pad[buf_tok].reshape(n_blocks, MOE_BLOCK, D)

    def expert_block(args):
        xblk, e = args
        return swiglu(xblk, w_gate[e], w_up[e], w_down[e])

    yb = lax.map(expert_block, (xb, block_e)).reshape(n_slots, D)
    contrib = yb[dest] * gates.reshape(-1)[order][:, None].astype(yb.dtype)
    out = jax.ops.segment_sum(contrib, tok_sorted, num_segments=n_tok)
    return out.reshape(B, S, D)


def setup_inputs(seed: int = 0) -> dict:
    key = jax.random.key(seed)
    ks = jax.random.split(key, 22)
    n_dense = (DEPTH + 1) // 2
    n_moe = DEPTH // 2
    nrm = jax.random.normal
    f32 = jnp.float32
    return {
        'x': nrm(ks[0], (BATCH, SEQ, D_MODEL), f32),
        'rel_bias': 0.5 * nrm(ks[1], (REL_BUCKETS, 2 * DA_HEADS), f32),
        'w_in': nrm(ks[2], (DEPTH, D_MODEL, IN_WIDTH), f32) * D_MODEL ** -0.5,
        'diff_lambda': 0.1 * nrm(ks[3], (DEPTH, 4, DA_HEAD_DIM), f32),
        'diff_subln_g': 1.0 + 0.1 * nrm(ks[4], (DEPTH, DA_V_DIM), f32),
        'pool_w': nrm(ks[5], (DEPTH, POOL_GROUPS, POOL_GROUP_DIM, POOL_GROUP_DIM), f32) * POOL_GROUP_DIM ** -0.5,
        'pool_scale': 1.0 + 0.1 * nrm(ks[6], (DEPTH, POOL_WIDTH), f32),
        'ret_gn_g': 1.0 + 0.1 * nrm(ks[7], (DEPTH, RET_WIDTH), f32),
        'w_out': nrm(ks[8], (DEPTH, MIX_WIDTH, D_MODEL), f32) * (MIX_WIDTH ** -0.5 * BETA),
        'ln1_g': 1.0 + 0.1 * nrm(ks[9], (DEPTH, D_MODEL), f32),
        'ln1_b': 0.02 * nrm(ks[10], (DEPTH, D_MODEL), f32),
        'ln2_g': 1.0 + 0.1 * nrm(ks[11], (DEPTH, D_MODEL), f32),
        'ln2_b': 0.02 * nrm(ks[12], (DEPTH, D_MODEL), f32),
        'ffn_w_gate': nrm(ks[13], (n_dense, D_MODEL, FFN_DIM), f32) * D_MODEL ** -0.5,
        'ffn_w_up': nrm(ks[14], (n_dense, D_MODEL, FFN_DIM), f32) * D_MODEL ** -0.5,
        'ffn_w_down': nrm(ks[15], (n_dense, FFN_DIM, D_MODEL), f32) * (FFN_DIM ** -0.5 * BETA),
        'router_w': nrm(ks[16], (n_moe, D_MODEL, N_EXPERTS), f32) * D_MODEL ** -0.5,
        'moe_w_gate': nrm(ks[17], (n_moe, N_EXPERTS, D_MODEL, EXPERT_DIM), f32) * D_MODEL ** -0.5,
        'moe_w_up': nrm(ks[18], (n_moe, N_EXPERTS, D_MODEL, EXPERT_DIM), f32) * D_MODEL ** -0.5,
        'moe_w_down': nrm(ks[19], (n_moe, N_EXPERTS, EXPERT_DIM, D_MODEL), f32) * (EXPERT_DIM ** -0.5 * BETA),
    }


def reference(x, rel_bias, w_in, diff_lambda, diff_subln_g, pool_w, pool_scale, ret_gn_g, w_out,
              ln1_g, ln1_b, ln2_g, ln2_b, ffn_w_gate, ffn_w_up, ffn_w_down,
              router_w, moe_w_gate, moe_w_up, moe_w_down):
    B, S, D = x.shape
    for l in range(DEPTH):
        proj = jnp.dot(x, w_in[l])
        dq, dk, dv, pp, rq, rk, rv, rg = jnp.split(proj, IN_SPLITS, axis=-1)
        lam_init = 0.8 - 0.6 * math.exp(-0.3 * l)
        lq1, lk1, lq2, lk2 = [diff_lambda[l, i].astype(jnp.float32) for i in range(4)]
        lam = jnp.exp(jnp.sum(lq1 * lk1)) - jnp.exp(jnp.sum(lq2 * lk2)) + lam_init
        y_da = diff_attention(dq.reshape(B, S, DA_HEADS, 2, DA_HEAD_DIM),
                              dk.reshape(B, S, DA_HEADS, 2, DA_HEAD_DIM),
                              dv.reshape(B, S, DA_HEADS, DA_V_DIM),
                              rel_bias, lam, lam_init, diff_subln_g[l])
        y_pool = pool_mixer(pp, pool_w[l], pool_scale[l])
        rshape = (B, S, RET_HEADS, RET_HEAD_DIM)
        y_ret = retention(rq.reshape(rshape), rk.reshape(rshape), rv.reshape(rshape),
                          rg.reshape(rshape), ret_gn_g[l])
        mix = jnp.dot(jnp.concatenate([y_da, y_pool, y_ret], axis=-1), w_out[l])
        x = layer_norm(ALPHA * x + mix, ln1_g[l], ln1_b[l])
        if l % 2 == 0:
            j = l // 2
            f = swiglu(x, ffn_w_gate[j], ffn_w_up[j], ffn_w_down[j])
        else:
            j = l // 2
            f = moe_swiglu(x, router_w[j], moe_w_gate[j], moe_w_up[j], moe_w_down[j])
        x = layer_norm(ALPHA * x + f, ln2_g[l], ln2_b[l])
    return x
```

```python
import functools
import math

import jax
import jax.numpy as jnp
from jax import lax
from jax.experimental import pallas as pl
from jax.experimental.pallas import tpu as pltpu

F32 = jnp.float32
BF16 = jnp.bfloat16

DEPTH = 2
DA_HEAD_DIM = 64
DA_V_DIM = 128
DA_HEADS = 4
DA_WIDTH = 512
POOL_WIDTH = 256
POOL_WINDOWS = (2, 4, 8, 16)
POOL_GROUP_DIM = 64
POOL_HALO = 16
RET_WIDTH = 256
RET_HEAD_DIM = 64
RET_HEADS = 4
RET_CHUNK = 128
REL_BUCKETS = 32
REL_MAX_DIST = 128
N_EXPERTS = 8
ALPHA = (2 * DEPTH) ** 0.25
LN_EPS = 1e-5
NORM_EPS = 1e-6
NEG_BIG = -1e30

POOL_COL_BLOCK = 3 * DA_WIDTH // POOL_WIDTH
RET_COL_BLOCK = POOL_COL_BLOCK + 1

VMEM_LIMIT = 56 * 1024 * 1024


def _cparams(sem, vmem=VMEM_LIMIT):
    return pltpu.CompilerParams(dimension_semantics=sem, vmem_limit_bytes=vmem)


def _nt_dot(a, b):
    return lax.dot_general(a, b, (((1,), (1,)), ((), ())), preferred_element_type=F32)


def _dot(a, b):
    return jnp.dot(a, b, preferred_element_type=F32)


def _layer_norm(z, g, b):
    mu = jnp.mean(z, axis=-1, keepdims=True)
    zc = z - mu
    var = jnp.mean(zc * zc, axis=-1, keepdims=True)
    return zc * lax.rsqrt(var + LN_EPS) * g + b


def _silu(x):
    return x / (1.0 + jnp.exp(-x))


def _inproj_kernel(x_ref, w_ref, o_ref, *, tn):
    xb = x_ref[...].astype(BF16)
    for j in range(0, w_ref.shape[1], tn):
        o_ref[:, j:j + tn] = _dot(xb, w_ref[:, j:j + tn]).astype(o_ref.dtype)


def in_proj(x2d, w, tm=512, tn=256):
    n, k = x2d.shape
    m = w.shape[1]
    tm = min(tm, n)
    return pl.pallas_call(
        functools.partial(_inproj_kernel, tn=tn),
        grid=(n // tm,),
        in_specs=[pl.BlockSpec((tm, k), lambda i: (i, 0)),
                  pl.BlockSpec((k, m), lambda i: (0, 0))],
        out_specs=pl.BlockSpec((tm, m), lambda i: (i, 0)),
        out_shape=jax.ShapeDtypeStruct((n, m), BF16),
        compiler_params=_cparams(("parallel",)),
        name="in_proj",
    )(x2d, w)


def _t5_bucket(dist):
    n = jnp.maximum(dist, 0)
    max_exact = REL_BUCKETS // 2
    nf = jnp.maximum(n, 1).astype(F32)
    large = max_exact + (jnp.log(nf / max_exact) / math.log(REL_MAX_DIST / max_exact)
                         * (REL_BUCKETS - max_exact)).astype(jnp.int32)
    large = jnp.minimum(large, REL_BUCKETS - 1)
    return jnp.where(n < max_exact, n, large)


def _attn_bias_tiles(rel_bias, t):
    table = rel_bias.astype(F32).reshape(REL_BUCKETS, DA_HEADS, 2)
    i = jnp.arange(t)
    d_diag = i[:, None] - i[None, :]
    diag = jnp.where((d_diag >= 0)[..., None, None], table[_t5_bucket(d_diag)], NEG_BIG)
    prev = table[_t5_bucket(d_diag + t)]
    tiles = jnp.concatenate([diag.transpose(2, 3, 0, 1), prev.transpose(2, 3, 0, 1)], axis=1)
    far = table[_t5_bucket(jnp.array(t + 1))]
    return tiles, far.reshape(-1)


def _attn_kernel(far_ref, q_ref, k_ref, v_ref, bias_ref, lam_ref, g_ref, o_ref,
                 m1, l1, a1, m2, l2, a2, *, t, lam_init):
    h = pl.program_id(1)
    qi = pl.program_id(2)
    q = q_ref[0] * jnp.asarray(DA_HEAD_DIM ** -0.5, BF16)
    lane = lax.broadcasted_iota(jnp.int32, q.shape, 1)
    zero = jnp.zeros_like(q)
    qa = jnp.where(lane < DA_HEAD_DIM, q, zero)
    qb = jnp.where(lane >= DA_HEAD_DIM, q, zero)

    for m_ref, l_ref, a_ref in ((m1, l1, a1), (m2, l2, a2)):
        m_ref[...] = jnp.full(m_ref.shape, NEG_BIG, F32)
        l_ref[...] = jnp.zeros(l_ref.shape, F32)
        a_ref[...] = jnp.zeros(a_ref.shape, F32)

    def update(s, v, m_ref, l_ref, a_ref):
        m_prev = m_ref[...]
        m_new = jnp.maximum(m_prev, jnp.max(s, axis=-1, keepdims=True))
        alpha = jnp.exp(m_prev - m_new)
        p = jnp.exp(s - m_new)
        l_ref[...] = alpha * l_ref[...] + jnp.sum(p, axis=-1, keepdims=True)
        a_ref[...] = alpha * a_ref[...] + _dot(p.astype(BF16), v)
        m_ref[...] = m_new

    def block(kstart, bias_a, bias_b):
        k = k_ref[0, pl.ds(kstart, t), :]
        v = v_ref[0, pl.ds(kstart, t), :]
        update(_nt_dot(qa, k) + bias_a, v, m1, l1, a1)
        update(_nt_dot(qb, k) + bias_b, v, m2, l2, a2)

    far_a = far_ref[2 * h]
    far_b = far_ref[2 * h + 1]

    def far_body(ki, carry):
        block(pl.multiple_of(ki * t, t), far_a, far_b)
        return carry

    lax.fori_loop(0, qi - 1, far_body, 0)

    @pl.when(qi >= 1)
    def _():
        block(pl.multiple_of((qi - 1) * t, t), bias_ref[0, 2], bias_ref[0, 3])

    block(pl.multiple_of(qi * t, t), bias_ref[0, 0], bias_ref[0, 1])

    lm = lam_ref[...]
    lam = (jnp.exp(jnp.sum(lm[0:1] * lm[1:2], keepdims=True))
           - jnp.exp(jnp.sum(lm[2:3] * lm[3:4], keepdims=True)) + lam_init)
    out = a1[...] / l1[...] - lam * (a2[...] / l2[...])
    out = out * lax.rsqrt(jnp.mean(out * out, axis=-1, keepdims=True) + NORM_EPS) * g_ref[...]
    o_ref[0] = (out * (1.0 - lam_init)).astype(o_ref.dtype)


def diff_attention(proj3, rel_bias, lam_params, subln_g, lam_init, t=512):
    b, s, _ = proj3.shape
    t = min(t, s)
    tiles, far = _attn_bias_tiles(rel_bias, t)
    nq = s // t
    grid_spec = pltpu.PrefetchScalarGridSpec(
        num_scalar_prefetch=1,
        grid=(b, DA_HEADS, nq),
        in_specs=[
            pl.BlockSpec((1, t, DA_V_DIM), lambda bi, h, qi, far: (bi, qi, h)),
            pl.BlockSpec((1, s, DA_V_DIM), lambda bi, h, qi, far: (bi, 0, DA_HEADS + h)),
            pl.BlockSpec((1, s, DA_V_DIM), lambda bi, h, qi, far: (bi, 0, 2 * DA_HEADS + h)),
            pl.BlockSpec((1, 4, t, t), lambda bi, h, qi, far: (h, 0, 0, 0)),
            pl.BlockSpec((4, DA_HEAD_DIM), lambda bi, h, qi, far: (0, 0)),
            pl.BlockSpec((1, DA_V_DIM), lambda bi, h, qi, far: (0, 0)),
        ],
        out_specs=pl.BlockSpec((1, t, DA_V_DIM), lambda bi, h, qi, far: (bi, qi, h)),
        scratch_shapes=[pltpu.VMEM((t, 1), F32), pltpu.VMEM((t, 1), F32), pltpu.VMEM((t, DA_V_DIM), F32),
                        pltpu.VMEM((t, 1), F32), pltpu.VMEM((t, 1), F32), pltpu.VMEM((t, DA_V_DIM), F32)],
    )
    return pl.pallas_call(
        functools.partial(_attn_kernel, t=t, lam_init=lam_init),
        grid_spec=grid_spec,
        out_shape=jax.ShapeDtypeStruct((b, s, DA_WIDTH), BF16),
        compiler_params=_cparams(("parallel", "parallel", "parallel")),
        name="diff_attention",
    )(far, proj3, proj3, proj3, tiles, lam_params, subln_g.reshape(1, DA_V_DIM))


def _pool_kernel(p_ref, w_ref, scale_ref, o_ref, halo, *, t):
    si = pl.program_id(1)

    @pl.when(si == 0)
    def _():
        halo[...] = jnp.zeros(halo.shape, F32)

    p = p_ref[0].astype(F32)
    ext = jnp.concatenate([halo[...], p], axis=0)
    halo[...] = p[t - POOL_HALO:, :]
    sums = {1: ext}
    w = 1
    while w < POOL_WINDOWS[-1]:
        sums[2 * w] = sums[w] + pltpu.roll(sums[w], w, 0)
        w *= 2
    lane = lax.broadcasted_iota(jnp.int32, (t, POOL_WIDTH), 1)
    pos = (si * t + lax.broadcasted_iota(jnp.int32, (t, POOL_WIDTH), 0) + 1).astype(F32)
    wsum = sums[POOL_WINDOWS[-1]][POOL_HALO:, :]
    cnt = jnp.minimum(pos, float(POOL_WINDOWS[-1]))
    for gi in range(len(POOL_WINDOWS) - 2, -1, -1):
        in_group = lane < (gi + 1) * POOL_GROUP_DIM
        wsum = jnp.where(in_group, sums[POOL_WINDOWS[gi]][POOL_HALO:, :], wsum)
        cnt = jnp.where(in_group, jnp.minimum(pos, float(POOL_WINDOWS[gi])), cnt)
    pooled = wsum / cnt - p
    mixed = _dot(pooled.astype(BF16), w_ref[...])
    o_ref[0] = (mixed * scale_ref[...]).astype(o_ref.dtype)


def pool_mixer(proj3, pool_w, pool_scale, t=512):
    b, s, _ = proj3.shape
    t = min(t, s)
    g = len(POOL_WINDOWS)
    wbd = (jnp.eye(g, dtype=F32)[:, None, :, None] * pool_w.astype(F32)[:, :, None, :]).reshape(
        POOL_WIDTH, POOL_WIDTH).astype(BF16)
    return pl.pallas_call(
        functools.partial(_pool_kernel, t=t),
        grid=(b, s // t),
        in_specs=[pl.BlockSpec((1, t, POOL_WIDTH), lambda bi, si: (bi, si, POOL_COL_BLOCK)),
                  pl.BlockSpec((POOL_WIDTH, POOL_WIDTH), lambda bi, si: (0, 0)),
                  pl.BlockSpec((1, POOL_WIDTH), lambda bi, si: (0, 0))],
        out_specs=pl.BlockSpec((1, t, POOL_WIDTH), lambda bi, si: (bi, si, 0)),
        out_shape=jax.ShapeDtypeStruct((b, s, POOL_WIDTH), BF16),
        scratch_shapes=[pltpu.VMEM((POOL_HALO, POOL_WIDTH), F32)],
        compiler_params=_cparams(("parallel", "arbitrary")),
        name="pool_mixer",
    )(proj3, wbd, pool_scale.reshape(1, POOL_WIDTH))


def _retention_tables(s, t):
    d, hn, c = RET_HEAD_DIM, RET_HEADS, RET_CHUNK
    half = d // 2
    inv = 10000.0 ** (-jnp.linspace(0.0, 1.0, half, dtype=F32))
    ang = jnp.arange(s)[:, None].astype(F32) * inv[None, :]
    cos, sin = jnp.cos(ang), jnp.sin(ang)
    cos_t = jnp.tile(jnp.concatenate([cos, cos], axis=-1), (1, hn))
    sin_t = jnp.tile(jnp.concatenate([-sin, sin], axis=-1), (1, hn))
    log_gamma = jnp.log(1.0 - 2.0 ** (-5.0 - jnp.arange(hn, dtype=F32)))
    idx = jnp.arange(c, dtype=F32)
    rel = idx[:, None] - idx[None, :]
    intra = jnp.where(rel >= 0, jnp.exp(log_gamma[:, None, None] * jnp.maximum(rel, 0.0)), 0.0)
    q_decay = jnp.exp(log_gamma[:, None] * (idx + 1.0))
    k_decay = jnp.exp(log_gamma[:, None] * (c - 1.0 - idx))
    chunk_decay = jnp.exp(log_gamma * c)
    lanes = lambda a: jnp.tile(jnp.repeat(a.T, d, axis=1), (t // c, 1))
    return cos_t, sin_t, intra, lanes(q_decay), lanes(k_decay), chunk_decay


def _ret_kernel(cd_ref, q_ref, k_ref, v_ref, g_ref, cos_ref, sin_ref, intra_ref, qd_ref, kd_ref, gn_ref,
                o_ref, state, *, t):
    si = pl.program_id(1)
    d, c = RET_HEAD_DIM, RET_CHUNK

    @pl.when(si == 0)
    def _():
        state[...] = jnp.zeros(state.shape, F32)

    lane = lax.broadcasted_iota(jnp.int32, (t, RET_WIDTH), 1)
    first_half = (lane % d) < (d // 2)

    def rotate(x):
        swapped = jnp.where(first_half, pltpu.roll(x, RET_WIDTH - d // 2, 1), pltpu.roll(x, d // 2, 1))
        return x * cos_ref[...] + swapped * sin_ref[...]

    q = rotate(q_ref[0].astype(F32))
    k = rotate(k_ref[0].astype(F32)) * (d ** -0.5)
    qs = (q * qd_ref[...]).astype(BF16)
    ks = (k * kd_ref[...]).astype(BF16)
    qb = q.astype(BF16)
    kb = k.astype(BF16)
    v = v_ref[0]

    for ci in range(t // c):
        rows = slice(ci * c, (ci + 1) * c)
        heads = []
        for h in range(RET_HEADS):
            cols = slice(h * d, (h + 1) * d)
            st = state[h]
            inner = _nt_dot(qb[rows, cols], kb[rows, cols]) * intra_ref[h]
            y = _dot(inner.astype(BF16), v[rows, cols]) + _dot(qs[rows, cols], st.astype(BF16))
            state[h] = st * cd_ref[h] + _dot(ks[rows, cols].T, v[rows, cols])
            mu = jnp.mean(y, axis=-1, keepdims=True)
            yc = y - mu
            var = jnp.mean(yc * yc, axis=-1, keepdims=True)
            heads.append(yc * lax.rsqrt(var + NORM_EPS))
        yn = jnp.concatenate(heads, axis=-1)
        gate = _silu(g_ref[0, rows, :].astype(F32))
        o_ref[0, rows, :] = (gate * (yn * gn_ref[...])).astype(o_ref.dtype)


def retention(proj3, gn_g, t=512):
    b, s, _ = proj3.shape
    t = min(t, s)
    cos_t, sin_t, intra, qd, kd, chunk_decay = _retention_tables(s, t)
    col = lambda j: pl.BlockSpec((1, t, RET_WIDTH), lambda bi, si, cd: (bi, si, RET_COL_BLOCK + j))
    const2 = lambda shape: pl.BlockSpec(shape, lambda bi, si, cd: (0, 0))
    grid_spec = pltpu.PrefetchScalarGridSpec(
        num_scalar_prefetch=1,
        grid=(b, s // t),
        in_specs=[col(0), col(1), col(2), col(3),
                  pl.BlockSpec((t, RET_WIDTH), lambda bi, si, cd: (si, 0)),
                  pl.BlockSpec((t, RET_WIDTH), lambda bi, si, cd: (si, 0)),
                  pl.BlockSpec((RET_HEADS, RET_CHUNK, RET_CHUNK), lambda bi, si, cd: (0, 0, 0)),
                  const2((t, RET_WIDTH)), const2((t, RET_WIDTH)), const2((1, RET_WIDTH))],
        out_specs=pl.BlockSpec((1, t, RET_WIDTH), lambda bi, si, cd: (bi, si, 0)),
        scratch_shapes=[pltpu.VMEM((RET_HEADS, RET_HEAD_DIM, RET_HEAD_DIM), F32)],
    )
    return pl.pallas_call(
        functools.partial(_ret_kernel, t=t),
        grid_spec=grid_spec,
        out_shape=jax.ShapeDtypeStruct((b, s, RET_WIDTH), BF16),
        compiler_params=_cparams(("parallel", "arbitrary")),
        name="retention",
    )(chunk_decay, proj3, proj3, proj3, proj3, cos_t, sin_t, intra, qd, kd, gn_g.reshape(1, RET_WIDTH))


def _outproj_kernel(da_ref, pool_ref, ret_ref, x_ref, w_ref, g_ref, b_ref, o_ref):
    e0, e1 = DA_WIDTH, DA_WIDTH + POOL_WIDTH
    mix = (_dot(da_ref[...], w_ref[0:e0, :]) + _dot(pool_ref[...], w_ref[e0:e1, :])
           + _dot(ret_ref[...], w_ref[e1:, :]))
    o_ref[...] = _layer_norm(ALPHA * x_ref[...] + mix, g_ref[...], b_ref[...])


def out_proj_ln(y_da, y_pool, y_ret, x2d, w, g, b, tm=512):
    n, dm = x2d.shape
    tm = min(tm, n)
    row = lambda width: pl.BlockSpec((tm, width), lambda i: (i, 0))
    const = lambda shape: pl.BlockSpec(shape, lambda i: (0, 0))
    return pl.pallas_call(
        _outproj_kernel,
        grid=(n // tm,),
        in_specs=[row(DA_WIDTH), row(POOL_WIDTH), row(RET_WIDTH), row(dm),
                  const(w.shape), const((1, dm)), const((1, dm))],
        out_specs=row(dm),
        out_shape=jax.ShapeDtypeStruct((n, dm), F32),
        compiler_params=_cparams(("parallel",)),
        name="out_proj_ln",
    )(y_da, y_pool, y_ret, x2d, w, g.reshape(1, dm), b.reshape(1, dm))


def _ffn_kernel(x_ref, wg_ref, wu_ref, wd_ref, g_ref, b_ref, o_ref, acc, xb):
    f = pl.program_id(1)

    @pl.when(f == 0)
    def _():
        xb[...] = x_ref[...].astype(BF16)
        acc[...] = jnp.zeros(acc.shape, F32)

    hidden = _silu(_dot(xb[...], wg_ref[...])) * _dot(xb[...], wu_ref[...])
    acc[...] += _dot(hidden.astype(BF16), wd_ref[...])

    @pl.when(f == pl.num_programs(1) - 1)
    def _():
        o_ref[...] = _layer_norm(ALPHA * x_ref[...] + acc[...], g_ref[...], b_ref[...])


def ffn_ln(x2d, wg, wu, wd, g, b, tm=1024, tf=256):
    n, dm = x2d.shape
    fdim = wg.shape[1]
    tm = min(tm, n)
    return pl.pallas_call(
        _ffn_kernel,
        grid=(n // tm, fdim // tf),
        in_specs=[pl.BlockSpec((tm, dm), lambda i, f: (i, 0)),
                  pl.BlockSpec((dm, tf), lambda i, f: (0, f)),
                  pl.BlockSpec((dm, tf), lambda i, f: (0, f)),
                  pl.BlockSpec((tf, dm), lambda i, f: (f, 0)),
                  pl.BlockSpec((1, dm), lambda i, f: (0, 0)),
                  pl.BlockSpec((1, dm), lambda i, f: (0, 0))],
        out_specs=pl.BlockSpec((tm, dm), lambda i, f: (i, 0)),
        out_shape=jax.ShapeDtypeStruct((n, dm), F32),
        scratch_shapes=[pltpu.VMEM((tm, dm), F32), pltpu.VMEM((tm, dm), BF16)],
        compiler_params=_cparams(("parallel", "arbitrary")),
        name="ffn_ln",
    )(x2d, wg, wu, wd, g.reshape(1, dm), b.reshape(1, dm))


ROUTE_ROWS = 8
LANES = 128


def _router_kernel(x_ref, w_ref, tri_ref, route_ref, route_t_ref, cnt_ref, carry, *, t):
    i = pl.program_id(0)

    @pl.when(i == 0)
    def _():
        carry[...] = jnp.zeros(carry.shape, F32)

    x = x_ref[...]
    xh = x.astype(BF16)
    xl = (x - xh.astype(F32)).astype(BF16)
    w = w_ref[...]
    wh = w.astype(BF16)
    wl = (w - wh.astype(F32)).astype(BF16)
    logits = _nt_dot(wh, xh) + _nt_dot(wl, xh) + _nt_dot(wh, xl)

    row = lax.broadcasted_iota(jnp.int32, logits.shape, 0)
    v0 = jnp.max(logits, axis=0, keepdims=True)
    i0 = jnp.min(jnp.where(logits == v0, row, N_EXPERTS), axis=0, keepdims=True)
    rest = jnp.where(row == i0, -jnp.inf, logits)
    v1 = jnp.max(rest, axis=0, keepdims=True)
    i1 = jnp.min(jnp.where(rest == v1, row, N_EXPERTS), axis=0, keepdims=True)
    ex = jnp.exp(v1 - v0)
    gate0 = 1.0 / (1.0 + ex)
    gate1 = ex / (1.0 + ex)

    oh0 = row == i0
    oh1 = row == i1
    member = jnp.where(oh0 | oh1, 1.0, 0.0)
    before = _dot(member.astype(BF16), tri_ref[...]) + carry[:, 0:1]
    rank0 = jnp.sum(jnp.where(oh0, before, 0.0), axis=0, keepdims=True)
    rank1 = jnp.sum(jnp.where(oh1, before, 0.0), axis=0, keepdims=True)
    carry[...] = carry[...] + jnp.sum(member, axis=1, keepdims=True)
    cnt_ref[...] = carry[...]

    route = jnp.concatenate([i0.astype(F32), i1.astype(F32), rank0, rank1, gate0, gate1,
                             jnp.zeros((2, t), F32)], axis=0)
    route_ref[...] = route
    padded = jnp.concatenate([route, jnp.zeros((LANES - ROUTE_ROWS, t), F32)], axis=0)
    route_t_ref[...] = padded.T


def route_tokens(x2d, router_w, t=1024):
    n, dm = x2d.shape
    t = min(t, n)
    tri = (jnp.arange(t)[:, None] < jnp.arange(t)[None, :]).astype(BF16)
    return pl.pallas_call(
        functools.partial(_router_kernel, t=t),
        grid=(n // t,),
        in_specs=[pl.BlockSpec((t, dm), lambda i: (i, 0)),
                  pl.BlockSpec((N_EXPERTS, dm), lambda i: (0, 0)),
                  pl.BlockSpec((t, t), lambda i: (0, 0))],
        out_specs=[pl.BlockSpec((ROUTE_ROWS, t), lambda i: (0, i)),
                   pl.BlockSpec((t, LANES), lambda i: (i, 0)),
                   pl.BlockSpec((N_EXPERTS, LANES), lambda i: (0, 0))],
        out_shape=[jax.ShapeDtypeStruct((ROUTE_ROWS, n), F32),
                   jax.ShapeDtypeStruct((n, LANES), F32),
                   jax.ShapeDtypeStruct((N_EXPERTS, LANES), F32)],
        scratch_shapes=[pltpu.VMEM((N_EXPERTS, LANES), F32)],
        compiler_params=_cparams(("arbitrary",)),
        name="route_tokens",
    )(x2d, router_w.T, tri)


def _dest_kernel(start_ref, route_ref, dest_ref):
    r = route_ref[...]
    for k in range(2):
        e = r[k:k + 1, :].astype(jnp.int32)
        base = jnp.zeros(e.shape, jnp.int32)
        for ei in range(N_EXPERTS):
            base = jnp.where(e == ei, start_ref[ei], base)
        dest_ref[k:k + 1, :] = base + r[2 + k:3 + k, :].astype(jnp.int32)


def slot_of_assignment(pad_start, route, t=1024):
    n = route.shape[1]
    t = min(t, n)
    grid_spec = pltpu.PrefetchScalarGridSpec(
        num_scalar_prefetch=1,
        grid=(n // t,),
        in_specs=[pl.BlockSpec((ROUTE_ROWS, t), lambda i, ps: (0, i))],
        out_specs=pl.BlockSpec((2, t), lambda i, ps: (0, i)),
    )
    return pl.pallas_call(
        _dest_kernel,
        grid_spec=grid_spec,
        out_shape=jax.ShapeDtypeStruct((2, n), jnp.int32),
        compiler_params=_cparams(("parallel",)),
        name="slot_of_assignment",
    )(pad_start, route)


def _row_copy(src, src_row, dst, dst_row, sem):
    return pltpu.make_async_copy(src.at[pl.ds(src_row, 1)], dst.at[pl.ds(dst_row, 1)], sem)


def _dispatch_kernel(dest_ref, x_ref, slots_in, slots_out, sem, *, t):
    del slots_in

    def issue(i, carry):
        for k in range(2):
            _row_copy(x_ref, i, slots_out, dest_ref[k, i], sem).start()
        return carry

    lax.fori_loop(0, t, issue, 0)

    def drain(i, carry):
        for k in range(2):
            _row_copy(x_ref, i, slots_out, dest_ref[k, i], sem).wait()
        return carry

    lax.fori_loop(0, t, drain, 0)


def dispatch_rows(dest, x2d, n_slots, t=512):
    n, dm = x2d.shape
    t = min(t, n)
    zeros = jnp.zeros((n_slots, dm), x2d.dtype)
    return pl.pallas_call(
        functools.partial(_dispatch_kernel, t=t),
        grid=(n // t,),
        in_specs=[pl.BlockSpec((2, t), lambda i: (0, i), memory_space=pltpu.SMEM),
                  pl.BlockSpec((t, dm), lambda i: (i, 0)),
                  pl.BlockSpec(memory_space=pl.ANY)],
        out_specs=pl.BlockSpec(memory_space=pl.ANY),
        out_shape=jax.ShapeDtypeStruct((n_slots, dm), x2d.dtype),
        scratch_shapes=[pltpu.SemaphoreType.DMA],
        input_output_aliases={2: 0},
        compiler_params=_cparams(("arbitrary",)),
        name="dispatch_rows",
    )(dest, x2d, zeros)


def _expert_kernel(be_ref, nv_ref, x_ref, wg_ref, wu_ref, wd_ref, o_ref, acc, xb):
    blk = pl.program_id(0)
    f = pl.program_id(1)
    valid = blk < nv_ref[0]

    @pl.when(valid & (f == 0))
    def _():
        xb[...] = x_ref[...].astype(BF16)
        acc[...] = jnp.zeros(acc.shape, F32)

    @pl.when(valid)
    def _():
        hidden = _silu(_dot(xb[...], wg_ref[0])) * _dot(xb[...], wu_ref[0])
        acc[...] += _dot(hidden.astype(BF16), wd_ref[0])

    last = f == pl.num_programs(1) - 1

    @pl.when(valid & last)
    def _():
        o_ref[...] = acc[...]

    @pl.when(jnp.logical_not(valid) & last)
    def _():
        o_ref[...] = jnp.zeros(o_ref.shape, o_ref.dtype)


def expert_swiglu(block_e, n_valid, slots, wg, wu, wd, bm, tf=512):
    n_slots, dm = slots.shape
    edim = wg.shape[2]
    grid_spec = pltpu.PrefetchScalarGridSpec(
        num_scalar_prefetch=2,
        grid=(n_slots // bm, edim // tf),
        in_specs=[pl.BlockSpec((bm, dm), lambda b, f, be, nv: (b, 0)),
                  pl.BlockSpec((1, dm, tf), lambda b, f, be, nv: (be[b], 0, f)),
                  pl.BlockSpec((1, dm, tf), lambda b, f, be, nv: (be[b], 0, f)),
                  pl.BlockSpec((1, tf, dm), lambda b, f, be, nv: (be[b], f, 0))],
        out_specs=pl.BlockSpec((bm, dm), lambda b, f, be, nv: (b, 0)),
        scratch_shapes=[pltpu.VMEM((bm, dm), F32), pltpu.VMEM((bm, dm), BF16)],
    )
    return pl.pallas_call(
        _expert_kernel,
        grid_spec=grid_spec,
        out_shape=jax.ShapeDtypeStruct((n_slots, dm), F32),
        compiler_params=_cparams(("parallel", "arbitrary")),
        name="expert_swiglu",
    )(block_e, n_valid, slots, wg, wu, wd)


def _combine_kernel(dest_ref, rt_ref, x_ref, y_hbm, g_ref, b_ref, o_ref, rows, sem, *, t):
    def issue(i, carry):
        for k in range(2):
            _row_copy(y_hbm, dest_ref[k, i], rows.at[k], i, sem).start()
        return carry

    lax.fori_loop(0, t, issue, 0)

    def drain(i, carry):
        for k in range(2):
            _row_copy(y_hbm, dest_ref[k, i], rows.at[k], i, sem).wait()
        return carry

    lax.fori_loop(0, t, drain, 0)

    rt = rt_ref[...]
    mixed = rt[:, 4:5] * rows[0] + rt[:, 5:6] * rows[1]
    o_ref[...] = _layer_norm(ALPHA * x_ref[...] + mixed, g_ref[...], b_ref[...])


def combine_ln(dest, route_t, x2d, y, g, b, t=512):
    n, dm = x2d.shape
    t = min(t, n)
    return pl.pallas_call(
        functools.partial(_combine_kernel, t=t),
        grid=(n // t,),
        in_specs=[pl.BlockSpec((2, t), lambda i: (0, i), memory_space=pltpu.SMEM),
                  pl.BlockSpec((t, LANES), lambda i: (i, 0)),
                  pl.BlockSpec((t, dm), lambda i: (i, 0)),
                  pl.BlockSpec(memory_space=pl.ANY),
                  pl.BlockSpec((1, dm), lambda i: (0, 0)),
                  pl.BlockSpec((1, dm), lambda i: (0, 0))],
        out_specs=pl.BlockSpec((t, dm), lambda i: (i, 0)),
        out_shape=jax.ShapeDtypeStruct((n, dm), F32),
        scratch_shapes=[pltpu.VMEM((2, t, dm), F32), pltpu.SemaphoreType.DMA],
        compiler_params=_cparams(("arbitrary",)),
        name="combine_ln",
    )(dest, route_t, x2d, y, g.reshape(1, dm), b.reshape(1, dm))


def moe_ln(x2d, router_w, wg, wu, wd, g, b, bm=1024):
    n, _ = x2d.shape
    bm = min(bm, n)
    route, route_t, cnt = route_tokens(x2d, router_w)
    counts = cnt[:, 0].astype(jnp.int32)
    padded = (counts + bm - 1) // bm * bm
    pad_end = jnp.cumsum(padded)
    pad_start = (pad_end - padded).astype(jnp.int32)
    n_blocks = 2 * n // bm + N_EXPERTS
    n_slots = n_blocks * bm
    block_e = jnp.minimum(jnp.searchsorted(pad_end, jnp.arange(n_blocks) * bm, side='right'),
                          N_EXPERTS - 1).astype(jnp.int32)
    n_valid = (pad_end[-1:] // bm).astype(jnp.int32)
    dest = slot_of_assignment(pad_start, route)
    slots = dispatch_rows(dest, x2d, n_slots)
    y = expert_swiglu(block_e, n_valid, slots, wg, wu, wd, bm)
    return combine_ln(dest, route_t, x2d, y, g, b)


def kernel(x, rel_bias, w_in, diff_lambda, diff_subln_g, pool_w, pool_scale, ret_gn_g, w_out,
           ln1_g, ln1_b, ln2_g, ln2_b, ffn_w_gate, ffn_w_up, ffn_w_down,
           router_w, moe_w_gate, moe_w_up, moe_w_down):
    bsz, seq, dm = x.shape
    x2d = x.reshape(bsz * seq, dm)
    for l in range(DEPTH):
        lam_init = 0.8 - 0.6 * math.exp(-0.3 * l)
        proj = in_proj(x2d, w_in[l].astype(BF16)).reshape(bsz, seq, -1)
        y_da = diff_attention(proj, rel_bias, diff_lambda[l], diff_subln_g[l], lam_init)
        y_pool = pool_mixer(proj, pool_w[l], pool_scale[l])
        y_ret = retention(proj, ret_gn_g[l])
        flat = lambda a: a.reshape(bsz * seq, -1)
        x2d = out_proj_ln(flat(y_da), flat(y_pool), flat(y_ret), x2d, w_out[l].astype(BF16),
                          ln1_g[l], ln1_b[l])
        j = l // 2
        if l % 2 == 0:
            x2d = ffn_ln(x2d, ffn_w_gate[j].astype(BF16), ffn_w_up[j].astype(BF16),
                         ffn_w_down[j].astype(BF16), ln2_g[l], ln2_b[l])
        else:
            x2d = moe_ln(x2d, router_w[j], moe_w_gate[j].astype(BF16), moe_w_up[j].astype(BF16),
                         moe_w_down[j].astype(BF16), ln2_g[l], ln2_b[l])
    return x2d.reshape(bsz, seq, dm)
```

```python
import functools
import math

import jax
import jax.numpy as jnp
from jax import lax
from jax.experimental import pallas as pl
from jax.experimental.pallas import tpu as pltpu

F32 = jnp.float32
BF16 = jnp.bfloat16

DEPTH = 2
DA_HEAD_DIM = 64
DA_V_DIM = 128
DA_HEADS = 4
DA_WIDTH = 512
POOL_WIDTH = 256
POOL_WINDOWS = (2, 4, 8, 16)
POOL_GROUP_DIM = 64
POOL_HALO = 16
RET_WIDTH = 256
RET_HEAD_DIM = 64
RET_HEADS = 4
RET_CHUNK = 128
REL_BUCKETS = 32
REL_MAX_DIST = 128
N_EXPERTS = 8
ALPHA = (2 * DEPTH) ** 0.25
LN_EPS = 1e-5
NORM_EPS = 1e-6
NEG_BIG = -1e30

POOL_COL_BLOCK = 3 * DA_WIDTH // POOL_WIDTH
RET_COL_BLOCK = POOL_COL_BLOCK + 1

VMEM_LIMIT = 56 * 1024 * 1024


def _cparams(sem, vmem=VMEM_LIMIT):
    return pltpu.CompilerParams(dimension_semantics=sem, vmem_limit_bytes=vmem)


def _nt_dot(a, b):
    return lax.dot_general(a, b, (((1,), (1,)), ((), ())), preferred_element_type=F32)


def _dot(a, b):
    return jnp.dot(a, b, preferred_element_type=F32)


def _layer_norm(z, g, b):
    mu = jnp.mean(z, axis=-1, keepdims=True)
    zc = z - mu
    var = jnp.mean(zc * zc, axis=-1, keepdims=True)
    return zc * lax.rsqrt(var + LN_EPS) * g + b


def _silu(x):
    return x / (1.0 + jnp.exp(-x))


def _inproj_kernel(x_ref, w_ref, o_ref, *, tn):
    xb = x_ref[...].astype(BF16)
    for j in range(0, w_ref.shape[1], tn):
        o_ref[:, j:j + tn] = _dot(xb, w_ref[:, j:j + tn]).astype(o_ref.dtype)


def in_proj(x2d, w, tm=512, tn=256):
    n, k = x2d.shape
    m = w.shape[1]
    tm = min(tm, n)
    return pl.pallas_call(
        functools.partial(_inproj_kernel, tn=tn),
        grid=(n // tm,),
        in_specs=[pl.BlockSpec((tm, k), lambda i: (i, 0)),
                  pl.BlockSpec((k, m), lambda i: (0, 0))],
        out_specs=pl.BlockSpec((tm, m), lambda i: (i, 0)),
        out_shape=jax.ShapeDtypeStruct((n, m), BF16),
        compiler_params=_cparams(("parallel",)),
        name="in_proj",
    )(x2d, w)


def _t5_bucket(dist):
    n = jnp.maximum(dist, 0)
    max_exact = REL_BUCKETS // 2
    nf = jnp.maximum(n, 1).astype(F32)
    large = max_exact + (jnp.log(nf / max_exact) / math.log(REL_MAX_DIST / max_exact)
                         * (REL_BUCKETS - max_exact)).astype(jnp.int32)
    large = jnp.minimum(large, REL_BUCKETS - 1)
    return jnp.where(n < max_exact, n, large)


LOG2E = math.log2(math.e)
Q_SCALE = DA_HEAD_DIM ** -0.5 * LOG2E
ATTN_BLOCK = 512
ONES_ROWS = 16


def attn_bias_tiles(rel_bias, t):
    table = rel_bias.astype(F32).reshape(REL_BUCKETS, DA_HEADS * 2)
    vec = table[_t5_bucket(jnp.arange(2 * t))]
    far = table[_t5_bucket(jnp.array(2 * t))]
    vec = ((vec - far[None, :]) * LOG2E).T
    masked = jnp.full((DA_HEADS * 2, t), NEG_BIG, F32)
    u_diag = jnp.concatenate([vec[:, :t], masked], axis=1)
    u_prev = jnp.concatenate([vec[:, t:], vec[:, :t]], axis=1)

    def toeplitz(u):
        skew = jnp.tile(u, (1, t))[:, :t * (2 * t - 1)].reshape(-1, t, 2 * t - 1)
        return skew[:, :, :t]

    def per_head(a):
        return a.reshape(DA_HEADS, 2, t, t).transpose(0, 2, 1, 3).reshape(DA_HEADS, t, 2 * t)

    return jnp.stack([per_head(toeplitz(u_diag)), per_head(toeplitz(u_prev))], axis=1)


def _attn_kernel(q_ref, k_ref, v_ref, bias_ref, lam_ref, g_ref, o_ref, vt, m_s, acc, s_a, s_b, *, t, lam_init):
    qi = pl.program_id(2)
    nk = vt.shape[0]

    @pl.when(qi == 0)
    def _():
        ones = jnp.ones((ONES_ROWS, t), BF16)
        for ki in range(nk):
            v_t = v_ref[0, ki * t:(ki + 1) * t, :].astype(F32).T.astype(BF16)
            vt[ki] = jnp.concatenate([v_t, ones], axis=0)

    q = q_ref[0]
    lane = lax.broadcasted_iota(jnp.int32, q.shape, 1)
    zero = jnp.zeros_like(q)
    qcat = jnp.concatenate([jnp.where(lane < DA_HEAD_DIM, q, zero),
                            jnp.where(lane >= DA_HEAD_DIM, q, zero)], axis=0)
    m_s[...] = jnp.full(m_s.shape, NEG_BIG, F32)
    acc[...] = jnp.zeros(acc.shape, F32)

    def scores(ki, s_ref):
        kb = k_ref[0, pl.ds(pl.multiple_of(ki * t, t), t), :]
        s_ref[...] = _nt_dot(kb, qcat)

    def consume(ki, s_ref, bias):
        s = s_ref[...]
        if bias is not None:
            s = s + bias
        m_prev = m_s[...]
        m_new = jnp.maximum(m_prev, jnp.max(s, axis=0, keepdims=True))
        alpha = jnp.exp2(m_prev - m_new)
        p = jnp.exp2(s - m_new).astype(BF16)
        acc[...] = alpha * acc[...] + _dot(vt[ki], p)
        m_s[...] = m_new

    n_far = qi - 1
    last_far = jnp.maximum(n_far - 1, 0)
    scores(qi, s_a)

    @pl.when(qi == 0)
    def _():
        consume(qi, s_a, bias_ref[0, 0])

    @pl.when(qi >= 1)
    def _():
        scores(qi - 1, s_b)
        consume(qi, s_a, bias_ref[0, 0])
        scores(0, s_a)
        consume(qi - 1, s_b, bias_ref[0, 1])

        def pair(jj, carry):
            j = 2 * jj
            scores(j + 1, s_b)
            consume(j, s_a, None)
            scores(jnp.minimum(j + 2, last_far), s_a)
            consume(j + 1, s_b, None)
            return carry

        lax.fori_loop(0, n_far // 2, pair, 0)

        @pl.when(n_far % 2 == 1)
        def _():
            consume(n_far - 1, s_a, None)

    lm = lam_ref[...]
    lam = (jnp.exp(jnp.sum(lm[0:1] * lm[1:2], keepdims=True))
           - jnp.exp(jnp.sum(lm[2:3] * lm[3:4], keepdims=True)) + lam_init)
    a = acc[...]
    o = a[:DA_V_DIM, :] / a[DA_V_DIM:DA_V_DIM + 1, :]
    out = o[:, :t] - lam * o[:, t:]
    out = out * lax.rsqrt(jnp.mean(out * out, axis=0, keepdims=True) + NORM_EPS) * g_ref[...]
    o_ref[0] = (out * (1.0 - lam_init)).T.astype(o_ref.dtype)


def diff_attention(proj3, bias_tiles, lam_params, subln_g, lam_init):
    b, s, _ = proj3.shape
    t = bias_tiles.shape[2]
    nq = s // t
    return pl.pallas_call(
        functools.partial(_attn_kernel, t=t, lam_init=lam_init),
        grid=(b, DA_HEADS, nq),
        in_specs=[
            pl.BlockSpec((1, t, DA_V_DIM), lambda bi, h, qi: (bi, qi, h)),
            pl.BlockSpec((1, s, DA_V_DIM), lambda bi, h, qi: (bi, 0, DA_HEADS + h)),
            pl.BlockSpec((1, s, DA_V_DIM), lambda bi, h, qi: (bi, 0, 2 * DA_HEADS + h)),
            pl.BlockSpec((1, 2, t, 2 * t), lambda bi, h, qi: (h, 0, 0, 0)),
            pl.BlockSpec((4, DA_HEAD_DIM), lambda bi, h, qi: (0, 0)),
            pl.BlockSpec((DA_V_DIM, 1), lambda bi, h, qi: (0, 0)),
        ],
        out_specs=pl.BlockSpec((1, t, DA_V_DIM), lambda bi, h, qi: (bi, qi, h)),
        out_shape=jax.ShapeDtypeStruct((b, s, DA_WIDTH), BF16),
        scratch_shapes=[pltpu.VMEM((nq, DA_V_DIM + ONES_ROWS, t), BF16),
                        pltpu.VMEM((1, 2 * t), F32),
                        pltpu.VMEM((DA_V_DIM + ONES_ROWS, 2 * t), F32),
                        pltpu.VMEM((t, 2 * t), F32), pltpu.VMEM((t, 2 * t), F32)],
        compiler_params=_cparams(("parallel", "parallel", "arbitrary")),
        name="diff_attention",
    )(proj3, proj3, proj3, bias_tiles, lam_params, subln_g.reshape(DA_V_DIM, 1))


def _pool_kernel(p_ref, w_ref, scale_ref, o_ref, halo, *, t):
    si = pl.program_id(1)

    @pl.when(si == 0)
    def _():
        halo[...] = jnp.zeros(halo.shape, F32)

    p = p_ref[0].astype(F32)
    ext = jnp.concatenate([halo[...], p], axis=0)
    halo[...] = p[t - POOL_HALO:, :]
    sums = {1: ext}
    w = 1
    while w < POOL_WINDOWS[-1]:
        sums[2 * w] = sums[w] + pltpu.roll(sums[w], w, 0)
        w *= 2
    lane = lax.broadcasted_iota(jnp.int32, (t, POOL_WIDTH), 1)
    pos = (si * t + lax.broadcasted_iota(jnp.int32, (t, POOL_WIDTH), 0) + 1).astype(F32)
    wsum = sums[POOL_WINDOWS[-1]][POOL_HALO:, :]
    cnt = jnp.minimum(pos, float(POOL_WINDOWS[-1]))
    for gi in range(len(POOL_WINDOWS) - 2, -1, -1):
        in_group = lane < (gi + 1) * POOL_GROUP_DIM
        wsum = jnp.where(in_group, sums[POOL_WINDOWS[gi]][POOL_HALO:, :], wsum)
        cnt = jnp.where(in_group, jnp.minimum(pos, float(POOL_WINDOWS[gi])), cnt)
    pooled = wsum / cnt - p
    mixed = _dot(pooled.astype(BF16), w_ref[...])
    o_ref[0] = (mixed * scale_ref[...]).astype(o_ref.dtype)


def pool_mixer(proj3, pool_w, pool_scale, t=512):
    b, s, _ = proj3.shape
    t = min(t, s)
    g = len(POOL_WINDOWS)
    wbd = (jnp.eye(g, dtype=F32)[:, None, :, None] * pool_w.astype(F32)[:, :, None, :]).reshape(
        POOL_WIDTH, POOL_WIDTH).astype(BF16)
    return pl.pallas_call(
        functools.partial(_pool_kernel, t=t),
        grid=(b, s // t),
        in_specs=[pl.BlockSpec((1, t, POOL_WIDTH), lambda bi, si: (bi, si, POOL_COL_BLOCK)),
                  pl.BlockSpec((POOL_WIDTH, POOL_WIDTH), lambda bi, si: (0, 0)),
                  pl.BlockSpec((1, POOL_WIDTH), lambda bi, si: (0, 0))],
        out_specs=pl.BlockSpec((1, t, POOL_WIDTH), lambda bi, si: (bi, si, 0)),
        out_shape=jax.ShapeDtypeStruct((b, s, POOL_WIDTH), BF16),
        scratch_shapes=[pltpu.VMEM((POOL_HALO, POOL_WIDTH), F32)],
        compiler_params=_cparams(("parallel", "arbitrary")),
        name="pool_mixer",
    )(proj3, wbd, pool_scale.reshape(1, POOL_WIDTH))


def _retention_tables(s, t):
    d, hn, c = RET_HEAD_DIM, RET_HEADS, RET_CHUNK
    half = d // 2
    inv = 10000.0 ** (-jnp.linspace(0.0, 1.0, half, dtype=F32))
    ang = jnp.arange(s)[:, None].astype(F32) * inv[None, :]
    cos, sin = jnp.cos(ang), jnp.sin(ang)
    cos_t = jnp.tile(jnp.concatenate([cos, cos], axis=-1), (1, hn))
    sin_t = jnp.tile(jnp.concatenate([-sin, sin], axis=-1), (1, hn))
    log_gamma = jnp.log(1.0 - 2.0 ** (-5.0 - jnp.arange(hn, dtype=F32)))
    idx = jnp.arange(c, dtype=F32)
    rel = idx[:, None] - idx[None, :]
    intra = jnp.where(rel >= 0, jnp.exp(log_gamma[:, None, None] * jnp.maximum(rel, 0.0)), 0.0)
    q_decay = jnp.exp(log_gamma[:, None] * (idx + 1.0))
    k_decay = jnp.exp(log_gamma[:, None] * (c - 1.0 - idx))
    chunk_decay = jnp.exp(log_gamma * c)
    lanes = lambda a: jnp.tile(jnp.repeat(a.T, d, axis=1), (t // c, 1))
    return cos_t, sin_t, intra, lanes(q_decay), lanes(k_decay), chunk_decay


def _ret_kernel(cd_ref, q_ref, k_ref, v_ref, g_ref, cos_ref, sin_ref, intra_ref, qd_ref, kd_ref, gn_ref,
                o_ref, state, *, t):
    si = pl.program_id(1)
    d, c = RET_HEAD_DIM, RET_CHUNK

    @pl.when(si == 0)
    def _():
        state[...] = jnp.zeros(state.shape, F32)

    lane = lax.broadcasted_iota(jnp.int32, (t, RET_WIDTH), 1)
    first_half = (lane % d) < (d // 2)

    def rotate(x):
        swapped = jnp.where(first_half, pltpu.roll(x, RET_WIDTH - d // 2, 1), pltpu.roll(x, d // 2, 1))
        return x * cos_ref[...] + swapped * sin_ref[...]

    q = rotate(q_ref[0].astype(F32))
    k = rotate(k_ref[0].astype(F32)) * (d ** -0.5)
    qs = (q * qd_ref[...]).astype(BF16)
    ks = (k * kd_ref[...]).astype(BF16)
    qb = q.astype(BF16)
    kb = k.astype(BF16)
    v = v_ref[0]

    for ci in range(t // c):
        rows = slice(ci * c, (ci + 1) * c)
        heads = []
        for h in range(RET_HEADS):
            cols = slice(h * d, (h + 1) * d)
            st = state[h]
            inner = _nt_dot(qb[rows, cols], kb[rows, cols]) * intra_ref[h]
            y = _dot(inner.astype(BF16), v[rows, cols]) + _dot(qs[rows, cols], st.astype(BF16))
            state[h] = st * cd_ref[h] + _dot(ks[rows, cols].T, v[rows, cols])
            mu = jnp.mean(y, axis=-1, keepdims=True)
            yc = y - mu
            var = jnp.mean(yc * yc, axis=-1, keepdims=True)
            heads.append(yc * lax.rsqrt(var + NORM_EPS))
        yn = jnp.concatenate(heads, axis=-1)
        gate = _silu(g_ref[0, rows, :].astype(F32))
        o_ref[0, rows, :] = (gate * (yn * gn_ref[...])).astype(o_ref.dtype)


def retention(proj3, gn_g, t=512):
    b, s, _ = proj3.shape
    t = min(t, s)
    cos_t, sin_t, intra, qd, kd, chunk_decay = _retention_tables(s, t)
    col = lambda j: pl.BlockSpec((1, t, RET_WIDTH), lambda bi, si, cd: (bi, si, RET_COL_BLOCK + j))
    const2 = lambda shape: pl.BlockSpec(shape, lambda bi, si, cd: (0, 0))
    grid_spec = pltpu.PrefetchScalarGridSpec(
        num_scalar_prefetch=1,
        grid=(b, s // t),
        in_specs=[col(0), col(1), col(2), col(3),
                  pl.BlockSpec((t, RET_WIDTH), lambda bi, si, cd: (si, 0)),
                  pl.BlockSpec((t, RET_WIDTH), lambda bi, si, cd: (si, 0)),
                  pl.BlockSpec((RET_HEADS, RET_CHUNK, RET_CHUNK), lambda bi, si, cd: (0, 0, 0)),
                  const2((t, RET_WIDTH)), const2((t, RET_WIDTH)), const2((1, RET_WIDTH))],
        out_specs=pl.BlockSpec((1, t, RET_WIDTH), lambda bi, si, cd: (bi, si, 0)),
        scratch_shapes=[pltpu.VMEM((RET_HEADS, RET_HEAD_DIM, RET_HEAD_DIM), F32)],
    )
    return pl.pallas_call(
        functools.partial(_ret_kernel, t=t),
        grid_spec=grid_spec,
        out_shape=jax.ShapeDtypeStruct((b, s, RET_WIDTH), BF16),
        compiler_params=_cparams(("parallel", "arbitrary")),
        name="retention",
    )(chunk_decay, proj3, proj3, proj3, proj3, cos_t, sin_t, intra, qd, kd, gn_g.reshape(1, RET_WIDTH))


def _outproj_kernel(da_ref, pool_ref, ret_ref, x_ref, w_ref, g_ref, b_ref, o_ref):
    e0, e1 = DA_WIDTH, DA_WIDTH + POOL_WIDTH
    mix = (_dot(da_ref[...], w_ref[0:e0, :]) + _dot(pool_ref[...], w_ref[e0:e1, :])
           + _dot(ret_ref[...], w_ref[e1:, :]))
    o_ref[...] = _layer_norm(ALPHA * x_ref[...] + mix, g_ref[...], b_ref[...])


def out_proj_ln(y_da, y_pool, y_ret, x2d, w, g, b, tm=512):
    n, dm = x2d.shape
    tm = min(tm, n)
    row = lambda width: pl.BlockSpec((tm, width), lambda i: (i, 0))
    const = lambda shape: pl.BlockSpec(shape, lambda i: (0, 0))
    return pl.pallas_call(
        _outproj_kernel,
        grid=(n // tm,),
        in_specs=[row(DA_WIDTH), row(POOL_WIDTH), row(RET_WIDTH), row(dm),
                  const(w.shape), const((1, dm)), const((1, dm))],
        out_specs=row(dm),
        out_shape=jax.ShapeDtypeStruct((n, dm), F32),
        compiler_params=_cparams(("parallel",)),
        name="out_proj_ln",
    )(y_da, y_pool, y_ret, x2d, w, g.reshape(1, dm), b.reshape(1, dm))


def _ffn_kernel(x_ref, wg_ref, wu_ref, wd_ref, g_ref, b_ref, o_ref, acc, xb):
    f = pl.program_id(1)

    @pl.when(f == 0)
    def _():
        xb[...] = x_ref[...].astype(BF16)
        acc[...] = jnp.zeros(acc.shape, F32)

    hidden = _silu(_dot(xb[...], wg_ref[...])) * _dot(xb[...], wu_ref[...])
    acc[...] += _dot(hidden.astype(BF16), wd_ref[...])

    @pl.when(f == pl.num_programs(1) - 1)
    def _():
        o_ref[...] = _layer_norm(ALPHA * x_ref[...] + acc[...], g_ref[...], b_ref[...])


def ffn_ln(x2d, wg, wu, wd, g, b, tm=1024, tf=256):
    n, dm = x2d.shape
    fdim = wg.shape[1]
    tm = min(tm, n)
    return pl.pallas_call(
        _ffn_kernel,
        grid=(n // tm, fdim // tf),
        in_specs=[pl.BlockSpec((tm, dm), lambda i, f: (i, 0)),
                  pl.BlockSpec((dm, tf), lambda i, f: (0, f)),
                  pl.BlockSpec((dm, tf), lambda i, f: (0, f)),
                  pl.BlockSpec((tf, dm), lambda i, f: (f, 0)),
                  pl.BlockSpec((1, dm), lambda i, f: (0, 0)),
                  pl.BlockSpec((1, dm), lambda i, f: (0, 0))],
        out_specs=pl.BlockSpec((tm, dm), lambda i, f: (i, 0)),
        out_shape=jax.ShapeDtypeStruct((n, dm), F32),
        scratch_shapes=[pltpu.VMEM((tm, dm), F32), pltpu.VMEM((tm, dm), BF16)],
        compiler_params=_cparams(("parallel", "arbitrary")),
        name="ffn_ln",
    )(x2d, wg, wu, wd, g.reshape(1, dm), b.reshape(1, dm))


ROUTE_ROWS = 8
LANES = 128


def _router_kernel(x_ref, w_ref, tri_ref, route_ref, route_t_ref, cnt_ref, carry, *, t):
    i = pl.program_id(0)

    @pl.when(i == 0)
    def _():
        carry[...] = jnp.zeros(carry.shape, F32)

    x = x_ref[...]
    xh = x.astype(BF16)
    xl = (x - xh.astype(F32)).astype(BF16)
    w = w_ref[...]
    wh = w.astype(BF16)
    wl = (w - wh.astype(F32)).astype(BF16)
    logits = _nt_dot(wh, xh) + _nt_dot(wl, xh) + _nt_dot(wh, xl)

    row = lax.broadcasted_iota(jnp.int32, logits.shape, 0)
    v0 = jnp.max(logits, axis=0, keepdims=True)
    i0 = jnp.min(jnp.where(logits == v0, row, N_EXPERTS), axis=0, keepdims=True)
    rest = jnp.where(row == i0, -jnp.inf, logits)
    v1 = jnp.max(rest, axis=0, keepdims=True)
    i1 = jnp.min(jnp.where(rest == v1, row, N_EXPERTS), axis=0, keepdims=True)
    ex = jnp.exp(v1 - v0)
    gate0 = 1.0 / (1.0 + ex)
    gate1 = ex / (1.0 + ex)

    oh0 = row == i0
    oh1 = row == i1
    member = jnp.where(oh0 | oh1, 1.0, 0.0)
    before = _dot(member.astype(BF16), tri_ref[...]) + carry[:, 0:1]
    rank0 = jnp.sum(jnp.where(oh0, before, 0.0), axis=0, keepdims=True)
    rank1 = jnp.sum(jnp.where(oh1, before, 0.0), axis=0, keepdims=True)
    carry[...] = carry[...] + jnp.sum(member, axis=1, keepdims=True)
    cnt_ref[...] = carry[...]

    route = jnp.concatenate([i0.astype(F32), i1.astype(F32), rank0, rank1, gate0, gate1,
                             jnp.zeros((2, t), F32)], axis=0)
    route_ref[...] = route
    padded = jnp.concatenate([route, jnp.zeros((LANES - ROUTE_ROWS, t), F32)], axis=0)
    route_t_ref[...] = padded.T


def route_tokens(x2d, router_w, t=1024):
    n, dm = x2d.shape
    t = min(t, n)
    tri = (jnp.arange(t)[:, None] < jnp.arange(t)[None, :]).astype(BF16)
    return pl.pallas_call(
        functools.partial(_router_kernel, t=t),
        grid=(n // t,),
        in_specs=[pl.BlockSpec((t, dm), lambda i: (i, 0)),
                  pl.BlockSpec((N_EXPERTS, dm), lambda i: (0, 0)),
                  pl.BlockSpec((t, t), lambda i: (0, 0))],
        out_specs=[pl.BlockSpec((ROUTE_ROWS, t), lambda i: (0, i)),
                   pl.BlockSpec((t, LANES), lambda i: (i, 0)),
                   pl.BlockSpec((N_EXPERTS, LANES), lambda i: (0, 0))],
        out_shape=[jax.ShapeDtypeStruct((ROUTE_ROWS, n), F32),
                   jax.ShapeDtypeStruct((n, LANES), F32),
                   jax.ShapeDtypeStruct((N_EXPERTS, LANES), F32)],
        scratch_shapes=[pltpu.VMEM((N_EXPERTS, LANES), F32)],
        compiler_params=_cparams(("arbitrary",)),
        name="route_tokens",
    )(x2d, router_w.T, tri)


def _dest_kernel(start_ref, route_ref, dest_ref):
    r = route_ref[...]
    for k in range(2):
        e = r[k:k + 1, :].astype(jnp.int32)
        base = jnp.zeros(e.shape, jnp.int32)
        for ei in range(N_EXPERTS):
            base = jnp.where(e == ei, start_ref[ei], base)
        dest_ref[k:k + 1, :] = base + r[2 + k:3 + k, :].astype(jnp.int32)


def slot_of_assignment(pad_start, route, t=1024):
    n = route.shape[1]
    t = min(t, n)
    grid_spec = pltpu.PrefetchScalarGridSpec(
        num_scalar_prefetch=1,
        grid=(n // t,),
        in_specs=[pl.BlockSpec((ROUTE_ROWS, t), lambda i, ps: (0, i))],
        out_specs=pl.BlockSpec((2, t), lambda i, ps: (0, i)),
    )
    return pl.pallas_call(
        _dest_kernel,
        grid_spec=grid_spec,
        out_shape=jax.ShapeDtypeStruct((2, n), jnp.int32),
        compiler_params=_cparams(("parallel",)),
        name="slot_of_assignment",
    )(pad_start, route)


def _row_copy(src, src_row, dst, dst_row, sem):
    return pltpu.make_async_copy(src.at[pl.ds(src_row, 1)], dst.at[pl.ds(dst_row, 1)], sem)


def _dispatch_kernel(dest_ref, x_ref, slots_in, slots_out, sem, *, t):
    del slots_in

    def issue(i, carry):
        for k in range(2):
            _row_copy(x_ref, i, slots_out, dest_ref[k, i], sem).start()
        return carry

    lax.fori_loop(0, t, issue, 0)

    def drain(i, carry):
        for k in range(2):
            _row_copy(x_ref, i, slots_out, dest_ref[k, i], sem).wait()
        return carry

    lax.fori_loop(0, t, drain, 0)


def dispatch_rows(dest, x2d, n_slots, t=512):
    n, dm = x2d.shape
    t = min(t, n)
    zeros = jnp.zeros((n_slots, dm), x2d.dtype)
    return pl.pallas_call(
        functools.partial(_dispatch_kernel, t=t),
        grid=(n // t,),
        in_specs=[pl.BlockSpec((2, t), lambda i: (0, i), memory_space=pltpu.SMEM),
                  pl.BlockSpec((t, dm), lambda i: (i, 0)),
                  pl.BlockSpec(memory_space=pl.ANY)],
        out_specs=pl.BlockSpec(memory_space=pl.ANY),
        out_shape=jax.ShapeDtypeStruct((n_slots, dm), x2d.dtype),
        scratch_shapes=[pltpu.SemaphoreType.DMA],
        input_output_aliases={2: 0},
        compiler_params=_cparams(("arbitrary",)),
        name="dispatch_rows",
    )(dest, x2d, zeros)


def _expert_kernel(be_ref, nv_ref, x_ref, wg_ref, wu_ref, wd_ref, o_ref, acc, xb):
    blk = pl.program_id(0)
    f = pl.program_id(1)
    valid = blk < nv_ref[0]

    @pl.when(valid & (f == 0))
    def _():
        xb[...] = x_ref[...].astype(BF16)
        acc[...] = jnp.zeros(acc.shape, F32)

    @pl.when(valid)
    def _():
        hidden = _silu(_dot(xb[...], wg_ref[0])) * _dot(xb[...], wu_ref[0])
        acc[...] += _dot(hidden.astype(BF16), wd_ref[0])

    last = f == pl.num_programs(1) - 1

    @pl.when(valid & last)
    def _():
        o_ref[...] = acc[...]

    @pl.when(jnp.logical_not(valid) & last)
    def _():
        o_ref[...] = jnp.zeros(o_ref.shape, o_ref.dtype)


def expert_swiglu(block_e, n_valid, slots, wg, wu, wd, bm, tf=512):
    n_slots, dm = slots.shape
    edim = wg.shape[2]
    grid_spec = pltpu.PrefetchScalarGridSpec(
        num_scalar_prefetch=2,
        grid=(n_slots // bm, edim // tf),
        in_specs=[pl.BlockSpec((bm, dm), lambda b, f, be, nv: (b, 0)),
                  pl.BlockSpec((1, dm, tf), lambda b, f, be, nv: (be[b], 0, f)),
                  pl.BlockSpec((1, dm, tf), lambda b, f, be, nv: (be[b], 0, f)),
                  pl.BlockSpec((1, tf, dm), lambda b, f, be, nv: (be[b], f, 0))],
        out_specs=pl.BlockSpec((bm, dm), lambda b, f, be, nv: (b, 0)),
        scratch_shapes=[pltpu.VMEM((bm, dm), F32), pltpu.VMEM((bm, dm), BF16)],
    )
    return pl.pallas_call(
        _expert_kernel,
        grid_spec=grid_spec,
        out_shape=jax.ShapeDtypeStruct((n_slots, dm), F32),
        compiler_params=_cparams(("parallel", "arbitrary")),
        name="expert_swiglu",
    )(block_e, n_valid, slots, wg, wu, wd)


def _combine_kernel(dest_ref, rt_ref, x_ref, y_hbm, g_ref, b_ref, o_ref, rows, sem, *, t):
    def issue(i, carry):
        for k in range(2):
            _row_copy(y_hbm, dest_ref[k, i], rows.at[k], i, sem).start()
        return carry

    lax.fori_loop(0, t, issue, 0)

    def drain(i, carry):
        for k in range(2):
            _row_copy(y_hbm, dest_ref[k, i], rows.at[k], i, sem).wait()
        return carry

    lax.fori_loop(0, t, drain, 0)

    rt = rt_ref[...]
    mixed = rt[:, 4:5] * rows[0] + rt[:, 5:6] * rows[1]
    o_ref[...] = _layer_norm(ALPHA * x_ref[...] + mixed, g_ref[...], b_ref[...])


def combine_ln(dest, route_t, x2d, y, g, b, t=512):
    n, dm = x2d.shape
    t = min(t, n)
    return pl.pallas_call(
        functools.partial(_combine_kernel, t=t),
        grid=(n // t,),
        in_specs=[pl.BlockSpec((2, t), lambda i: (0, i), memory_space=pltpu.SMEM),
                  pl.BlockSpec((t, LANES), lambda i: (i, 0)),
                  pl.BlockSpec((t, dm), lambda i: (i, 0)),
                  pl.BlockSpec(memory_space=pl.ANY),
                  pl.BlockSpec((1, dm), lambda i: (0, 0)),
                  pl.BlockSpec((1, dm), lambda i: (0, 0))],
        out_specs=pl.BlockSpec((t, dm), lambda i: (i, 0)),
        out_shape=jax.ShapeDtypeStruct((n, dm), F32),
        scratch_shapes=[pltpu.VMEM((2, t, dm), F32), pltpu.SemaphoreType.DMA],
        compiler_params=_cparams(("arbitrary",)),
        name="combine_ln",
    )(dest, route_t, x2d, y, g.reshape(1, dm), b.reshape(1, dm))


def moe_ln(x2d, router_w, wg, wu, wd, g, b, bm=1024):
    n, _ = x2d.shape
    bm = min(bm, n)
    route, route_t, cnt = route_tokens(x2d, router_w)
    counts = cnt[:, 0].astype(jnp.int32)
    padded = (counts + bm - 1) // bm * bm
    pad_end = jnp.cumsum(padded)
    pad_start = (pad_end - padded).astype(jnp.int32)
    n_blocks = 2 * n // bm + N_EXPERTS
    n_slots = n_blocks * bm
    block_e = jnp.minimum(jnp.searchsorted(pad_end, jnp.arange(n_blocks) * bm, side='right'),
                          N_EXPERTS - 1).astype(jnp.int32)
    n_valid = (pad_end[-1:] // bm).astype(jnp.int32)
    dest = slot_of_assignment(pad_start, route)
    slots = dispatch_rows(dest, x2d, n_slots)
    y = expert_swiglu(block_e, n_valid, slots, wg, wu, wd, bm)
    return combine_ln(dest, route_t, x2d, y, g, b)


def kernel(x, rel_bias, w_in, diff_lambda, diff_subln_g, pool_w, pool_scale, ret_gn_g, w_out,
           ln1_g, ln1_b, ln2_g, ln2_b, ffn_w_gate, ffn_w_up, ffn_w_down,
           router_w, moe_w_gate, moe_w_up, moe_w_down):
    bsz, seq, dm = x.shape
    x2d = x.reshape(bsz * seq, dm)
    bias_tiles = attn_bias_tiles(rel_bias, min(ATTN_BLOCK, seq))
    col_scale = jnp.where(jnp.arange(w_in.shape[2]) < DA_WIDTH, Q_SCALE, 1.0).astype(F32)
    for l in range(DEPTH):
        lam_init = 0.8 - 0.6 * math.exp(-0.3 * l)
        proj = in_proj(x2d, (w_in[l] * col_scale).astype(BF16)).reshape(bsz, seq, -1)
        y_da = diff_attention(proj, bias_tiles, diff_lambda[l], diff_subln_g[l], lam_init)
        y_pool = pool_mixer(proj, pool_w[l], pool_scale[l])
        y_ret = retention(proj, ret_gn_g[l])
        flat = lambda a: a.reshape(bsz * seq, -1)
        x2d = out_proj_ln(flat(y_da), flat(y_pool), flat(y_ret), x2d, w_out[l].astype(BF16),
                          ln1_g[l], ln1_b[l])
        j = l // 2
        if l % 2 == 0:
            x2d = ffn_ln(x2d, ffn_w_gate[j].astype(BF16), ffn_w_up[j].astype(BF16),
                         ffn_w_down[j].astype(BF16), ln2_g[l], ln2_b[l])
        else:
            x2d = moe_ln(x2d, router_w[j], moe_w_gate[j].astype(BF16), moe_w_up[j].astype(BF16),
                         moe_w_down[j].astype(BF16), ln2_g[l], ln2_b[l])
    return x2d.reshape(bsz, seq, dm)
```

```python
import functools
import math

import jax
import jax.numpy as jnp
from jax import lax
from jax.experimental import pallas as pl
from jax.experimental.pallas import tpu as pltpu

F32 = jnp.float32
BF16 = jnp.bfloat16

DEPTH = 2
DA_HEAD_DIM = 64
DA_V_DIM = 128
DA_HEADS = 4
DA_WIDTH = 512
POOL_WIDTH = 256
POOL_WINDOWS = (2, 4, 8, 16)
POOL_GROUP_DIM = 64
POOL_HALO = 16
RET_WIDTH = 256
RET_HEAD_DIM = 64
RET_HEADS = 4
RET_CHUNK = 128
REL_BUCKETS = 32
REL_MAX_DIST = 128
N_EXPERTS = 8
ALPHA = (2 * DEPTH) ** 0.25
LN_EPS = 1e-5
NORM_EPS = 1e-6
NEG_BIG = -1e30

POOL_COL_BLOCK = 3 * DA_WIDTH // POOL_WIDTH
RET_COL_BLOCK = POOL_COL_BLOCK + 1

VMEM_LIMIT = 56 * 1024 * 1024


def _cparams(sem, vmem=VMEM_LIMIT):
    return pltpu.CompilerParams(dimension_semantics=sem, vmem_limit_bytes=vmem)


def _nt_dot(a, b):
    return lax.dot_general(a, b, (((1,), (1,)), ((), ())), preferred_element_type=F32)


def _dot(a, b):
    return jnp.dot(a, b, preferred_element_type=F32)


def _layer_norm(z, g, b):
    mu = jnp.mean(z, axis=-1, keepdims=True)
    zc = z - mu
    var = jnp.mean(zc * zc, axis=-1, keepdims=True)
    return zc * lax.rsqrt(var + LN_EPS) * g + b


def _silu(x):
    return x / (1.0 + jnp.exp(-x))


def _inproj_kernel(x_ref, w_ref, o_ref, *, tn):
    xb = x_ref[...].astype(BF16)
    for j in range(0, w_ref.shape[1], tn):
        o_ref[:, j:j + tn] = _dot(xb, w_ref[:, j:j + tn]).astype(o_ref.dtype)


def in_proj(x2d, w, tm=512, tn=256):
    n, k = x2d.shape
    m = w.shape[1]
    tm = min(tm, n)
    return pl.pallas_call(
        functools.partial(_inproj_kernel, tn=tn),
        grid=(n // tm,),
        in_specs=[pl.BlockSpec((tm, k), lambda i: (i, 0)),
                  pl.BlockSpec((k, m), lambda i: (0, 0))],
        out_specs=pl.BlockSpec((tm, m), lambda i: (i, 0)),
        out_shape=jax.ShapeDtypeStruct((n, m), BF16),
        compiler_params=_cparams(("parallel",)),
        name="in_proj",
    )(x2d, w)


def _t5_bucket(dist):
    n = jnp.maximum(dist, 0)
    max_exact = REL_BUCKETS // 2
    nf = jnp.maximum(n, 1).astype(F32)
    large = max_exact + (jnp.log(nf / max_exact) / math.log(REL_MAX_DIST / max_exact)
                         * (REL_BUCKETS - max_exact)).astype(jnp.int32)
    large = jnp.minimum(large, REL_BUCKETS - 1)
    return jnp.where(n < max_exact, n, large)


LOG2E = math.log2(math.e)
Q_SCALE = DA_HEAD_DIM ** -0.5 * LOG2E
ATTN_BLOCK = 512
ATTN_CHUNK = 256
ONES_ROWS = 16


def attn_bias_tiles(rel_bias, t):
    table = rel_bias.astype(F32).reshape(REL_BUCKETS, DA_HEADS * 2)
    vec = table[_t5_bucket(jnp.arange(2 * t))]
    far = table[_t5_bucket(jnp.array(2 * t))]
    vec = ((vec - far[None, :]) * LOG2E).T
    masked = jnp.full((DA_HEADS * 2, t), NEG_BIG, F32)
    u_diag = jnp.concatenate([vec[:, :t], masked], axis=1)
    u_prev = jnp.concatenate([vec[:, t:], vec[:, :t]], axis=1)

    def toeplitz(u):
        skew = jnp.tile(u, (1, t))[:, :t * (2 * t - 1)].reshape(-1, t, 2 * t - 1)
        return skew[:, :, :t]

    def per_head(a):
        return a.reshape(DA_HEADS, 2, t, t).transpose(0, 2, 1, 3).reshape(DA_HEADS, t, 2 * t)

    return jnp.stack([per_head(toeplitz(u_diag)), per_head(toeplitz(u_prev))], axis=1)


def _attn_kernel(q_ref, k_ref, v_ref, bias_ref, lam_ref, g_ref, o_ref, vt, m_s, acc, s_a, s_b, *, t, lam_init):
    qi = pl.program_id(2)
    nk = vt.shape[0]

    @pl.when(qi == 0)
    def _():
        ones = jnp.ones((ONES_ROWS, t), BF16)
        for ki in range(nk):
            v_t = v_ref[0, ki * t:(ki + 1) * t, :].astype(F32).T.astype(BF16)
            vt[ki] = jnp.concatenate([v_t, ones], axis=0)

    q = q_ref[0]
    lane = lax.broadcasted_iota(jnp.int32, q.shape, 1)
    zero = jnp.zeros_like(q)
    qcat = jnp.concatenate([jnp.where(lane < DA_HEAD_DIM, q, zero),
                            jnp.where(lane >= DA_HEAD_DIM, q, zero)], axis=0)
    m_s[...] = jnp.full(m_s.shape, NEG_BIG, F32)
    acc[...] = jnp.zeros(acc.shape, F32)

    def step(nxt, cur):
        kb = None
        if nxt is not None:
            kb = k_ref[0, pl.ds(pl.multiple_of(nxt[0] * t, t), t), :]
        for c in range(0, 2 * t, ATTN_CHUNK):
            cols = slice(c, c + ATTN_CHUNK)
            if nxt is not None:
                nxt[1][:, cols] = _nt_dot(kb, qcat[cols, :])
            if cur is not None:
                ki, s_ref, kind = cur
                s = s_ref[:, cols]
                if kind is not None:
                    s = s + bias_ref[0, kind, :, cols]
                m_prev = m_s[:, cols]
                m_new = jnp.maximum(m_prev, jnp.max(s, axis=0, keepdims=True))
                alpha = jnp.exp2(m_prev - m_new)
                p = jnp.exp2(s - m_new).astype(BF16)
                acc[:, cols] = alpha * acc[:, cols] + _dot(vt[ki], p)
                m_s[:, cols] = m_new

    n_far = qi - 1
    last_far = jnp.maximum(n_far - 1, 0)
    step((qi, s_a), None)

    @pl.when(qi == 0)
    def _():
        step(None, (qi, s_a, 0))

    @pl.when(qi >= 1)
    def _():
        step((qi - 1, s_b), (qi, s_a, 0))
        step((0, s_a), (qi - 1, s_b, 1))

        def quad(jj, carry):
            j = 4 * jj
            step((j + 1, s_b), (j, s_a, None))
            step((j + 2, s_a), (j + 1, s_b, None))
            step((j + 3, s_b), (j + 2, s_a, None))
            step((jnp.minimum(j + 4, last_far), s_a), (j + 3, s_b, None))
            return carry

        n_quad = n_far // 4
        lax.fori_loop(0, n_quad, quad, 0)

        def pair(jj, carry):
            j = 2 * jj
            step((j + 1, s_b), (j, s_a, None))
            step((jnp.minimum(j + 2, last_far), s_a), (j + 1, s_b, None))
            return carry

        lax.fori_loop(2 * n_quad, n_far // 2, pair, 0)

        @pl.when(n_far % 2 == 1)
        def _():
            step(None, (n_far - 1, s_a, None))

    lm = lam_ref[...]
    lam = (jnp.exp(jnp.sum(lm[0:1] * lm[1:2], keepdims=True))
           - jnp.exp(jnp.sum(lm[2:3] * lm[3:4], keepdims=True)) + lam_init)
    a = acc[...]
    o = a[:DA_V_DIM, :] / a[DA_V_DIM:DA_V_DIM + 1, :]
    out = o[:, :t] - lam * o[:, t:]
    out = out * lax.rsqrt(jnp.mean(out * out, axis=0, keepdims=True) + NORM_EPS) * g_ref[...]
    o_ref[0] = (out * (1.0 - lam_init)).T.astype(o_ref.dtype)


def diff_attention(proj3, bias_tiles, lam_params, subln_g, lam_init):
    b, s, _ = proj3.shape
    t = bias_tiles.shape[2]
    nq = s // t
    return pl.pallas_call(
        functools.partial(_attn_kernel, t=t, lam_init=lam_init),
        grid=(b, DA_HEADS, nq),
        in_specs=[
            pl.BlockSpec((1, t, DA_V_DIM), lambda bi, h, qi: (bi, qi, h)),
            pl.BlockSpec((1, s, DA_V_DIM), lambda bi, h, qi: (bi, 0, DA_HEADS + h)),
            pl.BlockSpec((1, s, DA_V_DIM), lambda bi, h, qi: (bi, 0, 2 * DA_HEADS + h)),
            pl.BlockSpec((1, 2, t, 2 * t), lambda bi, h, qi: (h, 0, 0, 0)),
            pl.BlockSpec((4, DA_HEAD_DIM), lambda bi, h, qi: (0, 0)),
            pl.BlockSpec((DA_V_DIM, 1), lambda bi, h, qi: (0, 0)),
        ],
        out_specs=pl.BlockSpec((1, t, DA_V_DIM), lambda bi, h, qi: (bi, qi, h)),
        out_shape=jax.ShapeDtypeStruct((b, s, DA_WIDTH), BF16),
        scratch_shapes=[pltpu.VMEM((nq, DA_V_DIM + ONES_ROWS, t), BF16),
                        pltpu.VMEM((1, 2 * t), F32),
                        pltpu.VMEM((DA_V_DIM + ONES_ROWS, 2 * t), F32),
                        pltpu.VMEM((t, 2 * t), F32), pltpu.VMEM((t, 2 * t), F32)],
        compiler_params=_cparams(("parallel", "parallel", "arbitrary")),
        name="diff_attention",
    )(proj3, proj3, proj3, bias_tiles, lam_params, subln_g.reshape(DA_V_DIM, 1))


def _pool_kernel(p_ref, w_ref, scale_ref, o_ref, halo, *, t):
    si = pl.program_id(1)

    @pl.when(si == 0)
    def _():
        halo[...] = jnp.zeros(halo.shape, F32)

    p = p_ref[0].astype(F32)
    ext = jnp.concatenate([halo[...], p], axis=0)
    halo[...] = p[t - POOL_HALO:, :]
    sums = {1: ext}
    w = 1
    while w < POOL_WINDOWS[-1]:
        sums[2 * w] = sums[w] + pltpu.roll(sums[w], w, 0)
        w *= 2
    lane = lax.broadcasted_iota(jnp.int32, (t, POOL_WIDTH), 1)
    pos = (si * t + lax.broadcasted_iota(jnp.int32, (t, POOL_WIDTH), 0) + 1).astype(F32)
    wsum = sums[POOL_WINDOWS[-1]][POOL_HALO:, :]
    cnt = jnp.minimum(pos, float(POOL_WINDOWS[-1]))
    for gi in range(len(POOL_WINDOWS) - 2, -1, -1):
        in_group = lane < (gi + 1) * POOL_GROUP_DIM
        wsum = jnp.where(in_group, sums[POOL_WINDOWS[gi]][POOL_HALO:, :], wsum)
        cnt = jnp.where(in_group, jnp.minimum(pos, float(POOL_WINDOWS[gi])), cnt)
    pooled = wsum / cnt - p
    mixed = _dot(pooled.astype(BF16), w_ref[...])
    o_ref[0] = (mixed * scale_ref[...]).astype(o_ref.dtype)


def pool_mixer(proj3, pool_w, pool_scale, t=512):
    b, s, _ = proj3.shape
    t = min(t, s)
    g = len(POOL_WINDOWS)
    wbd = (jnp.eye(g, dtype=F32)[:, None, :, None] * pool_w.astype(F32)[:, :, None, :]).reshape(
        POOL_WIDTH, POOL_WIDTH).astype(BF16)
    return pl.pallas_call(
        functools.partial(_pool_kernel, t=t),
        grid=(b, s // t),
        in_specs=[pl.BlockSpec((1, t, POOL_WIDTH), lambda bi, si: (bi, si, POOL_COL_BLOCK)),
                  pl.BlockSpec((POOL_WIDTH, POOL_WIDTH), lambda bi, si: (0, 0)),
                  pl.BlockSpec((1, POOL_WIDTH), lambda bi, si: (0, 0))],
        out_specs=pl.BlockSpec((1, t, POOL_WIDTH), lambda bi, si: (bi, si, 0)),
        out_shape=jax.ShapeDtypeStruct((b, s, POOL_WIDTH), BF16),
        scratch_shapes=[pltpu.VMEM((POOL_HALO, POOL_WIDTH), F32)],
        compiler_params=_cparams(("parallel", "arbitrary")),
        name="pool_mixer",
    )(proj3, wbd, pool_scale.reshape(1, POOL_WIDTH))


def _retention_tables(s, t):
    d, hn, c = RET_HEAD_DIM, RET_HEADS, RET_CHUNK
    half = d // 2
    inv = 10000.0 ** (-jnp.linspace(0.0, 1.0, half, dtype=F32))
    ang = jnp.arange(s)[:, None].astype(F32) * inv[None, :]
    cos, sin = jnp.cos(ang), jnp.sin(ang)
    cos_t = jnp.tile(jnp.concatenate([cos, cos], axis=-1), (1, hn))
    sin_t = jnp.tile(jnp.concatenate([-sin, sin], axis=-1), (1, hn))
    log_gamma = jnp.log(1.0 - 2.0 ** (-5.0 - jnp.arange(hn, dtype=F32)))
    idx = jnp.arange(c, dtype=F32)
    rel = idx[:, None] - idx[None, :]
    intra = jnp.where(rel >= 0, jnp.exp(log_gamma[:, None, None] * jnp.maximum(rel, 0.0)), 0.0)
    q_decay = jnp.exp(log_gamma[:, None] * (idx + 1.0))
    k_decay = jnp.exp(log_gamma[:, None] * (c - 1.0 - idx))
    chunk_decay = jnp.exp(log_gamma * c)
    lanes = lambda a: jnp.tile(jnp.repeat(a.T, d, axis=1), (t // c, 1))
    return cos_t, sin_t, intra, lanes(q_decay), lanes(k_decay), chunk_decay


def _ret_kernel(cd_ref, q_ref, k_ref, v_ref, g_ref, cos_ref, sin_ref, intra_ref, qd_ref, kd_ref, gn_ref,
                o_ref, state, *, t):
    si = pl.program_id(1)
    d, c = RET_HEAD_DIM, RET_CHUNK

    @pl.when(si == 0)
    def _():
        state[...] = jnp.zeros(state.shape, F32)

    lane = lax.broadcasted_iota(jnp.int32, (t, RET_WIDTH), 1)
    first_half = (lane % d) < (d // 2)

    def rotate(x):
        swapped = jnp.where(first_half, pltpu.roll(x, RET_WIDTH - d // 2, 1), pltpu.roll(x, d // 2, 1))
        return x * cos_ref[...] + swapped * sin_ref[...]

    q = rotate(q_ref[0].astype(F32))
    k = rotate(k_ref[0].astype(F32)) * (d ** -0.5)
    qs = (q * qd_ref[...]).astype(BF16)
    ks = (k * kd_ref[...]).astype(BF16)
    qb = q.astype(BF16)
    kb = k.astype(BF16)
    v = v_ref[0]

    for ci in range(t // c):
        rows = slice(ci * c, (ci + 1) * c)
        heads = []
        for h in range(RET_HEADS):
            cols = slice(h * d, (h + 1) * d)
            st = state[h]
            inner = _nt_dot(qb[rows, cols], kb[rows, cols]) * intra_ref[h]
            y = _dot(inner.astype(BF16), v[rows, cols]) + _dot(qs[rows, cols], st.astype(BF16))
            state[h] = st * cd_ref[h] + _dot(ks[rows, cols].T, v[rows, cols])
            mu = jnp.mean(y, axis=-1, keepdims=True)
            yc = y - mu
            var = jnp.mean(yc * yc, axis=-1, keepdims=True)
            heads.append(yc * lax.rsqrt(var + NORM_EPS))
        yn = jnp.concatenate(heads, axis=-1)
        gate = _silu(g_ref[0, rows, :].astype(F32))
        o_ref[0, rows, :] = (gate * (yn * gn_ref[...])).astype(o_ref.dtype)


def retention(proj3, gn_g, t=512):
    b, s, _ = proj3.shape
    t = min(t, s)
    cos_t, sin_t, intra, qd, kd, chunk_decay = _retention_tables(s, t)
    col = lambda j: pl.BlockSpec((1, t, RET_WIDTH), lambda bi, si, cd: (bi, si, RET_COL_BLOCK + j))
    const2 = lambda shape: pl.BlockSpec(shape, lambda bi, si, cd: (0, 0))
    grid_spec = pltpu.PrefetchScalarGridSpec(
        num_scalar_prefetch=1,
        grid=(b, s // t),
        in_specs=[col(0), col(1), col(2), col(3),
                  pl.BlockSpec((t, RET_WIDTH), lambda bi, si, cd: (si, 0)),
                  pl.BlockSpec((t, RET_WIDTH), lambda bi, si, cd: (si, 0)),
                  pl.BlockSpec((RET_HEADS, RET_CHUNK, RET_CHUNK), lambda bi, si, cd: (0, 0, 0)),
                  const2((t, RET_WIDTH)), const2((t, RET_WIDTH)), const2((1, RET_WIDTH))],
        out_specs=pl.BlockSpec((1, t, RET_WIDTH), lambda bi, si, cd: (bi, si, 0)),
        scratch_shapes=[pltpu.VMEM((RET_HEADS, RET_HEAD_DIM, RET_HEAD_DIM), F32)],
    )
    return pl.pallas_call(
        functools.partial(_ret_kernel, t=t),
        grid_spec=grid_spec,
        out_shape=jax.ShapeDtypeStruct((b, s, RET_WIDTH), BF16),
        compiler_params=_cparams(("parallel", "arbitrary")),
        name="retention",
    )(chunk_decay, proj3, proj3, proj3, proj3, cos_t, sin_t, intra, qd, kd, gn_g.reshape(1, RET_WIDTH))


def _outproj_kernel(da_ref, pool_ref, ret_ref, x_ref, w_ref, g_ref, b_ref, o_ref):
    e0, e1 = DA_WIDTH, DA_WIDTH + POOL_WIDTH
    mix = (_dot(da_ref[...], w_ref[0:e0, :]) + _dot(pool_ref[...], w_ref[e0:e1, :])
           + _dot(ret_ref[...], w_ref[e1:, :]))
    o_ref[...] = _layer_norm(ALPHA * x_ref[...] + mix, g_ref[...], b_ref[...])


def out_proj_ln(y_da, y_pool, y_ret, x2d, w, g, b, tm=512):
    n, dm = x2d.shape
    tm = min(tm, n)
    row = lambda width: pl.BlockSpec((tm, width), lambda i: (i, 0))
    const = lambda shape: pl.BlockSpec(shape, lambda i: (0, 0))
    return pl.pallas_call(
        _outproj_kernel,
        grid=(n // tm,),
        in_specs=[row(DA_WIDTH), row(POOL_WIDTH), row(RET_WIDTH), row(dm),
                  const(w.shape), const((1, dm)), const((1, dm))],
        out_specs=row(dm),
        out_shape=jax.ShapeDtypeStruct((n, dm), F32),
        compiler_params=_cparams(("parallel",)),
        name="out_proj_ln",
    )(y_da, y_pool, y_ret, x2d, w, g.reshape(1, dm), b.reshape(1, dm))


def _ffn_kernel(x_ref, wg_ref, wu_ref, wd_ref, g_ref, b_ref, o_ref, *, tf):
    x = x_ref[...]
    xb = x.astype(BF16)
    acc = jnp.zeros(x.shape, F32)
    for f0 in range(0, wg_ref.shape[1], tf):
        hidden = _silu(_dot(xb, wg_ref[:, f0:f0 + tf])) * _dot(xb, wu_ref[:, f0:f0 + tf])
        acc = acc + _dot(hidden.astype(BF16), wd_ref[f0:f0 + tf, :])
    o_ref[...] = _layer_norm(ALPHA * x + acc, g_ref[...], b_ref[...])


def ffn_ln(x2d, wg, wu, wd, g, b, tm=512, tf=256):
    n, dm = x2d.shape
    tm = min(tm, n)
    resident = lambda shape: pl.BlockSpec(shape, lambda i: (0, 0), pipeline_mode=pl.Buffered(1))
    return pl.pallas_call(
        functools.partial(_ffn_kernel, tf=tf),
        grid=(n // tm,),
        in_specs=[pl.BlockSpec((tm, dm), lambda i: (i, 0)),
                  resident(wg.shape), resident(wu.shape), resident(wd.shape),
                  resident((1, dm)), resident((1, dm))],
        out_specs=pl.BlockSpec((tm, dm), lambda i: (i, 0)),
        out_shape=jax.ShapeDtypeStruct((n, dm), F32),
        compiler_params=_cparams(("parallel",)),
        name="ffn_ln",
    )(x2d, wg, wu, wd, g.reshape(1, dm), b.reshape(1, dm))


ROUTE_ROWS = 8
LANES = 128


def _router_kernel(x_ref, w_ref, tri_ref, route_ref, route_t_ref, cnt_ref, carry, *, t):
    i = pl.program_id(0)

    @pl.when(i == 0)
    def _():
        carry[...] = jnp.zeros(carry.shape, F32)

    x = x_ref[...]
    xh = x.astype(BF16)
    xl = (x - xh.astype(F32)).astype(BF16)
    w = w_ref[...]
    wh = w.astype(BF16)
    wl = (w - wh.astype(F32)).astype(BF16)
    logits = _nt_dot(wh, xh) + _nt_dot(wl, xh) + _nt_dot(wh, xl)

    row = lax.broadcasted_iota(jnp.int32, logits.shape, 0)
    v0 = jnp.max(logits, axis=0, keepdims=True)
    i0 = jnp.min(jnp.where(logits == v0, row, N_EXPERTS), axis=0, keepdims=True)
    rest = jnp.where(row == i0, -jnp.inf, logits)
    v1 = jnp.max(rest, axis=0, keepdims=True)
    i1 = jnp.min(jnp.where(rest == v1, row, N_EXPERTS), axis=0, keepdims=True)
    ex = jnp.exp(v1 - v0)
    gate0 = 1.0 / (1.0 + ex)
    gate1 = ex / (1.0 + ex)

    oh0 = row == i0
    oh1 = row == i1
    member = jnp.where(oh0 | oh1, 1.0, 0.0)
    before = _dot(member.astype(BF16), tri_ref[...]) + carry[:, 0:1]
    rank0 = jnp.sum(jnp.where(oh0, before, 0.0), axis=0, keepdims=True)
    rank1 = jnp.sum(jnp.where(oh1, before, 0.0), axis=0, keepdims=True)
    carry[...] = carry[...] + jnp.sum(member, axis=1, keepdims=True)
    cnt_ref[...] = carry[...]

    route = jnp.concatenate([i0.astype(F32), i1.astype(F32), rank0, rank1, gate0, gate1,
                             jnp.zeros((2, t), F32)], axis=0)
    route_ref[...] = route
    padded = jnp.concatenate([route, jnp.zeros((LANES - ROUTE_ROWS, t), F32)], axis=0)
    route_t_ref[...] = padded.T


def route_tokens(x2d, router_w, t=1024):
    n, dm = x2d.shape
    t = min(t, n)
    tri = (jnp.arange(t)[:, None] < jnp.arange(t)[None, :]).astype(BF16)
    return pl.pallas_call(
        functools.partial(_router_kernel, t=t),
        grid=(n // t,),
        in_specs=[pl.BlockSpec((t, dm), lambda i: (i, 0)),
                  pl.BlockSpec((N_EXPERTS, dm), lambda i: (0, 0)),
                  pl.BlockSpec((t, t), lambda i: (0, 0))],
        out_specs=[pl.BlockSpec((ROUTE_ROWS, t), lambda i: (0, i)),
                   pl.BlockSpec((t, LANES), lambda i: (i, 0)),
                   pl.BlockSpec((N_EXPERTS, LANES), lambda i: (0, 0))],
        out_shape=[jax.ShapeDtypeStruct((ROUTE_ROWS, n), F32),
                   jax.ShapeDtypeStruct((n, LANES), F32),
                   jax.ShapeDtypeStruct((N_EXPERTS, LANES), F32)],
        scratch_shapes=[pltpu.VMEM((N_EXPERTS, LANES), F32)],
        compiler_params=_cparams(("arbitrary",)),
        name="route_tokens",
    )(x2d, router_w.T, tri)


def _dest_kernel(start_ref, route_ref, dest_ref):
    r = route_ref[...]
    for k in range(2):
        e = r[k:k + 1, :].astype(jnp.int32)
        base = jnp.zeros(e.shape, jnp.int32)
        for ei in range(N_EXPERTS):
            base = jnp.where(e == ei, start_ref[ei], base)
        dest_ref[k:k + 1, :] = base + r[2 + k:3 + k, :].astype(jnp.int32)


def slot_of_assignment(pad_start, route, t=1024):
    n = route.shape[1]
    t = min(t, n)
    grid_spec = pltpu.PrefetchScalarGridSpec(
        num_scalar_prefetch=1,
        grid=(n // t,),
        in_specs=[pl.BlockSpec((ROUTE_ROWS, t), lambda i, ps: (0, i))],
        out_specs=pl.BlockSpec((2, t), lambda i, ps: (0, i)),
    )
    return pl.pallas_call(
        _dest_kernel,
        grid_spec=grid_spec,
        out_shape=jax.ShapeDtypeStruct((2, n), jnp.int32),
        compiler_params=_cparams(("parallel",)),
        name="slot_of_assignment",
    )(pad_start, route)


def _row_copy(src, src_row, dst, dst_row, sem):
    return pltpu.make_async_copy(src.at[pl.ds(src_row, 1)], dst.at[pl.ds(dst_row, 1)], sem)


ISSUE_UNROLL = 8


def _dispatch_kernel(dest_ref, x_ref, slots_in, slots_out, sem, *, t):
    del slots_in

    def issue(j, carry):
        for u in range(ISSUE_UNROLL):
            i = j * ISSUE_UNROLL + u
            for k in range(2):
                _row_copy(x_ref, i, slots_out, dest_ref[k, i], sem).start()
        return carry

    lax.fori_loop(0, t // ISSUE_UNROLL, issue, 0)
    for k in range(2):
        pltpu.make_async_copy(x_ref, slots_out.at[pl.ds(0, t)], sem).wait()


def dispatch_rows(dest, x2d, n_slots, t=1024):
    n, dm = x2d.shape
    t = min(t, n)
    zeros = jnp.zeros((n_slots, dm), x2d.dtype)
    return pl.pallas_call(
        functools.partial(_dispatch_kernel, t=t),
        grid=(n // t,),
        in_specs=[pl.BlockSpec((2, t), lambda i: (0, i), memory_space=pltpu.SMEM),
                  pl.BlockSpec((t, dm), lambda i: (i, 0)),
                  pl.BlockSpec(memory_space=pl.ANY)],
        out_specs=pl.BlockSpec(memory_space=pl.ANY),
        out_shape=jax.ShapeDtypeStruct((n_slots, dm), x2d.dtype),
        scratch_shapes=[pltpu.SemaphoreType.DMA],
        input_output_aliases={2: 0},
        compiler_params=_cparams(("arbitrary",)),
        name="dispatch_rows",
    )(dest, x2d, zeros)


def _expert_kernel(be_ref, nv_ref, x_ref, wg_ref, wu_ref, wd_ref, o_ref, acc, xb):
    blk = pl.program_id(0)
    f = pl.program_id(1)
    valid = blk < nv_ref[0]

    @pl.when(valid & (f == 0))
    def _():
        xb[...] = x_ref[...].astype(BF16)
        acc[...] = jnp.zeros(acc.shape, F32)

    @pl.when(valid)
    def _():
        hidden = _silu(_dot(xb[...], wg_ref[0])) * _dot(xb[...], wu_ref[0])
        acc[...] += _dot(hidden.astype(BF16), wd_ref[0])

    last = f == pl.num_programs(1) - 1

    @pl.when(valid & last)
    def _():
        o_ref[...] = acc[...]

    @pl.when(jnp.logical_not(valid) & last)
    def _():
        o_ref[...] = jnp.zeros(o_ref.shape, o_ref.dtype)


def expert_swiglu(block_e, n_valid, slots, wg, wu, wd, bm, tf=512):
    n_slots, dm = slots.shape
    edim = wg.shape[2]
    grid_spec = pltpu.PrefetchScalarGridSpec(
        num_scalar_prefetch=2,
        grid=(n_slots // bm, edim // tf),
        in_specs=[pl.BlockSpec((bm, dm), lambda b, f, be, nv: (b, 0)),
                  pl.BlockSpec((1, dm, tf), lambda b, f, be, nv: (be[b], 0, f)),
                  pl.BlockSpec((1, dm, tf), lambda b, f, be, nv: (be[b], 0, f)),
                  pl.BlockSpec((1, tf, dm), lambda b, f, be, nv: (be[b], f, 0))],
        out_specs=pl.BlockSpec((bm, dm), lambda b, f, be, nv: (b, 0)),
        scratch_shapes=[pltpu.VMEM((bm, dm), F32), pltpu.VMEM((bm, dm), BF16)],
    )
    return pl.pallas_call(
        _expert_kernel,
        grid_spec=grid_spec,
        out_shape=jax.ShapeDtypeStruct((n_slots, dm), F32),
        compiler_params=_cparams(("parallel", "arbitrary")),
        name="expert_swiglu",
    )(block_e, n_valid, slots, wg, wu, wd)


def _combine_kernel(dest_ref, dest_next_ref, rt_ref, x_ref, y_hbm, g_ref, b_ref, o_ref, rows, sem, *, t):
    i = pl.program_id(0)
    cur = i % 2

    def gather(d_ref, buf):
        def issue(j, carry):
            for u in range(ISSUE_UNROLL):
                r = j * ISSUE_UNROLL + u
                for k in range(2):
                    _row_copy(y_hbm, d_ref[k, r], rows.at[buf, k], r, sem.at[buf]).start()
            return carry

        lax.fori_loop(0, t // ISSUE_UNROLL, issue, 0)

    @pl.when(i == 0)
    def _():
        gather(dest_ref, 0)

    @pl.when(i + 1 < pl.num_programs(0))
    def _():
        gather(dest_next_ref, 1 - cur)

    for k in range(2):
        pltpu.make_async_copy(y_hbm.at[pl.ds(0, t)], rows.at[cur, k], sem.at[cur]).wait()

    rt = rt_ref[...]
    mixed = rt[:, 4:5] * rows[cur, 0] + rt[:, 5:6] * rows[cur, 1]
    o_ref[...] = _layer_norm(ALPHA * x_ref[...] + mixed, g_ref[...], b_ref[...])


def combine_ln(dest, route_t, x2d, y, g, b, t=512):
    n, dm = x2d.shape
    t = min(t, n)
    last = n // t - 1
    return pl.pallas_call(
        functools.partial(_combine_kernel, t=t),
        grid=(n // t,),
        in_specs=[pl.BlockSpec((2, t), lambda i: (0, i), memory_space=pltpu.SMEM),
                  pl.BlockSpec((2, t), lambda i: (0, jnp.minimum(i + 1, last)), memory_space=pltpu.SMEM),
                  pl.BlockSpec((t, LANES), lambda i: (i, 0)),
                  pl.BlockSpec((t, dm), lambda i: (i, 0)),
                  pl.BlockSpec(memory_space=pl.ANY),
                  pl.BlockSpec((1, dm), lambda i: (0, 0)),
                  pl.BlockSpec((1, dm), lambda i: (0, 0))],
        out_specs=pl.BlockSpec((t, dm), lambda i: (i, 0)),
        out_shape=jax.ShapeDtypeStruct((n, dm), F32),
        scratch_shapes=[pltpu.VMEM((2, 2, t, dm), F32), pltpu.SemaphoreType.DMA((2,))],
        compiler_params=_cparams(("arbitrary",)),
        name="combine_ln",
    )(dest, dest, route_t, x2d, y, g.reshape(1, dm), b.reshape(1, dm))


def moe_ln(x2d, router_w, wg, wu, wd, g, b, bm=1024):
    n, _ = x2d.shape
    bm = min(bm, n)
    route, route_t, cnt = route_tokens(x2d, router_w)
    counts = cnt[:, 0].astype(jnp.int32)
    padded = (counts + bm - 1) // bm * bm
    pad_end = jnp.cumsum(padded)
    pad_start = (pad_end - padded).astype(jnp.int32)
    n_blocks = 2 * n // bm + N_EXPERTS
    n_slots = n_blocks * bm
    block_e = jnp.minimum(jnp.searchsorted(pad_end, jnp.arange(n_blocks) * bm, side='right'),
                          N_EXPERTS - 1).astype(jnp.int32)
    n_valid = (pad_end[-1:] // bm).astype(jnp.int32)
    dest = slot_of_assignment(pad_start, route)
    slots = dispatch_rows(dest, x2d, n_slots)
    y = expert_swiglu(block_e, n_valid, slots, wg, wu, wd, bm)
    return combine_ln(dest, route_t, x2d, y, g, b)


def kernel(x, rel_bias, w_in, diff_lambda, diff_subln_g, pool_w, pool_scale, ret_gn_g, w_out,
           ln1_g, ln1_b, ln2_g, ln2_b, ffn_w_gate, ffn_w_up, ffn_w_down,
           router_w, moe_w_gate, moe_w_up, moe_w_down):
    bsz, seq, dm = x.shape
    x2d = x.reshape(bsz * seq, dm)
    bias_tiles = attn_bias_tiles(rel_bias, min(ATTN_BLOCK, seq))
    col_scale = jnp.where(jnp.arange(w_in.shape[2]) < DA_WIDTH, Q_SCALE, 1.0).astype(F32)
    for l in range(DEPTH):
        lam_init = 0.8 - 0.6 * math.exp(-0.3 * l)
        proj = in_proj(x2d, (w_in[l] * col_scale).astype(BF16)).reshape(bsz, seq, -1)
        y_da = diff_attention(proj, bias_tiles, diff_lambda[l], diff_subln_g[l], lam_init)
        y_pool = pool_mixer(proj, pool_w[l], pool_scale[l])
        y_ret = retention(proj, ret_gn_g[l])
        flat = lambda a: a.reshape(bsz * seq, -1)
        x2d = out_proj_ln(flat(y_da), flat(y_pool), flat(y_ret), x2d, w_out[l].astype(BF16),
                          ln1_g[l], ln1_b[l])
        j = l // 2
        if l % 2 == 0:
            x2d = ffn_ln(x2d, ffn_w_gate[j].astype(BF16), ffn_w_up[j].astype(BF16),
                         ffn_w_down[j].astype(BF16), ln2_g[l], ln2_b[l])
        else:
            x2d = moe_ln(x2d, router_w[j], moe_w_gate[j].astype(BF16), moe_w_up[j].astype(BF16),
                         moe_w_down[j].astype(BF16), ln2_g[l], ln2_b[l])
    return x2d.reshape(bsz, seq, dm)
```

```python
import functools
import math

import jax
import jax.numpy as jnp
from jax import lax
from jax.experimental import pallas as pl
from jax.experimental.pallas import tpu as pltpu

F32 = jnp.float32
BF16 = jnp.bfloat16

DEPTH = 2
DA_HEAD_DIM = 64
DA_V_DIM = 128
DA_HEADS = 4
DA_WIDTH = 512
POOL_WIDTH = 256
POOL_WINDOWS = (2, 4, 8, 16)
POOL_GROUP_DIM = 64
POOL_HALO = 16
RET_WIDTH = 256
RET_HEAD_DIM = 64
RET_HEADS = 4
RET_CHUNK = 128
REL_BUCKETS = 32
REL_MAX_DIST = 128
N_EXPERTS = 8
ALPHA = (2 * DEPTH) ** 0.25
LN_EPS = 1e-5
NORM_EPS = 1e-6
NEG_BIG = -1e30

POOL_COL_BLOCK = 3 * DA_WIDTH // POOL_WIDTH
RET_COL_BLOCK = POOL_COL_BLOCK + 1

VMEM_LIMIT = 56 * 1024 * 1024


def _cparams(sem, vmem=VMEM_LIMIT):
    return pltpu.CompilerParams(dimension_semantics=sem, vmem_limit_bytes=vmem)


def _nt_dot(a, b):
    return lax.dot_general(a, b, (((1,), (1,)), ((), ())), preferred_element_type=F32)


def _dot(a, b):
    return jnp.dot(a, b, preferred_element_type=F32)


def _layer_norm(z, g, b):
    mu = jnp.mean(z, axis=-1, keepdims=True)
    zc = z - mu
    var = jnp.mean(zc * zc, axis=-1, keepdims=True)
    return zc * lax.rsqrt(var + LN_EPS) * g + b


def _silu(x):
    return x / (1.0 + jnp.exp(-x))


def _inproj_kernel(x_ref, w_ref, o_ref, *, tn):
    xb = x_ref[...].astype(BF16)
    for j in range(0, w_ref.shape[1], tn):
        o_ref[:, j:j + tn] = _dot(xb, w_ref[:, j:j + tn]).astype(o_ref.dtype)


def in_proj(x2d, w, tm=512, tn=256):
    n, k = x2d.shape
    m = w.shape[1]
    tm = min(tm, n)
    return pl.pallas_call(
        functools.partial(_inproj_kernel, tn=tn),
        grid=(n // tm,),
        in_specs=[pl.BlockSpec((tm, k), lambda i: (i, 0)),
                  pl.BlockSpec((k, m), lambda i: (0, 0))],
        out_specs=pl.BlockSpec((tm, m), lambda i: (i, 0)),
        out_shape=jax.ShapeDtypeStruct((n, m), BF16),
        compiler_params=_cparams(("parallel",)),
        name="in_proj",
    )(x2d, w)


def _t5_bucket(dist):
    n = jnp.maximum(dist, 0)
    max_exact = REL_BUCKETS // 2
    nf = jnp.maximum(n, 1).astype(F32)
    large = max_exact + (jnp.log(nf / max_exact) / math.log(REL_MAX_DIST / max_exact)
                         * (REL_BUCKETS - max_exact)).astype(jnp.int32)
    large = jnp.minimum(large, REL_BUCKETS - 1)
    return jnp.where(n < max_exact, n, large)


LOG2E = math.log2(math.e)
Q_SCALE = DA_HEAD_DIM ** -0.5 * LOG2E
ATTN_BLOCK = 512
ATTN_CHUNK = 256
ONES_ROWS = 16


def attn_bias_tiles(rel_bias, t):
    table = rel_bias.astype(F32).reshape(REL_BUCKETS, DA_HEADS * 2)
    vec = table[_t5_bucket(jnp.arange(2 * t))]
    far = table[_t5_bucket(jnp.array(2 * t))]
    vec = ((vec - far[None, :]) * LOG2E).T
    masked = jnp.full((DA_HEADS * 2, t), NEG_BIG, F32)
    u_diag = jnp.concatenate([vec[:, :t], masked], axis=1)
    u_prev = jnp.concatenate([vec[:, t:], vec[:, :t]], axis=1)

    def toeplitz(u):
        skew = jnp.tile(u, (1, t))[:, :t * (2 * t - 1)].reshape(-1, t, 2 * t - 1)
        return skew[:, :, :t]

    def per_head(a):
        return a.reshape(DA_HEADS, 2, t, t).transpose(0, 2, 1, 3).reshape(DA_HEADS, t, 2 * t)

    return jnp.stack([per_head(toeplitz(u_diag)), per_head(toeplitz(u_prev))], axis=1)


def _attn_kernel(q_ref, k_ref, v_ref, bias_ref, lam_ref, g_ref, o_ref, vt, m_s, acc, s_a, s_b, *, t, lam_init):
    qi = pl.program_id(2)
    nk = vt.shape[0]

    @pl.when(qi == 0)
    def _():
        ones = jnp.ones((ONES_ROWS, t), BF16)
        for ki in range(nk):
            v_t = v_ref[0, ki * t:(ki + 1) * t, :].astype(F32).T.astype(BF16)
            vt[ki] = jnp.concatenate([v_t, ones], axis=0)

    q = q_ref[0]
    lane = lax.broadcasted_iota(jnp.int32, q.shape, 1)
    zero = jnp.zeros_like(q)
    qcat = jnp.concatenate([jnp.where(lane < DA_HEAD_DIM, q, zero),
                            jnp.where(lane >= DA_HEAD_DIM, q, zero)], axis=0)
    m_s[...] = jnp.full(m_s.shape, NEG_BIG, F32)
    acc[...] = jnp.zeros(acc.shape, F32)

    def step(nxt, cur):
        kb = None
        if nxt is not None:
            kb = k_ref[0, pl.ds(pl.multiple_of(nxt[0] * t, t), t), :]
        for c in range(0, 2 * t, ATTN_CHUNK):
            cols = slice(c, c + ATTN_CHUNK)
            if nxt is not None:
                nxt[1][:, cols] = _nt_dot(kb, qcat[cols, :])
            if cur is not None:
                ki, s_ref, kind = cur
                s = s_ref[:, cols]
                if kind is not None:
                    s = s + bias_ref[0, kind, :, cols]
                m_prev = m_s[:, cols]
                m_new = jnp.maximum(m_prev, jnp.max(s, axis=0, keepdims=True))
                alpha = jnp.exp2(m_prev - m_new)
                p = jnp.exp2(s - m_new).astype(BF16)
                acc[:, cols] = alpha * acc[:, cols] + _dot(vt[ki], p)
                m_s[:, cols] = m_new

    n_far = qi - 1
    last_far = jnp.maximum(n_far - 1, 0)
    step((qi, s_a), None)

    @pl.when(qi == 0)
    def _():
        step(None, (qi, s_a, 0))

    @pl.when(qi >= 1)
    def _():
        step((qi - 1, s_b), (qi, s_a, 0))
        step((0, s_a), (qi - 1, s_b, 1))

        def quad(jj, carry):
            j = 4 * jj
            step((j + 1, s_b), (j, s_a, None))
            step((j + 2, s_a), (j + 1, s_b, None))
            step((j + 3, s_b), (j + 2, s_a, None))
            step((jnp.minimum(j + 4, last_far), s_a), (j + 3, s_b, None))
            return carry

        n_quad = n_far // 4
        lax.fori_loop(0, n_quad, quad, 0)

        def pair(jj, carry):
            j = 2 * jj
            step((j + 1, s_b), (j, s_a, None))
            step((jnp.minimum(j + 2, last_far), s_a), (j + 1, s_b, None))
            return carry

        lax.fori_loop(2 * n_quad, n_far // 2, pair, 0)

        @pl.when(n_far % 2 == 1)
        def _():
            step(None, (n_far - 1, s_a, None))

    lm = lam_ref[...]
    lam = (jnp.exp(jnp.sum(lm[0:1] * lm[1:2], keepdims=True))
           - jnp.exp(jnp.sum(lm[2:3] * lm[3:4], keepdims=True)) + lam_init)
    a = acc[...]
    o = a[:DA_V_DIM, :] / a[DA_V_DIM:DA_V_DIM + 1, :]
    out = o[:, :t] - lam * o[:, t:]
    out = out * lax.rsqrt(jnp.mean(out * out, axis=0, keepdims=True) + NORM_EPS) * g_ref[...]
    o_ref[0] = (out * (1.0 - lam_init)).T.astype(o_ref.dtype)


def diff_attention(proj3, bias_tiles, lam_params, subln_g, lam_init):
    b, s, _ = proj3.shape
    t = bias_tiles.shape[2]
    nq = s // t
    return pl.pallas_call(
        functools.partial(_attn_kernel, t=t, lam_init=lam_init),
        grid=(b, DA_HEADS, nq),
        in_specs=[
            pl.BlockSpec((1, t, DA_V_DIM), lambda bi, h, qi: (bi, qi, h)),
            pl.BlockSpec((1, s, DA_V_DIM), lambda bi, h, qi: (bi, 0, DA_HEADS + h)),
            pl.BlockSpec((1, s, DA_V_DIM), lambda bi, h, qi: (bi, 0, 2 * DA_HEADS + h)),
            pl.BlockSpec((1, 2, t, 2 * t), lambda bi, h, qi: (h, 0, 0, 0)),
            pl.BlockSpec((4, DA_HEAD_DIM), lambda bi, h, qi: (0, 0)),
            pl.BlockSpec((DA_V_DIM, 1), lambda bi, h, qi: (0, 0)),
        ],
        out_specs=pl.BlockSpec((1, t, DA_V_DIM), lambda bi, h, qi: (bi, qi, h)),
        out_shape=jax.ShapeDtypeStruct((b, s, DA_WIDTH), BF16),
        scratch_shapes=[pltpu.VMEM((nq, DA_V_DIM + ONES_ROWS, t), BF16),
                        pltpu.VMEM((1, 2 * t), F32),
                        pltpu.VMEM((DA_V_DIM + ONES_ROWS, 2 * t), F32),
                        pltpu.VMEM((t, 2 * t), F32), pltpu.VMEM((t, 2 * t), F32)],
        compiler_params=_cparams(("parallel", "parallel", "arbitrary")),
        name="diff_attention",
    )(proj3, proj3, proj3, bias_tiles, lam_params, subln_g.reshape(DA_V_DIM, 1))


def _pool_kernel(p_ref, w_ref, scale_ref, o_ref, halo, *, t):
    si = pl.program_id(1)

    @pl.when(si == 0)
    def _():
        halo[...] = jnp.zeros(halo.shape, F32)

    p = p_ref[0].astype(F32)
    ext = jnp.concatenate([halo[...], p], axis=0)
    halo[...] = p[t - POOL_HALO:, :]
    sums = {1: ext}
    w = 1
    while w < POOL_WINDOWS[-1]:
        sums[2 * w] = sums[w] + pltpu.roll(sums[w], w, 0)
        w *= 2
    lane = lax.broadcasted_iota(jnp.int32, (t, POOL_WIDTH), 1)
    pos = (si * t + lax.broadcasted_iota(jnp.int32, (t, POOL_WIDTH), 0) + 1).astype(F32)
    wsum = sums[POOL_WINDOWS[-1]][POOL_HALO:, :]
    cnt = jnp.minimum(pos, float(POOL_WINDOWS[-1]))
    for gi in range(len(POOL_WINDOWS) - 2, -1, -1):
        in_group = lane < (gi + 1) * POOL_GROUP_DIM
        wsum = jnp.where(in_group, sums[POOL_WINDOWS[gi]][POOL_HALO:, :], wsum)
        cnt = jnp.where(in_group, jnp.minimum(pos, float(POOL_WINDOWS[gi])), cnt)
    pooled = wsum / cnt - p
    mixed = _dot(pooled.astype(BF16), w_ref[...])
    o_ref[0] = (mixed * scale_ref[...]).astype(o_ref.dtype)


def pool_mixer(proj3, pool_w, pool_scale, t=512):
    b, s, _ = proj3.shape
    t = min(t, s)
    g = len(POOL_WINDOWS)
    wbd = (jnp.eye(g, dtype=F32)[:, None, :, None] * pool_w.astype(F32)[:, :, None, :]).reshape(
        POOL_WIDTH, POOL_WIDTH).astype(BF16)
    return pl.pallas_call(
        functools.partial(_pool_kernel, t=t),
        grid=(b, s // t),
        in_specs=[pl.BlockSpec((1, t, POOL_WIDTH), lambda bi, si: (bi, si, POOL_COL_BLOCK)),
                  pl.BlockSpec((POOL_WIDTH, POOL_WIDTH), lambda bi, si: (0, 0)),
                  pl.BlockSpec((1, POOL_WIDTH), lambda bi, si: (0, 0))],
        out_specs=pl.BlockSpec((1, t, POOL_WIDTH), lambda bi, si: (bi, si, 0)),
        out_shape=jax.ShapeDtypeStruct((b, s, POOL_WIDTH), BF16),
        scratch_shapes=[pltpu.VMEM((POOL_HALO, POOL_WIDTH), F32)],
        compiler_params=_cparams(("parallel", "arbitrary")),
        name="pool_mixer",
    )(proj3, wbd, pool_scale.reshape(1, POOL_WIDTH))


def _retention_tables(s, t):
    d, hn, c = RET_HEAD_DIM, RET_HEADS, RET_CHUNK
    half = d // 2
    inv = 10000.0 ** (-jnp.linspace(0.0, 1.0, half, dtype=F32))
    ang = jnp.arange(s)[:, None].astype(F32) * inv[None, :]
    cos, sin = jnp.cos(ang), jnp.sin(ang)
    cos_t = jnp.tile(jnp.concatenate([cos, cos], axis=-1), (1, hn))
    sin_t = jnp.tile(jnp.concatenate([-sin, sin], axis=-1), (1, hn))
    log_gamma = jnp.log(1.0 - 2.0 ** (-5.0 - jnp.arange(hn, dtype=F32)))
    idx = jnp.arange(c, dtype=F32)
    rel = idx[:, None] - idx[None, :]
    intra = jnp.where(rel >= 0, jnp.exp(log_gamma[:, None, None] * jnp.maximum(rel, 0.0)), 0.0)
    q_decay = jnp.exp(log_gamma[:, None] * (idx + 1.0))
    k_decay = jnp.exp(log_gamma[:, None] * (c - 1.0 - idx))
    chunk_decay = jnp.exp(log_gamma * c)
    lanes = lambda a: jnp.tile(jnp.repeat(a.T, d, axis=1), (t // c, 1))
    l = jnp.arange(hn * d)
    partner = jnp.where(l % d < half, l + half, l - half)
    swap = (l[:, None] == partner[None, :]).astype(BF16)
    same = (jnp.arange(2 * d)[:, None] // d) == (jnp.arange(2 * d)[None, :] // d)
    decay_bd = jnp.where(same[None], jnp.repeat(chunk_decay, d).reshape(hn // 2, 2 * d, 1), 0.0)
    avg = jnp.where(same, 1.0 / d, 0.0).astype(BF16)
    return cos_t, sin_t, intra, lanes(q_decay), lanes(k_decay), swap, decay_bd.astype(F32), avg


def _ret_kernel(q_ref, k_ref, v_ref, g_ref, cos_ref, sin_ref, intra_ref, qd_ref, kd_ref, swap_ref, decay_ref,
                avg_ref, gn_ref, o_ref, state, *, t):
    si = pl.program_id(1)
    d, c = RET_HEAD_DIM, RET_CHUNK
    pw = 2 * d

    @pl.when(si == 0)
    def _():
        state[...] = jnp.zeros(state.shape, F32)

    def rotate(x_ref):
        x = x_ref[0]
        return x.astype(F32) * cos_ref[...] + _dot(x, swap_ref[...]) * sin_ref[...]

    def lane_mean(a):
        hi = a.astype(BF16)
        lo = (a - hi.astype(F32)).astype(BF16)
        return _dot(hi, avg_ref[...]) + _dot(lo, avg_ref[...])

    q = rotate(q_ref)
    k = rotate(k_ref) * (d ** -0.5)
    qs = (q * qd_ref[...]).astype(BF16)
    ks = (k * kd_ref[...]).astype(BF16)
    qb = q.astype(BF16)
    kb = k.astype(BF16)
    v = v_ref[0]
    first = lax.broadcasted_iota(jnp.int32, (c, pw), 1) < d
    same = ((lax.broadcasted_iota(jnp.int32, (pw, pw), 0) < d)
            == (lax.broadcasted_iota(jnp.int32, (pw, pw), 1) < d))
    zero = jnp.zeros((c, pw), BF16)

    chunks = [slice(ci * c, (ci + 1) * c) for ci in range(t // c)]
    pairs = [slice(p * pw, (p + 1) * pw) for p in range(RET_HEADS // 2)]
    scores = {}
    for p, cols in enumerate(pairs):
        for ci, rows in enumerate(chunks):
            qp, kp = qb[rows, cols], kb[rows, cols]
            scores[p, ci] = ((_nt_dot(jnp.where(first, qp, zero), kp) * intra_ref[2 * p]).astype(BF16),
                             (_nt_dot(jnp.where(first, zero, qp), kp) * intra_ref[2 * p + 1]).astype(BF16))
    intra_out, increment = {}, {}
    for p, cols in enumerate(pairs):
        for ci, rows in enumerate(chunks):
            vp = v[rows, cols]
            intra_out[p, ci] = jnp.where(first, _dot(scores[p, ci][0], vp), _dot(scores[p, ci][1], vp))
            increment[p, ci] = jnp.where(same, _dot(ks[rows, cols].T, vp), 0.0)
    ys = {}
    for p, cols in enumerate(pairs):
        st = state[p]
        for ci, rows in enumerate(chunks):
            ys[p, ci] = intra_out[p, ci] + _dot(qs[rows, cols], st.astype(BF16))
            st = st * decay_ref[p] + increment[p, ci]
        state[p] = st
    y = jnp.concatenate([jnp.concatenate([ys[p, ci] for ci in range(len(chunks))], axis=0)
                         for p in range(len(pairs))], axis=1)
    mean = jnp.concatenate([lane_mean(y[:, cols]) for cols in pairs], axis=1)
    yc = y - mean
    var = jnp.concatenate([lane_mean((yc * yc)[:, cols]) for cols in pairs], axis=1)
    yn = yc * lax.rsqrt(var + NORM_EPS)
    gate = _silu(g_ref[0].astype(F32))
    o_ref[0] = (gate * (yn * gn_ref[...])).astype(o_ref.dtype)


def retention(proj3, gn_g, t=512):
    b, s, _ = proj3.shape
    t = min(t, s)
    cos_t, sin_t, intra, qd, kd, swap, decay_bd, avg = _retention_tables(s, t)
    pw = 2 * RET_HEAD_DIM
    col = lambda j: pl.BlockSpec((1, t, RET_WIDTH), lambda bi, si: (bi, si, RET_COL_BLOCK + j))
    const2 = lambda shape: pl.BlockSpec(shape, lambda bi, si: (0, 0))
    const3 = lambda shape: pl.BlockSpec(shape, lambda bi, si: (0, 0, 0))
    return pl.pallas_call(
        functools.partial(_ret_kernel, t=t),
        grid=(b, s // t),
        in_specs=[col(0), col(1), col(2), col(3),
                  pl.BlockSpec((t, RET_WIDTH), lambda bi, si: (si, 0)),
                  pl.BlockSpec((t, RET_WIDTH), lambda bi, si: (si, 0)),
                  const3((RET_HEADS, RET_CHUNK, RET_CHUNK)),
                  const2((t, RET_WIDTH)), const2((t, RET_WIDTH)), const2((RET_WIDTH, RET_WIDTH)),
                  const3((RET_HEADS // 2, pw, pw)), const2((pw, pw)), const2((1, RET_WIDTH))],
        out_specs=pl.BlockSpec((1, t, RET_WIDTH), lambda bi, si: (bi, si, 0)),
        out_shape=jax.ShapeDtypeStruct((b, s, RET_WIDTH), BF16),
        scratch_shapes=[pltpu.VMEM((RET_HEADS // 2, pw, pw), F32)],
        compiler_params=_cparams(("parallel", "arbitrary")),
        name="retention",
    )(proj3, proj3, proj3, proj3, cos_t, sin_t, intra, qd, kd, swap, decay_bd, avg, gn_g.reshape(1, RET_WIDTH))


def _outproj_kernel(da_ref, pool_ref, ret_ref, x_ref, w_ref, g_ref, b_ref, o_ref):
    e0, e1 = DA_WIDTH, DA_WIDTH + POOL_WIDTH
    mix = (_dot(da_ref[...], w_ref[0:e0, :]) + _dot(pool_ref[...], w_ref[e0:e1, :])
           + _dot(ret_ref[...], w_ref[e1:, :]))
    o_ref[...] = _layer_norm(ALPHA * x_ref[...] + mix, g_ref[...], b_ref[...])


def out_proj_ln(y_da, y_pool, y_ret, x2d, w, g, b, tm=512):
    n, dm = x2d.shape
    tm = min(tm, n)
    row = lambda width: pl.BlockSpec((tm, width), lambda i: (i, 0))
    const = lambda shape: pl.BlockSpec(shape, lambda i: (0, 0))
    return pl.pallas_call(
        _outproj_kernel,
        grid=(n // tm,),
        in_specs=[row(DA_WIDTH), row(POOL_WIDTH), row(RET_WIDTH), row(dm),
                  const(w.shape), const((1, dm)), const((1, dm))],
        out_specs=row(dm),
        out_shape=jax.ShapeDtypeStruct((n, dm), F32),
        compiler_params=_cparams(("parallel",)),
        name="out_proj_ln",
    )(y_da, y_pool, y_ret, x2d, w, g.reshape(1, dm), b.reshape(1, dm))


def _ffn_kernel(x_ref, wg_ref, wu_ref, wd_ref, g_ref, b_ref, o_ref, *, tf):
    x = x_ref[...]
    xb = x.astype(BF16)
    acc = jnp.zeros(x.shape, F32)
    for f0 in range(0, wg_ref.shape[1], tf):
        hidden = _silu(_dot(xb, wg_ref[:, f0:f0 + tf])) * _dot(xb, wu_ref[:, f0:f0 + tf])
        acc = acc + _dot(hidden.astype(BF16), wd_ref[f0:f0 + tf, :])
    o_ref[...] = _layer_norm(ALPHA * x + acc, g_ref[...], b_ref[...])


def ffn_ln(x2d, wg, wu, wd, g, b, tm=512, tf=256):
    n, dm = x2d.shape
    tm = min(tm, n)
    resident = lambda shape: pl.BlockSpec(shape, lambda i: (0, 0), pipeline_mode=pl.Buffered(1))
    return pl.pallas_call(
        functools.partial(_ffn_kernel, tf=tf),
        grid=(n // tm,),
        in_specs=[pl.BlockSpec((tm, dm), lambda i: (i, 0)),
                  resident(wg.shape), resident(wu.shape), resident(wd.shape),
                  resident((1, dm)), resident((1, dm))],
        out_specs=pl.BlockSpec((tm, dm), lambda i: (i, 0)),
        out_shape=jax.ShapeDtypeStruct((n, dm), F32),
        compiler_params=_cparams(("parallel",)),
        name="ffn_ln",
    )(x2d, wg, wu, wd, g.reshape(1, dm), b.reshape(1, dm))


ROUTE_ROWS = 8
LANES = 128


def _router_kernel(x_ref, w_ref, tri_ref, route_ref, route_t_ref, cnt_ref, carry, *, t):
    i = pl.program_id(0)

    @pl.when(i == 0)
    def _():
        carry[...] = jnp.zeros(carry.shape, F32)

    x = x_ref[...]
    xh = x.astype(BF16)
    xl = (x - xh.astype(F32)).astype(BF16)
    w = w_ref[...]
    wh = w.astype(BF16)
    wl = (w - wh.astype(F32)).astype(BF16)
    logits = _nt_dot(wh, xh) + _nt_dot(wl, xh) + _nt_dot(wh, xl)

    row = lax.broadcasted_iota(jnp.int32, logits.shape, 0)
    v0 = jnp.max(logits, axis=0, keepdims=True)
    i0 = jnp.min(jnp.where(logits == v0, row, N_EXPERTS), axis=0, keepdims=True)
    rest = jnp.where(row == i0, -jnp.inf, logits)
    v1 = jnp.max(rest, axis=0, keepdims=True)
    i1 = jnp.min(jnp.where(rest == v1, row, N_EXPERTS), axis=0, keepdims=True)
    ex = jnp.exp(v1 - v0)
    gate0 = 1.0 / (1.0 + ex)
    gate1 = ex / (1.0 + ex)

    oh0 = row == i0
    oh1 = row == i1
    member = jnp.where(oh0 | oh1, 1.0, 0.0)
    before = _dot(member.astype(BF16), tri_ref[...]) + carry[:, 0:1]
    rank0 = jnp.sum(jnp.where(oh0, before, 0.0), axis=0, keepdims=True)
    rank1 = jnp.sum(jnp.where(oh1, before, 0.0), axis=0, keepdims=True)
    carry[...] = carry[...] + jnp.sum(member, axis=1, keepdims=True)
    cnt_ref[...] = carry[...]

    route = jnp.concatenate([i0.astype(F32), i1.astype(F32), rank0, rank1, gate0, gate1,
                             jnp.zeros((2, t), F32)], axis=0)
    route_ref[...] = route
    padded = jnp.concatenate([route, jnp.zeros((LANES - ROUTE_ROWS, t), F32)], axis=0)
    route_t_ref[...] = padded.T


def route_tokens(x2d, router_w, t=1024):
    n, dm = x2d.shape
    t = min(t, n)
    tri = (jnp.arange(t)[:, None] < jnp.arange(t)[None, :]).astype(BF16)
    return pl.pallas_call(
        functools.partial(_router_kernel, t=t),
        grid=(n // t,),
        in_specs=[pl.BlockSpec((t, dm), lambda i: (i, 0)),
                  pl.BlockSpec((N_EXPERTS, dm), lambda i: (0, 0)),
                  pl.BlockSpec((t, t), lambda i: (0, 0))],
        out_specs=[pl.BlockSpec((ROUTE_ROWS, t), lambda i: (0, i)),
                   pl.BlockSpec((t, LANES), lambda i: (i, 0)),
                   pl.BlockSpec((N_EXPERTS, LANES), lambda i: (0, 0))],
        out_shape=[jax.ShapeDtypeStruct((ROUTE_ROWS, n), F32),
                   jax.ShapeDtypeStruct((n, LANES), F32),
                   jax.ShapeDtypeStruct((N_EXPERTS, LANES), F32)],
        scratch_shapes=[pltpu.VMEM((N_EXPERTS, LANES), F32)],
        compiler_params=_cparams(("arbitrary",)),
        name="route_tokens",
    )(x2d, router_w.T, tri)


def _dest_kernel(start_ref, route_ref, dest_ref):
    r = route_ref[...]
    for k in range(2):
        e = r[k:k + 1, :].astype(jnp.int32)
        base = jnp.zeros(e.shape, jnp.int32)
        for ei in range(N_EXPERTS):
            base = jnp.where(e == ei, start_ref[ei], base)
        dest_ref[k:k + 1, :] = base + r[2 + k:3 + k, :].astype(jnp.int32)


def slot_of_assignment(pad_start, route, t=1024):
    n = route.shape[1]
    t = min(t, n)
    grid_spec = pltpu.PrefetchScalarGridSpec(
        num_scalar_prefetch=1,
        grid=(n // t,),
        in_specs=[pl.BlockSpec((ROUTE_ROWS, t), lambda i, ps: (0, i))],
        out_specs=pl.BlockSpec((2, t), lambda i, ps: (0, i)),
    )
    return pl.pallas_call(
        _dest_kernel,
        grid_spec=grid_spec,
        out_shape=jax.ShapeDtypeStruct((2, n), jnp.int32),
        compiler_params=_cparams(("parallel",)),
        name="slot_of_assignment",
    )(pad_start, route)


def _row_copy(src, src_row, dst, dst_row, sem):
    return pltpu.make_async_copy(src.at[pl.ds(src_row, 1)], dst.at[pl.ds(dst_row, 1)], sem)


ISSUE_UNROLL = 8


def _dispatch_kernel(pad_lo_ref, pad_n_ref, dest_ref, x_ref, slots_out, zrow, sem, zsem, *, t):
    @pl.when(pl.program_id(0) == 0)
    def _():
        zrow[...] = jnp.zeros(zrow.shape, zrow.dtype)
        for e in range(N_EXPERTS):
            lo = pad_lo_ref[e]

            def zero_start(r, carry):
                _row_copy(zrow, 0, slots_out, lo + r, zsem).start()
                return carry

            def zero_wait(r, carry):
                _row_copy(zrow, 0, slots_out, lo + r, zsem).wait()
                return carry

            lax.fori_loop(0, pad_n_ref[e], zero_start, 0)
            lax.fori_loop(0, pad_n_ref[e], zero_wait, 0)

        tail_lo = pad_lo_ref[N_EXPERTS]
        group = zrow.shape[0]

        def tail_copy(r):
            dst = slots_out.at[pl.ds(pl.multiple_of(tail_lo + r * group, group), group)]
            return pltpu.make_async_copy(zrow, dst, zsem)

        def tail_start(r, carry):
            tail_copy(r).start()
            return carry

        def tail_wait(r, carry):
            tail_copy(r).wait()
            return carry

        lax.fori_loop(0, pad_n_ref[N_EXPERTS], tail_start, 0)
        lax.fori_loop(0, pad_n_ref[N_EXPERTS], tail_wait, 0)

    def issue(j, carry):
        for u in range(ISSUE_UNROLL):
            i = j * ISSUE_UNROLL + u
            for k in range(2):
                _row_copy(x_ref, i, slots_out, dest_ref[k, i], sem).start()
        return carry

    lax.fori_loop(0, t // ISSUE_UNROLL, issue, 0)
    for k in range(2):
        pltpu.make_async_copy(x_ref, slots_out.at[pl.ds(0, t)], sem).wait()


def dispatch_rows(pad_lo, pad_n, dest, x2d, n_slots, t=1024):
    n, dm = x2d.shape
    t = min(t, n)
    grid_spec = pltpu.PrefetchScalarGridSpec(
        num_scalar_prefetch=2,
        grid=(n // t,),
        in_specs=[pl.BlockSpec((2, t), lambda i, lo, cnt: (0, i), memory_space=pltpu.SMEM),
                  pl.BlockSpec((t, dm), lambda i, lo, cnt: (i, 0))],
        out_specs=pl.BlockSpec(memory_space=pl.ANY),
        scratch_shapes=[pltpu.VMEM((8, dm), x2d.dtype), pltpu.SemaphoreType.DMA, pltpu.SemaphoreType.DMA],
    )
    return pl.pallas_call(
        functools.partial(_dispatch_kernel, t=t),
        grid_spec=grid_spec,
        out_shape=jax.ShapeDtypeStruct((n_slots, dm), x2d.dtype),
        compiler_params=_cparams(("arbitrary",)),
        name="dispatch_rows",
    )(pad_lo, pad_n, dest, x2d)


def _expert_kernel(be_ref, nv_ref, x_ref, wg_ref, wu_ref, wd_ref, o_ref, acc, xb):
    blk = pl.program_id(0)
    f = pl.program_id(1)
    valid = blk < nv_ref[0]

    @pl.when(valid & (f == 0))
    def _():
        xb[...] = x_ref[...].astype(BF16)
        acc[...] = jnp.zeros(acc.shape, F32)

    @pl.when(valid)
    def _():
        hidden = (_silu(_dot(xb[...], wg_ref[0].astype(BF16)))
                  * _dot(xb[...], wu_ref[0].astype(BF16)))
        acc[...] += _dot(hidden.astype(BF16), wd_ref[0].astype(BF16))

    last = f == pl.num_programs(1) - 1

    @pl.when(valid & last)
    def _():
        o_ref[...] = acc[...]

    @pl.when(jnp.logical_not(valid) & last)
    def _():
        o_ref[...] = jnp.zeros(o_ref.shape, o_ref.dtype)


def expert_swiglu(block_e, n_valid, slots, wg, wu, wd, bm, tf=512):
    n_slots, dm = slots.shape
    edim = wg.shape[2]
    grid_spec = pltpu.PrefetchScalarGridSpec(
        num_scalar_prefetch=2,
        grid=(n_slots // bm, edim // tf),
        in_specs=[pl.BlockSpec((bm, dm), lambda b, f, be, nv: (jnp.minimum(b, nv[0] - 1), 0)),
                  pl.BlockSpec((1, dm, tf), lambda b, f, be, nv: (be[b], 0, f)),
                  pl.BlockSpec((1, dm, tf), lambda b, f, be, nv: (be[b], 0, f)),
                  pl.BlockSpec((1, tf, dm), lambda b, f, be, nv: (be[b], f, 0))],
        out_specs=pl.BlockSpec((bm, dm), lambda b, f, be, nv: (b, 0)),
        scratch_shapes=[pltpu.VMEM((bm, dm), F32), pltpu.VMEM((bm, dm), BF16)],
    )
    return pl.pallas_call(
        _expert_kernel,
        grid_spec=grid_spec,
        out_shape=jax.ShapeDtypeStruct((n_slots, dm), F32),
        compiler_params=_cparams(("parallel", "arbitrary")),
        name="expert_swiglu",
    )(block_e, n_valid, slots, wg, wu, wd)


def _combine_kernel(dest_ref, dest_next_ref, rt_ref, x_ref, y_hbm, g_ref, b_ref, o_ref, rows, sem, *, t):
    i = pl.program_id(0)
    cur = i % 2

    def gather(d_ref, buf):
        def issue(j, carry):
            for u in range(ISSUE_UNROLL):
                r = j * ISSUE_UNROLL + u
                for k in range(2):
                    _row_copy(y_hbm, d_ref[k, r], rows.at[buf, k], r, sem.at[buf]).start()
            return carry

        lax.fori_loop(0, t // ISSUE_UNROLL, issue, 0)

    @pl.when(i == 0)
    def _():
        gather(dest_ref, 0)

    @pl.when(i + 1 < pl.num_programs(0))
    def _():
        gather(dest_next_ref, 1 - cur)

    for k in range(2):
        pltpu.make_async_copy(y_hbm.at[pl.ds(0, t)], rows.at[cur, k], sem.at[cur]).wait()

    rt = rt_ref[...]
    mixed = rt[:, 4:5] * rows[cur, 0] + rt[:, 5:6] * rows[cur, 1]
    o_ref[...] = _layer_norm(ALPHA * x_ref[...] + mixed, g_ref[...], b_ref[...])


def combine_ln(dest, route_t, x2d, y, g, b, t=512):
    n, dm = x2d.shape
    t = min(t, n)
    last = n // t - 1
    return pl.pallas_call(
        functools.partial(_combine_kernel, t=t),
        grid=(n // t,),
        in_specs=[pl.BlockSpec((2, t), lambda i: (0, i), memory_space=pltpu.SMEM),
                  pl.BlockSpec((2, t), lambda i: (0, jnp.minimum(i + 1, last)), memory_space=pltpu.SMEM),
                  pl.BlockSpec((t, LANES), lambda i: (i, 0)),
                  pl.BlockSpec((t, dm), lambda i: (i, 0)),
                  pl.BlockSpec(memory_space=pl.ANY),
                  pl.BlockSpec((1, dm), lambda i: (0, 0)),
                  pl.BlockSpec((1, dm), lambda i: (0, 0))],
        out_specs=pl.BlockSpec((t, dm), lambda i: (i, 0)),
        out_shape=jax.ShapeDtypeStruct((n, dm), F32),
        scratch_shapes=[pltpu.VMEM((2, 2, t, dm), F32), pltpu.SemaphoreType.DMA((2,))],
        compiler_params=_cparams(("arbitrary",)),
        name="combine_ln",
    )(dest, dest, route_t, x2d, y, g.reshape(1, dm), b.reshape(1, dm))


def moe_ln(x2d, router_w, wg, wu, wd, g, b, bm=1024):
    n, _ = x2d.shape
    bm = min(bm, n)
    route, route_t, cnt = route_tokens(x2d, router_w)
    counts = cnt[:, 0].astype(jnp.int32)
    padded = (counts + bm - 1) // bm * bm
    pad_end = jnp.cumsum(padded)
    pad_start = (pad_end - padded).astype(jnp.int32)
    n_blocks = 2 * n // bm + N_EXPERTS
    n_slots = n_blocks * bm
    block_e = jnp.minimum(jnp.searchsorted(pad_end, jnp.arange(n_blocks) * bm, side='right'),
                          N_EXPERTS - 1).astype(jnp.int32)
    n_valid = (pad_end[-1:] // bm).astype(jnp.int32)
    dest = slot_of_assignment(pad_start, route)
    pad_lo = jnp.concatenate([pad_start + counts, pad_end[-1:]]).astype(jnp.int32)
    pad_n = jnp.concatenate([padded - counts, (n_slots - pad_end[-1:]) // 8]).astype(jnp.int32)
    slots = dispatch_rows(pad_lo, pad_n, dest, x2d, n_slots)
    y = expert_swiglu(block_e, n_valid, slots, wg, wu, wd, bm)
    return combine_ln(dest, route_t, x2d, y, g, b)


def kernel(x, rel_bias, w_in, diff_lambda, diff_subln_g, pool_w, pool_scale, ret_gn_g, w_out,
           ln1_g, ln1_b, ln2_g, ln2_b, ffn_w_gate, ffn_w_up, ffn_w_down,
           router_w, moe_w_gate, moe_w_up, moe_w_down):
    bsz, seq, dm = x.shape
    x2d = x.reshape(bsz * seq, dm)
    bias_tiles = attn_bias_tiles(rel_bias, min(ATTN_BLOCK, seq))
    col_scale = jnp.where(jnp.arange(w_in.shape[2]) < DA_WIDTH, Q_SCALE, 1.0).astype(F32)
    for l in range(DEPTH):
        lam_init = 0.8 - 0.6 * math.exp(-0.3 * l)
        proj = in_proj(x2d, (w_in[l] * col_scale).astype(BF16)).reshape(bsz, seq, -1)
        y_da = diff_attention(proj, bias_tiles, diff_lambda[l], diff_subln_g[l], lam_init)
        y_pool = pool_mixer(proj, pool_w[l], pool_scale[l])
        y_ret = retention(proj, ret_gn_g[l])
        flat = lambda a: a.reshape(bsz * seq, -1)
        x2d = out_proj_ln(flat(y_da), flat(y_pool), flat(y_ret), x2d, w_out[l].astype(BF16),
                          ln1_g[l], ln1_b[l])
        j = l // 2
        if l % 2 == 0:
            x2d = ffn_ln(x2d, ffn_w_gate[j].astype(BF16), ffn_w_up[j].astype(BF16),
                         ffn_w_down[j].astype(BF16), ln2_g[l], ln2_b[l])
        else:
            x2d = moe_ln(x2d, router_w[j], moe_w_gate[j], moe_w_up[j], moe_w_down[j], ln2_g[l], ln2_b[l])
    return x2d.reshape(bsz, seq, dm)
```

```python
import functools
import math

import jax
import jax.numpy as jnp
from jax import lax
from jax.experimental import pallas as pl
from jax.experimental.pallas import tpu as pltpu

F32 = jnp.float32
BF16 = jnp.bfloat16

DEPTH = 2
DA_HEAD_DIM = 64
DA_V_DIM = 128
DA_HEADS = 4
DA_WIDTH = 512
POOL_WIDTH = 256
POOL_WINDOWS = (2, 4, 8, 16)
POOL_GROUP_DIM = 64
POOL_HALO = 16
RET_WIDTH = 256
RET_HEAD_DIM = 64
RET_HEADS = 4
RET_CHUNK = 128
REL_BUCKETS = 32
REL_MAX_DIST = 128
N_EXPERTS = 8
ALPHA = (2 * DEPTH) ** 0.25
LN_EPS = 1e-5
NORM_EPS = 1e-6
NEG_BIG = -1e30

POOL_COL_BLOCK = 3 * DA_WIDTH // POOL_WIDTH
RET_COL_BLOCK = POOL_COL_BLOCK + 1

VMEM_LIMIT = 56 * 1024 * 1024


def _cparams(sem, vmem=VMEM_LIMIT):
    return pltpu.CompilerParams(dimension_semantics=sem, vmem_limit_bytes=vmem)


def _nt_dot(a, b):
    return lax.dot_general(a, b, (((1,), (1,)), ((), ())), preferred_element_type=F32)


def _dot(a, b):
    return jnp.dot(a, b, preferred_element_type=F32)


def _layer_norm(z, g, b):
    mu = jnp.mean(z, axis=-1, keepdims=True)
    zc = z - mu
    var = jnp.mean(zc * zc, axis=-1, keepdims=True)
    return zc * lax.rsqrt(var + LN_EPS) * g + b


def _silu(x):
    return x / (1.0 + jnp.exp(-x))


def _inproj_kernel(x_ref, w_ref, o_ref, *, tn):
    xb = x_ref[...].astype(BF16)
    for j in range(0, w_ref.shape[1], tn):
        o_ref[:, j:j + tn] = _dot(xb, w_ref[:, j:j + tn]).astype(o_ref.dtype)


def in_proj(x2d, w, tm=512, tn=256):
    n, k = x2d.shape
    m = w.shape[1]
    tm = min(tm, n)
    return pl.pallas_call(
        functools.partial(_inproj_kernel, tn=tn),
        grid=(n // tm,),
        in_specs=[pl.BlockSpec((tm, k), lambda i: (i, 0)),
                  pl.BlockSpec((k, m), lambda i: (0, 0))],
        out_specs=pl.BlockSpec((tm, m), lambda i: (i, 0)),
        out_shape=jax.ShapeDtypeStruct((n, m), BF16),
        compiler_params=_cparams(("parallel",)),
        name="in_proj",
    )(x2d, w)


def _t5_bucket(dist):
    n = jnp.maximum(dist, 0)
    max_exact = REL_BUCKETS // 2
    nf = jnp.maximum(n, 1).astype(F32)
    large = max_exact + (jnp.log(nf / max_exact) / math.log(REL_MAX_DIST / max_exact)
                         * (REL_BUCKETS - max_exact)).astype(jnp.int32)
    large = jnp.minimum(large, REL_BUCKETS - 1)
    return jnp.where(n < max_exact, n, large)


LOG2E = math.log2(math.e)
Q_SCALE = DA_HEAD_DIM ** -0.5 * LOG2E
ATTN_BLOCK = 512
ATTN_CHUNK = 256
ONES_ROWS = 16


def attn_bias_tiles(rel_bias, t):
    table = rel_bias.astype(F32).reshape(REL_BUCKETS, DA_HEADS * 2)
    vec = table[_t5_bucket(jnp.arange(2 * t))]
    far = table[_t5_bucket(jnp.array(2 * t))]
    vec = ((vec - far[None, :]) * LOG2E).T
    masked = jnp.full((DA_HEADS * 2, t), NEG_BIG, F32)
    u_diag = jnp.concatenate([vec[:, :t], masked], axis=1)
    u_prev = jnp.concatenate([vec[:, t:], vec[:, :t]], axis=1)

    def toeplitz(u):
        skew = jnp.tile(u, (1, t))[:, :t * (2 * t - 1)].reshape(-1, t, 2 * t - 1)
        return skew[:, :, :t]

    def per_head(a):
        return a.reshape(DA_HEADS, 2, t, t).transpose(0, 2, 1, 3).reshape(DA_HEADS, t, 2 * t)

    return jnp.stack([per_head(toeplitz(u_diag)), per_head(toeplitz(u_prev))], axis=1)


def _attn_kernel(q_ref, k_ref, v_ref, bias_ref, lam_ref, g_ref, o_ref, vt, m_s, acc, s_a, s_b, *, t, lam_init):
    qi = pl.program_id(2)
    nk = vt.shape[0]

    @pl.when(qi == 0)
    def _():
        ones = jnp.ones((ONES_ROWS, t), BF16)
        for ki in range(nk):
            v_t = v_ref[0, ki * t:(ki + 1) * t, :].astype(F32).T.astype(BF16)
            vt[ki] = jnp.concatenate([v_t, ones], axis=0)

    tq = 2 * t
    q = q_ref[0]
    lane = lax.broadcasted_iota(jnp.int32, q.shape, 1)
    zero = jnp.zeros_like(q)
    qcat = jnp.concatenate([jnp.where(lane < DA_HEAD_DIM, q, zero),
                            jnp.where(lane >= DA_HEAD_DIM, q, zero)], axis=0)
    m_s[...] = jnp.full(m_s.shape, NEG_BIG, F32)
    acc[...] = jnp.zeros(acc.shape, F32)

    chunk = min(ATTN_CHUNK, t)
    DIAG, PREV, FAR, SKIP = 0, 1, None, "skip"

    def step(nxt, cur):
        kb = None
        if nxt is not None:
            kb = k_ref[0, pl.ds(pl.multiple_of(nxt[0] * t, t), t), :]
        for c in range(0, 2 * tq, chunk):
            cols = slice(c, c + chunk)
            mp, qoff = divmod(c, tq)
            half, ioff = divmod(qoff, t)
            if nxt is not None and not (len(nxt) > 2 and nxt[2][half] == SKIP):
                nxt[1][:, cols] = _nt_dot(kb, qcat[cols, :])
            if cur is not None:
                ki, s_ref, kinds = cur
                kind = kinds[half]
                if kind == SKIP:
                    continue
                s = s_ref[:, cols]
                if kind is not FAR:
                    s = s + bias_ref[0, kind, :, mp * t + ioff:mp * t + ioff + chunk]
                m_prev = m_s[:, cols]
                m_new = jnp.maximum(m_prev, jnp.max(s, axis=0, keepdims=True))
                alpha = jnp.exp2(m_prev - m_new)
                p = jnp.exp2(s - m_new).astype(BF16)
                acc[:, cols] = alpha * acc[:, cols] + _dot(vt[ki], p)
                m_s[:, cols] = m_new

    upper = (SKIP, DIAG)
    lower = (DIAG, PREV)
    before = (PREV, FAR)
    plain = (FAR, FAR)
    step((2 * qi + 1, s_a, upper), None)
    step((2 * qi, s_b), (2 * qi + 1, s_a, upper))

    @pl.when(qi == 0)
    def _():
        step(None, (0, s_b, lower))

    @pl.when(qi >= 1)
    def _():
        n_far = 2 * qi - 1
        step((2 * qi - 1, s_a), (2 * qi, s_b, lower))
        step((0, s_b), (2 * qi - 1, s_a, before))

        def pair(jj, carry):
            j = 2 * jj
            step((j + 1, s_a), (j, s_b, plain))
            step((j + 2, s_b), (j + 1, s_a, plain))
            return carry

        lax.fori_loop(0, n_far // 2, pair, 0)
        step(None, (n_far - 1, s_b, plain))

    lm = lam_ref[...]
    lam = (jnp.exp(jnp.sum(lm[0:1] * lm[1:2], keepdims=True))
           - jnp.exp(jnp.sum(lm[2:3] * lm[3:4], keepdims=True)) + lam_init)
    a = acc[...]
    o = a[:DA_V_DIM, :] / a[DA_V_DIM:DA_V_DIM + 1, :]
    out = o[:, :tq] - lam * o[:, tq:]
    out = out * lax.rsqrt(jnp.mean(out * out, axis=0, keepdims=True) + NORM_EPS) * g_ref[...]
    o_ref[0] = (out * (1.0 - lam_init)).T.astype(o_ref.dtype)


def diff_attention(proj3, bias_tiles, lam_params, subln_g, lam_init):
    b, s, _ = proj3.shape
    t = bias_tiles.shape[2]
    tq = 2 * t
    return pl.pallas_call(
        functools.partial(_attn_kernel, t=t, lam_init=lam_init),
        grid=(b, DA_HEADS, s // tq),
        in_specs=[
            pl.BlockSpec((1, tq, DA_V_DIM), lambda bi, h, qi: (bi, qi, h)),
            pl.BlockSpec((1, s, DA_V_DIM), lambda bi, h, qi: (bi, 0, DA_HEADS + h)),
            pl.BlockSpec((1, s, DA_V_DIM), lambda bi, h, qi: (bi, 0, 2 * DA_HEADS + h)),
            pl.BlockSpec((1, 2, t, 2 * t), lambda bi, h, qi: (h, 0, 0, 0)),
            pl.BlockSpec((4, DA_HEAD_DIM), lambda bi, h, qi: (0, 0)),
            pl.BlockSpec((DA_V_DIM, 1), lambda bi, h, qi: (0, 0)),
        ],
        out_specs=pl.BlockSpec((1, tq, DA_V_DIM), lambda bi, h, qi: (bi, qi, h)),
        out_shape=jax.ShapeDtypeStruct((b, s, DA_WIDTH), BF16),
        scratch_shapes=[pltpu.VMEM((s // t, DA_V_DIM + ONES_ROWS, t), BF16),
                        pltpu.VMEM((1, 2 * tq), F32),
                        pltpu.VMEM((DA_V_DIM + ONES_ROWS, 2 * tq), F32),
                        pltpu.VMEM((t, 2 * tq), F32), pltpu.VMEM((t, 2 * tq), F32)],
        compiler_params=_cparams(("parallel", "parallel", "arbitrary")),
        name="diff_attention",
    )(proj3, proj3, proj3, bias_tiles, lam_params, subln_g.reshape(DA_V_DIM, 1))


def _pool_kernel(p_ref, w_ref, scale_ref, o_ref, halo, *, t):
    si = pl.program_id(1)

    @pl.when(si == 0)
    def _():
        halo[...] = jnp.zeros(halo.shape, F32)

    p = p_ref[0].astype(F32)
    ext = jnp.concatenate([halo[...], p], axis=0)
    halo[...] = p[t - POOL_HALO:, :]
    sums = {1: ext}
    w = 1
    while w < POOL_WINDOWS[-1]:
        sums[2 * w] = sums[w] + pltpu.roll(sums[w], w, 0)
        w *= 2
    lane = lax.broadcasted_iota(jnp.int32, (t, POOL_WIDTH), 1)
    pos = (si * t + lax.broadcasted_iota(jnp.int32, (t, POOL_WIDTH), 0) + 1).astype(F32)
    wsum = sums[POOL_WINDOWS[-1]][POOL_HALO:, :]
    cnt = jnp.minimum(pos, float(POOL_WINDOWS[-1]))
    for gi in range(len(POOL_WINDOWS) - 2, -1, -1):
        in_group = lane < (gi + 1) * POOL_GROUP_DIM
        wsum = jnp.where(in_group, sums[POOL_WINDOWS[gi]][POOL_HALO:, :], wsum)
        cnt = jnp.where(in_group, jnp.minimum(pos, float(POOL_WINDOWS[gi])), cnt)
    pooled = wsum / cnt - p
    mixed = _dot(pooled.astype(BF16), w_ref[...])
    o_ref[0] = (mixed * scale_ref[...]).astype(o_ref.dtype)


def pool_mixer(proj3, pool_w, pool_scale, t=512):
    b, s, _ = proj3.shape
    t = min(t, s)
    g = len(POOL_WINDOWS)
    wbd = (jnp.eye(g, dtype=F32)[:, None, :, None] * pool_w.astype(F32)[:, :, None, :]).reshape(
        POOL_WIDTH, POOL_WIDTH).astype(BF16)
    return pl.pallas_call(
        functools.partial(_pool_kernel, t=t),
        grid=(b, s // t),
        in_specs=[pl.BlockSpec((1, t, POOL_WIDTH), lambda bi, si: (bi, si, POOL_COL_BLOCK)),
                  pl.BlockSpec((POOL_WIDTH, POOL_WIDTH), lambda bi, si: (0, 0)),
                  pl.BlockSpec((1, POOL_WIDTH), lambda bi, si: (0, 0))],
        out_specs=pl.BlockSpec((1, t, POOL_WIDTH), lambda bi, si: (bi, si, 0)),
        out_shape=jax.ShapeDtypeStruct((b, s, POOL_WIDTH), BF16),
        scratch_shapes=[pltpu.VMEM((POOL_HALO, POOL_WIDTH), F32)],
        compiler_params=_cparams(("parallel", "arbitrary")),
        name="pool_mixer",
    )(proj3, wbd, pool_scale.reshape(1, POOL_WIDTH))


def _retention_tables(s, t):
    d, hn, c = RET_HEAD_DIM, RET_HEADS, RET_CHUNK
    half = d // 2
    inv = 10000.0 ** (-jnp.linspace(0.0, 1.0, half, dtype=F32))
    ang = jnp.arange(s)[:, None].astype(F32) * inv[None, :]
    cos, sin = jnp.cos(ang), jnp.sin(ang)
    cos_t = jnp.tile(jnp.concatenate([cos, cos], axis=-1), (1, hn))
    sin_t = jnp.tile(jnp.concatenate([-sin, sin], axis=-1), (1, hn))
    log_gamma = jnp.log(1.0 - 2.0 ** (-5.0 - jnp.arange(hn, dtype=F32)))
    idx = jnp.arange(c, dtype=F32)
    rel = idx[:, None] - idx[None, :]
    intra = jnp.where(rel >= 0, jnp.exp(log_gamma[:, None, None] * jnp.maximum(rel, 0.0)), 0.0)
    q_decay = jnp.exp(log_gamma[:, None] * (idx + 1.0))
    k_decay = jnp.exp(log_gamma[:, None] * (c - 1.0 - idx))
    chunk_decay = jnp.exp(log_gamma * c)
    lanes = lambda a: jnp.tile(jnp.repeat(a.T, d, axis=1), (t // c, 1))
    l = jnp.arange(hn * d)
    partner = jnp.where(l % d < half, l + half, l - half)
    swap = (l[:, None] == partner[None, :]).astype(BF16)
    same = (jnp.arange(2 * d)[:, None] // d) == (jnp.arange(2 * d)[None, :] // d)
    decay_bd = jnp.where(same[None], jnp.repeat(chunk_decay, d).reshape(hn // 2, 2 * d, 1), 0.0)
    avg = jnp.where(same, 1.0 / d, 0.0).astype(BF16)
    return cos_t, sin_t, intra, lanes(q_decay), lanes(k_decay), swap, decay_bd.astype(F32), avg


def _ret_kernel(q_ref, k_ref, v_ref, g_ref, cos_ref, sin_ref, intra_ref, qd_ref, kd_ref, swap_ref, decay_ref,
                avg_ref, gn_ref, o_ref, state, *, t):
    si = pl.program_id(1)
    d, c = RET_HEAD_DIM, RET_CHUNK
    pw = 2 * d

    @pl.when(si == 0)
    def _():
        state[...] = jnp.zeros(state.shape, F32)

    def rotate(x_ref):
        x = x_ref[0]
        return x.astype(F32) * cos_ref[...] + _dot(x, swap_ref[...]) * sin_ref[...]

    def lane_mean(a):
        hi = a.astype(BF16)
        lo = (a - hi.astype(F32)).astype(BF16)
        return _dot(hi, avg_ref[...]) + _dot(lo, avg_ref[...])

    q = rotate(q_ref)
    k = rotate(k_ref) * (d ** -0.5)
    qs = (q * qd_ref[...]).astype(BF16)
    ks = (k * kd_ref[...]).astype(BF16)
    qb = q.astype(BF16)
    kb = k.astype(BF16)
    v = v_ref[0]
    first = lax.broadcasted_iota(jnp.int32, (c, pw), 1) < d
    same = ((lax.broadcasted_iota(jnp.int32, (pw, pw), 0) < d)
            == (lax.broadcasted_iota(jnp.int32, (pw, pw), 1) < d))
    zero = jnp.zeros((c, pw), BF16)

    chunks = [slice(ci * c, (ci + 1) * c) for ci in range(t // c)]
    pairs = [slice(p * pw, (p + 1) * pw) for p in range(RET_HEADS // 2)]
    scores = {}
    for p, cols in enumerate(pairs):
        for ci, rows in enumerate(chunks):
            qp, kp = qb[rows, cols], kb[rows, cols]
            scores[p, ci] = ((_nt_dot(jnp.where(first, qp, zero), kp) * intra_ref[2 * p]).astype(BF16),
                             (_nt_dot(jnp.where(first, zero, qp), kp) * intra_ref[2 * p + 1]).astype(BF16))
    intra_out, increment = {}, {}
    for p, cols in enumerate(pairs):
        for ci, rows in enumerate(chunks):
            vp = v[rows, cols]
            intra_out[p, ci] = jnp.where(first, _dot(scores[p, ci][0], vp), _dot(scores[p, ci][1], vp))
            increment[p, ci] = jnp.where(same, _dot(ks[rows, cols].T, vp), 0.0)
    ys = {}
    for p, cols in enumerate(pairs):
        st = state[p]
        for ci, rows in enumerate(chunks):
            ys[p, ci] = intra_out[p, ci] + _dot(qs[rows, cols], st.astype(BF16))
            st = st * decay_ref[p] + increment[p, ci]
        state[p] = st
    y = jnp.concatenate([jnp.concatenate([ys[p, ci] for ci in range(len(chunks))], axis=0)
                         for p in range(len(pairs))], axis=1)
    mean = jnp.concatenate([lane_mean(y[:, cols]) for cols in pairs], axis=1)
    yc = y - mean
    var = jnp.concatenate([lane_mean((yc * yc)[:, cols]) for cols in pairs], axis=1)
    yn = yc * lax.rsqrt(var + NORM_EPS)
    gate = _silu(g_ref[0].astype(F32))
    o_ref[0] = (gate * (yn * gn_ref[...])).astype(o_ref.dtype)


def retention(proj3, gn_g, t=512):
    b, s, _ = proj3.shape
    t = min(t, s)
    cos_t, sin_t, intra, qd, kd, swap, decay_bd, avg = _retention_tables(s, t)
    pw = 2 * RET_HEAD_DIM
    col = lambda j: pl.BlockSpec((1, t, RET_WIDTH), lambda bi, si: (bi, si, RET_COL_BLOCK + j))
    const2 = lambda shape: pl.BlockSpec(shape, lambda bi, si: (0, 0))
    const3 = lambda shape: pl.BlockSpec(shape, lambda bi, si: (0, 0, 0))
    return pl.pallas_call(
        functools.partial(_ret_kernel, t=t),
        grid=(b, s // t),
        in_specs=[col(0), col(1), col(2), col(3),
                  pl.BlockSpec((t, RET_WIDTH), lambda bi, si: (si, 0)),
                  pl.BlockSpec((t, RET_WIDTH), lambda bi, si: (si, 0)),
                  const3((RET_HEADS, RET_CHUNK, RET_CHUNK)),
                  const2((t, RET_WIDTH)), const2((t, RET_WIDTH)), const2((RET_WIDTH, RET_WIDTH)),
                  const3((RET_HEADS // 2, pw, pw)), const2((pw, pw)), const2((1, RET_WIDTH))],
        out_specs=pl.BlockSpec((1, t, RET_WIDTH), lambda bi, si: (bi, si, 0)),
        out_shape=jax.ShapeDtypeStruct((b, s, RET_WIDTH), BF16),
        scratch_shapes=[pltpu.VMEM((RET_HEADS // 2, pw, pw), F32)],
        compiler_params=_cparams(("parallel", "arbitrary")),
        name="retention",
    )(proj3, proj3, proj3, proj3, cos_t, sin_t, intra, qd, kd, swap, decay_bd, avg, gn_g.reshape(1, RET_WIDTH))


def _mix_ln(da_ref, pool_ref, ret_ref, x_ref, w_ref, g_ref, b_ref):
    e0, e1 = DA_WIDTH, DA_WIDTH + POOL_WIDTH
    mix = (_dot(da_ref[...], w_ref[0:e0, :]) + _dot(pool_ref[...], w_ref[e0:e1, :])
           + _dot(ret_ref[...], w_ref[e1:, :]))
    return _layer_norm(ALPHA * x_ref[...] + mix, g_ref[...], b_ref[...])


def _outproj_kernel(da_ref, pool_ref, ret_ref, x_ref, w_ref, g_ref, b_ref, o_ref):
    o_ref[...] = _mix_ln(da_ref, pool_ref, ret_ref, x_ref, w_ref, g_ref, b_ref)


def out_proj_ln(y_da, y_pool, y_ret, x2d, w, g, b, tm=512):
    n, dm = x2d.shape
    tm = min(tm, n)
    row = lambda width: pl.BlockSpec((tm, width), lambda i: (i, 0))
    const = lambda shape: pl.BlockSpec(shape, lambda i: (0, 0))
    return pl.pallas_call(
        _outproj_kernel,
        grid=(n // tm,),
        in_specs=[row(DA_WIDTH), row(POOL_WIDTH), row(RET_WIDTH), row(dm),
                  const(w.shape), const((1, dm)), const((1, dm))],
        out_specs=row(dm),
        out_shape=jax.ShapeDtypeStruct((n, dm), F32),
        compiler_params=_cparams(("parallel",)),
        name="out_proj_ln",
    )(y_da, y_pool, y_ret, x2d, w, g.reshape(1, dm), b.reshape(1, dm))


def _mix_ffn_kernel(da_ref, pool_ref, ret_ref, x_ref, wo_ref, g1_ref, b1_ref, wg_ref, wu_ref, wd_ref,
                    g2_ref, b2_ref, o_ref, *, tf):
    x = _mix_ln(da_ref, pool_ref, ret_ref, x_ref, wo_ref, g1_ref, b1_ref)
    xb = x.astype(BF16)
    acc = jnp.zeros(x.shape, F32)
    for f0 in range(0, wg_ref.shape[1], tf):
        hidden = _silu(_dot(xb, wg_ref[:, f0:f0 + tf])) * _dot(xb, wu_ref[:, f0:f0 + tf])
        acc = acc + _dot(hidden.astype(BF16), wd_ref[f0:f0 + tf, :])
    o_ref[...] = _layer_norm(ALPHA * x + acc, g2_ref[...], b2_ref[...])


def out_proj_ffn_ln(y_da, y_pool, y_ret, x2d, wo, g1, b1, wg, wu, wd, g2, b2, tm=512, tf=256):
    n, dm = x2d.shape
    tm = min(tm, n)
    row = lambda width: pl.BlockSpec((tm, width), lambda i: (i, 0))
    resident = lambda shape: pl.BlockSpec(shape, lambda i: (0, 0), pipeline_mode=pl.Buffered(1))
    vec = lambda a: a.reshape(1, dm)
    return pl.pallas_call(
        functools.partial(_mix_ffn_kernel, tf=tf),
        grid=(n // tm,),
        in_specs=[row(DA_WIDTH), row(POOL_WIDTH), row(RET_WIDTH), row(dm),
                  resident(wo.shape), resident((1, dm)), resident((1, dm)),
                  resident(wg.shape), resident(wu.shape), resident(wd.shape),
                  resident((1, dm)), resident((1, dm))],
        out_specs=row(dm),
        out_shape=jax.ShapeDtypeStruct((n, dm), F32),
        compiler_params=_cparams(("parallel",)),
        name="out_proj_ffn_ln",
    )(y_da, y_pool, y_ret, x2d, wo, vec(g1), vec(b1), wg, wu, wd, vec(g2), vec(b2))


ROUTE_ROWS = 8
LANES = 128


def _router_kernel(x_ref, w_ref, tri_ref, route_ref, route_t_ref, cnt_ref, carry, *, t):
    i = pl.program_id(0)

    @pl.when(i == 0)
    def _():
        carry[...] = jnp.zeros(carry.shape, F32)

    x = x_ref[...]
    xh = x.astype(BF16)
    xl = (x - xh.astype(F32)).astype(BF16)
    w = w_ref[...]
    wh = w.astype(BF16)
    wl = (w - wh.astype(F32)).astype(BF16)
    logits = _nt_dot(wh, xh) + _nt_dot(wl, xh) + _nt_dot(wh, xl)

    row = lax.broadcasted_iota(jnp.int32, logits.shape, 0)
    v0 = jnp.max(logits, axis=0, keepdims=True)
    i0 = jnp.min(jnp.where(logits == v0, row, N_EXPERTS), axis=0, keepdims=True)
    rest = jnp.where(row == i0, -jnp.inf, logits)
    v1 = jnp.max(rest, axis=0, keepdims=True)
    i1 = jnp.min(jnp.where(rest == v1, row, N_EXPERTS), axis=0, keepdims=True)
    ex = jnp.exp(v1 - v0)
    gate0 = 1.0 / (1.0 + ex)
    gate1 = ex / (1.0 + ex)

    oh0 = row == i0
    oh1 = row == i1
    member = jnp.where(oh0 | oh1, 1.0, 0.0)
    before = _dot(member.astype(BF16), tri_ref[...]) + carry[:, 0:1]
    rank0 = jnp.sum(jnp.where(oh0, before, 0.0), axis=0, keepdims=True)
    rank1 = jnp.sum(jnp.where(oh1, before, 0.0), axis=0, keepdims=True)
    carry[...] = carry[...] + jnp.sum(member, axis=1, keepdims=True)
    cnt_ref[...] = carry[...]

    route = jnp.concatenate([i0.astype(F32), i1.astype(F32), rank0, rank1, gate0, gate1,
                             jnp.zeros((2, t), F32)], axis=0)
    route_ref[...] = route
    padded = jnp.concatenate([route, jnp.zeros((LANES - ROUTE_ROWS, t), F32)], axis=0)
    route_t_ref[...] = padded.T


def route_tokens(x2d, router_w, t=1024):
    n, dm = x2d.shape
    t = min(t, n)
    tri = (jnp.arange(t)[:, None] < jnp.arange(t)[None, :]).astype(BF16)
    return pl.pallas_call(
        functools.partial(_router_kernel, t=t),
        grid=(n // t,),
        in_specs=[pl.BlockSpec((t, dm), lambda i: (i, 0)),
                  pl.BlockSpec((N_EXPERTS, dm), lambda i: (0, 0)),
                  pl.BlockSpec((t, t), lambda i: (0, 0))],
        out_specs=[pl.BlockSpec((ROUTE_ROWS, t), lambda i: (0, i)),
                   pl.BlockSpec((t, LANES), lambda i: (i, 0)),
                   pl.BlockSpec((N_EXPERTS, LANES), lambda i: (0, 0))],
        out_shape=[jax.ShapeDtypeStruct((ROUTE_ROWS, n), F32),
                   jax.ShapeDtypeStruct((n, LANES), F32),
                   jax.ShapeDtypeStruct((N_EXPERTS, LANES), F32)],
        scratch_shapes=[pltpu.VMEM((N_EXPERTS, LANES), F32)],
        compiler_params=_cparams(("arbitrary",)),
        name="route_tokens",
    )(x2d, router_w.T, tri)


def _dest_kernel(start_ref, route_ref, dest_ref):
    r = route_ref[...]
    for k in range(2):
        e = r[k:k + 1, :].astype(jnp.int32)
        base = jnp.zeros(e.shape, jnp.int32)
        for ei in range(N_EXPERTS):
            base = jnp.where(e == ei, start_ref[ei], base)
        dest_ref[k:k + 1, :] = base + r[2 + k:3 + k, :].astype(jnp.int32)


def slot_of_assignment(pad_start, route, t=1024):
    n = route.shape[1]
    t = min(t, n)
    grid_spec = pltpu.PrefetchScalarGridSpec(
        num_scalar_prefetch=1,
        grid=(n // t,),
        in_specs=[pl.BlockSpec((ROUTE_ROWS, t), lambda i, ps: (0, i))],
        out_specs=pl.BlockSpec((2, t), lambda i, ps: (0, i)),
    )
    return pl.pallas_call(
        _dest_kernel,
        grid_spec=grid_spec,
        out_shape=jax.ShapeDtypeStruct((2, n), jnp.int32),
        compiler_params=_cparams(("parallel",)),
        name="slot_of_assignment",
    )(pad_start, route)


def _row_copy(src, src_row, dst, dst_row, sem):
    return pltpu.make_async_copy(src.at[pl.ds(src_row, 1)], dst.at[pl.ds(dst_row, 1)], sem)


ISSUE_UNROLL = 8


def _dispatch_kernel(pad_lo_ref, pad_n_ref, dest_ref, x_ref, slots_out, zrow, sem, zsem, *, t):
    @pl.when(pl.program_id(0) == 0)
    def _():
        zrow[...] = jnp.zeros(zrow.shape, zrow.dtype)
        for e in range(N_EXPERTS):
            lo = pad_lo_ref[e]

            def zero_start(r, carry):
                _row_copy(zrow, 0, slots_out, lo + r, zsem).start()
                return carry

            def zero_wait(r, carry):
                _row_copy(zrow, 0, slots_out, lo + r, zsem).wait()
                return carry

            lax.fori_loop(0, pad_n_ref[e], zero_start, 0)
            lax.fori_loop(0, pad_n_ref[e], zero_wait, 0)

        tail_lo = pad_lo_ref[N_EXPERTS]
        group = zrow.shape[0]

        def tail_copy(r):
            dst = slots_out.at[pl.ds(pl.multiple_of(tail_lo + r * group, group), group)]
            return pltpu.make_async_copy(zrow, dst, zsem)

        def tail_start(r, carry):
            tail_copy(r).start()
            return carry

        def tail_wait(r, carry):
            tail_copy(r).wait()
            return carry

        lax.fori_loop(0, pad_n_ref[N_EXPERTS], tail_start, 0)
        lax.fori_loop(0, pad_n_ref[N_EXPERTS], tail_wait, 0)

    def issue(j, carry):
        for u in range(ISSUE_UNROLL):
            i = j * ISSUE_UNROLL + u
            for k in range(2):
                _row_copy(x_ref, i, slots_out, dest_ref[k, i], sem).start()
        return carry

    lax.fori_loop(0, t // ISSUE_UNROLL, issue, 0)
    for k in range(2):
        pltpu.make_async_copy(x_ref, slots_out.at[pl.ds(0, t)], sem).wait()


def dispatch_rows(pad_lo, pad_n, dest, x2d, n_slots, t=1024):
    n, dm = x2d.shape
    t = min(t, n)
    grid_spec = pltpu.PrefetchScalarGridSpec(
        num_scalar_prefetch=2,
        grid=(n // t,),
        in_specs=[pl.BlockSpec((2, t), lambda i, lo, cnt: (0, i), memory_space=pltpu.SMEM),
                  pl.BlockSpec((t, dm), lambda i, lo, cnt: (i, 0))],
        out_specs=pl.BlockSpec(memory_space=pl.ANY),
        scratch_shapes=[pltpu.VMEM((8, dm), x2d.dtype), pltpu.SemaphoreType.DMA, pltpu.SemaphoreType.DMA],
    )
    return pl.pallas_call(
        functools.partial(_dispatch_kernel, t=t),
        grid_spec=grid_spec,
        out_shape=jax.ShapeDtypeStruct((n_slots, dm), x2d.dtype),
        compiler_params=_cparams(("arbitrary",)),
        name="dispatch_rows",
    )(pad_lo, pad_n, dest, x2d)


def _expert_kernel(be_ref, nv_ref, x_ref, wg_ref, wu_ref, wd_ref, o_ref, xb):
    blk = pl.program_id(0)
    f = pl.program_id(1)
    valid = blk < nv_ref[0]

    @pl.when(f == 0)
    def _():
        o_ref[...] = jnp.zeros(o_ref.shape, o_ref.dtype)

    @pl.when(valid & (f == 0))
    def _():
        xb[...] = x_ref[...].astype(BF16)

    @pl.when(valid)
    def _():
        hidden = (_silu(_dot(xb[...], wg_ref[0].astype(BF16)))
                  * _dot(xb[...], wu_ref[0].astype(BF16)))
        o_ref[...] += _dot(hidden.astype(BF16), wd_ref[0].astype(BF16))


def expert_swiglu(block_e, n_valid, slots, wg, wu, wd, bm, tf=512):
    n_slots, dm = slots.shape
    edim = wg.shape[2]
    grid_spec = pltpu.PrefetchScalarGridSpec(
        num_scalar_prefetch=2,
        grid=(n_slots // bm, edim // tf),
        in_specs=[pl.BlockSpec((bm, dm), lambda b, f, be, nv: (jnp.minimum(b, nv[0] - 1), 0)),
                  pl.BlockSpec((1, dm, tf), lambda b, f, be, nv: (be[b], 0, f)),
                  pl.BlockSpec((1, dm, tf), lambda b, f, be, nv: (be[b], 0, f)),
                  pl.BlockSpec((1, tf, dm), lambda b, f, be, nv: (be[b], f, 0))],
        out_specs=pl.BlockSpec((bm, dm), lambda b, f, be, nv: (b, 0)),
        scratch_shapes=[pltpu.VMEM((bm, dm), BF16)],
    )
    return pl.pallas_call(
        _expert_kernel,
        grid_spec=grid_spec,
        out_shape=jax.ShapeDtypeStruct((n_slots, dm), F32),
        compiler_params=_cparams(("parallel", "arbitrary")),
        name="expert_swiglu",
    )(block_e, n_valid, slots, wg, wu, wd)


def _combine_kernel(dest_ref, dest_next_ref, rt_ref, x_ref, y_hbm, g_ref, b_ref, o_ref, rows, sem, *, t):
    i = pl.program_id(0)
    cur = i % 2

    def gather(d_ref, buf):
        def issue(j, carry):
            for u in range(ISSUE_UNROLL):
                r = j * ISSUE_UNROLL + u
                for k in range(2):
                    _row_copy(y_hbm, d_ref[k, r], rows.at[buf, k], r, sem.at[buf]).start()
            return carry

        lax.fori_loop(0, t // ISSUE_UNROLL, issue, 0)

    @pl.when(i == 0)
    def _():
        gather(dest_ref, 0)

    @pl.when(i + 1 < pl.num_programs(0))
    def _():
        gather(dest_next_ref, 1 - cur)

    for k in range(2):
        pltpu.make_async_copy(y_hbm.at[pl.ds(0, t)], rows.at[cur, k], sem.at[cur]).wait()

    rt = rt_ref[...]
    mixed = rt[:, 4:5] * rows[cur, 0] + rt[:, 5:6] * rows[cur, 1]
    o_ref[...] = _layer_norm(ALPHA * x_ref[...] + mixed, g_ref[...], b_ref[...])


def combine_ln(dest, route_t, x2d, y, g, b, t=512):
    n, dm = x2d.shape
    t = min(t, n)
    last = n // t - 1
    return pl.pallas_call(
        functools.partial(_combine_kernel, t=t),
        grid=(n // t,),
        in_specs=[pl.BlockSpec((2, t), lambda i: (0, i), memory_space=pltpu.SMEM),
                  pl.BlockSpec((2, t), lambda i: (0, jnp.minimum(i + 1, last)), memory_space=pltpu.SMEM),
                  pl.BlockSpec((t, LANES), lambda i: (i, 0)),
                  pl.BlockSpec((t, dm), lambda i: (i, 0)),
                  pl.BlockSpec(memory_space=pl.ANY),
                  pl.BlockSpec((1, dm), lambda i: (0, 0)),
                  pl.BlockSpec((1, dm), lambda i: (0, 0))],
        out_specs=pl.BlockSpec((t, dm), lambda i: (i, 0)),
        out_shape=jax.ShapeDtypeStruct((n, dm), F32),
        scratch_shapes=[pltpu.VMEM((2, 2, t, dm), F32), pltpu.SemaphoreType.DMA((2,))],
        compiler_params=_cparams(("arbitrary",)),
        name="combine_ln",
    )(dest, dest, route_t, x2d, y, g.reshape(1, dm), b.reshape(1, dm))


def moe_ln(x2d, router_w, wg, wu, wd, g, b, bm=1024):
    n, _ = x2d.shape
    bm = min(bm, n)
    route, route_t, cnt = route_tokens(x2d, router_w)
    counts = cnt[:, 0].astype(jnp.int32)
    padded = (counts + bm - 1) // bm * bm
    pad_end = jnp.cumsum(padded)
    pad_start = (pad_end - padded).astype(jnp.int32)
    n_blocks = 2 * n // bm + N_EXPERTS
    n_slots = n_blocks * bm
    block_e = jnp.minimum(jnp.searchsorted(pad_end, jnp.arange(n_blocks) * bm, side='right'),
                          N_EXPERTS - 1).astype(jnp.int32)
    n_valid = (pad_end[-1:] // bm).astype(jnp.int32)
    dest = slot_of_assignment(pad_start, route)
    pad_lo = jnp.concatenate([pad_start + counts, pad_end[-1:]]).astype(jnp.int32)
    pad_n = jnp.concatenate([padded - counts, (n_slots - pad_end[-1:]) // 8]).astype(jnp.int32)
    slots = dispatch_rows(pad_lo, pad_n, dest, x2d, n_slots)
    y = expert_swiglu(block_e, n_valid, slots, wg, wu, wd, bm)
    return combine_ln(dest, route_t, x2d, y, g, b)


def kernel(x, rel_bias, w_in, diff_lambda, diff_subln_g, pool_w, pool_scale, ret_gn_g, w_out,
           ln1_g, ln1_b, ln2_g, ln2_b, ffn_w_gate, ffn_w_up, ffn_w_down,
           router_w, moe_w_gate, moe_w_up, moe_w_down):
    bsz, seq, dm = x.shape
    x2d = x.reshape(bsz * seq, dm)
    bias_tiles = attn_bias_tiles(rel_bias, min(ATTN_BLOCK, seq))
    col_scale = jnp.where(jnp.arange(w_in.shape[2]) < DA_WIDTH, Q_SCALE, 1.0).astype(F32)
    for l in range(DEPTH):
        lam_init = 0.8 - 0.6 * math.exp(-0.3 * l)
        proj = in_proj(x2d, (w_in[l] * col_scale).astype(BF16)).reshape(bsz, seq, -1)
        y_da = diff_attention(proj, bias_tiles, diff_lambda[l], diff_subln_g[l], lam_init)
        y_pool = pool_mixer(proj, pool_w[l], pool_scale[l])
        y_ret = retention(proj, ret_gn_g[l])
        flat = lambda a: a.reshape(bsz * seq, -1)
        mixed = (flat(y_da), flat(y_pool), flat(y_ret), x2d, w_out[l].astype(BF16), ln1_g[l], ln1_b[l])
        j = l // 2
        if l % 2 == 0:
            x2d = out_proj_ffn_ln(*mixed, ffn_w_gate[j].astype(BF16), ffn_w_up[j].astype(BF16),
                                  ffn_w_down[j].astype(BF16), ln2_g[l], ln2_b[l])
        else:
            x2d = out_proj_ln(*mixed)
            x2d = moe_ln(x2d, router_w[j], moe_w_gate[j], moe_w_up[j], moe_w_down[j], ln2_g[l], ln2_b[l])
    return x2d.reshape(bsz, seq, dm)
```

```python
import functools
import math

import jax
import jax.numpy as jnp
from jax import lax
from jax.experimental import pallas as pl
from jax.experimental.pallas import tpu as pltpu

F32 = jnp.float32
BF16 = jnp.bfloat16

DEPTH = 2
DA_HEAD_DIM = 64
DA_V_DIM = 128
DA_HEADS = 4
DA_WIDTH = 512
POOL_WIDTH = 256
POOL_WINDOWS = (2, 4, 8, 16)
POOL_GROUP_DIM = 64
POOL_HALO = 16
RET_WIDTH = 256
RET_HEAD_DIM = 64
RET_HEADS = 4
RET_CHUNK = 128
REL_BUCKETS = 32
REL_MAX_DIST = 128
N_EXPERTS = 8
ALPHA = (2 * DEPTH) ** 0.25
LN_EPS = 1e-5
NORM_EPS = 1e-6
NEG_BIG = -1e30

POOL_COL_BLOCK = 3 * DA_WIDTH // POOL_WIDTH
RET_COL_BLOCK = POOL_COL_BLOCK + 1

VMEM_LIMIT = 56 * 1024 * 1024


def _cparams(sem, vmem=VMEM_LIMIT):
    return pltpu.CompilerParams(dimension_semantics=sem, vmem_limit_bytes=vmem)


def _nt_dot(a, b):
    return lax.dot_general(a, b, (((1,), (1,)), ((), ())), preferred_element_type=F32)


def _dot(a, b):
    return jnp.dot(a, b, preferred_element_type=F32)


def _layer_norm(z, g, b):
    mu = jnp.mean(z, axis=-1, keepdims=True)
    zc = z - mu
    var = jnp.mean(zc * zc, axis=-1, keepdims=True)
    return zc * lax.rsqrt(var + LN_EPS) * g + b


def _silu(x):
    return x / (1.0 + jnp.exp(-x))


def _inproj_kernel(x_ref, w_ref, o_ref, *, tn):
    xb = x_ref[...].astype(BF16)
    for j in range(0, w_ref.shape[1], tn):
        o_ref[:, j:j + tn] = _dot(xb, w_ref[:, j:j + tn]).astype(o_ref.dtype)


def in_proj(x2d, w, tm=512, tn=256):
    n, k = x2d.shape
    m = w.shape[1]
    tm = min(tm, n)
    return pl.pallas_call(
        functools.partial(_inproj_kernel, tn=tn),
        grid=(n // tm,),
        in_specs=[pl.BlockSpec((tm, k), lambda i: (i, 0)),
                  pl.BlockSpec((k, m), lambda i: (0, 0))],
        out_specs=pl.BlockSpec((tm, m), lambda i: (i, 0)),
        out_shape=jax.ShapeDtypeStruct((n, m), BF16),
        compiler_params=_cparams(("parallel",)),
        name="in_proj",
    )(x2d, w)


def _t5_bucket(dist):
    n = jnp.maximum(dist, 0)
    max_exact = REL_BUCKETS // 2
    nf = jnp.maximum(n, 1).astype(F32)
    large = max_exact + (jnp.log(nf / max_exact) / math.log(REL_MAX_DIST / max_exact)
                         * (REL_BUCKETS - max_exact)).astype(jnp.int32)
    large = jnp.minimum(large, REL_BUCKETS - 1)
    return jnp.where(n < max_exact, n, large)


LOG2E = math.log2(math.e)
Q_SCALE = DA_HEAD_DIM ** -0.5 * LOG2E
ATTN_BLOCK = 512
ATTN_CHUNK = 256
ONES_ROWS = 16


def attn_bias_tiles(rel_bias, t):
    table = rel_bias.astype(F32).reshape(REL_BUCKETS, DA_HEADS * 2)
    vec = table[_t5_bucket(jnp.arange(2 * t))]
    far = table[_t5_bucket(jnp.array(2 * t))]
    vec = ((vec - far[None, :]) * LOG2E).T
    masked = jnp.full((DA_HEADS * 2, t), NEG_BIG, F32)
    u_diag = jnp.concatenate([vec[:, :t], masked], axis=1)
    u_prev = jnp.concatenate([vec[:, t:], vec[:, :t]], axis=1)

    def toeplitz(u):
        skew = jnp.tile(u, (1, t))[:, :t * (2 * t - 1)].reshape(-1, t, 2 * t - 1)
        return skew[:, :, :t]

    def per_head(a):
        return a.reshape(DA_HEADS, 2, t, t).transpose(0, 2, 1, 3).reshape(DA_HEADS, t, 2 * t)

    return jnp.stack([per_head(toeplitz(u_diag)), per_head(toeplitz(u_prev))], axis=1)


def _attn_kernel(q_ref, k_ref, v_ref, bias_ref, lam_ref, g_ref, o_ref, vt, m_s, acc, s_a, s_b, *, t, lam_init):
    nk = vt.shape[0]
    ones = jnp.ones((ONES_ROWS, t), BF16)
    for ki in range(nk):
        v_t = v_ref[0, ki * t:(ki + 1) * t, :].astype(F32).T.astype(BF16)
        vt[ki] = jnp.concatenate([v_t, ones], axis=0)

    def query_tile(qi, carry):
        _attn_query_tile(qi, q_ref, k_ref, bias_ref, lam_ref, g_ref, o_ref, vt, m_s, acc, s_a, s_b,
                         t=t, lam_init=lam_init)
        return carry

    lax.fori_loop(0, nk // 2, query_tile, 0)


def _attn_query_tile(qi, q_ref, k_ref, bias_ref, lam_ref, g_ref, o_ref, vt, m_s, acc, s_a, s_b, *, t, lam_init):
    tq = 2 * t
    q_rows = pl.ds(pl.multiple_of(qi * tq, tq), tq)
    q = q_ref[0, q_rows, :]
    lane = lax.broadcasted_iota(jnp.int32, q.shape, 1)
    zero = jnp.zeros_like(q)
    qcat = jnp.concatenate([jnp.where(lane < DA_HEAD_DIM, q, zero),
                            jnp.where(lane >= DA_HEAD_DIM, q, zero)], axis=0)
    m_s[...] = jnp.full(m_s.shape, NEG_BIG, F32)
    acc[...] = jnp.zeros(acc.shape, F32)

    chunk = min(ATTN_CHUNK, t)
    DIAG, PREV, FAR, SKIP = 0, 1, None, "skip"

    def step(nxt, cur):
        kb = None
        if nxt is not None:
            kb = k_ref[0, pl.ds(pl.multiple_of(nxt[0] * t, t), t), :]
        for c in range(0, 2 * tq, chunk):
            cols = slice(c, c + chunk)
            mp, qoff = divmod(c, tq)
            half, ioff = divmod(qoff, t)
            if nxt is not None and not (len(nxt) > 2 and nxt[2][half] == SKIP):
                nxt[1][:, cols] = _nt_dot(kb, qcat[cols, :])
            if cur is not None:
                ki, s_ref, kinds = cur
                kind = kinds[half]
                if kind == SKIP:
                    continue
                s = s_ref[:, cols]
                if kind is not FAR:
                    s = s + bias_ref[0, kind, :, mp * t + ioff:mp * t + ioff + chunk]
                m_prev = m_s[:, cols]
                m_new = jnp.maximum(m_prev, jnp.max(s, axis=0, keepdims=True))
                alpha = jnp.exp2(m_prev - m_new)
                p = jnp.exp2(s - m_new).astype(BF16)
                acc[:, cols] = alpha * acc[:, cols] + _dot(vt[ki], p)
                m_s[:, cols] = m_new

    upper = (SKIP, DIAG)
    lower = (DIAG, PREV)
    before = (PREV, FAR)
    plain = (FAR, FAR)
    step((2 * qi + 1, s_a, upper), None)
    step((2 * qi, s_b), (2 * qi + 1, s_a, upper))

    @pl.when(qi == 0)
    def _():
        step(None, (0, s_b, lower))

    @pl.when(qi >= 1)
    def _():
        n_far = 2 * qi - 1
        step((2 * qi - 1, s_a), (2 * qi, s_b, lower))
        step((0, s_b), (2 * qi - 1, s_a, before))

        def pair(jj, carry):
            j = 2 * jj
            step((j + 1, s_a), (j, s_b, plain))
            step((j + 2, s_b), (j + 1, s_a, plain))
            return carry

        lax.fori_loop(0, n_far // 2, pair, 0)
        step(None, (n_far - 1, s_b, plain))

    lm = lam_ref[...]
    lam = (jnp.exp(jnp.sum(lm[0:1] * lm[1:2], keepdims=True))
           - jnp.exp(jnp.sum(lm[2:3] * lm[3:4], keepdims=True)) + lam_init)
    a = acc[...]
    o = a[:DA_V_DIM, :] / a[DA_V_DIM:DA_V_DIM + 1, :]
    out = o[:, :tq] - lam * o[:, tq:]
    out = out * lax.rsqrt(jnp.mean(out * out, axis=0, keepdims=True) + NORM_EPS) * g_ref[...]
    o_ref[0, q_rows, :] = (out * (1.0 - lam_init)).T.astype(o_ref.dtype)


def diff_attention(proj3, bias_tiles, lam_params, subln_g, lam_init):
    b, s, _ = proj3.shape
    t = bias_tiles.shape[2]
    tq = 2 * t
    seq = lambda col0: pl.BlockSpec((1, s, DA_V_DIM), lambda bi, h: (bi, 0, col0 + h))
    return pl.pallas_call(
        functools.partial(_attn_kernel, t=t, lam_init=lam_init),
        grid=(b, DA_HEADS),
        in_specs=[
            seq(0), seq(DA_HEADS), seq(2 * DA_HEADS),
            pl.BlockSpec((1, 2, t, 2 * t), lambda bi, h: (h, 0, 0, 0)),
            pl.BlockSpec((4, DA_HEAD_DIM), lambda bi, h: (0, 0)),
            pl.BlockSpec((DA_V_DIM, 1), lambda bi, h: (0, 0)),
        ],
        out_specs=pl.BlockSpec((1, s, DA_V_DIM), lambda bi, h: (bi, 0, h)),
        out_shape=jax.ShapeDtypeStruct((b, s, DA_WIDTH), BF16),
        scratch_shapes=[pltpu.VMEM((s // t, DA_V_DIM + ONES_ROWS, t), BF16),
                        pltpu.VMEM((1, 2 * tq), F32),
                        pltpu.VMEM((DA_V_DIM + ONES_ROWS, 2 * tq), F32),
                        pltpu.VMEM((t, 2 * tq), F32), pltpu.VMEM((t, 2 * tq), F32)],
        compiler_params=_cparams(("parallel", "parallel")),
        name="diff_attention",
    )(proj3, proj3, proj3, bias_tiles, lam_params, subln_g.reshape(DA_V_DIM, 1))


def _pool_kernel(p_ref, w_ref, scale_ref, o_ref, halo, *, t):
    si = pl.program_id(1)

    @pl.when(si == 0)
    def _():
        halo[...] = jnp.zeros(halo.shape, F32)

    p = p_ref[0].astype(F32)
    ext = jnp.concatenate([halo[...], p], axis=0)
    halo[...] = p[t - POOL_HALO:, :]
    sums = {1: ext}
    w = 1
    while w < POOL_WINDOWS[-1]:
        sums[2 * w] = sums[w] + pltpu.roll(sums[w], w, 0)
        w *= 2
    lane = lax.broadcasted_iota(jnp.int32, (t, POOL_WIDTH), 1)
    pos = (si * t + lax.broadcasted_iota(jnp.int32, (t, POOL_WIDTH), 0) + 1).astype(F32)
    wsum = sums[POOL_WINDOWS[-1]][POOL_HALO:, :]
    cnt = jnp.minimum(pos, float(POOL_WINDOWS[-1]))
    for gi in range(len(POOL_WINDOWS) - 2, -1, -1):
        in_group = lane < (gi + 1) * POOL_GROUP_DIM
        wsum = jnp.where(in_group, sums[POOL_WINDOWS[gi]][POOL_HALO:, :], wsum)
        cnt = jnp.where(in_group, jnp.minimum(pos, float(POOL_WINDOWS[gi])), cnt)
    pooled = wsum / cnt - p
    mixed = _dot(pooled.astype(BF16), w_ref[...])
    o_ref[0] = (mixed * scale_ref[...]).astype(o_ref.dtype)


def pool_mixer(proj3, pool_w, pool_scale, t=512):
    b, s, _ = proj3.shape
    t = min(t, s)
    g = len(POOL_WINDOWS)
    wbd = (jnp.eye(g, dtype=F32)[:, None, :, None] * pool_w.astype(F32)[:, :, None, :]).reshape(
        POOL_WIDTH, POOL_WIDTH).astype(BF16)
    return pl.pallas_call(
        functools.partial(_pool_kernel, t=t),
        grid=(b, s // t),
        in_specs=[pl.BlockSpec((1, t, POOL_WIDTH), lambda bi, si: (bi, si, POOL_COL_BLOCK)),
                  pl.BlockSpec((POOL_WIDTH, POOL_WIDTH), lambda bi, si: (0, 0)),
                  pl.BlockSpec((1, POOL_WIDTH), lambda bi, si: (0, 0))],
        out_specs=pl.BlockSpec((1, t, POOL_WIDTH), lambda bi, si: (bi, si, 0)),
        out_shape=jax.ShapeDtypeStruct((b, s, POOL_WIDTH), BF16),
        scratch_shapes=[pltpu.VMEM((POOL_HALO, POOL_WIDTH), F32)],
        compiler_params=_cparams(("parallel", "arbitrary")),
        name="pool_mixer",
    )(proj3, wbd, pool_scale.reshape(1, POOL_WIDTH))


def _retention_tables(s, t):
    d, hn, c = RET_HEAD_DIM, RET_HEADS, RET_CHUNK
    half = d // 2
    inv = 10000.0 ** (-jnp.linspace(0.0, 1.0, half, dtype=F32))
    ang = jnp.arange(s)[:, None].astype(F32) * inv[None, :]
    cos, sin = jnp.cos(ang), jnp.sin(ang)
    cos_t = jnp.tile(jnp.concatenate([cos, cos], axis=-1), (1, hn))
    sin_t = jnp.tile(jnp.concatenate([-sin, sin], axis=-1), (1, hn))
    log_gamma = jnp.log(1.0 - 2.0 ** (-5.0 - jnp.arange(hn, dtype=F32)))
    idx = jnp.arange(c, dtype=F32)
    rel = idx[:, None] - idx[None, :]
    intra = jnp.where(rel >= 0, jnp.exp(log_gamma[:, None, None] * jnp.maximum(rel, 0.0)), 0.0)
    q_decay = jnp.exp(log_gamma[:, None] * (idx + 1.0))
    k_decay = jnp.exp(log_gamma[:, None] * (c - 1.0 - idx))
    chunk_decay = jnp.exp(log_gamma * c)
    lanes = lambda a: jnp.tile(jnp.repeat(a.T, d, axis=1), (t // c, 1))
    l = jnp.arange(hn * d)
    partner = jnp.where(l % d < half, l + half, l - half)
    swap = (l[:, None] == partner[None, :]).astype(BF16)
    same = (jnp.arange(2 * d)[:, None] // d) == (jnp.arange(2 * d)[None, :] // d)
    decay_bd = jnp.where(same[None], jnp.repeat(chunk_decay, d).reshape(hn // 2, 2 * d, 1), 0.0)
    avg = jnp.where(same, 1.0 / d, 0.0).astype(BF16)
    return cos_t, sin_t, intra, lanes(q_decay), lanes(k_decay), swap, decay_bd.astype(F32), avg


def _ret_kernel(q_ref, k_ref, v_ref, g_ref, cos_ref, sin_ref, intra_ref, qd_ref, kd_ref, swap_ref, decay_ref,
                avg_ref, gn_ref, o_ref, state, *, t):
    si = pl.program_id(1)
    d, c = RET_HEAD_DIM, RET_CHUNK
    pw = 2 * d

    @pl.when(si == 0)
    def _():
        state[...] = jnp.zeros(state.shape, F32)

    def rotate(x_ref):
        x = x_ref[0]
        return x.astype(F32) * cos_ref[...] + _dot(x, swap_ref[...]) * sin_ref[...]

    def lane_mean(a):
        hi = a.astype(BF16)
        lo = (a - hi.astype(F32)).astype(BF16)
        return _dot(hi, avg_ref[...]) + _dot(lo, avg_ref[...])

    q = rotate(q_ref)
    k = rotate(k_ref) * (d ** -0.5)
    qs = (q * qd_ref[...]).astype(BF16)
    ks = (k * kd_ref[...]).astype(BF16)
    qb = q.astype(BF16)
    kb = k.astype(BF16)
    v = v_ref[0]
    first = lax.broadcasted_iota(jnp.int32, (c, pw), 1) < d
    same = ((lax.broadcasted_iota(jnp.int32, (pw, pw), 0) < d)
            == (lax.broadcasted_iota(jnp.int32, (pw, pw), 1) < d))
    zero = jnp.zeros((c, pw), BF16)

    chunks = [slice(ci * c, (ci + 1) * c) for ci in range(t // c)]
    pairs = [slice(p * pw, (p + 1) * pw) for p in range(RET_HEADS // 2)]
    scores = {}
    for p, cols in enumerate(pairs):
        for ci, rows in enumerate(chunks):
            qp, kp = qb[rows, cols], kb[rows, cols]
            scores[p, ci] = ((_nt_dot(jnp.where(first, qp, zero), kp) * intra_ref[2 * p]).astype(BF16),
                             (_nt_dot(jnp.where(first, zero, qp), kp) * intra_ref[2 * p + 1]).astype(BF16))
    intra_out, increment = {}, {}
    for p, cols in enumerate(pairs):
        for ci, rows in enumerate(chunks):
            vp = v[rows, cols]
            intra_out[p, ci] = jnp.where(first, _dot(scores[p, ci][0], vp), _dot(scores[p, ci][1], vp))
            increment[p, ci] = jnp.where(same, _dot(ks[rows, cols].T, vp), 0.0)
    ys = {}
    for p, cols in enumerate(pairs):
        st = state[p]
        for ci, rows in enumerate(chunks):
            ys[p, ci] = intra_out[p, ci] + _dot(qs[rows, cols], st.astype(BF16))
            st = st * decay_ref[p] + increment[p, ci]
        state[p] = st
    y = jnp.concatenate([jnp.concatenate([ys[p, ci] for ci in range(len(chunks))], axis=0)
                         for p in range(len(pairs))], axis=1)
    mean = jnp.concatenate([lane_mean(y[:, cols]) for cols in pairs], axis=1)
    yc = y - mean
    var = jnp.concatenate([lane_mean((yc * yc)[:, cols]) for cols in pairs], axis=1)
    yn = yc * lax.rsqrt(var + NORM_EPS)
    gate = _silu(g_ref[0].astype(F32))
    o_ref[0] = (gate * (yn * gn_ref[...])).astype(o_ref.dtype)


def retention(proj3, gn_g, t=512):
    b, s, _ = proj3.shape
    t = min(t, s)
    cos_t, sin_t, intra, qd, kd, swap, decay_bd, avg = _retention_tables(s, t)
    pw = 2 * RET_HEAD_DIM
    col = lambda j: pl.BlockSpec((1, t, RET_WIDTH), lambda bi, si: (bi, si, RET_COL_BLOCK + j))
    const2 = lambda shape: pl.BlockSpec(shape, lambda bi, si: (0, 0))
    const3 = lambda shape: pl.BlockSpec(shape, lambda bi, si: (0, 0, 0))
    return pl.pallas_call(
        functools.partial(_ret_kernel, t=t),
        grid=(b, s // t),
        in_specs=[col(0), col(1), col(2), col(3),
                  pl.BlockSpec((t, RET_WIDTH), lambda bi, si: (si, 0)),
                  pl.BlockSpec((t, RET_WIDTH), lambda bi, si: (si, 0)),
                  const3((RET_HEADS, RET_CHUNK, RET_CHUNK)),
                  const2((t, RET_WIDTH)), const2((t, RET_WIDTH)), const2((RET_WIDTH, RET_WIDTH)),
                  const3((RET_HEADS // 2, pw, pw)), const2((pw, pw)), const2((1, RET_WIDTH))],
        out_specs=pl.BlockSpec((1, t, RET_WIDTH), lambda bi, si: (bi, si, 0)),
        out_shape=jax.ShapeDtypeStruct((b, s, RET_WIDTH), BF16),
        scratch_shapes=[pltpu.VMEM((RET_HEADS // 2, pw, pw), F32)],
        compiler_params=_cparams(("parallel", "arbitrary")),
        name="retention",
    )(proj3, proj3, proj3, proj3, cos_t, sin_t, intra, qd, kd, swap, decay_bd, avg, gn_g.reshape(1, RET_WIDTH))


def _mix_ln(da_ref, pool_ref, ret_ref, x_ref, w_ref, g_ref, b_ref):
    e0, e1 = DA_WIDTH, DA_WIDTH + POOL_WIDTH
    mix = (_dot(da_ref[...], w_ref[0:e0, :]) + _dot(pool_ref[...], w_ref[e0:e1, :])
           + _dot(ret_ref[...], w_ref[e1:, :]))
    return _layer_norm(ALPHA * x_ref[...] + mix, g_ref[...], b_ref[...])


def _outproj_kernel(da_ref, pool_ref, ret_ref, x_ref, w_ref, g_ref, b_ref, o_ref):
    o_ref[...] = _mix_ln(da_ref, pool_ref, ret_ref, x_ref, w_ref, g_ref, b_ref)


def out_proj_ln(y_da, y_pool, y_ret, x2d, w, g, b, tm=512):
    n, dm = x2d.shape
    tm = min(tm, n)
    row = lambda width: pl.BlockSpec((tm, width), lambda i: (i, 0))
    const = lambda shape: pl.BlockSpec(shape, lambda i: (0, 0))
    return pl.pallas_call(
        _outproj_kernel,
        grid=(n // tm,),
        in_specs=[row(DA_WIDTH), row(POOL_WIDTH), row(RET_WIDTH), row(dm),
                  const(w.shape), const((1, dm)), const((1, dm))],
        out_specs=row(dm),
        out_shape=jax.ShapeDtypeStruct((n, dm), F32),
        compiler_params=_cparams(("parallel",)),
        name="out_proj_ln",
    )(y_da, y_pool, y_ret, x2d, w, g.reshape(1, dm), b.reshape(1, dm))


def _mix_ffn_kernel(da_ref, pool_ref, ret_ref, x_ref, wo_ref, g1_ref, b1_ref, wg_ref, wu_ref, wd_ref,
                    g2_ref, b2_ref, o_ref, *, tf):
    x = _mix_ln(da_ref, pool_ref, ret_ref, x_ref, wo_ref, g1_ref, b1_ref)
    xb = x.astype(BF16)
    acc = jnp.zeros(x.shape, F32)
    for f0 in range(0, wg_ref.shape[1], tf):
        hidden = _silu(_dot(xb, wg_ref[:, f0:f0 + tf])) * _dot(xb, wu_ref[:, f0:f0 + tf])
        acc = acc + _dot(hidden.astype(BF16), wd_ref[f0:f0 + tf, :])
    o_ref[...] = _layer_norm(ALPHA * x + acc, g2_ref[...], b2_ref[...])


def out_proj_ffn_ln(y_da, y_pool, y_ret, x2d, wo, g1, b1, wg, wu, wd, g2, b2, tm=512, tf=256):
    n, dm = x2d.shape
    tm = min(tm, n)
    row = lambda width: pl.BlockSpec((tm, width), lambda i: (i, 0))
    resident = lambda shape: pl.BlockSpec(shape, lambda i: (0, 0), pipeline_mode=pl.Buffered(1))
    vec = lambda a: a.reshape(1, dm)
    return pl.pallas_call(
        functools.partial(_mix_ffn_kernel, tf=tf),
        grid=(n // tm,),
        in_specs=[row(DA_WIDTH), row(POOL_WIDTH), row(RET_WIDTH), row(dm),
                  resident(wo.shape), resident((1, dm)), resident((1, dm)),
                  resident(wg.shape), resident(wu.shape), resident(wd.shape),
                  resident((1, dm)), resident((1, dm))],
        out_specs=row(dm),
        out_shape=jax.ShapeDtypeStruct((n, dm), F32),
        compiler_params=_cparams(("parallel",)),
        name="out_proj_ffn_ln",
    )(y_da, y_pool, y_ret, x2d, wo, vec(g1), vec(b1), wg, wu, wd, vec(g2), vec(b2))


ROUTE_ROWS = 8
LANES = 128


def _router_kernel(x_ref, w_ref, tri_ref, route_ref, route_t_ref, cnt_ref, carry, *, t):
    i = pl.program_id(0)

    @pl.when(i == 0)
    def _():
        carry[...] = jnp.zeros(carry.shape, F32)

    x = x_ref[...]
    xh = x.astype(BF16)
    xl = (x - xh.astype(F32)).astype(BF16)
    w = w_ref[...]
    wh = w.astype(BF16)
    wl = (w - wh.astype(F32)).astype(BF16)
    logits = _nt_dot(wh, xh) + _nt_dot(wl, xh) + _nt_dot(wh, xl)

    row = lax.broadcasted_iota(jnp.int32, logits.shape, 0)
    v0 = jnp.max(logits, axis=0, keepdims=True)
    i0 = jnp.min(jnp.where(logits == v0, row, N_EXPERTS), axis=0, keepdims=True)
    rest = jnp.where(row == i0, -jnp.inf, logits)
    v1 = jnp.max(rest, axis=0, keepdims=True)
    i1 = jnp.min(jnp.where(rest == v1, row, N_EXPERTS), axis=0, keepdims=True)
    ex = jnp.exp(v1 - v0)
    gate0 = 1.0 / (1.0 + ex)
    gate1 = ex / (1.0 + ex)

    oh0 = row == i0
    oh1 = row == i1
    member = jnp.where(oh0 | oh1, 1.0, 0.0)
    before = _dot(member.astype(BF16), tri_ref[...]) + carry[:, 0:1]
    rank0 = jnp.sum(jnp.where(oh0, before, 0.0), axis=0, keepdims=True)
    rank1 = jnp.sum(jnp.where(oh1, before, 0.0), axis=0, keepdims=True)
    carry[...] = carry[...] + jnp.sum(member, axis=1, keepdims=True)
    cnt_ref[...] = carry[...]

    route = jnp.concatenate([i0.astype(F32), i1.astype(F32), rank0, rank1, gate0, gate1,
                             jnp.zeros((2, t), F32)], axis=0)
    route_ref[...] = route
    padded = jnp.concatenate([route, jnp.zeros((LANES - ROUTE_ROWS, t), F32)], axis=0)
    route_t_ref[...] = padded.T


def route_tokens(x2d, router_w, t=1024):
    n, dm = x2d.shape
    t = min(t, n)
    tri = (jnp.arange(t)[:, None] < jnp.arange(t)[None, :]).astype(BF16)
    return pl.pallas_call(
        functools.partial(_router_kernel, t=t),
        grid=(n // t,),
        in_specs=[pl.BlockSpec((t, dm), lambda i: (i, 0)),
                  pl.BlockSpec((N_EXPERTS, dm), lambda i: (0, 0)),
                  pl.BlockSpec((t, t), lambda i: (0, 0))],
        out_specs=[pl.BlockSpec((ROUTE_ROWS, t), lambda i: (0, i)),
                   pl.BlockSpec((t, LANES), lambda i: (i, 0)),
                   pl.BlockSpec((N_EXPERTS, LANES), lambda i: (0, 0))],
        out_shape=[jax.ShapeDtypeStruct((ROUTE_ROWS, n), F32),
                   jax.ShapeDtypeStruct((n, LANES), F32),
                   jax.ShapeDtypeStruct((N_EXPERTS, LANES), F32)],
        scratch_shapes=[pltpu.VMEM((N_EXPERTS, LANES), F32)],
        compiler_params=_cparams(("arbitrary",)),
        name="route_tokens",
    )(x2d, router_w.T, tri)


def _dest_kernel(start_ref, route_ref, dest_ref):
    r = route_ref[...]
    for k in range(2):
        e = r[k:k + 1, :].astype(jnp.int32)
        base = jnp.zeros(e.shape, jnp.int32)
        for ei in range(N_EXPERTS):
            base = jnp.where(e == ei, start_ref[ei], base)
        dest_ref[k:k + 1, :] = base + r[2 + k:3 + k, :].astype(jnp.int32)


def slot_of_assignment(pad_start, route, t=1024):
    n = route.shape[1]
    t = min(t, n)
    grid_spec = pltpu.PrefetchScalarGridSpec(
        num_scalar_prefetch=1,
        grid=(n // t,),
        in_specs=[pl.BlockSpec((ROUTE_ROWS, t), lambda i, ps: (0, i))],
        out_specs=pl.BlockSpec((2, t), lambda i, ps: (0, i)),
    )
    return pl.pallas_call(
        _dest_kernel,
        grid_spec=grid_spec,
        out_shape=jax.ShapeDtypeStruct((2, n), jnp.int32),
        compiler_params=_cparams(("parallel",)),
        name="slot_of_assignment",
    )(pad_start, route)


def _row_copy(src, src_row, dst, dst_row, sem):
    return pltpu.make_async_copy(src.at[pl.ds(src_row, 1)], dst.at[pl.ds(dst_row, 1)], sem)


ISSUE_UNROLL = 8


def _dispatch_kernel(pad_lo_ref, pad_n_ref, dest_ref, x_ref, slots_out, zrow, sem, zsem, *, t):
    @pl.when(pl.program_id(0) == 0)
    def _():
        zrow[...] = jnp.zeros(zrow.shape, zrow.dtype)
        for e in range(N_EXPERTS):
            lo = pad_lo_ref[e]

            def zero_start(r, carry):
                _row_copy(zrow, 0, slots_out, lo + r, zsem).start()
                return carry

            def zero_wait(r, carry):
                _row_copy(zrow, 0, slots_out, lo + r, zsem).wait()
                return carry

            lax.fori_loop(0, pad_n_ref[e], zero_start, 0)
            lax.fori_loop(0, pad_n_ref[e], zero_wait, 0)

        tail_lo = pad_lo_ref[N_EXPERTS]
        group = zrow.shape[0]

        def tail_copy(r):
            dst = slots_out.at[pl.ds(pl.multiple_of(tail_lo + r * group, group), group)]
            return pltpu.make_async_copy(zrow, dst, zsem)

        def tail_start(r, carry):
            tail_copy(r).start()
            return carry

        def tail_wait(r, carry):
            tail_copy(r).wait()
            return carry

        lax.fori_loop(0, pad_n_ref[N_EXPERTS], tail_start, 0)
        lax.fori_loop(0, pad_n_ref[N_EXPERTS], tail_wait, 0)

    def issue(j, carry):
        for u in range(ISSUE_UNROLL):
            i = j * ISSUE_UNROLL + u
            for k in range(2):
                _row_copy(x_ref, i, slots_out, dest_ref[k, i], sem).start()
        return carry

    lax.fori_loop(0, t // ISSUE_UNROLL, issue, 0)
    for k in range(2):
        pltpu.make_async_copy(x_ref, slots_out.at[pl.ds(0, t)], sem).wait()


def dispatch_rows(pad_lo, pad_n, dest, x2d, n_slots, t=1024):
    n, dm = x2d.shape
    t = min(t, n)
    grid_spec = pltpu.PrefetchScalarGridSpec(
        num_scalar_prefetch=2,
        grid=(n // t,),
        in_specs=[pl.BlockSpec((2, t), lambda i, lo, cnt: (0, i), memory_space=pltpu.SMEM),
                  pl.BlockSpec((t, dm), lambda i, lo, cnt: (i, 0))],
        out_specs=pl.BlockSpec(memory_space=pl.ANY),
        scratch_shapes=[pltpu.VMEM((8, dm), x2d.dtype), pltpu.SemaphoreType.DMA, pltpu.SemaphoreType.DMA],
    )
    return pl.pallas_call(
        functools.partial(_dispatch_kernel, t=t),
        grid_spec=grid_spec,
        out_shape=jax.ShapeDtypeStruct((n_slots, dm), x2d.dtype),
        compiler_params=_cparams(("arbitrary",)),
        name="dispatch_rows",
    )(pad_lo, pad_n, dest, x2d)


def _expert_kernel(be_ref, nv_ref, x_ref, wg_ref, wu_ref, wd_ref, o_ref, xb):
    blk = pl.program_id(0)
    f = pl.program_id(1)
    valid = blk < nv_ref[0]

    @pl.when(f == 0)
    def _():
        o_ref[...] = jnp.zeros(o_ref.shape, o_ref.dtype)

    @pl.when(valid & (f == 0))
    def _():
        xb[...] = x_ref[...].astype(BF16)

    @pl.when(valid)
    def _():
        hidden = (_silu(_dot(xb[...], wg_ref[0].astype(BF16)))
                  * _dot(xb[...], wu_ref[0].astype(BF16)))
        o_ref[...] += _dot(hidden.astype(BF16), wd_ref[0].astype(BF16))


def expert_swiglu(block_e, n_valid, slots, wg, wu, wd, bm, tf=512):
    n_slots, dm = slots.shape
    edim = wg.shape[2]
    grid_spec = pltpu.PrefetchScalarGridSpec(
        num_scalar_prefetch=2,
        grid=(n_slots // bm, edim // tf),
        in_specs=[pl.BlockSpec((bm, dm), lambda b, f, be, nv: (jnp.minimum(b, nv[0] - 1), 0)),
                  pl.BlockSpec((1, dm, tf), lambda b, f, be, nv: (be[b], 0, f)),
                  pl.BlockSpec((1, dm, tf), lambda b, f, be, nv: (be[b], 0, f)),
                  pl.BlockSpec((1, tf, dm), lambda b, f, be, nv: (be[b], f, 0))],
        out_specs=pl.BlockSpec((bm, dm), lambda b, f, be, nv: (b, 0)),
        scratch_shapes=[pltpu.VMEM((bm, dm), BF16)],
    )
    return pl.pallas_call(
        _expert_kernel,
        grid_spec=grid_spec,
        out_shape=jax.ShapeDtypeStruct((n_slots, dm), F32),
        compiler_params=_cparams(("parallel", "arbitrary")),
        name="expert_swiglu",
    )(block_e, n_valid, slots, wg, wu, wd)


def _combine_kernel(dest_ref, dest_next_ref, rt_ref, x_ref, y_hbm, g_ref, b_ref, o_ref, rows, sem, *, t):
    i = pl.program_id(0)
    cur = i % 2

    def gather(d_ref, buf):
        def issue(j, carry):
            for u in range(ISSUE_UNROLL):
                r = j * ISSUE_UNROLL + u
                for k in range(2):
                    _row_copy(y_hbm, d_ref[k, r], rows.at[buf, k], r, sem.at[buf]).start()
            return carry

        lax.fori_loop(0, t // ISSUE_UNROLL, issue, 0)

    @pl.when(i == 0)
    def _():
        gather(dest_ref, 0)

    @pl.when(i + 1 < pl.num_programs(0))
    def _():
        gather(dest_next_ref, 1 - cur)

    for k in range(2):
        pltpu.make_async_copy(y_hbm.at[pl.ds(0, t)], rows.at[cur, k], sem.at[cur]).wait()

    rt = rt_ref[...]
    mixed = rt[:, 4:5] * rows[cur, 0] + rt[:, 5:6] * rows[cur, 1]
    o_ref[...] = _layer_norm(ALPHA * x_ref[...] + mixed, g_ref[...], b_ref[...])


def combine_ln(dest, route_t, x2d, y, g, b, t=512):
    n, dm = x2d.shape
    t = min(t, n)
    last = n // t - 1
    return pl.pallas_call(
        functools.partial(_combine_kernel, t=t),
        grid=(n // t,),
        in_specs=[pl.BlockSpec((2, t), lambda i: (0, i), memory_space=pltpu.SMEM),
                  pl.BlockSpec((2, t), lambda i: (0, jnp.minimum(i + 1, last)), memory_space=pltpu.SMEM),
                  pl.BlockSpec((t, LANES), lambda i: (i, 0)),
                  pl.BlockSpec((t, dm), lambda i: (i, 0)),
                  pl.BlockSpec(memory_space=pl.ANY),
                  pl.BlockSpec((1, dm), lambda i: (0, 0)),
                  pl.BlockSpec((1, dm), lambda i: (0, 0))],
        out_specs=pl.BlockSpec((t, dm), lambda i: (i, 0)),
        out_shape=jax.ShapeDtypeStruct((n, dm), F32),
        scratch_shapes=[pltpu.VMEM((2, 2, t, dm), F32), pltpu.SemaphoreType.DMA((2,))],
        compiler_params=_cparams(("arbitrary",)),
        name="combine_ln",
    )(dest, dest, route_t, x2d, y, g.reshape(1, dm), b.reshape(1, dm))


def moe_ln(x2d, router_w, wg, wu, wd, g, b, bm=1024):
    n, _ = x2d.shape
    bm = min(bm, n)
    route, route_t, cnt = route_tokens(x2d, router_w)
    counts = cnt[:, 0].astype(jnp.int32)
    padded = (counts + bm - 1) // bm * bm
    pad_end = jnp.cumsum(padded)
    pad_start = (pad_end - padded).astype(jnp.int32)
    n_blocks = 2 * n // bm + N_EXPERTS
    n_slots = n_blocks * bm
    block_e = jnp.minimum(jnp.searchsorted(pad_end, jnp.arange(n_blocks) * bm, side='right'),
                          N_EXPERTS - 1).astype(jnp.int32)
    n_valid = (pad_end[-1:] // bm).astype(jnp.int32)
    dest = slot_of_assignment(pad_start, route)
    pad_lo = jnp.concatenate([pad_start + counts, pad_end[-1:]]).astype(jnp.int32)
    pad_n = jnp.concatenate([padded - counts, (n_slots - pad_end[-1:]) // 8]).astype(jnp.int32)
    slots = dispatch_rows(pad_lo, pad_n, dest, x2d, n_slots)
    y = expert_swiglu(block_e, n_valid, slots, wg, wu, wd, bm)
    return combine_ln(dest, route_t, x2d, y, g, b)


def kernel(x, rel_bias, w_in, diff_lambda, diff_subln_g, pool_w, pool_scale, ret_gn_g, w_out,
           ln1_g, ln1_b, ln2_g, ln2_b, ffn_w_gate, ffn_w_up, ffn_w_down,
           router_w, moe_w_gate, moe_w_up, moe_w_down):
    bsz, seq, dm = x.shape
    x2d = x.reshape(bsz * seq, dm)
    bias_tiles = attn_bias_tiles(rel_bias, min(ATTN_BLOCK, seq))
    col_scale = jnp.where(jnp.arange(w_in.shape[2]) < DA_WIDTH, Q_SCALE, 1.0).astype(F32)
    for l in range(DEPTH):
        lam_init = 0.8 - 0.6 * math.exp(-0.3 * l)
        proj = in_proj(x2d, (w_in[l] * col_scale).astype(BF16)).reshape(bsz, seq, -1)
        y_da = diff_attention(proj, bias_tiles, diff_lambda[l], diff_subln_g[l], lam_init)
        y_pool = pool_mixer(proj, pool_w[l], pool_scale[l])
        y_ret = retention(proj, ret_gn_g[l])
        flat = lambda a: a.reshape(bsz * seq, -1)
        mixed = (flat(y_da), flat(y_pool), flat(y_ret), x2d, w_out[l].astype(BF16), ln1_g[l], ln1_b[l])
        j = l // 2
        if l % 2 == 0:
            x2d = out_proj_ffn_ln(*mixed, ffn_w_gate[j].astype(BF16), ffn_w_up[j].astype(BF16),
                                  ffn_w_down[j].astype(BF16), ln2_g[l], ln2_b[l])
        else:
            x2d = out_proj_ln(*mixed)
            x2d = moe_ln(x2d, router_w[j], moe_w_gate[j], moe_w_up[j], moe_w_down[j], ln2_g[l], ln2_b[l])
    return x2d.reshape(bsz, seq, dm)
```

```python
import functools
import math

import jax
import jax.numpy as jnp
from jax import lax
from jax.experimental import pallas as pl
from jax.experimental.pallas import tpu as pltpu

F32 = jnp.float32
BF16 = jnp.bfloat16

DEPTH = 2
DA_HEAD_DIM = 64
DA_V_DIM = 128
DA_HEADS = 4
DA_WIDTH = 512
POOL_WIDTH = 256
POOL_WINDOWS = (2, 4, 8, 16)
POOL_GROUP_DIM = 64
POOL_HALO = 16
RET_WIDTH = 256
RET_HEAD_DIM = 64
RET_HEADS = 4
RET_CHUNK = 128
REL_BUCKETS = 32
REL_MAX_DIST = 128
N_EXPERTS = 8
ALPHA = (2 * DEPTH) ** 0.25
LN_EPS = 1e-5
NORM_EPS = 1e-6
NEG_BIG = -1e30

POOL_COL_BLOCK = 3 * DA_WIDTH // POOL_WIDTH
RET_COL_BLOCK = POOL_COL_BLOCK + 1

VMEM_LIMIT = 56 * 1024 * 1024


def _cparams(sem, vmem=VMEM_LIMIT):
    return pltpu.CompilerParams(dimension_semantics=sem, vmem_limit_bytes=vmem)


def _nt_dot(a, b):
    return lax.dot_general(a, b, (((1,), (1,)), ((), ())), preferred_element_type=F32)


def _dot(a, b):
    return jnp.dot(a, b, preferred_element_type=F32)


def _layer_norm(z, g, b):
    mu = jnp.mean(z, axis=-1, keepdims=True)
    zc = z - mu
    var = jnp.mean(zc * zc, axis=-1, keepdims=True)
    return zc * lax.rsqrt(var + LN_EPS) * g + b


def _silu(x):
    return x / (1.0 + jnp.exp(-x))


def _inproj_kernel(x_ref, w_ref, o_ref, *, tn):
    xb = x_ref[...].astype(BF16)
    for j in range(0, w_ref.shape[1], tn):
        o_ref[:, j:j + tn] = _dot(xb, w_ref[:, j:j + tn]).astype(o_ref.dtype)


def in_proj(x2d, w, tm=512, tn=256):
    n, k = x2d.shape
    m = w.shape[1]
    tm = min(tm, n)
    return pl.pallas_call(
        functools.partial(_inproj_kernel, tn=tn),
        grid=(n // tm,),
        in_specs=[pl.BlockSpec((tm, k), lambda i: (i, 0)),
                  pl.BlockSpec((k, m), lambda i: (0, 0))],
        out_specs=pl.BlockSpec((tm, m), lambda i: (i, 0)),
        out_shape=jax.ShapeDtypeStruct((n, m), BF16),
        compiler_params=_cparams(("parallel",)),
        name="in_proj",
    )(x2d, w)


def _t5_bucket(dist):
    n = jnp.maximum(dist, 0)
    max_exact = REL_BUCKETS // 2
    nf = jnp.maximum(n, 1).astype(F32)
    large = max_exact + (jnp.log(nf / max_exact) / math.log(REL_MAX_DIST / max_exact)
                         * (REL_BUCKETS - max_exact)).astype(jnp.int32)
    large = jnp.minimum(large, REL_BUCKETS - 1)
    return jnp.where(n < max_exact, n, large)


LOG2E = math.log2(math.e)
Q_SCALE = DA_HEAD_DIM ** -0.5 * LOG2E
ATTN_BLOCK = 512
ATTN_CHUNK = 256
ONES_ROWS = 16


def attn_bias_tiles(rel_bias, t):
    table = rel_bias.astype(F32).reshape(REL_BUCKETS, DA_HEADS * 2)
    vec = table[_t5_bucket(jnp.arange(2 * t))]
    far = table[_t5_bucket(jnp.array(2 * t))]
    vec = ((vec - far[None, :]) * LOG2E).T
    masked = jnp.full((DA_HEADS * 2, t), NEG_BIG, F32)
    u_diag = jnp.concatenate([vec[:, :t], masked], axis=1)
    u_prev = jnp.concatenate([vec[:, t:], vec[:, :t]], axis=1)

    def toeplitz(u):
        skew = jnp.tile(u, (1, t))[:, :t * (2 * t - 1)].reshape(-1, t, 2 * t - 1)
        return skew[:, :, :t]

    def per_head(a):
        return a.reshape(DA_HEADS, 2, t, t).transpose(0, 2, 1, 3).reshape(DA_HEADS, t, 2 * t)

    return jnp.stack([per_head(toeplitz(u_diag)), per_head(toeplitz(u_prev))], axis=1)


def _attn_kernel(q_ref, k_ref, v_ref, bias_ref, lam_ref, g_ref, o_ref, vt, m_s, acc, s_a, s_b, *, t, lam_init):
    nk = vt.shape[0]
    ones = jnp.ones((ONES_ROWS, t), BF16)
    for ki in range(nk):
        v_t = v_ref[0, ki * t:(ki + 1) * t, :].astype(F32).T.astype(BF16)
        vt[ki] = jnp.concatenate([v_t, ones], axis=0)

    def query_tile(qi, carry):
        _attn_query_tile(qi, q_ref, k_ref, bias_ref, lam_ref, g_ref, o_ref, vt, m_s, acc, s_a, s_b,
                         t=t, lam_init=lam_init)
        return carry

    lax.fori_loop(0, nk // 2, query_tile, 0)


def _attn_query_tile(qi, q_ref, k_ref, bias_ref, lam_ref, g_ref, o_ref, vt, m_s, acc, s_a, s_b, *, t, lam_init):
    tq = 2 * t
    q_rows = pl.ds(pl.multiple_of(qi * tq, tq), tq)
    q = q_ref[0, q_rows, :]
    lane = lax.broadcasted_iota(jnp.int32, q.shape, 1)
    zero = jnp.zeros_like(q)
    qcat = jnp.concatenate([jnp.where(lane < DA_HEAD_DIM, q, zero),
                            jnp.where(lane >= DA_HEAD_DIM, q, zero)], axis=0)
    m_s[...] = jnp.full(m_s.shape, NEG_BIG, F32)
    acc[...] = jnp.zeros(acc.shape, F32)

    chunk = min(ATTN_CHUNK, t)
    DIAG, PREV, FAR, SKIP = 0, 1, None, "skip"

    def step(nxt, cur):
        kb = None
        if nxt is not None:
            kb = k_ref[0, pl.ds(pl.multiple_of(nxt[0] * t, t), t), :]
        for c in range(0, 2 * tq, chunk):
            cols = slice(c, c + chunk)
            mp, qoff = divmod(c, tq)
            half, ioff = divmod(qoff, t)
            if nxt is not None and not (len(nxt) > 2 and nxt[2][half] == SKIP):
                nxt[1][:, cols] = _nt_dot(kb, qcat[cols, :])
            if cur is not None:
                ki, s_ref, kinds = cur
                kind = kinds[half]
                if kind == SKIP:
                    continue
                s = s_ref[:, cols]
                if kind is not FAR:
                    s = s + bias_ref[0, kind, :, mp * t + ioff:mp * t + ioff + chunk]
                m_prev = m_s[:, cols]
                m_new = jnp.maximum(m_prev, jnp.max(s, axis=0, keepdims=True))
                alpha = jnp.exp2(m_prev - m_new)
                p = jnp.exp2(s - m_new).astype(BF16)
                acc[:, cols] = alpha * acc[:, cols] + _dot(vt[ki], p)
                m_s[:, cols] = m_new

    upper = (SKIP, DIAG)
    lower = (DIAG, PREV)
    before = (PREV, FAR)
    plain = (FAR, FAR)
    step((2 * qi + 1, s_a, upper), None)
    step((2 * qi, s_b), (2 * qi + 1, s_a, upper))

    @pl.when(qi == 0)
    def _():
        step(None, (0, s_b, lower))

    @pl.when(qi >= 1)
    def _():
        n_far = 2 * qi - 1
        step((2 * qi - 1, s_a), (2 * qi, s_b, lower))
        step((0, s_b), (2 * qi - 1, s_a, before))

        def pair(jj, carry):
            j = 2 * jj
            step((j + 1, s_a), (j, s_b, plain))
            step((j + 2, s_b), (j + 1, s_a, plain))
            return carry

        lax.fori_loop(0, n_far // 2, pair, 0)
        step(None, (n_far - 1, s_b, plain))

    lm = lam_ref[...]
    lam = (jnp.exp(jnp.sum(lm[0:1] * lm[1:2], keepdims=True))
           - jnp.exp(jnp.sum(lm[2:3] * lm[3:4], keepdims=True)) + lam_init)
    a = acc[...]
    o = a[:DA_V_DIM, :] / a[DA_V_DIM:DA_V_DIM + 1, :]
    out = o[:, :tq] - lam * o[:, tq:]
    out = out * lax.rsqrt(jnp.mean(out * out, axis=0, keepdims=True) + NORM_EPS) * g_ref[...]
    o_ref[0, q_rows, :] = (out * (1.0 - lam_init)).T.astype(o_ref.dtype)


def diff_attention(proj3, bias_tiles, lam_params, subln_g, lam_init):
    b, s, _ = proj3.shape
    t = bias_tiles.shape[2]
    tq = 2 * t
    seq = lambda col0: pl.BlockSpec((1, s, DA_V_DIM), lambda bi, h: (bi, 0, col0 + h))
    return pl.pallas_call(
        functools.partial(_attn_kernel, t=t, lam_init=lam_init),
        grid=(b, DA_HEADS),
        in_specs=[
            seq(0), seq(DA_HEADS), seq(2 * DA_HEADS),
            pl.BlockSpec((1, 2, t, 2 * t), lambda bi, h: (h, 0, 0, 0)),
            pl.BlockSpec((4, DA_HEAD_DIM), lambda bi, h: (0, 0)),
            pl.BlockSpec((DA_V_DIM, 1), lambda bi, h: (0, 0)),
        ],
        out_specs=pl.BlockSpec((1, s, DA_V_DIM), lambda bi, h: (bi, 0, h)),
        out_shape=jax.ShapeDtypeStruct((b, s, DA_WIDTH), BF16),
        scratch_shapes=[pltpu.VMEM((s // t, DA_V_DIM + ONES_ROWS, t), BF16),
                        pltpu.VMEM((1, 2 * tq), F32),
                        pltpu.VMEM((DA_V_DIM + ONES_ROWS, 2 * tq), F32),
                        pltpu.VMEM((t, 2 * tq), F32), pltpu.VMEM((t, 2 * tq), F32)],
        compiler_params=_cparams(("parallel", "parallel")),
        name="diff_attention",
    )(proj3, proj3, proj3, bias_tiles, lam_params, subln_g.reshape(DA_V_DIM, 1))


def _pool_kernel(p_ref, w_ref, scale_ref, o_ref, halo, *, t):
    si = pl.program_id(1)

    @pl.when(si == 0)
    def _():
        halo[...] = jnp.zeros(halo.shape, F32)

    p = p_ref[0].astype(F32)
    ext = jnp.concatenate([halo[...], p], axis=0)
    halo[...] = p[t - POOL_HALO:, :]
    sums = {1: ext}
    w = 1
    while w < POOL_WINDOWS[-1]:
        sums[2 * w] = sums[w] + pltpu.roll(sums[w], w, 0)
        w *= 2
    lane = lax.broadcasted_iota(jnp.int32, (t, POOL_WIDTH), 1)
    pos = (si * t + lax.broadcasted_iota(jnp.int32, (t, POOL_WIDTH), 0) + 1).astype(F32)
    wsum = sums[POOL_WINDOWS[-1]][POOL_HALO:, :]
    cnt = jnp.minimum(pos, float(POOL_WINDOWS[-1]))
    for gi in range(len(POOL_WINDOWS) - 2, -1, -1):
        in_group = lane < (gi + 1) * POOL_GROUP_DIM
        wsum = jnp.where(in_group, sums[POOL_WINDOWS[gi]][POOL_HALO:, :], wsum)
        cnt = jnp.where(in_group, jnp.minimum(pos, float(POOL_WINDOWS[gi])), cnt)
    pooled = wsum / cnt - p
    mixed = _dot(pooled.astype(BF16), w_ref[...])
    o_ref[0] = (mixed * scale_ref[...]).astype(o_ref.dtype)


def pool_mixer(proj3, pool_w, pool_scale, t=512):
    b, s, _ = proj3.shape
    t = min(t, s)
    g = len(POOL_WINDOWS)
    wbd = (jnp.eye(g, dtype=F32)[:, None, :, None] * pool_w.astype(F32)[:, :, None, :]).reshape(
        POOL_WIDTH, POOL_WIDTH).astype(BF16)
    return pl.pallas_call(
        functools.partial(_pool_kernel, t=t),
        grid=(b, s // t),
        in_specs=[pl.BlockSpec((1, t, POOL_WIDTH), lambda bi, si: (bi, si, POOL_COL_BLOCK)),
                  pl.BlockSpec((POOL_WIDTH, POOL_WIDTH), lambda bi, si: (0, 0)),
                  pl.BlockSpec((1, POOL_WIDTH), lambda bi, si: (0, 0))],
        out_specs=pl.BlockSpec((1, t, POOL_WIDTH), lambda bi, si: (bi, si, 0)),
        out_shape=jax.ShapeDtypeStruct((b, s, POOL_WIDTH), BF16),
        scratch_shapes=[pltpu.VMEM((POOL_HALO, POOL_WIDTH), F32)],
        compiler_params=_cparams(("parallel", "arbitrary")),
        name="pool_mixer",
    )(proj3, wbd, pool_scale.reshape(1, POOL_WIDTH))


def _retention_tables(s, t):
    d, hn, c = RET_HEAD_DIM, RET_HEADS, RET_CHUNK
    half = d // 2
    inv = 10000.0 ** (-jnp.linspace(0.0, 1.0, half, dtype=F32))
    ang = jnp.arange(s)[:, None].astype(F32) * inv[None, :]
    cos, sin = jnp.cos(ang), jnp.sin(ang)
    cos_t = jnp.tile(jnp.concatenate([cos, cos], axis=-1), (1, hn))
    sin_t = jnp.tile(jnp.concatenate([-sin, sin], axis=-1), (1, hn))
    log_gamma = jnp.log(1.0 - 2.0 ** (-5.0 - jnp.arange(hn, dtype=F32)))
    idx = jnp.arange(c, dtype=F32)
    rel = idx[:, None] - idx[None, :]
    intra = jnp.where(rel >= 0, jnp.exp(log_gamma[:, None, None] * jnp.maximum(rel, 0.0)), 0.0)
    q_decay = jnp.exp(log_gamma[:, None] * (idx + 1.0))
    k_decay = jnp.exp(log_gamma[:, None] * (c - 1.0 - idx))
    chunk_decay = jnp.exp(log_gamma * c)
    lanes = lambda a: jnp.tile(jnp.repeat(a.T, d, axis=1), (t // c, 1))
    l = jnp.arange(hn * d)
    partner = jnp.where(l % d < half, l + half, l - half)
    swap = (l[:, None] == partner[None, :]).astype(BF16)
    same = (jnp.arange(2 * d)[:, None] // d) == (jnp.arange(2 * d)[None, :] // d)
    decay_bd = jnp.where(same[None], jnp.repeat(chunk_decay, d).reshape(hn // 2, 2 * d, 1), 0.0)
    avg = jnp.where(same, 1.0 / d, 0.0).astype(BF16)
    return cos_t, sin_t, intra, lanes(q_decay), lanes(k_decay), swap, decay_bd.astype(F32), avg


def _ret_kernel(q_ref, k_ref, v_ref, g_ref, cos_ref, sin_ref, intra_ref, qd_ref, kd_ref, swap_ref, decay_ref,
                avg_ref, gn_ref, o_ref, state, *, t):
    si = pl.program_id(1)
    d, c = RET_HEAD_DIM, RET_CHUNK
    pw = 2 * d

    @pl.when(si == 0)
    def _():
        state[...] = jnp.zeros(state.shape, F32)

    def rotate(x_ref):
        x = x_ref[0]
        return x.astype(F32) * cos_ref[...] + _dot(x, swap_ref[...]) * sin_ref[...]

    def lane_mean(a):
        hi = a.astype(BF16)
        lo = (a - hi.astype(F32)).astype(BF16)
        return _dot(hi, avg_ref[...]) + _dot(lo, avg_ref[...])

    q = rotate(q_ref)
    k = rotate(k_ref) * (d ** -0.5)
    qs = (q * qd_ref[...]).astype(BF16)
    ks = (k * kd_ref[...]).astype(BF16)
    qb = q.astype(BF16)
    kb = k.astype(BF16)
    v = v_ref[0]
    first = lax.broadcasted_iota(jnp.int32, (c, pw), 1) < d
    same = ((lax.broadcasted_iota(jnp.int32, (pw, pw), 0) < d)
            == (lax.broadcasted_iota(jnp.int32, (pw, pw), 1) < d))
    zero = jnp.zeros((c, pw), BF16)

    chunks = [slice(ci * c, (ci + 1) * c) for ci in range(t // c)]
    pairs = [slice(p * pw, (p + 1) * pw) for p in range(RET_HEADS // 2)]
    scores = {}
    for p, cols in enumerate(pairs):
        for ci, rows in enumerate(chunks):
            qp, kp = qb[rows, cols], kb[rows, cols]
            scores[p, ci] = ((_nt_dot(jnp.where(first, qp, zero), kp) * intra_ref[2 * p]).astype(BF16),
                             (_nt_dot(jnp.where(first, zero, qp), kp) * intra_ref[2 * p + 1]).astype(BF16))
    intra_out, increment = {}, {}
    for p, cols in enumerate(pairs):
        for ci, rows in enumerate(chunks):
            vp = v[rows, cols]
            intra_out[p, ci] = jnp.where(first, _dot(scores[p, ci][0], vp), _dot(scores[p, ci][1], vp))
            increment[p, ci] = jnp.where(same, _dot(ks[rows, cols].T, vp), 0.0)
    ys = {}
    for p, cols in enumerate(pairs):
        st = state[p]
        for ci, rows in enumerate(chunks):
            ys[p, ci] = intra_out[p, ci] + _dot(qs[rows, cols], st.astype(BF16))
            st = st * decay_ref[p] + increment[p, ci]
        state[p] = st
    y = jnp.concatenate([jnp.concatenate([ys[p, ci] for ci in range(len(chunks))], axis=0)
                         for p in range(len(pairs))], axis=1)
    mean = jnp.concatenate([lane_mean(y[:, cols]) for cols in pairs], axis=1)
    yc = y - mean
    var = jnp.concatenate([lane_mean((yc * yc)[:, cols]) for cols in pairs], axis=1)
    yn = yc * lax.rsqrt(var + NORM_EPS)
    gate = _silu(g_ref[0].astype(F32))
    o_ref[0] = (gate * (yn * gn_ref[...])).astype(o_ref.dtype)


def retention(proj3, gn_g, t=512):
    b, s, _ = proj3.shape
    t = min(t, s)
    cos_t, sin_t, intra, qd, kd, swap, decay_bd, avg = _retention_tables(s, t)
    pw = 2 * RET_HEAD_DIM
    col = lambda j: pl.BlockSpec((1, t, RET_WIDTH), lambda bi, si: (bi, si, RET_COL_BLOCK + j))
    const2 = lambda shape: pl.BlockSpec(shape, lambda bi, si: (0, 0))
    const3 = lambda shape: pl.BlockSpec(shape, lambda bi, si: (0, 0, 0))
    return pl.pallas_call(
        functools.partial(_ret_kernel, t=t),
        grid=(b, s // t),
        in_specs=[col(0), col(1), col(2), col(3),
                  pl.BlockSpec((t, RET_WIDTH), lambda bi, si: (si, 0)),
                  pl.BlockSpec((t, RET_WIDTH), lambda bi, si: (si, 0)),
                  const3((RET_HEADS, RET_CHUNK, RET_CHUNK)),
                  const2((t, RET_WIDTH)), const2((t, RET_WIDTH)), const2((RET_WIDTH, RET_WIDTH)),
                  const3((RET_HEADS // 2, pw, pw)), const2((pw, pw)), const2((1, RET_WIDTH))],
        out_specs=pl.BlockSpec((1, t, RET_WIDTH), lambda bi, si: (bi, si, 0)),
        out_shape=jax.ShapeDtypeStruct((b, s, RET_WIDTH), BF16),
        scratch_shapes=[pltpu.VMEM((RET_HEADS // 2, pw, pw), F32)],
        compiler_params=_cparams(("parallel", "arbitrary")),
        name="retention",
    )(proj3, proj3, proj3, proj3, cos_t, sin_t, intra, qd, kd, swap, decay_bd, avg, gn_g.reshape(1, RET_WIDTH))


def _mix_ln(da_ref, pool_ref, ret_ref, x_ref, w_ref, g_ref, b_ref):
    e0, e1 = DA_WIDTH, DA_WIDTH + POOL_WIDTH
    mix = (_dot(da_ref[...], w_ref[0:e0, :]) + _dot(pool_ref[...], w_ref[e0:e1, :])
           + _dot(ret_ref[...], w_ref[e1:, :]))
    return _layer_norm(ALPHA * x_ref[...] + mix, g_ref[...], b_ref[...])


def _outproj_kernel(da_ref, pool_ref, ret_ref, x_ref, w_ref, g_ref, b_ref, o_ref):
    o_ref[...] = _mix_ln(da_ref, pool_ref, ret_ref, x_ref, w_ref, g_ref, b_ref)


def out_proj_ln(y_da, y_pool, y_ret, x2d, w, g, b, tm=512):
    n, dm = x2d.shape
    tm = min(tm, n)
    row = lambda width: pl.BlockSpec((tm, width), lambda i: (i, 0))
    const = lambda shape: pl.BlockSpec(shape, lambda i: (0, 0))
    return pl.pallas_call(
        _outproj_kernel,
        grid=(n // tm,),
        in_specs=[row(DA_WIDTH), row(POOL_WIDTH), row(RET_WIDTH), row(dm),
                  const(w.shape), const((1, dm)), const((1, dm))],
        out_specs=row(dm),
        out_shape=jax.ShapeDtypeStruct((n, dm), F32),
        compiler_params=_cparams(("parallel",)),
        name="out_proj_ln",
    )(y_da, y_pool, y_ret, x2d, w, g.reshape(1, dm), b.reshape(1, dm))


def _mix_ffn_kernel(da_ref, pool_ref, ret_ref, x_ref, wo_ref, g1_ref, b1_ref, wg_ref, wu_ref, wd_ref,
                    g2_ref, b2_ref, o_ref, *, tf):
    x = _mix_ln(da_ref, pool_ref, ret_ref, x_ref, wo_ref, g1_ref, b1_ref)
    xb = x.astype(BF16)
    acc = jnp.zeros(x.shape, F32)
    for f0 in range(0, wg_ref.shape[1], tf):
        hidden = _silu(_dot(xb, wg_ref[:, f0:f0 + tf])) * _dot(xb, wu_ref[:, f0:f0 + tf])
        acc = acc + _dot(hidden.astype(BF16), wd_ref[f0:f0 + tf, :])
    o_ref[...] = _layer_norm(ALPHA * x + acc, g2_ref[...], b2_ref[...])


def out_proj_ffn_ln(y_da, y_pool, y_ret, x2d, wo, g1, b1, wg, wu, wd, g2, b2, tm=512, tf=256):
    n, dm = x2d.shape
    tm = min(tm, n)
    row = lambda width: pl.BlockSpec((tm, width), lambda i: (i, 0))
    resident = lambda shape: pl.BlockSpec(shape, lambda i: (0, 0), pipeline_mode=pl.Buffered(1))
    vec = lambda a: a.reshape(1, dm)
    return pl.pallas_call(
        functools.partial(_mix_ffn_kernel, tf=tf),
        grid=(n // tm,),
        in_specs=[row(DA_WIDTH), row(POOL_WIDTH), row(RET_WIDTH), row(dm),
                  resident(wo.shape), resident((1, dm)), resident((1, dm)),
                  resident(wg.shape), resident(wu.shape), resident(wd.shape),
                  resident((1, dm)), resident((1, dm))],
        out_specs=row(dm),
        out_shape=jax.ShapeDtypeStruct((n, dm), F32),
        compiler_params=_cparams(("parallel",)),
        name="out_proj_ffn_ln",
    )(y_da, y_pool, y_ret, x2d, wo, vec(g1), vec(b1), wg, wu, wd, vec(g2), vec(b2))


ROUTE_ROWS = 8
LANES = 128


def _router_kernel(x_ref, w_ref, tri_ref, route_ref, route_t_ref, cnt_ref, carry, *, t):
    i = pl.program_id(0)

    @pl.when(i == 0)
    def _():
        carry[...] = jnp.zeros(carry.shape, F32)

    x = x_ref[...]
    xh = x.astype(BF16)
    xl = (x - xh.astype(F32)).astype(BF16)
    w = w_ref[...]
    wh = w.astype(BF16)
    wl = (w - wh.astype(F32)).astype(BF16)
    logits = _nt_dot(wh, xh) + _nt_dot(wl, xh) + _nt_dot(wh, xl)

    row = lax.broadcasted_iota(jnp.int32, logits.shape, 0)
    v0 = jnp.max(logits, axis=0, keepdims=True)
    i0 = jnp.min(jnp.where(logits == v0, row, N_EXPERTS), axis=0, keepdims=True)
    rest = jnp.where(row == i0, -jnp.inf, logits)
    v1 = jnp.max(rest, axis=0, keepdims=True)
    i1 = jnp.min(jnp.where(rest == v1, row, N_EXPERTS), axis=0, keepdims=True)
    ex = jnp.exp(v1 - v0)
    gate0 = 1.0 / (1.0 + ex)
    gate1 = ex / (1.0 + ex)

    oh0 = row == i0
    oh1 = row == i1
    member = jnp.where(oh0 | oh1, 1.0, 0.0)
    before = _dot(member.astype(BF16), tri_ref[...]) + carry[:, 0:1]
    rank0 = jnp.sum(jnp.where(oh0, before, 0.0), axis=0, keepdims=True)
    rank1 = jnp.sum(jnp.where(oh1, before, 0.0), axis=0, keepdims=True)
    carry[...] = carry[...] + jnp.sum(member, axis=1, keepdims=True)
    cnt_ref[...] = carry[...]

    route = jnp.concatenate([i0.astype(F32), i1.astype(F32), rank0, rank1, gate0, gate1,
                             jnp.zeros((2, t), F32)], axis=0)
    route_ref[...] = route
    padded = jnp.concatenate([route, jnp.zeros((LANES - ROUTE_ROWS, t), F32)], axis=0)
    route_t_ref[...] = padded.T


def route_tokens(x2d, router_w, t=1024):
    n, dm = x2d.shape
    t = min(t, n)
    tri = (jnp.arange(t)[:, None] < jnp.arange(t)[None, :]).astype(BF16)
    return pl.pallas_call(
        functools.partial(_router_kernel, t=t),
        grid=(n // t,),
        in_specs=[pl.BlockSpec((t, dm), lambda i: (i, 0)),
                  pl.BlockSpec((N_EXPERTS, dm), lambda i: (0, 0)),
                  pl.BlockSpec((t, t), lambda i: (0, 0))],
        out_specs=[pl.BlockSpec((ROUTE_ROWS, t), lambda i: (0, i)),
                   pl.BlockSpec((t, LANES), lambda i: (i, 0)),
                   pl.BlockSpec((N_EXPERTS, LANES), lambda i: (0, 0))],
        out_shape=[jax.ShapeDtypeStruct((ROUTE_ROWS, n), F32),
                   jax.ShapeDtypeStruct((n, LANES), F32),
                   jax.ShapeDtypeStruct((N_EXPERTS, LANES), F32)],
        scratch_shapes=[pltpu.VMEM((N_EXPERTS, LANES), F32)],
        compiler_params=_cparams(("arbitrary",)),
        name="route_tokens",
    )(x2d, router_w.T, tri)


def _dest_kernel(start_ref, route_ref, dest_ref):
    r = route_ref[...]
    for k in range(2):
        e = r[k:k + 1, :].astype(jnp.int32)
        base = jnp.zeros(e.shape, jnp.int32)
        for ei in range(N_EXPERTS):
            base = jnp.where(e == ei, start_ref[ei], base)
        dest_ref[k:k + 1, :] = base + r[2 + k:3 + k, :].astype(jnp.int32)


def slot_of_assignment(pad_start, route, t=1024):
    n = route.shape[1]
    t = min(t, n)
    grid_spec = pltpu.PrefetchScalarGridSpec(
        num_scalar_prefetch=1,
        grid=(n // t,),
        in_specs=[pl.BlockSpec((ROUTE_ROWS, t), lambda i, ps: (0, i))],
        out_specs=pl.BlockSpec((2, t), lambda i, ps: (0, i)),
    )
    return pl.pallas_call(
        _dest_kernel,
        grid_spec=grid_spec,
        out_shape=jax.ShapeDtypeStruct((2, n), jnp.int32),
        compiler_params=_cparams(("parallel",)),
        name="slot_of_assignment",
    )(pad_start, route)


def _row_copy(src, src_row, dst, dst_row, sem):
    return pltpu.make_async_copy(src.at[pl.ds(src_row, 1)], dst.at[pl.ds(dst_row, 1)], sem)


ISSUE_UNROLL = 8


def _dispatch_kernel(pad_lo_ref, pad_n_ref, d0_ref, d1_ref, x_ref, slots_out, zrow, sem, zsem, *, t):
    @pl.when(pl.program_id(0) == 0)
    def _():
        zrow[...] = jnp.zeros(zrow.shape, zrow.dtype)
        for e in range(N_EXPERTS):
            lo = pad_lo_ref[e]

            def zero_start(r, carry):
                _row_copy(zrow, 0, slots_out, lo + r, zsem).start()
                return carry

            def zero_wait(r, carry):
                _row_copy(zrow, 0, slots_out, lo + r, zsem).wait()
                return carry

            lax.fori_loop(0, pad_n_ref[e], zero_start, 0)
            lax.fori_loop(0, pad_n_ref[e], zero_wait, 0)

        tail_lo = pad_lo_ref[N_EXPERTS]
        group = zrow.shape[0]

        def tail_copy(r):
            dst = slots_out.at[pl.ds(pl.multiple_of(tail_lo + r * group, group), group)]
            return pltpu.make_async_copy(zrow, dst, zsem)

        def tail_start(r, carry):
            tail_copy(r).start()
            return carry

        def tail_wait(r, carry):
            tail_copy(r).wait()
            return carry

        lax.fori_loop(0, pad_n_ref[N_EXPERTS], tail_start, 0)
        lax.fori_loop(0, pad_n_ref[N_EXPERTS], tail_wait, 0)

    def issue(j, carry):
        base = pl.multiple_of(j * ISSUE_UNROLL, ISSUE_UNROLL)
        group = x_ref.at[pl.ds(base, ISSUE_UNROLL)]
        for u in range(ISSUE_UNROLL):
            for k, d_ref in enumerate((d0_ref, d1_ref)):
                _row_copy(group, u, slots_out, d_ref[base + u], sem).start()
        return carry

    lax.fori_loop(0, t // ISSUE_UNROLL, issue, 0)
    for k in range(2):
        pltpu.make_async_copy(x_ref, slots_out.at[pl.ds(0, t)], sem).wait()


def dispatch_rows(pad_lo, pad_n, dest, x2d, n_slots, t=1024):
    n, dm = x2d.shape
    t = min(t, n)
    grid_spec = pltpu.PrefetchScalarGridSpec(
        num_scalar_prefetch=2,
        grid=(n // t,),
        in_specs=[pl.BlockSpec((t,), lambda i, lo, cnt: (i,), memory_space=pltpu.SMEM),
                  pl.BlockSpec((t,), lambda i, lo, cnt: (i,), memory_space=pltpu.SMEM),
                  pl.BlockSpec((t, dm), lambda i, lo, cnt: (i, 0))],
        out_specs=pl.BlockSpec(memory_space=pl.ANY),
        scratch_shapes=[pltpu.VMEM((8, dm), x2d.dtype), pltpu.SemaphoreType.DMA, pltpu.SemaphoreType.DMA],
    )
    return pl.pallas_call(
        functools.partial(_dispatch_kernel, t=t),
        grid_spec=grid_spec,
        out_shape=jax.ShapeDtypeStruct((n_slots, dm), x2d.dtype),
        compiler_params=_cparams(("arbitrary",)),
        name="dispatch_rows",
    )(pad_lo, pad_n, dest[0], dest[1], x2d)


def _expert_kernel(be_ref, nv_ref, x_ref, wg_ref, wu_ref, wd_ref, o_ref, xb):
    blk = pl.program_id(0)
    f = pl.program_id(1)
    valid = blk < nv_ref[0]

    @pl.when(f == 0)
    def _():
        o_ref[...] = jnp.zeros(o_ref.shape, o_ref.dtype)

    @pl.when(valid & (f == 0))
    def _():
        xb[...] = x_ref[...].astype(BF16)

    @pl.when(valid)
    def _():
        hidden = (_silu(_dot(xb[...], wg_ref[0].astype(BF16)))
                  * _dot(xb[...], wu_ref[0].astype(BF16)))
        o_ref[...] += _dot(hidden.astype(BF16), wd_ref[0].astype(BF16))


def expert_swiglu(block_e, n_valid, slots, wg, wu, wd, bm, tf=512):
    n_slots, dm = slots.shape
    edim = wg.shape[2]
    grid_spec = pltpu.PrefetchScalarGridSpec(
        num_scalar_prefetch=2,
        grid=(n_slots // bm, edim // tf),
        in_specs=[pl.BlockSpec((bm, dm), lambda b, f, be, nv: (jnp.minimum(b, nv[0] - 1), 0)),
                  pl.BlockSpec((1, dm, tf), lambda b, f, be, nv: (be[b], 0, f)),
                  pl.BlockSpec((1, dm, tf), lambda b, f, be, nv: (be[b], 0, f)),
                  pl.BlockSpec((1, tf, dm), lambda b, f, be, nv: (be[b], f, 0))],
        out_specs=pl.BlockSpec((bm, dm), lambda b, f, be, nv: (b, 0)),
        scratch_shapes=[pltpu.VMEM((bm, dm), BF16)],
    )
    return pl.pallas_call(
        _expert_kernel,
        grid_spec=grid_spec,
        out_shape=jax.ShapeDtypeStruct((n_slots, dm), F32),
        compiler_params=_cparams(("parallel", "arbitrary")),
        name="expert_swiglu",
    )(block_e, n_valid, slots, wg, wu, wd)


def _combine_kernel(d0_ref, d1_ref, d0_next_ref, d1_next_ref, rt_ref, x_ref, y_hbm, g_ref, b_ref, o_ref,
                    rows, sem, *, t):
    i = pl.program_id(0)
    cur = i % 2

    def gather(d_refs, buf):
        def issue(j, carry):
            for u in range(ISSUE_UNROLL):
                for k in range(2):
                    _row_copy(y_hbm, d_refs[k][j * ISSUE_UNROLL + u], rows.at[buf, k, j], u, sem.at[buf]).start()
            return carry

        lax.fori_loop(0, t // ISSUE_UNROLL, issue, 0)

    @pl.when(i == 0)
    def _():
        gather((d0_ref, d1_ref), 0)

    @pl.when(i + 1 < pl.num_programs(0))
    def _():
        gather((d0_next_ref, d1_next_ref), 1 - cur)

    for k in range(2):
        pltpu.make_async_copy(y_hbm.at[pl.ds(0, t)], o_ref, sem.at[cur]).wait()

    rt = rt_ref[...]
    dm = o_ref.shape[1]
    mixed = rt[:, 4:5] * rows[cur, 0].reshape(t, dm) + rt[:, 5:6] * rows[cur, 1].reshape(t, dm)
    o_ref[...] = _layer_norm(ALPHA * x_ref[...] + mixed, g_ref[...], b_ref[...])


def combine_ln(dest, route_t, x2d, y, g, b, t=512):
    n, dm = x2d.shape
    t = min(t, n)
    last = n // t - 1
    return pl.pallas_call(
        functools.partial(_combine_kernel, t=t),
        grid=(n // t,),
        in_specs=[pl.BlockSpec((t,), lambda i: (i,), memory_space=pltpu.SMEM),
                  pl.BlockSpec((t,), lambda i: (i,), memory_space=pltpu.SMEM),
                  pl.BlockSpec((t,), lambda i: (jnp.minimum(i + 1, last),), memory_space=pltpu.SMEM),
                  pl.BlockSpec((t,), lambda i: (jnp.minimum(i + 1, last),), memory_space=pltpu.SMEM),
                  pl.BlockSpec((t, LANES), lambda i: (i, 0)),
                  pl.BlockSpec((t, dm), lambda i: (i, 0)),
                  pl.BlockSpec(memory_space=pl.ANY),
                  pl.BlockSpec((1, dm), lambda i: (0, 0)),
                  pl.BlockSpec((1, dm), lambda i: (0, 0))],
        out_specs=pl.BlockSpec((t, dm), lambda i: (i, 0)),
        out_shape=jax.ShapeDtypeStruct((n, dm), F32),
        scratch_shapes=[pltpu.VMEM((2, 2, t // ISSUE_UNROLL, ISSUE_UNROLL, dm), F32),
                        pltpu.SemaphoreType.DMA((2,))],
        compiler_params=_cparams(("arbitrary",)),
        name="combine_ln",
    )(dest[0], dest[1], dest[0], dest[1], route_t, x2d, y, g.reshape(1, dm), b.reshape(1, dm))


def moe_ln(x2d, router_w, wg, wu, wd, g, b, bm=1024):
    n, _ = x2d.shape
    bm = min(bm, n)
    route, route_t, cnt = route_tokens(x2d, router_w)
    counts = cnt[:, 0].astype(jnp.int32)
    padded = (counts + bm - 1) // bm * bm
    pad_end = jnp.cumsum(padded)
    pad_start = (pad_end - padded).astype(jnp.int32)
    n_blocks = 2 * n // bm + N_EXPERTS
    n_slots = n_blocks * bm
    block_e = jnp.minimum(jnp.searchsorted(pad_end, jnp.arange(n_blocks) * bm, side='right'),
                          N_EXPERTS - 1).astype(jnp.int32)
    n_valid = (pad_end[-1:] // bm).astype(jnp.int32)
    dest = slot_of_assignment(pad_start, route)
    pad_lo = jnp.concatenate([pad_start + counts, pad_end[-1:]]).astype(jnp.int32)
    pad_n = jnp.concatenate([padded - counts, (n_slots - pad_end[-1:]) // 8]).astype(jnp.int32)
    slots = dispatch_rows(pad_lo, pad_n, dest, x2d, n_slots)
    y = expert_swiglu(block_e, n_valid, slots, wg, wu, wd, bm)
    return combine_ln(dest, route_t, x2d, y, g, b)


def kernel(x, rel_bias, w_in, diff_lambda, diff_subln_g, pool_w, pool_scale, ret_gn_g, w_out,
           ln1_g, ln1_b, ln2_g, ln2_b, ffn_w_gate, ffn_w_up, ffn_w_down,
           router_w, moe_w_gate, moe_w_up, moe_w_down):
    bsz, seq, dm = x.shape
    x2d = x.reshape(bsz * seq, dm)
    bias_tiles = attn_bias_tiles(rel_bias, min(ATTN_BLOCK, seq))
    col_scale = jnp.where(jnp.arange(w_in.shape[2]) < DA_WIDTH, Q_SCALE, 1.0).astype(F32)
    for l in range(DEPTH):
        lam_init = 0.8 - 0.6 * math.exp(-0.3 * l)
        proj = in_proj(x2d, (w_in[l] * col_scale).astype(BF16)).reshape(bsz, seq, -1)
        y_da = diff_attention(proj, bias_tiles, diff_lambda[l], diff_subln_g[l], lam_init)
        y_pool = pool_mixer(proj, pool_w[l], pool_scale[l])
        y_ret = retention(proj, ret_gn_g[l])
        flat = lambda a: a.reshape(bsz * seq, -1)
        mixed = (flat(y_da), flat(y_pool), flat(y_ret), x2d, w_out[l].astype(BF16), ln1_g[l], ln1_b[l])
        j = l // 2
        if l % 2 == 0:
            x2d = out_proj_ffn_ln(*mixed, ffn_w_gate[j].astype(BF16), ffn_w_up[j].astype(BF16),
                                  ffn_w_down[j].astype(BF16), ln2_g[l], ln2_b[l])
        else:
            x2d = out_proj_ln(*mixed)
            x2d = moe_ln(x2d, router_w[j], moe_w_gate[j], moe_w_up[j], moe_w_down[j], ln2_g[l], ln2_b[l])
    return x2d.reshape(bsz, seq, dm)
```

```python
import functools
import math

import jax
import jax.numpy as jnp
from jax import lax
from jax.experimental import pallas as pl
from jax.experimental.pallas import tpu as pltpu

F32 = jnp.float32
BF16 = jnp.bfloat16

DEPTH = 2
DA_HEAD_DIM = 64
DA_V_DIM = 128
DA_HEADS = 4
DA_WIDTH = 512
POOL_WIDTH = 256
POOL_WINDOWS = (2, 4, 8, 16)
POOL_GROUP_DIM = 64
POOL_HALO = 16
RET_WIDTH = 256
RET_HEAD_DIM = 64
RET_HEADS = 4
RET_CHUNK = 128
REL_BUCKETS = 32
REL_MAX_DIST = 128
N_EXPERTS = 8
ALPHA = (2 * DEPTH) ** 0.25
LN_EPS = 1e-5
NORM_EPS = 1e-6
NEG_BIG = -1e30

POOL_COL_BLOCK = 3 * DA_WIDTH // POOL_WIDTH
RET_COL_BLOCK = POOL_COL_BLOCK + 1

VMEM_LIMIT = 56 * 1024 * 1024


def _cparams(sem, vmem=VMEM_LIMIT):
    return pltpu.CompilerParams(dimension_semantics=sem, vmem_limit_bytes=vmem)


def _nt_dot(a, b):
    return lax.dot_general(a, b, (((1,), (1,)), ((), ())), preferred_element_type=F32)


def _dot(a, b):
    return jnp.dot(a, b, preferred_element_type=F32)


def _layer_norm(z, g, b):
    mu = jnp.mean(z, axis=-1, keepdims=True)
    zc = z - mu
    var = jnp.mean(zc * zc, axis=-1, keepdims=True)
    return zc * lax.rsqrt(var + LN_EPS) * g + b


def _silu(x):
    return x / (1.0 + jnp.exp(-x))


def _inproj_kernel(x_ref, w_ref, o_ref, *, tn):
    xb = x_ref[...].astype(BF16)
    for j in range(0, w_ref.shape[1], tn):
        o_ref[:, j:j + tn] = _dot(xb, w_ref[:, j:j + tn]).astype(o_ref.dtype)


def in_proj(x2d, w, tm=512, tn=256):
    n, k = x2d.shape
    m = w.shape[1]
    tm = min(tm, n)
    return pl.pallas_call(
        functools.partial(_inproj_kernel, tn=tn),
        grid=(n // tm,),
        in_specs=[pl.BlockSpec((tm, k), lambda i: (i, 0)),
                  pl.BlockSpec((k, m), lambda i: (0, 0))],
        out_specs=pl.BlockSpec((tm, m), lambda i: (i, 0)),
        out_shape=jax.ShapeDtypeStruct((n, m), BF16),
        compiler_params=_cparams(("parallel",)),
        name="in_proj",
    )(x2d, w)


def _t5_bucket(dist):
    n = jnp.maximum(dist, 0)
    max_exact = REL_BUCKETS // 2
    nf = jnp.maximum(n, 1).astype(F32)
    large = max_exact + (jnp.log(nf / max_exact) / math.log(REL_MAX_DIST / max_exact)
                         * (REL_BUCKETS - max_exact)).astype(jnp.int32)
    large = jnp.minimum(large, REL_BUCKETS - 1)
    return jnp.where(n < max_exact, n, large)


LOG2E = math.log2(math.e)
Q_SCALE = DA_HEAD_DIM ** -0.5 * LOG2E
ATTN_BLOCK = 512
ATTN_CHUNK = 256
ONES_ROWS = 16


def attn_bias_tiles(rel_bias, t):
    table = rel_bias.astype(F32).reshape(REL_BUCKETS, DA_HEADS * 2)
    vec = table[_t5_bucket(jnp.arange(2 * t))]
    far = table[_t5_bucket(jnp.array(2 * t))]
    vec = ((vec - far[None, :]) * LOG2E).T
    masked = jnp.full((DA_HEADS * 2, t), NEG_BIG, F32)
    u_diag = jnp.concatenate([vec[:, :t], masked], axis=1)
    u_prev = jnp.concatenate([vec[:, t:], vec[:, :t]], axis=1)

    def toeplitz(u):
        skew = jnp.tile(u, (1, t))[:, :t * (2 * t - 1)].reshape(-1, t, 2 * t - 1)
        return skew[:, :, :t]

    def per_head(a):
        return a.reshape(DA_HEADS, 2, t, t).transpose(0, 2, 1, 3).reshape(DA_HEADS, t, 2 * t)

    return jnp.stack([per_head(toeplitz(u_diag)), per_head(toeplitz(u_prev))], axis=1)


def _attn_kernel(q_ref, k_ref, v_ref, bias_ref, lam_ref, g_ref, o_ref, vt, m_s, acc, s_a, s_b, *, t, lam_init):
    nk = vt.shape[0]
    ones = jnp.ones((ONES_ROWS, t), BF16)
    for ki in range(nk):
        v_t = v_ref[0, ki * t:(ki + 1) * t, :].astype(F32).T.astype(BF16)
        vt[ki] = jnp.concatenate([v_t, ones], axis=0)

    def query_tile(qi, carry):
        _attn_query_tile(qi, q_ref, k_ref, bias_ref, lam_ref, g_ref, o_ref, vt, m_s, acc, s_a, s_b,
                         t=t, lam_init=lam_init)
        return carry

    lax.fori_loop(0, nk // 2, query_tile, 0)


def _attn_query_tile(qi, q_ref, k_ref, bias_ref, lam_ref, g_ref, o_ref, vt, m_s, acc, s_a, s_b, *, t, lam_init):
    tq = 2 * t
    q_rows = pl.ds(pl.multiple_of(qi * tq, tq), tq)
    q = q_ref[0, q_rows, :]
    lane = lax.broadcasted_iota(jnp.int32, q.shape, 1)
    zero = jnp.zeros_like(q)
    qcat = jnp.concatenate([jnp.where(lane < DA_HEAD_DIM, q, zero),
                            jnp.where(lane >= DA_HEAD_DIM, q, zero)], axis=0)
    m_s[...] = jnp.full(m_s.shape, NEG_BIG, F32)
    acc[...] = jnp.zeros(acc.shape, F32)

    chunk = min(ATTN_CHUNK, t)
    DIAG, PREV, FAR, SKIP = 0, 1, None, "skip"

    def step(nxt, cur):
        kb = None
        if nxt is not None:
            kb = k_ref[0, pl.ds(pl.multiple_of(nxt[0] * t, t), t), :]
        for c in range(0, 2 * tq, chunk):
            cols = slice(c, c + chunk)
            mp, qoff = divmod(c, tq)
            half, ioff = divmod(qoff, t)
            ci = c // chunk
            if nxt is not None and not (len(nxt) > 2 and nxt[2][half] == SKIP):
                nxt[1][ci] = _nt_dot(kb, qcat[cols, :])
            if cur is not None:
                ki, s_ref, kinds = cur
                kind = kinds[half]
                if kind == SKIP:
                    continue
                s = s_ref[ci]
                if kind is not FAR:
                    s = s + bias_ref[0, kind, :, mp * t + ioff:mp * t + ioff + chunk]
                m_prev = m_s[ci]
                m_new = jnp.maximum(m_prev, jnp.max(s, axis=0, keepdims=True))
                alpha = jnp.exp2(m_prev - m_new)
                p = jnp.exp2(s - m_new).astype(BF16)
                acc[ci] = alpha * acc[ci] + _dot(vt[ki], p)
                m_s[ci] = m_new

    upper = (SKIP, DIAG)
    lower = (DIAG, PREV)
    before = (PREV, FAR)
    plain = (FAR, FAR)
    step((2 * qi + 1, s_a, upper), None)
    step((2 * qi, s_b), (2 * qi + 1, s_a, upper))

    @pl.when(qi == 0)
    def _():
        step(None, (0, s_b, lower))

    @pl.when(qi >= 1)
    def _():
        n_far = 2 * qi - 1
        step((2 * qi - 1, s_a), (2 * qi, s_b, lower))
        step((0, s_b), (2 * qi - 1, s_a, before))

        def pair(jj, carry):
            j = 2 * jj
            step((j + 1, s_a), (j, s_b, plain))
            step((j + 2, s_b), (j + 1, s_a, plain))
            return carry

        lax.fori_loop(0, n_far // 2, pair, 0)
        step(None, (n_far - 1, s_b, plain))

    lm = lam_ref[...]
    lam = (jnp.exp(jnp.sum(lm[0:1] * lm[1:2], keepdims=True))
           - jnp.exp(jnp.sum(lm[2:3] * lm[3:4], keepdims=True)) + lam_init)
    a = jnp.concatenate([acc[ci] for ci in range(acc.shape[0])], axis=1)
    o = a[:DA_V_DIM, :] / a[DA_V_DIM:DA_V_DIM + 1, :]
    out = o[:, :tq] - lam * o[:, tq:]
    out = out * lax.rsqrt(jnp.mean(out * out, axis=0, keepdims=True) + NORM_EPS) * g_ref[...]
    o_ref[0, q_rows, :] = (out * (1.0 - lam_init)).T.astype(o_ref.dtype)


def diff_attention(proj3, bias_tiles, lam_params, subln_g, lam_init):
    b, s, _ = proj3.shape
    t = bias_tiles.shape[2]
    chunk = min(ATTN_CHUNK, t)
    nch = 4 * t // chunk
    seq = lambda col0: pl.BlockSpec((1, s, DA_V_DIM), lambda bi, h: (bi, 0, col0 + h))
    return pl.pallas_call(
        functools.partial(_attn_kernel, t=t, lam_init=lam_init),
        grid=(b, DA_HEADS),
        in_specs=[
            seq(0), seq(DA_HEADS), seq(2 * DA_HEADS),
            pl.BlockSpec((1, 2, t, 2 * t), lambda bi, h: (h, 0, 0, 0)),
            pl.BlockSpec((4, DA_HEAD_DIM), lambda bi, h: (0, 0)),
            pl.BlockSpec((DA_V_DIM, 1), lambda bi, h: (0, 0)),
        ],
        out_specs=pl.BlockSpec((1, s, DA_V_DIM), lambda bi, h: (bi, 0, h)),
        out_shape=jax.ShapeDtypeStruct((b, s, DA_WIDTH), BF16),
        scratch_shapes=[pltpu.VMEM((s // t, DA_V_DIM + ONES_ROWS, t), BF16),
                        pltpu.VMEM((nch, 1, chunk), F32),
                        pltpu.VMEM((nch, DA_V_DIM + ONES_ROWS, chunk), F32),
                        pltpu.VMEM((nch, t, chunk), F32), pltpu.VMEM((nch, t, chunk), F32)],
        compiler_params=_cparams(("parallel", "parallel")),
        name="diff_attention",
    )(proj3, proj3, proj3, bias_tiles, lam_params, subln_g.reshape(DA_V_DIM, 1))


def _pool_kernel(p_ref, w_ref, scale_ref, o_ref, halo, *, t):
    si = pl.program_id(1)

    @pl.when(si == 0)
    def _():
        halo[...] = jnp.zeros(halo.shape, F32)

    p = p_ref[0].astype(F32)
    ext = jnp.concatenate([halo[...], p], axis=0)
    halo[...] = p[t - POOL_HALO:, :]
    sums = {1: ext}
    w = 1
    while w < POOL_WINDOWS[-1]:
        sums[2 * w] = sums[w] + pltpu.roll(sums[w], w, 0)
        w *= 2
    lane = lax.broadcasted_iota(jnp.int32, (t, POOL_WIDTH), 1)
    pos = (si * t + lax.broadcasted_iota(jnp.int32, (t, POOL_WIDTH), 0) + 1).astype(F32)
    wsum = sums[POOL_WINDOWS[-1]][POOL_HALO:, :]
    cnt = jnp.minimum(pos, float(POOL_WINDOWS[-1]))
    for gi in range(len(POOL_WINDOWS) - 2, -1, -1):
        in_group = lane < (gi + 1) * POOL_GROUP_DIM
        wsum = jnp.where(in_group, sums[POOL_WINDOWS[gi]][POOL_HALO:, :], wsum)
        cnt = jnp.where(in_group, jnp.minimum(pos, float(POOL_WINDOWS[gi])), cnt)
    pooled = wsum / cnt - p
    mixed = _dot(pooled.astype(BF16), w_ref[...])
    o_ref[0] = (mixed * scale_ref[...]).astype(o_ref.dtype)


def pool_mixer(proj3, pool_w, pool_scale, t=512):
    b, s, _ = proj3.shape
    t = min(t, s)
    g = len(POOL_WINDOWS)
    wbd = (jnp.eye(g, dtype=F32)[:, None, :, None] * pool_w.astype(F32)[:, :, None, :]).reshape(
        POOL_WIDTH, POOL_WIDTH).astype(BF16)
    return pl.pallas_call(
        functools.partial(_pool_kernel, t=t),
        grid=(b, s // t),
        in_specs=[pl.BlockSpec((1, t, POOL_WIDTH), lambda bi, si: (bi, si, POOL_COL_BLOCK)),
                  pl.BlockSpec((POOL_WIDTH, POOL_WIDTH), lambda bi, si: (0, 0)),
                  pl.BlockSpec((1, POOL_WIDTH), lambda bi, si: (0, 0))],
        out_specs=pl.BlockSpec((1, t, POOL_WIDTH), lambda bi, si: (bi, si, 0)),
        out_shape=jax.ShapeDtypeStruct((b, s, POOL_WIDTH), BF16),
        scratch_shapes=[pltpu.VMEM((POOL_HALO, POOL_WIDTH), F32)],
        compiler_params=_cparams(("parallel", "arbitrary")),
        name="pool_mixer",
    )(proj3, wbd, pool_scale.reshape(1, POOL_WIDTH))


def _retention_tables(s, t):
    d, hn, c = RET_HEAD_DIM, RET_HEADS, RET_CHUNK
    half = d // 2
    inv = 10000.0 ** (-jnp.linspace(0.0, 1.0, half, dtype=F32))
    ang = jnp.arange(s)[:, None].astype(F32) * inv[None, :]
    cos, sin = jnp.cos(ang), jnp.sin(ang)
    cos_t = jnp.tile(jnp.concatenate([cos, cos], axis=-1), (1, hn))
    sin_t = jnp.tile(jnp.concatenate([-sin, sin], axis=-1), (1, hn))
    log_gamma = jnp.log(1.0 - 2.0 ** (-5.0 - jnp.arange(hn, dtype=F32)))
    idx = jnp.arange(c, dtype=F32)
    rel = idx[:, None] - idx[None, :]
    intra = jnp.where(rel >= 0, jnp.exp(log_gamma[:, None, None] * jnp.maximum(rel, 0.0)), 0.0)
    q_decay = jnp.exp(log_gamma[:, None] * (idx + 1.0))
    k_decay = jnp.exp(log_gamma[:, None] * (c - 1.0 - idx))
    chunk_decay = jnp.exp(log_gamma * c)
    lanes = lambda a: jnp.tile(jnp.repeat(a.T, d, axis=1), (t // c, 1))
    l = jnp.arange(hn * d)
    partner = jnp.where(l % d < half, l + half, l - half)
    swap = (l[:, None] == partner[None, :]).astype(BF16)
    same = (jnp.arange(2 * d)[:, None] // d) == (jnp.arange(2 * d)[None, :] // d)
    decay_bd = jnp.where(same[None], jnp.repeat(chunk_decay, d).reshape(hn // 2, 2 * d, 1), 0.0)
    avg = jnp.where(same, 1.0 / d, 0.0).astype(BF16)
    return cos_t, sin_t, intra, lanes(q_decay), lanes(k_decay), swap, decay_bd.astype(F32), avg


def _ret_kernel(q_ref, k_ref, v_ref, g_ref, cos_ref, sin_ref, intra_ref, qd_ref, kd_ref, swap_ref, decay_ref,
                avg_ref, gn_ref, o_ref, state, *, t):
    si = pl.program_id(1)
    d, c = RET_HEAD_DIM, RET_CHUNK
    pw = 2 * d

    @pl.when(si == 0)
    def _():
        state[...] = jnp.zeros(state.shape, F32)

    def rotate(x_ref):
        x = x_ref[0]
        return x.astype(F32) * cos_ref[...] + _dot(x, swap_ref[...]) * sin_ref[...]

    def lane_mean(a):
        hi = a.astype(BF16)
        lo = (a - hi.astype(F32)).astype(BF16)
        return _dot(hi, avg_ref[...]) + _dot(lo, avg_ref[...])

    q = rotate(q_ref)
    k = rotate(k_ref) * (d ** -0.5)
    qs = (q * qd_ref[...]).astype(BF16)
    ks = (k * kd_ref[...]).astype(BF16)
    qb = q.astype(BF16)
    kb = k.astype(BF16)
    v = v_ref[0]
    first = lax.broadcasted_iota(jnp.int32, (c, pw), 1) < d
    same = ((lax.broadcasted_iota(jnp.int32, (pw, pw), 0) < d)
            == (lax.broadcasted_iota(jnp.int32, (pw, pw), 1) < d))
    zero = jnp.zeros((c, pw), BF16)

    chunks = [slice(ci * c, (ci + 1) * c) for ci in range(t // c)]
    pairs = [slice(p * pw, (p + 1) * pw) for p in range(RET_HEADS // 2)]
    scores = {}
    for p, cols in enumerate(pairs):
        for ci, rows in enumerate(chunks):
            qp, kp = qb[rows, cols], kb[rows, cols]
            scores[p, ci] = ((_nt_dot(jnp.where(first, qp, zero), kp) * intra_ref[2 * p]).astype(BF16),
                             (_nt_dot(jnp.where(first, zero, qp), kp) * intra_ref[2 * p + 1]).astype(BF16))
    intra_out, increment = {}, {}
    for p, cols in enumerate(pairs):
        for ci, rows in enumerate(chunks):
            vp = v[rows, cols]
            intra_out[p, ci] = jnp.where(first, _dot(scores[p, ci][0], vp), _dot(scores[p, ci][1], vp))
            increment[p, ci] = jnp.where(same, _dot(ks[rows, cols].T, vp), 0.0)
    ys = {}
    for p, cols in enumerate(pairs):
        st = state[p]
        for ci, rows in enumerate(chunks):
            ys[p, ci] = intra_out[p, ci] + _dot(qs[rows, cols], st.astype(BF16))
            st = st * decay_ref[p] + increment[p, ci]
        state[p] = st
    y = jnp.concatenate([jnp.concatenate([ys[p, ci] for ci in range(len(chunks))], axis=0)
                         for p in range(len(pairs))], axis=1)
    mean = jnp.concatenate([lane_mean(y[:, cols]) for cols in pairs], axis=1)
    yc = y - mean
    var = jnp.concatenate([lane_mean((yc * yc)[:, cols]) for cols in pairs], axis=1)
    yn = yc * lax.rsqrt(var + NORM_EPS)
    gate = _silu(g_ref[0].astype(F32))
    o_ref[0] = (gate * (yn * gn_ref[...])).astype(o_ref.dtype)


def retention(proj3, gn_g, t=512):
    b, s, _ = proj3.shape
    t = min(t, s)
    cos_t, sin_t, intra, qd, kd, swap, decay_bd, avg = _retention_tables(s, t)
    pw = 2 * RET_HEAD_DIM
    col = lambda j: pl.BlockSpec((1, t, RET_WIDTH), lambda bi, si: (bi, si, RET_COL_BLOCK + j))
    const2 = lambda shape: pl.BlockSpec(shape, lambda bi, si: (0, 0))
    const3 = lambda shape: pl.BlockSpec(shape, lambda bi, si: (0, 0, 0))
    return pl.pallas_call(
        functools.partial(_ret_kernel, t=t),
        grid=(b, s // t),
        in_specs=[col(0), col(1), col(2), col(3),
                  pl.BlockSpec((t, RET_WIDTH), lambda bi, si: (si, 0)),
                  pl.BlockSpec((t, RET_WIDTH), lambda bi, si: (si, 0)),
                  const3((RET_HEADS, RET_CHUNK, RET_CHUNK)),
                  const2((t, RET_WIDTH)), const2((t, RET_WIDTH)), const2((RET_WIDTH, RET_WIDTH)),
                  const3((RET_HEADS // 2, pw, pw)), const2((pw, pw)), const2((1, RET_WIDTH))],
        out_specs=pl.BlockSpec((1, t, RET_WIDTH), lambda bi, si: (bi, si, 0)),
        out_shape=jax.ShapeDtypeStruct((b, s, RET_WIDTH), BF16),
        scratch_shapes=[pltpu.VMEM((RET_HEADS // 2, pw, pw), F32)],
        compiler_params=_cparams(("parallel", "arbitrary")),
        name="retention",
    )(proj3, proj3, proj3, proj3, cos_t, sin_t, intra, qd, kd, swap, decay_bd, avg, gn_g.reshape(1, RET_WIDTH))


def _mix_ln(da_ref, pool_ref, ret_ref, x_ref, w_ref, g_ref, b_ref):
    e0, e1 = DA_WIDTH, DA_WIDTH + POOL_WIDTH
    mix = (_dot(da_ref[...], w_ref[0:e0, :]) + _dot(pool_ref[...], w_ref[e0:e1, :])
           + _dot(ret_ref[...], w_ref[e1:, :]))
    return _layer_norm(ALPHA * x_ref[...] + mix, g_ref[...], b_ref[...])


def _outproj_kernel(da_ref, pool_ref, ret_ref, x_ref, w_ref, g_ref, b_ref, o_ref):
    o_ref[...] = _mix_ln(da_ref, pool_ref, ret_ref, x_ref, w_ref, g_ref, b_ref)


def out_proj_ln(y_da, y_pool, y_ret, x2d, w, g, b, tm=512):
    n, dm = x2d.shape
    tm = min(tm, n)
    row = lambda width: pl.BlockSpec((tm, width), lambda i: (i, 0))
    const = lambda shape: pl.BlockSpec(shape, lambda i: (0, 0))
    return pl.pallas_call(
        _outproj_kernel,
        grid=(n // tm,),
        in_specs=[row(DA_WIDTH), row(POOL_WIDTH), row(RET_WIDTH), row(dm),
                  const(w.shape), const((1, dm)), const((1, dm))],
        out_specs=row(dm),
        out_shape=jax.ShapeDtypeStruct((n, dm), F32),
        compiler_params=_cparams(("parallel",)),
        name="out_proj_ln",
    )(y_da, y_pool, y_ret, x2d, w, g.reshape(1, dm), b.reshape(1, dm))


def _mix_ffn_kernel(da_ref, pool_ref, ret_ref, x_ref, wo_ref, g1_ref, b1_ref, wg_ref, wu_ref, wd_ref,
                    g2_ref, b2_ref, o_ref, *, tf):
    x = _mix_ln(da_ref, pool_ref, ret_ref, x_ref, wo_ref, g1_ref, b1_ref)
    xb = x.astype(BF16)
    acc = jnp.zeros(x.shape, F32)
    for f0 in range(0, wg_ref.shape[1], tf):
        hidden = _silu(_dot(xb, wg_ref[:, f0:f0 + tf])) * _dot(xb, wu_ref[:, f0:f0 + tf])
        acc = acc + _dot(hidden.astype(BF16), wd_ref[f0:f0 + tf, :])
    o_ref[...] = _layer_norm(ALPHA * x + acc, g2_ref[...], b2_ref[...])


def out_proj_ffn_ln(y_da, y_pool, y_ret, x2d, wo, g1, b1, wg, wu, wd, g2, b2, tm=512, tf=256):
    n, dm = x2d.shape
    tm = min(tm, n)
    row = lambda width: pl.BlockSpec((tm, width), lambda i: (i, 0))
    resident = lambda shape: pl.BlockSpec(shape, lambda i: (0, 0), pipeline_mode=pl.Buffered(1))
    vec = lambda a: a.reshape(1, dm)
    return pl.pallas_call(
        functools.partial(_mix_ffn_kernel, tf=tf),
        grid=(n // tm,),
        in_specs=[row(DA_WIDTH), row(POOL_WIDTH), row(RET_WIDTH), row(dm),
                  resident(wo.shape), resident((1, dm)), resident((1, dm)),
                  resident(wg.shape), resident(wu.shape), resident(wd.shape),
                  resident((1, dm)), resident((1, dm))],
        out_specs=row(dm),
        out_shape=jax.ShapeDtypeStruct((n, dm), F32),
        compiler_params=_cparams(("parallel",)),
        name="out_proj_ffn_ln",
    )(y_da, y_pool, y_ret, x2d, wo, vec(g1), vec(b1), wg, wu, wd, vec(g2), vec(b2))


ROUTE_ROWS = 8
LANES = 128


def _router_kernel(x_ref, w_ref, tri_ref, route_ref, route_t_ref, cnt_ref, carry, *, t):
    i = pl.program_id(0)

    @pl.when(i == 0)
    def _():
        carry[...] = jnp.zeros(carry.shape, F32)

    x = x_ref[...]
    xh = x.astype(BF16)
    xl = (x - xh.astype(F32)).astype(BF16)
    w = w_ref[...]
    wh = w.astype(BF16)
    wl = (w - wh.astype(F32)).astype(BF16)
    logits = _nt_dot(wh, xh) + _nt_dot(wl, xh) + _nt_dot(wh, xl)

    row = lax.broadcasted_iota(jnp.int32, logits.shape, 0)
    v0 = jnp.max(logits, axis=0, keepdims=True)
    i0 = jnp.min(jnp.where(logits == v0, row, N_EXPERTS), axis=0, keepdims=True)
    rest = jnp.where(row == i0, -jnp.inf, logits)
    v1 = jnp.max(rest, axis=0, keepdims=True)
    i1 = jnp.min(jnp.where(rest == v1, row, N_EXPERTS), axis=0, keepdims=True)
    ex = jnp.exp(v1 - v0)
    gate0 = 1.0 / (1.0 + ex)
    gate1 = ex / (1.0 + ex)

    oh0 = row == i0
    oh1 = row == i1
    member = jnp.where(oh0 | oh1, 1.0, 0.0)
    before = _dot(member.astype(BF16), tri_ref[...]) + carry[:, 0:1]
    rank0 = jnp.sum(jnp.where(oh0, before, 0.0), axis=0, keepdims=True)
    rank1 = jnp.sum(jnp.where(oh1, before, 0.0), axis=0, keepdims=True)
    carry[...] = carry[...] + jnp.sum(member, axis=1, keepdims=True)
    cnt_ref[...] = carry[...]

    route = jnp.concatenate([i0.astype(F32), i1.astype(F32), rank0, rank1, gate0, gate1,
                             jnp.zeros((2, t), F32)], axis=0)
    route_ref[...] = route
    padded = jnp.concatenate([route, jnp.zeros((LANES - ROUTE_ROWS, t), F32)], axis=0)
    route_t_ref[...] = padded.T


def route_tokens(x2d, router_w, t=1024):
    n, dm = x2d.shape
    t = min(t, n)
    tri = (jnp.arange(t)[:, None] < jnp.arange(t)[None, :]).astype(BF16)
    return pl.pallas_call(
        functools.partial(_router_kernel, t=t),
        grid=(n // t,),
        in_specs=[pl.BlockSpec((t, dm), lambda i: (i, 0)),
                  pl.BlockSpec((N_EXPERTS, dm), lambda i: (0, 0)),
                  pl.BlockSpec((t, t), lambda i: (0, 0))],
        out_specs=[pl.BlockSpec((ROUTE_ROWS, t), lambda i: (0, i)),
                   pl.BlockSpec((t, LANES), lambda i: (i, 0)),
                   pl.BlockSpec((N_EXPERTS, LANES), lambda i: (0, 0))],
        out_shape=[jax.ShapeDtypeStruct((ROUTE_ROWS, n), F32),
                   jax.ShapeDtypeStruct((n, LANES), F32),
                   jax.ShapeDtypeStruct((N_EXPERTS, LANES), F32)],
        scratch_shapes=[pltpu.VMEM((N_EXPERTS, LANES), F32)],
        compiler_params=_cparams(("arbitrary",)),
        name="route_tokens",
    )(x2d, router_w.T, tri)


def _dest_kernel(start_ref, route_ref, dest_ref):
    r = route_ref[...]
    for k in range(2):
        e = r[k:k + 1, :].astype(jnp.int32)
        base = jnp.zeros(e.shape, jnp.int32)
        for ei in range(N_EXPERTS):
            base = jnp.where(e == ei, start_ref[ei], base)
        dest_ref[k:k + 1, :] = base + r[2 + k:3 + k, :].astype(jnp.int32)


def slot_of_assignment(pad_start, route, t=1024):
    n = route.shape[1]
    t = min(t, n)
    grid_spec = pltpu.PrefetchScalarGridSpec(
        num_scalar_prefetch=1,
        grid=(n // t,),
        in_specs=[pl.BlockSpec((ROUTE_ROWS, t), lambda i, ps: (0, i))],
        out_specs=pl.BlockSpec((2, t), lambda i, ps: (0, i)),
    )
    return pl.pallas_call(
        _dest_kernel,
        grid_spec=grid_spec,
        out_shape=jax.ShapeDtypeStruct((2, n), jnp.int32),
        compiler_params=_cparams(("parallel",)),
        name="slot_of_assignment",
    )(pad_start, route)


def _row_copy(src, src_row, dst, dst_row, sem):
    return pltpu.make_async_copy(src.at[pl.ds(src_row, 1)], dst.at[pl.ds(dst_row, 1)], sem)


ISSUE_UNROLL = 8


def _dispatch_kernel(pad_lo_ref, pad_n_ref, d0_ref, d1_ref, x_ref, slots_out, zrow, sem, zsem, *, t):
    @pl.when(pl.program_id(0) == 0)
    def _():
        zrow[...] = jnp.zeros(zrow.shape, zrow.dtype)
        for e in range(N_EXPERTS):
            lo = pad_lo_ref[e]

            def zero_start(r, carry):
                _row_copy(zrow, 0, slots_out, lo + r, zsem).start()
                return carry

            def zero_wait(r, carry):
                _row_copy(zrow, 0, slots_out, lo + r, zsem).wait()
                return carry

            lax.fori_loop(0, pad_n_ref[e], zero_start, 0)
            lax.fori_loop(0, pad_n_ref[e], zero_wait, 0)

        tail_lo = pad_lo_ref[N_EXPERTS]
        group = zrow.shape[0]

        def tail_copy(r):
            dst = slots_out.at[pl.ds(pl.multiple_of(tail_lo + r * group, group), group)]
            return pltpu.make_async_copy(zrow, dst, zsem)

        def tail_start(r, carry):
            tail_copy(r).start()
            return carry

        def tail_wait(r, carry):
            tail_copy(r).wait()
            return carry

        lax.fori_loop(0, pad_n_ref[N_EXPERTS], tail_start, 0)
        lax.fori_loop(0, pad_n_ref[N_EXPERTS], tail_wait, 0)

    def issue(j, carry):
        base = pl.multiple_of(j * ISSUE_UNROLL, ISSUE_UNROLL)
        group = x_ref.at[pl.ds(base, ISSUE_UNROLL)]
        for u in range(ISSUE_UNROLL):
            for k, d_ref in enumerate((d0_ref, d1_ref)):
                _row_copy(group, u, slots_out, d_ref[base + u], sem).start()
        return carry

    lax.fori_loop(0, t // ISSUE_UNROLL, issue, 0)
    for k in range(2):
        pltpu.make_async_copy(x_ref, slots_out.at[pl.ds(0, t)], sem).wait()


def dispatch_rows(pad_lo, pad_n, dest, x2d, n_slots, t=1024):
    n, dm = x2d.shape
    t = min(t, n)
    grid_spec = pltpu.PrefetchScalarGridSpec(
        num_scalar_prefetch=2,
        grid=(n // t,),
        in_specs=[pl.BlockSpec((t,), lambda i, lo, cnt: (i,), memory_space=pltpu.SMEM),
                  pl.BlockSpec((t,), lambda i, lo, cnt: (i,), memory_space=pltpu.SMEM),
                  pl.BlockSpec((t, dm), lambda i, lo, cnt: (i, 0))],
        out_specs=pl.BlockSpec(memory_space=pl.ANY),
        scratch_shapes=[pltpu.VMEM((8, dm), x2d.dtype), pltpu.SemaphoreType.DMA, pltpu.SemaphoreType.DMA],
    )
    return pl.pallas_call(
        functools.partial(_dispatch_kernel, t=t),
        grid_spec=grid_spec,
        out_shape=jax.ShapeDtypeStruct((n_slots, dm), x2d.dtype),
        compiler_params=_cparams(("arbitrary",)),
        name="dispatch_rows",
    )(pad_lo, pad_n, dest[0], dest[1], x2d)


def _expert_kernel(be_ref, nv_ref, x_ref, wg_ref, wu_ref, wd_ref, o_ref, xb):
    blk = pl.program_id(0)
    f = pl.program_id(1)
    valid = blk < nv_ref[0]

    @pl.when(f == 0)
    def _():
        o_ref[...] = jnp.zeros(o_ref.shape, o_ref.dtype)

    @pl.when(valid & (f == 0))
    def _():
        xb[...] = x_ref[...].astype(BF16)

    @pl.when(valid)
    def _():
        hidden = (_silu(_dot(xb[...], wg_ref[0].astype(BF16)))
                  * _dot(xb[...], wu_ref[0].astype(BF16)))
        o_ref[...] += _dot(hidden.astype(BF16), wd_ref[0].astype(BF16))


def expert_swiglu(block_e, n_valid, slots, wg, wu, wd, bm, tf=512):
    n_slots, dm = slots.shape
    edim = wg.shape[2]
    grid_spec = pltpu.PrefetchScalarGridSpec(
        num_scalar_prefetch=2,
        grid=(n_slots // bm, edim // tf),
        in_specs=[pl.BlockSpec((bm, dm), lambda b, f, be, nv: (jnp.minimum(b, nv[0] - 1), 0)),
                  pl.BlockSpec((1, dm, tf), lambda b, f, be, nv: (be[b], 0, f)),
                  pl.BlockSpec((1, dm, tf), lambda b, f, be, nv: (be[b], 0, f)),
                  pl.BlockSpec((1, tf, dm), lambda b, f, be, nv: (be[b], f, 0))],
        out_specs=pl.BlockSpec((bm, dm), lambda b, f, be, nv: (b, 0)),
        scratch_shapes=[pltpu.VMEM((bm, dm), BF16)],
    )
    return pl.pallas_call(
        _expert_kernel,
        grid_spec=grid_spec,
        out_shape=jax.ShapeDtypeStruct((n_slots, dm), F32),
        compiler_params=_cparams(("parallel", "arbitrary")),
        name="expert_swiglu",
    )(block_e, n_valid, slots, wg, wu, wd)


def _combine_kernel(d0_ref, d1_ref, d0_next_ref, d1_next_ref, rt_ref, x_ref, y_hbm, g_ref, b_ref, o_ref,
                    rows, sem, *, t):
    i = pl.program_id(0)
    cur = i % 2

    def gather(d_refs, buf):
        def issue(j, carry):
            for u in range(ISSUE_UNROLL):
                for k in range(2):
                    _row_copy(y_hbm, d_refs[k][j * ISSUE_UNROLL + u], rows.at[buf, k, j], u, sem.at[buf]).start()
            return carry

        lax.fori_loop(0, t // ISSUE_UNROLL, issue, 0)

    @pl.when(i == 0)
    def _():
        gather((d0_ref, d1_ref), 0)

    @pl.when(i + 1 < pl.num_programs(0))
    def _():
        gather((d0_next_ref, d1_next_ref), 1 - cur)

    for k in range(2):
        pltpu.make_async_copy(y_hbm.at[pl.ds(0, t)], o_ref, sem.at[cur]).wait()

    rt = rt_ref[...]
    dm = o_ref.shape[1]
    mixed = rt[:, 4:5] * rows[cur, 0].reshape(t, dm) + rt[:, 5:6] * rows[cur, 1].reshape(t, dm)
    o_ref[...] = _layer_norm(ALPHA * x_ref[...] + mixed, g_ref[...], b_ref[...])


def combine_ln(dest, route_t, x2d, y, g, b, t=512):
    n, dm = x2d.shape
    t = min(t, n)
    last = n // t - 1
    return pl.pallas_call(
        functools.partial(_combine_kernel, t=t),
        grid=(n // t,),
        in_specs=[pl.BlockSpec((t,), lambda i: (i,), memory_space=pltpu.SMEM),
                  pl.BlockSpec((t,), lambda i: (i,), memory_space=pltpu.SMEM),
                  pl.BlockSpec((t,), lambda i: (jnp.minimum(i + 1, last),), memory_space=pltpu.SMEM),
                  pl.BlockSpec((t,), lambda i: (jnp.minimum(i + 1, last),), memory_space=pltpu.SMEM),
                  pl.BlockSpec((t, LANES), lambda i: (i, 0)),
                  pl.BlockSpec((t, dm), lambda i: (i, 0)),
                  pl.BlockSpec(memory_space=pl.ANY),
                  pl.BlockSpec((1, dm), lambda i: (0, 0)),
                  pl.BlockSpec((1, dm), lambda i: (0, 0))],
        out_specs=pl.BlockSpec((t, dm), lambda i: (i, 0)),
        out_shape=jax.ShapeDtypeStruct((n, dm), F32),
        scratch_shapes=[pltpu.VMEM((2, 2, t // ISSUE_UNROLL, ISSUE_UNROLL, dm), F32),
                        pltpu.SemaphoreType.DMA((2,))],
        compiler_params=_cparams(("arbitrary",)),
        name="combine_ln",
    )(dest[0], dest[1], dest[0], dest[1], route_t, x2d, y, g.reshape(1, dm), b.reshape(1, dm))


def moe_ln(x2d, router_w, wg, wu, wd, g, b, bm=1024):
    n, _ = x2d.shape
    bm = min(bm, n)
    route, route_t, cnt = route_tokens(x2d, router_w)
    counts = cnt[:, 0].astype(jnp.int32)
    padded = (counts + bm - 1) // bm * bm
    pad_end = jnp.cumsum(padded)
    pad_start = (pad_end - padded).astype(jnp.int32)
    n_blocks = 2 * n // bm + N_EXPERTS
    n_slots = n_blocks * bm
    block_start = jnp.arange(n_blocks, dtype=jnp.int32) * bm
    block_e = jnp.minimum(jnp.sum(pad_end[None, :] <= block_start[:, None], axis=1),
                          N_EXPERTS - 1).astype(jnp.int32)
    n_valid = (pad_end[-1:] // bm).astype(jnp.int32)
    dest = slot_of_assignment(pad_start, route)
    pad_lo = jnp.concatenate([pad_start + counts, pad_end[-1:]]).astype(jnp.int32)
    pad_n = jnp.concatenate([padded - counts, (n_slots - pad_end[-1:]) // 8]).astype(jnp.int32)
    slots = dispatch_rows(pad_lo, pad_n, dest, x2d, n_slots)
    y = expert_swiglu(block_e, n_valid, slots, wg, wu, wd, bm)
    return combine_ln(dest, route_t, x2d, y, g, b)


def kernel(x, rel_bias, w_in, diff_lambda, diff_subln_g, pool_w, pool_scale, ret_gn_g, w_out,
           ln1_g, ln1_b, ln2_g, ln2_b, ffn_w_gate, ffn_w_up, ffn_w_down,
           router_w, moe_w_gate, moe_w_up, moe_w_down):
    bsz, seq, dm = x.shape
    x2d = x.reshape(bsz * seq, dm)
    bias_tiles = attn_bias_tiles(rel_bias, min(ATTN_BLOCK, seq))
    col_scale = jnp.where(jnp.arange(w_in.shape[2]) < DA_WIDTH, Q_SCALE, 1.0).astype(F32)
    for l in range(DEPTH):
        lam_init = 0.8 - 0.6 * math.exp(-0.3 * l)
        proj = in_proj(x2d, (w_in[l] * col_scale).astype(BF16)).reshape(bsz, seq, -1)
        y_da = diff_attention(proj, bias_tiles, diff_lambda[l], diff_subln_g[l], lam_init)
        y_pool = pool_mixer(proj, pool_w[l], pool_scale[l])
        y_ret = retention(proj, ret_gn_g[l])
        flat = lambda a: a.reshape(bsz * seq, -1)
        mixed = (flat(y_da), flat(y_pool), flat(y_ret), x2d, w_out[l].astype(BF16), ln1_g[l], ln1_b[l])
        j = l // 2
        if l % 2 == 0:
            x2d = out_proj_ffn_ln(*mixed, ffn_w_gate[j].astype(BF16), ffn_w_up[j].astype(BF16),
                                  ffn_w_down[j].astype(BF16), ln2_g[l], ln2_b[l])
        else:
            x2d = out_proj_ln(*mixed)
            x2d = moe_ln(x2d, router_w[j], moe_w_gate[j], moe_w_up[j], moe_w_down[j], ln2_g[l], ln2_b[l])
    return x2d.reshape(bsz, seq, dm)
```

```python
import functools
import math

import jax
import jax.numpy as jnp
from jax import lax
from jax.experimental import pallas as pl
from jax.experimental.pallas import tpu as pltpu

F32 = jnp.float32
BF16 = jnp.bfloat16

DEPTH = 2
DA_HEAD_DIM = 64
DA_V_DIM = 128
DA_HEADS = 4
DA_WIDTH = 512
POOL_WIDTH = 256
POOL_WINDOWS = (2, 4, 8, 16)
POOL_GROUP_DIM = 64
POOL_HALO = 16
RET_WIDTH = 256
RET_HEAD_DIM = 64
RET_HEADS = 4
RET_CHUNK = 128
REL_BUCKETS = 32
REL_MAX_DIST = 128
N_EXPERTS = 8
ALPHA = (2 * DEPTH) ** 0.25
LN_EPS = 1e-5
NORM_EPS = 1e-6
NEG_BIG = -1e30

POOL_COL_BLOCK = 3 * DA_WIDTH // POOL_WIDTH
RET_COL_BLOCK = POOL_COL_BLOCK + 1

VMEM_LIMIT = 56 * 1024 * 1024


def _cparams(sem, vmem=VMEM_LIMIT):
    return pltpu.CompilerParams(dimension_semantics=sem, vmem_limit_bytes=vmem)


def _nt_dot(a, b):
    return lax.dot_general(a, b, (((1,), (1,)), ((), ())), preferred_element_type=F32)


def _dot(a, b):
    return jnp.dot(a, b, preferred_element_type=F32)


def _layer_norm(z, g, b):
    mu = jnp.mean(z, axis=-1, keepdims=True)
    zc = z - mu
    var = jnp.mean(zc * zc, axis=-1, keepdims=True)
    return zc * lax.rsqrt(var + LN_EPS) * g + b


def _silu(x):
    return x / (1.0 + jnp.exp(-x))


def _inproj_kernel(x_ref, w_ref, o_ref, *, tn):
    xb = x_ref[...].astype(BF16)
    for j in range(0, w_ref.shape[1], tn):
        o_ref[:, j:j + tn] = _dot(xb, w_ref[:, j:j + tn]).astype(o_ref.dtype)


def in_proj(x2d, w, tm=512, tn=256):
    n, k = x2d.shape
    m = w.shape[1]
    tm = min(tm, n)
    return pl.pallas_call(
        functools.partial(_inproj_kernel, tn=tn),
        grid=(n // tm,),
        in_specs=[pl.BlockSpec((tm, k), lambda i: (i, 0)),
                  pl.BlockSpec((k, m), lambda i: (0, 0))],
        out_specs=pl.BlockSpec((tm, m), lambda i: (i, 0)),
        out_shape=jax.ShapeDtypeStruct((n, m), BF16),
        compiler_params=_cparams(("parallel",)),
        name="in_proj",
    )(x2d, w)


def _t5_bucket(dist):
    n = jnp.maximum(dist, 0)
    max_exact = REL_BUCKETS // 2
    nf = jnp.maximum(n, 1).astype(F32)
    large = max_exact + (jnp.log(nf / max_exact) / math.log(REL_MAX_DIST / max_exact)
                         * (REL_BUCKETS - max_exact)).astype(jnp.int32)
    large = jnp.minimum(large, REL_BUCKETS - 1)
    return jnp.where(n < max_exact, n, large)


LOG2E = math.log2(math.e)
Q_SCALE = DA_HEAD_DIM ** -0.5 * LOG2E
ATTN_BLOCK = 512
ATTN_CHUNK = 256
ONES_ROWS = 16


def attn_bias_tiles(rel_bias, t):
    table = rel_bias.astype(F32).reshape(REL_BUCKETS, DA_HEADS * 2)
    vec = table[_t5_bucket(jnp.arange(2 * t))]
    far = table[_t5_bucket(jnp.array(2 * t))]
    vec = ((vec - far[None, :]) * LOG2E).T
    masked = jnp.full((DA_HEADS * 2, t), NEG_BIG, F32)
    u_diag = jnp.concatenate([vec[:, :t], masked], axis=1)
    u_prev = jnp.concatenate([vec[:, t:], vec[:, :t]], axis=1)

    def toeplitz(u):
        skew = jnp.tile(u, (1, t))[:, :t * (2 * t - 1)].reshape(-1, t, 2 * t - 1)
        return skew[:, :, :t]

    def per_head(a):
        return a.reshape(DA_HEADS, 2, t, t).transpose(0, 2, 1, 3).reshape(DA_HEADS, t, 2 * t)

    return jnp.stack([per_head(toeplitz(u_diag)), per_head(toeplitz(u_prev))], axis=1)


def _attn_kernel(q_ref, k_ref, v_ref, bias_ref, lam_ref, g_ref, o_ref, vt, m_s, acc, s_a, s_b, *, t, lam_init):
    nk = vt.shape[0]
    ones = jnp.ones((ONES_ROWS, t), BF16)
    for ki in range(nk):
        v_t = v_ref[0, ki * t:(ki + 1) * t, :].astype(F32).T.astype(BF16)
        vt[ki] = jnp.concatenate([v_t, ones], axis=0)

    def query_tile(qi, carry):
        _attn_query_tile(qi, q_ref, k_ref, bias_ref, lam_ref, g_ref, o_ref, vt, m_s, acc, s_a, s_b,
                         t=t, lam_init=lam_init)
        return carry

    lax.fori_loop(0, nk // 2, query_tile, 0)


def _attn_query_tile(qi, q_ref, k_ref, bias_ref, lam_ref, g_ref, o_ref, vt, m_s, acc, s_a, s_b, *, t, lam_init):
    tq = 2 * t
    q_rows = pl.ds(pl.multiple_of(qi * tq, tq), tq)
    q = q_ref[0, q_rows, :]
    lane = lax.broadcasted_iota(jnp.int32, q.shape, 1)
    zero = jnp.zeros_like(q)
    qcat = jnp.concatenate([jnp.where(lane < DA_HEAD_DIM, q, zero),
                            jnp.where(lane >= DA_HEAD_DIM, q, zero)], axis=0)
    m_s[...] = jnp.full(m_s.shape, NEG_BIG, F32)
    acc[...] = jnp.zeros(acc.shape, F32)

    chunk = min(ATTN_CHUNK, t)
    DIAG, PREV, FAR, SKIP = 0, 1, None, "skip"

    def step(nxt, cur):
        kb = None
        if nxt is not None:
            kb = k_ref[0, pl.ds(pl.multiple_of(nxt[0] * t, t), t), :]
        for c in range(0, 2 * tq, chunk):
            cols = slice(c, c + chunk)
            mp, qoff = divmod(c, tq)
            half, ioff = divmod(qoff, t)
            ci = c // chunk
            if nxt is not None and not (len(nxt) > 2 and nxt[2][half] == SKIP):
                nxt[1][ci] = _nt_dot(kb, qcat[cols, :])
            if cur is not None:
                ki, s_ref, kinds = cur
                kind = kinds[half]
                if kind == SKIP:
                    continue
                s = s_ref[ci]
                if kind is not FAR:
                    s = s + bias_ref[0, kind, :, mp * t + ioff:mp * t + ioff + chunk]
                m_prev = m_s[ci]
                m_new = jnp.maximum(m_prev, jnp.max(s, axis=0, keepdims=True))
                alpha = jnp.exp2(m_prev - m_new)
                p = jnp.exp2(s - m_new).astype(BF16)
                acc[ci] = alpha * acc[ci] + _dot(vt[ki], p)
                m_s[ci] = m_new

    upper = (SKIP, DIAG)
    lower = (DIAG, PREV)
    before = (PREV, FAR)
    plain = (FAR, FAR)
    step((2 * qi + 1, s_a, upper), None)
    step((2 * qi, s_b), (2 * qi + 1, s_a, upper))

    @pl.when(qi == 0)
    def _():
        step(None, (0, s_b, lower))

    @pl.when(qi >= 1)
    def _():
        n_far = 2 * qi - 1
        step((2 * qi - 1, s_a), (2 * qi, s_b, lower))
        step((0, s_b), (2 * qi - 1, s_a, before))

        def pair(jj, carry):
            j = 2 * jj
            step((j + 1, s_a), (j, s_b, plain))
            step((j + 2, s_b), (j + 1, s_a, plain))
            return carry

        lax.fori_loop(0, n_far // 2, pair, 0)
        step(None, (n_far - 1, s_b, plain))

    lm = lam_ref[...]
    lam = (jnp.exp(jnp.sum(lm[0:1] * lm[1:2], keepdims=True))
           - jnp.exp(jnp.sum(lm[2:3] * lm[3:4], keepdims=True)) + lam_init)
    a = jnp.concatenate([acc[ci] for ci in range(acc.shape[0])], axis=1)
    o = a[:DA_V_DIM, :] / a[DA_V_DIM:DA_V_DIM + 1, :]
    out = o[:, :tq] - lam * o[:, tq:]
    out = out * lax.rsqrt(jnp.mean(out * out, axis=0, keepdims=True) + NORM_EPS) * g_ref[...]
    o_ref[0, q_rows, :] = (out * (1.0 - lam_init)).T.astype(o_ref.dtype)


def diff_attention(proj3, bias_tiles, lam_params, subln_g, lam_init):
    b, s, _ = proj3.shape
    t = bias_tiles.shape[2]
    chunk = min(ATTN_CHUNK, t)
    nch = 4 * t // chunk
    seq = lambda col0: pl.BlockSpec((1, s, DA_V_DIM), lambda bi, h: (bi, 0, col0 + h))
    return pl.pallas_call(
        functools.partial(_attn_kernel, t=t, lam_init=lam_init),
        grid=(b, DA_HEADS),
        in_specs=[
            seq(0), seq(DA_HEADS), seq(2 * DA_HEADS),
            pl.BlockSpec((1, 2, t, 2 * t), lambda bi, h: (h, 0, 0, 0)),
            pl.BlockSpec((4, DA_HEAD_DIM), lambda bi, h: (0, 0)),
            pl.BlockSpec((DA_V_DIM, 1), lambda bi, h: (0, 0)),
        ],
        out_specs=pl.BlockSpec((1, s, DA_V_DIM), lambda bi, h: (bi, 0, h)),
        out_shape=jax.ShapeDtypeStruct((b, s, DA_WIDTH), BF16),
        scratch_shapes=[pltpu.VMEM((s // t, DA_V_DIM + ONES_ROWS, t), BF16),
                        pltpu.VMEM((nch, 1, chunk), F32),
                        pltpu.VMEM((nch, DA_V_DIM + ONES_ROWS, chunk), F32),
                        pltpu.VMEM((nch, t, chunk), F32), pltpu.VMEM((nch, t, chunk), F32)],
        compiler_params=_cparams(("parallel", "parallel")),
        name="diff_attention",
    )(proj3, proj3, proj3, bias_tiles, lam_params, subln_g.reshape(DA_V_DIM, 1))


def _pool_kernel(p_ref, w_ref, scale_ref, o_ref, halo, *, t):
    si = pl.program_id(1)

    @pl.when(si == 0)
    def _():
        halo[...] = jnp.zeros(halo.shape, F32)

    p = p_ref[0].astype(F32)
    ext = jnp.concatenate([halo[...], p], axis=0)
    halo[...] = p[t - POOL_HALO:, :]
    sums = {1: ext}
    w = 1
    while w < POOL_WINDOWS[-1]:
        sums[2 * w] = sums[w] + pltpu.roll(sums[w], w, 0)
        w *= 2
    lane = lax.broadcasted_iota(jnp.int32, (t, POOL_WIDTH), 1)
    pos = (si * t + lax.broadcasted_iota(jnp.int32, (t, POOL_WIDTH), 0) + 1).astype(F32)
    wsum = sums[POOL_WINDOWS[-1]][POOL_HALO:, :]
    cnt = jnp.minimum(pos, float(POOL_WINDOWS[-1]))
    for gi in range(len(POOL_WINDOWS) - 2, -1, -1):
        in_group = lane < (gi + 1) * POOL_GROUP_DIM
        wsum = jnp.where(in_group, sums[POOL_WINDOWS[gi]][POOL_HALO:, :], wsum)
        cnt = jnp.where(in_group, jnp.minimum(pos, float(POOL_WINDOWS[gi])), cnt)
    pooled = wsum / cnt - p
    mixed = _dot(pooled.astype(BF16), w_ref[...])
    o_ref[0] = (mixed * scale_ref[...]).astype(o_ref.dtype)


def pool_mixer(proj3, pool_w, pool_scale, t=512):
    b, s, _ = proj3.shape
    t = min(t, s)
    g = len(POOL_WINDOWS)
    wbd = (jnp.eye(g, dtype=F32)[:, None, :, None] * pool_w.astype(F32)[:, :, None, :]).reshape(
        POOL_WIDTH, POOL_WIDTH).astype(BF16)
    return pl.pallas_call(
        functools.partial(_pool_kernel, t=t),
        grid=(b, s // t),
        in_specs=[pl.BlockSpec((1, t, POOL_WIDTH), lambda bi, si: (bi, si, POOL_COL_BLOCK)),
                  pl.BlockSpec((POOL_WIDTH, POOL_WIDTH), lambda bi, si: (0, 0)),
                  pl.BlockSpec((1, POOL_WIDTH), lambda bi, si: (0, 0))],
        out_specs=pl.BlockSpec((1, t, POOL_WIDTH), lambda bi, si: (bi, si, 0)),
        out_shape=jax.ShapeDtypeStruct((b, s, POOL_WIDTH), BF16),
        scratch_shapes=[pltpu.VMEM((POOL_HALO, POOL_WIDTH), F32)],
        compiler_params=_cparams(("parallel", "arbitrary")),
        name="pool_mixer",
    )(proj3, wbd, pool_scale.reshape(1, POOL_WIDTH))


def _retention_tables(s, t):
    d, hn, c = RET_HEAD_DIM, RET_HEADS, RET_CHUNK
    half = d // 2
    inv = 10000.0 ** (-jnp.linspace(0.0, 1.0, half, dtype=F32))
    ang = jnp.arange(s)[:, None].astype(F32) * inv[None, :]
    cos, sin = jnp.cos(ang), jnp.sin(ang)
    cos_t = jnp.tile(jnp.concatenate([cos, cos], axis=-1), (1, hn))
    sin_t = jnp.tile(jnp.concatenate([-sin, sin], axis=-1), (1, hn))
    log_gamma = jnp.log(1.0 - 2.0 ** (-5.0 - jnp.arange(hn, dtype=F32)))
    idx = jnp.arange(c, dtype=F32)
    rel = idx[:, None] - idx[None, :]
    intra = jnp.where(rel >= 0, jnp.exp(log_gamma[:, None, None] * jnp.maximum(rel, 0.0)), 0.0)
    q_decay = jnp.exp(log_gamma[:, None] * (idx + 1.0))
    k_decay = jnp.exp(log_gamma[:, None] * (c - 1.0 - idx))
    chunk_decay = jnp.exp(log_gamma * c)
    lanes = lambda a: jnp.tile(jnp.repeat(a.T, d, axis=1), (t // c, 1))
    l = jnp.arange(hn * d)
    partner = jnp.where(l % d < half, l + half, l - half)
    swap = (l[:, None] == partner[None, :]).astype(BF16)
    same = (jnp.arange(2 * d)[:, None] // d) == (jnp.arange(2 * d)[None, :] // d)
    decay_bd = jnp.where(same[None], jnp.repeat(chunk_decay, d).reshape(hn // 2, 2 * d, 1), 0.0)
    avg = jnp.where(same, 1.0 / d, 0.0).astype(BF16)
    return cos_t, sin_t, intra, lanes(q_decay), lanes(k_decay), swap, decay_bd.astype(F32), avg


def _ret_kernel(q_ref, k_ref, v_ref, g_ref, cos_ref, sin_ref, intra_ref, qd_ref, kd_ref, swap_ref, decay_ref,
                avg_ref, gn_ref, o_ref, state, *, t):
    si = pl.program_id(1)
    d, c = RET_HEAD_DIM, RET_CHUNK
    pw = 2 * d

    @pl.when(si == 0)
    def _():
        state[...] = jnp.zeros(state.shape, F32)

    def rotate(x_ref):
        x = x_ref[0]
        return x.astype(F32) * cos_ref[...] + _dot(x, swap_ref[...]) * sin_ref[...]

    def lane_mean(a):
        hi = a.astype(BF16)
        lo = (a - hi.astype(F32)).astype(BF16)
        return _dot(hi, avg_ref[...]) + _dot(lo, avg_ref[...])

    q = rotate(q_ref)
    k = rotate(k_ref) * (d ** -0.5)
    qs = (q * qd_ref[...]).astype(BF16)
    ks = (k * kd_ref[...]).astype(BF16)
    qb = q.astype(BF16)
    kb = k.astype(BF16)
    v = v_ref[0]
    first = lax.broadcasted_iota(jnp.int32, (c, pw), 1) < d
    same = ((lax.broadcasted_iota(jnp.int32, (pw, pw), 0) < d)
            == (lax.broadcasted_iota(jnp.int32, (pw, pw), 1) < d))
    zero = jnp.zeros((c, pw), BF16)

    chunks = [slice(ci * c, (ci + 1) * c) for ci in range(t // c)]
    pairs = [slice(p * pw, (p + 1) * pw) for p in range(RET_HEADS // 2)]
    scores = {}
    for p, cols in enumerate(pairs):
        for ci, rows in enumerate(chunks):
            qp, kp = qb[rows, cols], kb[rows, cols]
            scores[p, ci] = ((_nt_dot(jnp.where(first, qp, zero), kp) * intra_ref[2 * p]).astype(BF16),
                             (_nt_dot(jnp.where(first, zero, qp), kp) * intra_ref[2 * p + 1]).astype(BF16))
    intra_out, increment = {}, {}
    for p, cols in enumerate(pairs):
        for ci, rows in enumerate(chunks):
            vp = v[rows, cols]
            intra_out[p, ci] = jnp.where(first, _dot(scores[p, ci][0], vp), _dot(scores[p, ci][1], vp))
            increment[p, ci] = jnp.where(same, _dot(ks[rows, cols].T, vp), 0.0)
    ys = {}
    for p, cols in enumerate(pairs):
        st = state[p]
        for ci, rows in enumerate(chunks):
            ys[p, ci] = intra_out[p, ci] + _dot(qs[rows, cols], st.astype(BF16))
            st = st * decay_ref[p] + increment[p, ci]
        state[p] = st
    y = jnp.concatenate([jnp.concatenate([ys[p, ci] for ci in range(len(chunks))], axis=0)
                         for p in range(len(pairs))], axis=1)
    mean = jnp.concatenate([lane_mean(y[:, cols]) for cols in pairs], axis=1)
    yc = y - mean
    var = jnp.concatenate([lane_mean((yc * yc)[:, cols]) for cols in pairs], axis=1)
    yn = yc * lax.rsqrt(var + NORM_EPS)
    gate = _silu(g_ref[0].astype(F32))
    o_ref[0] = (gate * (yn * gn_ref[...])).astype(o_ref.dtype)


def retention(proj3, gn_g, t=512):
    b, s, _ = proj3.shape
    t = min(t, s)
    cos_t, sin_t, intra, qd, kd, swap, decay_bd, avg = _retention_tables(s, t)
    pw = 2 * RET_HEAD_DIM
    col = lambda j: pl.BlockSpec((1, t, RET_WIDTH), lambda bi, si: (bi, si, RET_COL_BLOCK + j))
    const2 = lambda shape: pl.BlockSpec(shape, lambda bi, si: (0, 0))
    const3 = lambda shape: pl.BlockSpec(shape, lambda bi, si: (0, 0, 0))
    return pl.pallas_call(
        functools.partial(_ret_kernel, t=t),
        grid=(b, s // t),
        in_specs=[col(0), col(1), col(2), col(3),
                  pl.BlockSpec((t, RET_WIDTH), lambda bi, si: (si, 0)),
                  pl.BlockSpec((t, RET_WIDTH), lambda bi, si: (si, 0)),
                  const3((RET_HEADS, RET_CHUNK, RET_CHUNK)),
                  const2((t, RET_WIDTH)), const2((t, RET_WIDTH)), const2((RET_WIDTH, RET_WIDTH)),
                  const3((RET_HEADS // 2, pw, pw)), const2((pw, pw)), const2((1, RET_WIDTH))],
        out_specs=pl.BlockSpec((1, t, RET_WIDTH), lambda bi, si: (bi, si, 0)),
        out_shape=jax.ShapeDtypeStruct((b, s, RET_WIDTH), BF16),
        scratch_shapes=[pltpu.VMEM((RET_HEADS // 2, pw, pw), F32)],
        compiler_params=_cparams(("parallel", "arbitrary")),
        name="retention",
    )(proj3, proj3, proj3, proj3, cos_t, sin_t, intra, qd, kd, swap, decay_bd, avg, gn_g.reshape(1, RET_WIDTH))


def _mix_ln(da_ref, pool_ref, ret_ref, x_ref, w_ref, g_ref, b_ref):
    e0, e1 = DA_WIDTH, DA_WIDTH + POOL_WIDTH
    mix = (_dot(da_ref[...], w_ref[0:e0, :]) + _dot(pool_ref[...], w_ref[e0:e1, :])
           + _dot(ret_ref[...], w_ref[e1:, :]))
    return _layer_norm(ALPHA * x_ref[...] + mix, g_ref[...], b_ref[...])


def _outproj_kernel(da_ref, pool_ref, ret_ref, x_ref, w_ref, g_ref, b_ref, o_ref):
    o_ref[...] = _mix_ln(da_ref, pool_ref, ret_ref, x_ref, w_ref, g_ref, b_ref)


def out_proj_ln(y_da, y_pool, y_ret, x2d, w, g, b, tm=512):
    n, dm = x2d.shape
    tm = min(tm, n)
    row = lambda width: pl.BlockSpec((tm, width), lambda i: (i, 0))
    const = lambda shape: pl.BlockSpec(shape, lambda i: (0, 0))
    return pl.pallas_call(
        _outproj_kernel,
        grid=(n // tm,),
        in_specs=[row(DA_WIDTH), row(POOL_WIDTH), row(RET_WIDTH), row(dm),
                  const(w.shape), const((1, dm)), const((1, dm))],
        out_specs=row(dm),
        out_shape=jax.ShapeDtypeStruct((n, dm), F32),
        compiler_params=_cparams(("parallel",)),
        name="out_proj_ln",
    )(y_da, y_pool, y_ret, x2d, w, g.reshape(1, dm), b.reshape(1, dm))


def _mix_ffn_kernel(da_ref, pool_ref, ret_ref, x_ref, wo_ref, g1_ref, b1_ref, wg_ref, wu_ref, wd_ref,
                    g2_ref, b2_ref, o_ref, *, tf):
    x = _mix_ln(da_ref, pool_ref, ret_ref, x_ref, wo_ref, g1_ref, b1_ref)
    xb = x.astype(BF16)
    acc = jnp.zeros(x.shape, F32)
    for f0 in range(0, wg_ref.shape[1], tf):
        hidden = _silu(_dot(xb, wg_ref[:, f0:f0 + tf])) * _dot(xb, wu_ref[:, f0:f0 + tf])
        acc = acc + _dot(hidden.astype(BF16), wd_ref[f0:f0 + tf, :])
    o_ref[...] = _layer_norm(ALPHA * x + acc, g2_ref[...], b2_ref[...])


def out_proj_ffn_ln(y_da, y_pool, y_ret, x2d, wo, g1, b1, wg, wu, wd, g2, b2, tm=512, tf=256):
    n, dm = x2d.shape
    tm = min(tm, n)
    row = lambda width: pl.BlockSpec((tm, width), lambda i: (i, 0))
    resident = lambda shape: pl.BlockSpec(shape, lambda i: (0, 0), pipeline_mode=pl.Buffered(1))
    vec = lambda a: a.reshape(1, dm)
    return pl.pallas_call(
        functools.partial(_mix_ffn_kernel, tf=tf),
        grid=(n // tm,),
        in_specs=[row(DA_WIDTH), row(POOL_WIDTH), row(RET_WIDTH), row(dm),
                  resident(wo.shape), resident((1, dm)), resident((1, dm)),
                  resident(wg.shape), resident(wu.shape), resident(wd.shape),
                  resident((1, dm)), resident((1, dm))],
        out_specs=row(dm),
        out_shape=jax.ShapeDtypeStruct((n, dm), F32),
        compiler_params=_cparams(("parallel",)),
        name="out_proj_ffn_ln",
    )(y_da, y_pool, y_ret, x2d, wo, vec(g1), vec(b1), wg, wu, wd, vec(g2), vec(b2))


ROUTE_ROWS = 8
LANES = 128


def _router_kernel(x_ref, w_ref, tri_ref, route_ref, route_t_ref, cnt_ref, carry, *, t):
    i = pl.program_id(0)

    @pl.when(i == 0)
    def _():
        carry[...] = jnp.zeros(carry.shape, F32)

    x = x_ref[...]
    xh = x.astype(BF16)
    xl = (x - xh.astype(F32)).astype(BF16)
    w = w_ref[...]
    wh = w.astype(BF16)
    wl = (w - wh.astype(F32)).astype(BF16)
    logits = _nt_dot(wh, xh) + _nt_dot(wl, xh) + _nt_dot(wh, xl)

    row = lax.broadcasted_iota(jnp.int32, logits.shape, 0)
    v0 = jnp.max(logits, axis=0, keepdims=True)
    i0 = jnp.min(jnp.where(logits == v0, row, N_EXPERTS), axis=0, keepdims=True)
    rest = jnp.where(row == i0, -jnp.inf, logits)
    v1 = jnp.max(rest, axis=0, keepdims=True)
    i1 = jnp.min(jnp.where(rest == v1, row, N_EXPERTS), axis=0, keepdims=True)
    ex = jnp.exp(v1 - v0)
    gate0 = 1.0 / (1.0 + ex)
    gate1 = ex / (1.0 + ex)

    oh0 = row == i0
    oh1 = row == i1
    member = jnp.where(oh0 | oh1, 1.0, 0.0)
    before = _dot(member.astype(BF16), tri_ref[...]) + carry[:, 0:1]
    rank0 = jnp.sum(jnp.where(oh0, before, 0.0), axis=0, keepdims=True)
    rank1 = jnp.sum(jnp.where(oh1, before, 0.0), axis=0, keepdims=True)
    carry[...] = carry[...] + jnp.sum(member, axis=1, keepdims=True)
    cnt_ref[...] = carry[...]

    route = jnp.concatenate([i0.astype(F32), i1.astype(F32), rank0, rank1, gate0, gate1,
                             jnp.zeros((2, t), F32)], axis=0)
    route_ref[...] = route
    padded = jnp.concatenate([route, jnp.zeros((LANES - ROUTE_ROWS, t), F32)], axis=0)
    route_t_ref[...] = padded.T


def route_tokens(x2d, router_w, t=1024):
    n, dm = x2d.shape
    t = min(t, n)
    tri = (jnp.arange(t)[:, None] < jnp.arange(t)[None, :]).astype(BF16)
    return pl.pallas_call(
        functools.partial(_router_kernel, t=t),
        grid=(n // t,),
        in_specs=[pl.BlockSpec((t, dm), lambda i: (i, 0)),
                  pl.BlockSpec((N_EXPERTS, dm), lambda i: (0, 0)),
                  pl.BlockSpec((t, t), lambda i: (0, 0))],
        out_specs=[pl.BlockSpec((ROUTE_ROWS, t), lambda i: (0, i)),
                   pl.BlockSpec((t, LANES), lambda i: (i, 0)),
                   pl.BlockSpec((N_EXPERTS, LANES), lambda i: (0, 0))],
        out_shape=[jax.ShapeDtypeStruct((ROUTE_ROWS, n), F32),
                   jax.ShapeDtypeStruct((n, LANES), F32),
                   jax.ShapeDtypeStruct((N_EXPERTS, LANES), F32)],
        scratch_shapes=[pltpu.VMEM((N_EXPERTS, LANES), F32)],
        compiler_params=_cparams(("arbitrary",)),
        name="route_tokens",
    )(x2d, router_w.T, tri)


def _dest_kernel(start_ref, route_ref, dest_ref):
    r = route_ref[...]
    for k in range(2):
        e = r[k:k + 1, :].astype(jnp.int32)
        base = jnp.zeros(e.shape, jnp.int32)
        for ei in range(N_EXPERTS):
            base = jnp.where(e == ei, start_ref[ei], base)
        dest_ref[k:k + 1, :] = base + r[2 + k:3 + k, :].astype(jnp.int32)


def slot_of_assignment(pad_start, route, t=1024):
    n = route.shape[1]
    t = min(t, n)
    grid_spec = pltpu.PrefetchScalarGridSpec(
        num_scalar_prefetch=1,
        grid=(n // t,),
        in_specs=[pl.BlockSpec((ROUTE_ROWS, t), lambda i, ps: (0, i))],
        out_specs=pl.BlockSpec((2, t), lambda i, ps: (0, i)),
    )
    return pl.pallas_call(
        _dest_kernel,
        grid_spec=grid_spec,
        out_shape=jax.ShapeDtypeStruct((2, n), jnp.int32),
        compiler_params=_cparams(("parallel",)),
        name="slot_of_assignment",
    )(pad_start, route)


def _row_copy(src, src_row, dst, dst_row, sem):
    return pltpu.make_async_copy(src.at[pl.ds(src_row, 1)], dst.at[pl.ds(dst_row, 1)], sem)


ISSUE_UNROLL = 8


def _dispatch_kernel(pad_lo_ref, pad_n_ref, d0_ref, d1_ref, x_ref, slots_out, zrow, sem, zsem, *, t):
    @pl.when(pl.program_id(0) == 0)
    def _():
        zrow[...] = jnp.zeros(zrow.shape, zrow.dtype)
        for e in range(N_EXPERTS):
            lo = pad_lo_ref[e]

            def zero_start(r, carry):
                _row_copy(zrow, 0, slots_out, lo + r, zsem).start()
                return carry

            def zero_wait(r, carry):
                _row_copy(zrow, 0, slots_out, lo + r, zsem).wait()
                return carry

            lax.fori_loop(0, pad_n_ref[e], zero_start, 0)
            lax.fori_loop(0, pad_n_ref[e], zero_wait, 0)

        tail_lo = pad_lo_ref[N_EXPERTS]
        group = zrow.shape[0]

        def tail_copy(r):
            dst = slots_out.at[pl.ds(pl.multiple_of(tail_lo + r * group, group), group)]
            return pltpu.make_async_copy(zrow, dst, zsem)

        def tail_start(r, carry):
            tail_copy(r).start()
            return carry

        def tail_wait(r, carry):
            tail_copy(r).wait()
            return carry

        lax.fori_loop(0, pad_n_ref[N_EXPERTS], tail_start, 0)
        lax.fori_loop(0, pad_n_ref[N_EXPERTS], tail_wait, 0)

    def issue(j, carry):
        base = pl.multiple_of(j * ISSUE_UNROLL, ISSUE_UNROLL)
        group = x_ref.at[pl.ds(base, ISSUE_UNROLL)]
        for u in range(ISSUE_UNROLL):
            for k, d_ref in enumerate((d0_ref, d1_ref)):
                _row_copy(group, u, slots_out, d_ref[base + u], sem).start()
        return carry

    lax.fori_loop(0, t // ISSUE_UNROLL, issue, 0)
    for k in range(2):
        pltpu.make_async_copy(x_ref, slots_out.at[pl.ds(0, t)], sem).wait()


def dispatch_rows(pad_lo, pad_n, dest, x2d, n_slots, t=1024):
    n, dm = x2d.shape
    t = min(t, n)
    grid_spec = pltpu.PrefetchScalarGridSpec(
        num_scalar_prefetch=2,
        grid=(n // t,),
        in_specs=[pl.BlockSpec((t,), lambda i, lo, cnt: (i,), memory_space=pltpu.SMEM),
                  pl.BlockSpec((t,), lambda i, lo, cnt: (i,), memory_space=pltpu.SMEM),
                  pl.BlockSpec((t, dm), lambda i, lo, cnt: (i, 0))],
        out_specs=pl.BlockSpec(memory_space=pl.ANY),
        scratch_shapes=[pltpu.VMEM((8, dm), x2d.dtype), pltpu.SemaphoreType.DMA, pltpu.SemaphoreType.DMA],
    )
    return pl.pallas_call(
        functools.partial(_dispatch_kernel, t=t),
        grid_spec=grid_spec,
        out_shape=jax.ShapeDtypeStruct((n_slots, dm), x2d.dtype),
        compiler_params=_cparams(("arbitrary",)),
        name="dispatch_rows",
    )(pad_lo, pad_n, dest[0], dest[1], x2d)


def _expert_kernel(be_ref, nv_ref, x_ref, wg_ref, wu_ref, wd_ref, o_ref, xb):
    blk = pl.program_id(0)
    f = pl.program_id(1)
    valid = blk < nv_ref[0]

    @pl.when(f == 0)
    def _():
        o_ref[...] = jnp.zeros(o_ref.shape, o_ref.dtype)

    @pl.when(valid & (f == 0))
    def _():
        xb[...] = x_ref[...].astype(BF16)

    @pl.when(valid)
    def _():
        hidden = (_silu(_dot(xb[...], wg_ref[0].astype(BF16)))
                  * _dot(xb[...], wu_ref[0].astype(BF16)))
        o_ref[...] += _dot(hidden.astype(BF16), wd_ref[0].astype(BF16))


def expert_swiglu(block_e, n_valid, slots, wg, wu, wd, bm, tf=512):
    n_slots, dm = slots.shape
    edim = wg.shape[2]
    grid_spec = pltpu.PrefetchScalarGridSpec(
        num_scalar_prefetch=2,
        grid=(n_slots // bm, edim // tf),
        in_specs=[pl.BlockSpec((bm, dm), lambda b, f, be, nv: (jnp.minimum(b, nv[0] - 1), 0)),
                  pl.BlockSpec((1, dm, tf), lambda b, f, be, nv: (be[b], 0, f)),
                  pl.BlockSpec((1, dm, tf), lambda b, f, be, nv: (be[b], 0, f)),
                  pl.BlockSpec((1, tf, dm), lambda b, f, be, nv: (be[b], f, 0))],
        out_specs=pl.BlockSpec((bm, dm), lambda b, f, be, nv: (b, 0)),
        scratch_shapes=[pltpu.VMEM((bm, dm), BF16)],
    )
    return pl.pallas_call(
        _expert_kernel,
        grid_spec=grid_spec,
        out_shape=jax.ShapeDtypeStruct((n_slots, dm), F32),
        compiler_params=_cparams(("parallel", "arbitrary")),
        name="expert_swiglu",
    )(block_e, n_valid, slots, wg, wu, wd)


def _combine_kernel(d0_ref, d1_ref, d0_next_ref, d1_next_ref, rt_ref, x_ref, y_hbm, g_ref, b_ref, o_ref,
                    rows, sem, *, t):
    i = pl.program_id(0)
    cur = i % 2

    def gather(d_refs, buf):
        def issue(j, carry):
            for u in range(ISSUE_UNROLL):
                for k in range(2):
                    _row_copy(y_hbm, d_refs[k][j * ISSUE_UNROLL + u], rows.at[buf, k, j], u, sem.at[buf]).start()
            return carry

        lax.fori_loop(0, t // ISSUE_UNROLL, issue, 0)

    def wait_rows(buf):
        for k in range(2):
            pltpu.make_async_copy(y_hbm.at[pl.ds(0, t)], o_ref, sem.at[buf]).wait()

    @pl.when(i == 0)
    def _():
        gather((d0_ref, d1_ref), 0)

    wait_rows(cur)
    for r in range(t):
        for k, d_ref in enumerate((d0_next_ref, d1_next_ref)):
            _row_copy(y_hbm, d_ref[r], rows.at[1 - cur, k, r // ISSUE_UNROLL], r % ISSUE_UNROLL,
                      sem.at[1 - cur]).start()

    rt = rt_ref[...]
    dm = o_ref.shape[1]
    mixed = rt[:, 4:5] * rows[cur, 0].reshape(t, dm) + rt[:, 5:6] * rows[cur, 1].reshape(t, dm)
    o_ref[...] = _layer_norm(ALPHA * x_ref[...] + mixed, g_ref[...], b_ref[...])

    @pl.when(i + 1 == pl.num_programs(0))
    def _():
        wait_rows(1 - cur)


def combine_ln(dest, route_t, x2d, y, g, b, t=512):
    n, dm = x2d.shape
    t = min(t, n)
    last = n // t - 1
    return pl.pallas_call(
        functools.partial(_combine_kernel, t=t),
        grid=(n // t,),
        in_specs=[pl.BlockSpec((t,), lambda i: (i,), memory_space=pltpu.SMEM),
                  pl.BlockSpec((t,), lambda i: (i,), memory_space=pltpu.SMEM),
                  pl.BlockSpec((t,), lambda i: (jnp.minimum(i + 1, last),), memory_space=pltpu.SMEM),
                  pl.BlockSpec((t,), lambda i: (jnp.minimum(i + 1, last),), memory_space=pltpu.SMEM),
                  pl.BlockSpec((t, LANES), lambda i: (i, 0)),
                  pl.BlockSpec((t, dm), lambda i: (i, 0)),
                  pl.BlockSpec(memory_space=pl.ANY),
                  pl.BlockSpec((1, dm), lambda i: (0, 0)),
                  pl.BlockSpec((1, dm), lambda i: (0, 0))],
        out_specs=pl.BlockSpec((t, dm), lambda i: (i, 0)),
        out_shape=jax.ShapeDtypeStruct((n, dm), F32),
        scratch_shapes=[pltpu.VMEM((2, 2, t // ISSUE_UNROLL, ISSUE_UNROLL, dm), F32),
                        pltpu.SemaphoreType.DMA((2,))],
        compiler_params=_cparams(("arbitrary",)),
        name="combine_ln",
    )(dest[0], dest[1], dest[0], dest[1], route_t, x2d, y, g.reshape(1, dm), b.reshape(1, dm))


def moe_ln(x2d, router_w, wg, wu, wd, g, b, bm=1024):
    n, _ = x2d.shape
    bm = min(bm, n)
    route, route_t, cnt = route_tokens(x2d, router_w)
    counts = cnt[:, 0].astype(jnp.int32)
    padded = (counts + bm - 1) // bm * bm
    pad_end = jnp.cumsum(padded)
    pad_start = (pad_end - padded).astype(jnp.int32)
    n_blocks = 2 * n // bm + N_EXPERTS
    n_slots = n_blocks * bm
    block_start = jnp.arange(n_blocks, dtype=jnp.int32) * bm
    block_e = jnp.minimum(jnp.sum(pad_end[None, :] <= block_start[:, None], axis=1),
                          N_EXPERTS - 1).astype(jnp.int32)
    n_valid = (pad_end[-1:] // bm).astype(jnp.int32)
    dest = slot_of_assignment(pad_start, route)
    pad_lo = jnp.concatenate([pad_start + counts, pad_end[-1:]]).astype(jnp.int32)
    pad_n = jnp.concatenate([padded - counts, (n_slots - pad_end[-1:]) // 8]).astype(jnp.int32)
    slots = dispatch_rows(pad_lo, pad_n, dest, x2d, n_slots)
    y = expert_swiglu(block_e, n_valid, slots, wg, wu, wd, bm)
    return combine_ln(dest, route_t, x2d, y, g, b)


def kernel(x, rel_bias, w_in, diff_lambda, diff_subln_g, pool_w, pool_scale, ret_gn_g, w_out,
           ln1_g, ln1_b, ln2_g, ln2_b, ffn_w_gate, ffn_w_up, ffn_w_down,
           router_w, moe_w_gate, moe_w_up, moe_w_down):
    bsz, seq, dm = x.shape
    x2d = x.reshape(bsz * seq, dm)
    bias_tiles = attn_bias_tiles(rel_bias, min(ATTN_BLOCK, seq))
    col_scale = jnp.where(jnp.arange(w_in.shape[2]) < DA_WIDTH, Q_SCALE, 1.0).astype(F32)
    for l in range(DEPTH):
        lam_init = 0.8 - 0.6 * math.exp(-0.3 * l)
        proj = in_proj(x2d, (w_in[l] * col_scale).astype(BF16)).reshape(bsz, seq, -1)
        y_da = diff_attention(proj, bias_tiles, diff_lambda[l], diff_subln_g[l], lam_init)
        y_pool = pool_mixer(proj, pool_w[l], pool_scale[l])
        y_ret = retention(proj, ret_gn_g[l])
        flat = lambda a: a.reshape(bsz * seq, -1)
        mixed = (flat(y_da), flat(y_pool), flat(y_ret), x2d, w_out[l].astype(BF16), ln1_g[l], ln1_b[l])
        j = l // 2
        if l % 2 == 0:
            x2d = out_proj_ffn_ln(*mixed, ffn_w_gate[j].astype(BF16), ffn_w_up[j].astype(BF16),
                                  ffn_w_down[j].astype(BF16), ln2_g[l], ln2_b[l])
        else:
            x2d = out_proj_ln(*mixed)
            x2d = moe_ln(x2d, router_w[j], moe_w_gate[j], moe_w_up[j], moe_w_down[j], ln2_g[l], ln2_b[l])
    return x2d.reshape(bsz, seq, dm)
```

```python
import functools
import math

import jax
import jax.numpy as jnp
from jax import lax
from jax.experimental import pallas as pl
from jax.experimental.pallas import tpu as pltpu

F32 = jnp.float32
BF16 = jnp.bfloat16

DEPTH = 2
DA_HEAD_DIM = 64
DA_V_DIM = 128
DA_HEADS = 4
DA_WIDTH = 512
POOL_WIDTH = 256
POOL_WINDOWS = (2, 4, 8, 16)
POOL_GROUP_DIM = 64
POOL_HALO = 16
RET_WIDTH = 256
RET_HEAD_DIM = 64
RET_HEADS = 4
RET_CHUNK = 128
REL_BUCKETS = 32
REL_MAX_DIST = 128
N_EXPERTS = 8
ALPHA = (2 * DEPTH) ** 0.25
LN_EPS = 1e-5
NORM_EPS = 1e-6
NEG_BIG = -1e30

POOL_COL_BLOCK = 3 * DA_WIDTH // POOL_WIDTH
RET_COL_BLOCK = POOL_COL_BLOCK + 1

VMEM_LIMIT = 56 * 1024 * 1024


def _cparams(sem, vmem=VMEM_LIMIT):
    return pltpu.CompilerParams(dimension_semantics=sem, vmem_limit_bytes=vmem)


def _nt_dot(a, b):
    return lax.dot_general(a, b, (((1,), (1,)), ((), ())), preferred_element_type=F32)


def _dot(a, b):
    return jnp.dot(a, b, preferred_element_type=F32)


def _layer_norm(z, g, b):
    mu = jnp.mean(z, axis=-1, keepdims=True)
    zc = z - mu
    var = jnp.mean(zc * zc, axis=-1, keepdims=True)
    return zc * lax.rsqrt(var + LN_EPS) * g + b


def _silu(x):
    return x / (1.0 + jnp.exp(-x))


def _inproj_kernel(x_ref, w_ref, o_ref, *, tn):
    xb = x_ref[...].astype(BF16)
    for j in range(0, w_ref.shape[1], tn):
        o_ref[:, j:j + tn] = _dot(xb, w_ref[:, j:j + tn]).astype(o_ref.dtype)


def in_proj(x2d, w, tm=512, tn=256):
    n, k = x2d.shape
    m = w.shape[1]
    tm = min(tm, n)
    return pl.pallas_call(
        functools.partial(_inproj_kernel, tn=tn),
        grid=(n // tm,),
        in_specs=[pl.BlockSpec((tm, k), lambda i: (i, 0)),
                  pl.BlockSpec((k, m), lambda i: (0, 0))],
        out_specs=pl.BlockSpec((tm, m), lambda i: (i, 0)),
        out_shape=jax.ShapeDtypeStruct((n, m), BF16),
        compiler_params=_cparams(("parallel",)),
        name="in_proj",
    )(x2d, w)


def _t5_bucket(dist):
    n = jnp.maximum(dist, 0)
    max_exact = REL_BUCKETS // 2
    nf = jnp.maximum(n, 1).astype(F32)
    large = max_exact + (jnp.log(nf / max_exact) / math.log(REL_MAX_DIST / max_exact)
                         * (REL_BUCKETS - max_exact)).astype(jnp.int32)
    large = jnp.minimum(large, REL_BUCKETS - 1)
    return jnp.where(n < max_exact, n, large)


LOG2E = math.log2(math.e)
Q_SCALE = DA_HEAD_DIM ** -0.5 * LOG2E
ATTN_BLOCK = 512
ATTN_CHUNK = 256
ONES_ROWS = 16


def attn_bias_tiles(rel_bias, t):
    table = rel_bias.astype(F32).reshape(REL_BUCKETS, DA_HEADS * 2)
    vec = table[_t5_bucket(jnp.arange(2 * t))]
    far = table[_t5_bucket(jnp.array(2 * t))]
    vec = ((vec - far[None, :]) * LOG2E).T
    masked = jnp.full((DA_HEADS * 2, t), NEG_BIG, F32)
    u_diag = jnp.concatenate([vec[:, :t], masked], axis=1)
    u_prev = jnp.concatenate([vec[:, t:], vec[:, :t]], axis=1)

    def toeplitz(u):
        skew = jnp.tile(u, (1, t))[:, :t * (2 * t - 1)].reshape(-1, t, 2 * t - 1)
        return skew[:, :, :t]

    def per_head(a):
        return a.reshape(DA_HEADS, 2, t, t).transpose(0, 2, 1, 3).reshape(DA_HEADS, t, 2 * t)

    return jnp.stack([per_head(toeplitz(u_diag)), per_head(toeplitz(u_prev))], axis=1)


def _attn_kernel(q_ref, k_ref, v_ref, bias_ref, lam_ref, g_ref, o_ref, vt, m_s, acc, s_a, s_b, *, t, lam_init):
    nk = vt.shape[0]
    ones = jnp.ones((ONES_ROWS, t), BF16)
    for ki in range(nk):
        v_t = v_ref[0, ki * t:(ki + 1) * t, :].astype(F32).T.astype(BF16)
        vt[ki] = jnp.concatenate([v_t, ones], axis=0)

    def query_tile(qi, carry):
        _attn_query_tile(qi, q_ref, k_ref, bias_ref, lam_ref, g_ref, o_ref, vt, m_s, acc, s_a, s_b,
                         t=t, lam_init=lam_init)
        return carry

    lax.fori_loop(0, nk // 2, query_tile, 0)


def _attn_query_tile(qi, q_ref, k_ref, bias_ref, lam_ref, g_ref, o_ref, vt, m_s, acc, s_a, s_b, *, t, lam_init):
    tq = 2 * t
    q_rows = pl.ds(pl.multiple_of(qi * tq, tq), tq)
    q = q_ref[0, q_rows, :]
    lane = lax.broadcasted_iota(jnp.int32, q.shape, 1)
    zero = jnp.zeros_like(q)
    qcat = jnp.concatenate([jnp.where(lane < DA_HEAD_DIM, q, zero),
                            jnp.where(lane >= DA_HEAD_DIM, q, zero)], axis=0)
    m_s[...] = jnp.full(m_s.shape, NEG_BIG, F32)
    acc[...] = jnp.zeros(acc.shape, F32)

    chunk = min(ATTN_CHUNK, t)
    DIAG, PREV, FAR, SKIP = 0, 1, None, "skip"

    def step(nxt, cur):
        kb = None
        if nxt is not None:
            kb = k_ref[0, pl.ds(pl.multiple_of(nxt[0] * t, t), t), :]
        for c in range(0, 2 * tq, chunk):
            cols = slice(c, c + chunk)
            mp, qoff = divmod(c, tq)
            half, ioff = divmod(qoff, t)
            ci = c // chunk
            if nxt is not None and not (len(nxt) > 2 and nxt[2][half] == SKIP):
                nxt[1][ci] = _nt_dot(kb, qcat[cols, :])
            if cur is not None:
                ki, s_ref, kinds = cur
                kind = kinds[half]
                if kind == SKIP:
                    continue
                s = s_ref[ci]
                if kind is not FAR:
                    s = s + bias_ref[0, kind, :, mp * t + ioff:mp * t + ioff + chunk]
                m_prev = m_s[ci]
                m_new = jnp.maximum(m_prev, jnp.max(s, axis=0, keepdims=True))
                alpha = jnp.exp2(m_prev - m_new)
                p = jnp.exp2(s - m_new).astype(BF16)
                acc[ci] = alpha * acc[ci] + _dot(vt[ki], p)
                m_s[ci] = m_new

    upper = (SKIP, DIAG)
    lower = (DIAG, PREV)
    before = (PREV, FAR)
    plain = (FAR, FAR)
    step((2 * qi + 1, s_a, upper), None)
    step((2 * qi, s_b), (2 * qi + 1, s_a, upper))

    @pl.when(qi == 0)
    def _():
        step(None, (0, s_b, lower))

    @pl.when(qi >= 1)
    def _():
        n_far = 2 * qi - 1
        step((2 * qi - 1, s_a), (2 * qi, s_b, lower))
        step((0, s_b), (2 * qi - 1, s_a, before))

        def pair(jj, carry):
            j = 2 * jj
            step((j + 1, s_a), (j, s_b, plain))
            step((j + 2, s_b), (j + 1, s_a, plain))
            return carry

        lax.fori_loop(0, n_far // 2, pair, 0)
        step(None, (n_far - 1, s_b, plain))

    lm = lam_ref[...]
    lam = (jnp.exp(jnp.sum(lm[0:1] * lm[1:2], keepdims=True))
           - jnp.exp(jnp.sum(lm[2:3] * lm[3:4], keepdims=True)) + lam_init)
    a = jnp.concatenate([acc[ci] for ci in range(acc.shape[0])], axis=1)
    o = a[:DA_V_DIM, :] / a[DA_V_DIM:DA_V_DIM + 1, :]
    out = o[:, :tq] - lam * o[:, tq:]
    out = out * lax.rsqrt(jnp.mean(out * out, axis=0, keepdims=True) + NORM_EPS) * g_ref[...]
    o_ref[0, q_rows, :] = (out * (1.0 - lam_init)).T.astype(o_ref.dtype)


def diff_attention(proj3, bias_tiles, lam_params, subln_g, lam_init):
    b, s, _ = proj3.shape
    t = bias_tiles.shape[2]
    chunk = min(ATTN_CHUNK, t)
    nch = 4 * t // chunk
    seq = lambda col0: pl.BlockSpec((1, s, DA_V_DIM), lambda bi, h: (bi, 0, col0 + h))
    return pl.pallas_call(
        functools.partial(_attn_kernel, t=t, lam_init=lam_init),
        grid=(b, DA_HEADS),
        in_specs=[
            seq(0), seq(DA_HEADS), seq(2 * DA_HEADS),
            pl.BlockSpec((1, 2, t, 2 * t), lambda bi, h: (h, 0, 0, 0)),
            pl.BlockSpec((4, DA_HEAD_DIM), lambda bi, h: (0, 0)),
            pl.BlockSpec((DA_V_DIM, 1), lambda bi, h: (0, 0)),
        ],
        out_specs=pl.BlockSpec((1, s, DA_V_DIM), lambda bi, h: (bi, 0, h)),
        out_shape=jax.ShapeDtypeStruct((b, s, DA_WIDTH), BF16),
        scratch_shapes=[pltpu.VMEM((s // t, DA_V_DIM + ONES_ROWS, t), BF16),
                        pltpu.VMEM((nch, 1, chunk), F32),
                        pltpu.VMEM((nch, DA_V_DIM + ONES_ROWS, chunk), F32),
                        pltpu.VMEM((nch, t, chunk), F32), pltpu.VMEM((nch, t, chunk), F32)],
        compiler_params=_cparams(("parallel", "parallel")),
        name="diff_attention",
    )(proj3, proj3, proj3, bias_tiles, lam_params, subln_g.reshape(DA_V_DIM, 1))


def _pool_kernel(p_ref, w_ref, scale_ref, o_ref, halo, *, t):
    si = pl.program_id(1)

    @pl.when(si == 0)
    def _():
        halo[...] = jnp.zeros(halo.shape, F32)

    p = p_ref[0].astype(F32)
    ext = jnp.concatenate([halo[...], p], axis=0)
    halo[...] = p[t - POOL_HALO:, :]
    sums = {1: ext}
    w = 1
    while w < POOL_WINDOWS[-1]:
        sums[2 * w] = sums[w] + pltpu.roll(sums[w], w, 0)
        w *= 2
    lane = lax.broadcasted_iota(jnp.int32, (t, POOL_WIDTH), 1)
    pos = (si * t + lax.broadcasted_iota(jnp.int32, (t, POOL_WIDTH), 0) + 1).astype(F32)
    wsum = sums[POOL_WINDOWS[-1]][POOL_HALO:, :]
    cnt = jnp.minimum(pos, float(POOL_WINDOWS[-1]))
    for gi in range(len(POOL_WINDOWS) - 2, -1, -1):
        in_group = lane < (gi + 1) * POOL_GROUP_DIM
        wsum = jnp.where(in_group, sums[POOL_WINDOWS[gi]][POOL_HALO:, :], wsum)
        cnt = jnp.where(in_group, jnp.minimum(pos, float(POOL_WINDOWS[gi])), cnt)
    pooled = wsum / cnt - p
    mixed = _dot(pooled.astype(BF16), w_ref[...])
    o_ref[0] = (mixed * scale_ref[...]).astype(o_ref.dtype)


def pool_mixer(proj3, pool_w, pool_scale, t=512):
    b, s, _ = proj3.shape
    t = min(t, s)
    g = len(POOL_WINDOWS)
    wbd = (jnp.eye(g, dtype=F32)[:, None, :, None] * pool_w.astype(F32)[:, :, None, :]).reshape(
        POOL_WIDTH, POOL_WIDTH).astype(BF16)
    return pl.pallas_call(
        functools.partial(_pool_kernel, t=t),
        grid=(b, s // t),
        in_specs=[pl.BlockSpec((1, t, POOL_WIDTH), lambda bi, si: (bi, si, POOL_COL_BLOCK)),
                  pl.BlockSpec((POOL_WIDTH, POOL_WIDTH), lambda bi, si: (0, 0)),
                  pl.BlockSpec((1, POOL_WIDTH), lambda bi, si: (0, 0))],
        out_specs=pl.BlockSpec((1, t, POOL_WIDTH), lambda bi, si: (bi, si, 0)),
        out_shape=jax.ShapeDtypeStruct((b, s, POOL_WIDTH), BF16),
        scratch_shapes=[pltpu.VMEM((POOL_HALO, POOL_WIDTH), F32)],
        compiler_params=_cparams(("parallel", "arbitrary")),
        name="pool_mixer",
    )(proj3, wbd, pool_scale.reshape(1, POOL_WIDTH))


def _retention_tables(s, t):
    d, hn, c = RET_HEAD_DIM, RET_HEADS, RET_CHUNK
    half = d // 2
    inv = 10000.0 ** (-jnp.linspace(0.0, 1.0, half, dtype=F32))
    ang = jnp.arange(s)[:, None].astype(F32) * inv[None, :]
    cos, sin = jnp.cos(ang), jnp.sin(ang)
    cos_t = jnp.tile(jnp.concatenate([cos, cos], axis=-1), (1, hn))
    sin_t = jnp.tile(jnp.concatenate([-sin, sin], axis=-1), (1, hn))
    log_gamma = jnp.log(1.0 - 2.0 ** (-5.0 - jnp.arange(hn, dtype=F32)))
    idx = jnp.arange(c, dtype=F32)
    rel = idx[:, None] - idx[None, :]
    intra = jnp.where(rel >= 0, jnp.exp(log_gamma[:, None, None] * jnp.maximum(rel, 0.0)), 0.0)
    q_decay = jnp.exp(log_gamma[:, None] * (idx + 1.0))
    k_decay = jnp.exp(log_gamma[:, None] * (c - 1.0 - idx))
    chunk_decay = jnp.exp(log_gamma * c)
    lanes = lambda a: jnp.tile(jnp.repeat(a.T, d, axis=1), (t // c, 1))
    l = jnp.arange(hn * d)
    partner = jnp.where(l % d < half, l + half, l - half)
    swap = (l[:, None] == partner[None, :]).astype(BF16)
    same = (jnp.arange(2 * d)[:, None] // d) == (jnp.arange(2 * d)[None, :] // d)
    decay_bd = jnp.where(same[None], jnp.repeat(chunk_decay, d).reshape(hn // 2, 2 * d, 1), 0.0)
    avg = jnp.where(same, 1.0 / d, 0.0).astype(BF16)
    return cos_t, sin_t, intra, lanes(q_decay), lanes(k_decay), swap, decay_bd.astype(F32), avg


def _ret_kernel(q_ref, k_ref, v_ref, g_ref, cos_ref, sin_ref, intra_ref, qd_ref, kd_ref, swap_ref, decay_ref,
                avg_ref, gn_ref, o_ref, state, *, t):
    si = pl.program_id(1)
    d, c = RET_HEAD_DIM, RET_CHUNK
    pw = 2 * d

    @pl.when(si == 0)
    def _():
        state[...] = jnp.zeros(state.shape, F32)

    def rotate(x_ref):
        x = x_ref[0]
        return x.astype(F32) * cos_ref[...] + _dot(x, swap_ref[...]) * sin_ref[...]

    def lane_mean(a):
        hi = a.astype(BF16)
        lo = (a - hi.astype(F32)).astype(BF16)
        return _dot(hi, avg_ref[...]) + _dot(lo, avg_ref[...])

    q = rotate(q_ref)
    k = rotate(k_ref) * (d ** -0.5)
    qs = (q * qd_ref[...]).astype(BF16)
    ks = (k * kd_ref[...]).astype(BF16)
    qb = q.astype(BF16)
    kb = k.astype(BF16)
    v = v_ref[0]
    first = lax.broadcasted_iota(jnp.int32, (c, pw), 1) < d
    same = ((lax.broadcasted_iota(jnp.int32, (pw, pw), 0) < d)
            == (lax.broadcasted_iota(jnp.int32, (pw, pw), 1) < d))
    zero = jnp.zeros((c, pw), BF16)

    chunks = [slice(ci * c, (ci + 1) * c) for ci in range(t // c)]
    pairs = [slice(p * pw, (p + 1) * pw) for p in range(RET_HEADS // 2)]
    scores = {}
    for p, cols in enumerate(pairs):
        for ci, rows in enumerate(chunks):
            qp, kp = qb[rows, cols], kb[rows, cols]
            scores[p, ci] = ((_nt_dot(jnp.where(first, qp, zero), kp) * intra_ref[2 * p]).astype(BF16),
                             (_nt_dot(jnp.where(first, zero, qp), kp) * intra_ref[2 * p + 1]).astype(BF16))
    intra_out, increment = {}, {}
    for p, cols in enumerate(pairs):
        for ci, rows in enumerate(chunks):
            vp = v[rows, cols]
            intra_out[p, ci] = jnp.where(first, _dot(scores[p, ci][0], vp), _dot(scores[p, ci][1], vp))
            increment[p, ci] = jnp.where(same, _dot(ks[rows, cols].T, vp), 0.0)
    ys = {}
    for p, cols in enumerate(pairs):
        st = state[p]
        for ci, rows in enumerate(chunks):
            ys[p, ci] = intra_out[p, ci] + _dot(qs[rows, cols], st.astype(BF16))
            st = st * decay_ref[p] + increment[p, ci]
        state[p] = st
    y = jnp.concatenate([jnp.concatenate([ys[p, ci] for ci in range(len(chunks))], axis=0)
                         for p in range(len(pairs))], axis=1)
    mean = jnp.concatenate([lane_mean(y[:, cols]) for cols in pairs], axis=1)
    yc = y - mean
    var = jnp.concatenate([lane_mean((yc * yc)[:, cols]) for cols in pairs], axis=1)
    yn = yc * lax.rsqrt(var + NORM_EPS)
    gate = _silu(g_ref[0].astype(F32))
    o_ref[0] = (gate * (yn * gn_ref[...])).astype(o_ref.dtype)


def retention(proj3, gn_g, t=512):
    b, s, _ = proj3.shape
    t = min(t, s)
    cos_t, sin_t, intra, qd, kd, swap, decay_bd, avg = _retention_tables(s, t)
    pw = 2 * RET_HEAD_DIM
    col = lambda j: pl.BlockSpec((1, t, RET_WIDTH), lambda bi, si: (bi, si, RET_COL_BLOCK + j))
    const2 = lambda shape: pl.BlockSpec(shape, lambda bi, si: (0, 0))
    const3 = lambda shape: pl.BlockSpec(shape, lambda bi, si: (0, 0, 0))
    return pl.pallas_call(
        functools.partial(_ret_kernel, t=t),
        grid=(b, s // t),
        in_specs=[col(0), col(1), col(2), col(3),
                  pl.BlockSpec((t, RET_WIDTH), lambda bi, si: (si, 0)),
                  pl.BlockSpec((t, RET_WIDTH), lambda bi, si: (si, 0)),
                  const3((RET_HEADS, RET_CHUNK, RET_CHUNK)),
                  const2((t, RET_WIDTH)), const2((t, RET_WIDTH)), const2((RET_WIDTH, RET_WIDTH)),
                  const3((RET_HEADS // 2, pw, pw)), const2((pw, pw)), const2((1, RET_WIDTH))],
        out_specs=pl.BlockSpec((1, t, RET_WIDTH), lambda bi, si: (bi, si, 0)),
        out_shape=jax.ShapeDtypeStruct((b, s, RET_WIDTH), BF16),
        scratch_shapes=[pltpu.VMEM((RET_HEADS // 2, pw, pw), F32)],
        compiler_params=_cparams(("parallel", "arbitrary")),
        name="retention",
    )(proj3, proj3, proj3, proj3, cos_t, sin_t, intra, qd, kd, swap, decay_bd, avg, gn_g.reshape(1, RET_WIDTH))


def _mix_ln(da_ref, pool_ref, ret_ref, x_ref, w_ref, g_ref, b_ref):
    e0, e1 = DA_WIDTH, DA_WIDTH + POOL_WIDTH
    mix = (_dot(da_ref[...], w_ref[0:e0, :]) + _dot(pool_ref[...], w_ref[e0:e1, :])
           + _dot(ret_ref[...], w_ref[e1:, :]))
    return _layer_norm(ALPHA * x_ref[...] + mix, g_ref[...], b_ref[...])


def _outproj_kernel(da_ref, pool_ref, ret_ref, x_ref, w_ref, g_ref, b_ref, o_ref):
    o_ref[...] = _mix_ln(da_ref, pool_ref, ret_ref, x_ref, w_ref, g_ref, b_ref)


def out_proj_ln(y_da, y_pool, y_ret, x2d, w, g, b, tm=512):
    n, dm = x2d.shape
    tm = min(tm, n)
    row = lambda width: pl.BlockSpec((tm, width), lambda i: (i, 0))
    const = lambda shape: pl.BlockSpec(shape, lambda i: (0, 0))
    return pl.pallas_call(
        _outproj_kernel,
        grid=(n // tm,),
        in_specs=[row(DA_WIDTH), row(POOL_WIDTH), row(RET_WIDTH), row(dm),
                  const(w.shape), const((1, dm)), const((1, dm))],
        out_specs=row(dm),
        out_shape=jax.ShapeDtypeStruct((n, dm), F32),
        compiler_params=_cparams(("parallel",)),
        name="out_proj_ln",
    )(y_da, y_pool, y_ret, x2d, w, g.reshape(1, dm), b.reshape(1, dm))


def _mix_ffn_kernel(da_ref, pool_ref, ret_ref, x_ref, wo_ref, g1_ref, b1_ref, wg_ref, wu_ref, wd_ref,
                    g2_ref, b2_ref, o_ref, *, tf):
    x = _mix_ln(da_ref, pool_ref, ret_ref, x_ref, wo_ref, g1_ref, b1_ref)
    xb = x.astype(BF16)
    acc = jnp.zeros(x.shape, F32)
    for f0 in range(0, wg_ref.shape[1], tf):
        hidden = _silu(_dot(xb, wg_ref[:, f0:f0 + tf])) * _dot(xb, wu_ref[:, f0:f0 + tf])
        acc = acc + _dot(hidden.astype(BF16), wd_ref[f0:f0 + tf, :])
    o_ref[...] = _layer_norm(ALPHA * x + acc, g2_ref[...], b2_ref[...])


def out_proj_ffn_ln(y_da, y_pool, y_ret, x2d, wo, g1, b1, wg, wu, wd, g2, b2, tm=512, tf=256):
    n, dm = x2d.shape
    tm = min(tm, n)
    row = lambda width: pl.BlockSpec((tm, width), lambda i: (i, 0))
    resident = lambda shape: pl.BlockSpec(shape, lambda i: (0, 0), pipeline_mode=pl.Buffered(1))
    vec = lambda a: a.reshape(1, dm)
    return pl.pallas_call(
        functools.partial(_mix_ffn_kernel, tf=tf),
        grid=(n // tm,),
        in_specs=[row(DA_WIDTH), row(POOL_WIDTH), row(RET_WIDTH), row(dm),
                  resident(wo.shape), resident((1, dm)), resident((1, dm)),
                  resident(wg.shape), resident(wu.shape), resident(wd.shape),
                  resident((1, dm)), resident((1, dm))],
        out_specs=row(dm),
        out_shape=jax.ShapeDtypeStruct((n, dm), F32),
        compiler_params=_cparams(("parallel",)),
        name="out_proj_ffn_ln",
    )(y_da, y_pool, y_ret, x2d, wo, vec(g1), vec(b1), wg, wu, wd, vec(g2), vec(b2))


ROUTE_ROWS = 8
LANES = 128


def _router_kernel(x_ref, w_ref, tri_ref, route_ref, route_t_ref, cnt_ref, carry, *, t):
    i = pl.program_id(0)

    @pl.when(i == 0)
    def _():
        carry[...] = jnp.zeros(carry.shape, F32)

    x = x_ref[...]
    xh = x.astype(BF16)
    xl = (x - xh.astype(F32)).astype(BF16)
    w = w_ref[...]
    wh = w.astype(BF16)
    wl = (w - wh.astype(F32)).astype(BF16)
    logits = _nt_dot(wh, xh) + _nt_dot(wl, xh) + _nt_dot(wh, xl)

    row = lax.broadcasted_iota(jnp.int32, logits.shape, 0)
    v0 = jnp.max(logits, axis=0, keepdims=True)
    i0 = jnp.min(jnp.where(logits == v0, row, N_EXPERTS), axis=0, keepdims=True)
    rest = jnp.where(row == i0, -jnp.inf, logits)
    v1 = jnp.max(rest, axis=0, keepdims=True)
    i1 = jnp.min(jnp.where(rest == v1, row, N_EXPERTS), axis=0, keepdims=True)
    ex = jnp.exp(v1 - v0)
    gate0 = 1.0 / (1.0 + ex)
    gate1 = ex / (1.0 + ex)

    oh0 = row == i0
    oh1 = row == i1
    member = jnp.where(oh0 | oh1, 1.0, 0.0)
    before = _dot(member.astype(BF16), tri_ref[...]) + carry[:, 0:1]
    rank0 = jnp.sum(jnp.where(oh0, before, 0.0), axis=0, keepdims=True)
    rank1 = jnp.sum(jnp.where(oh1, before, 0.0), axis=0, keepdims=True)
    carry[...] = carry[...] + jnp.sum(member, axis=1, keepdims=True)
    cnt_ref[...] = carry[...]

    route = jnp.concatenate([i0.astype(F32), i1.astype(F32), rank0, rank1, gate0, gate1,
                             jnp.zeros((2, t), F32)], axis=0)
    route_ref[...] = route
    padded = jnp.concatenate([route, jnp.zeros((LANES - ROUTE_ROWS, t), F32)], axis=0)
    route_t_ref[...] = padded.T


def route_tokens(x2d, router_w, t=1024):
    n, dm = x2d.shape
    t = min(t, n)
    tri = (jnp.arange(t)[:, None] < jnp.arange(t)[None, :]).astype(BF16)
    return pl.pallas_call(
        functools.partial(_router_kernel, t=t),
        grid=(n // t,),
        in_specs=[pl.BlockSpec((t, dm), lambda i: (i, 0)),
                  pl.BlockSpec((N_EXPERTS, dm), lambda i: (0, 0)),
                  pl.BlockSpec((t, t), lambda i: (0, 0))],
        out_specs=[pl.BlockSpec((ROUTE_ROWS, t), lambda i: (0, i)),
                   pl.BlockSpec((t, LANES), lambda i: (i, 0)),
                   pl.BlockSpec((N_EXPERTS, LANES), lambda i: (0, 0))],
        out_shape=[jax.ShapeDtypeStruct((ROUTE_ROWS, n), F32),
                   jax.ShapeDtypeStruct((n, LANES), F32),
                   jax.ShapeDtypeStruct((N_EXPERTS, LANES), F32)],
        scratch_shapes=[pltpu.VMEM((N_EXPERTS, LANES), F32)],
        compiler_params=_cparams(("arbitrary",)),
        name="route_tokens",
    )(x2d, router_w.T, tri)


def _dest_kernel(start_ref, route_ref, dest_ref):
    r = route_ref[...]
    for k in range(2):
        e = r[k:k + 1, :].astype(jnp.int32)
        base = jnp.zeros(e.shape, jnp.int32)
        for ei in range(N_EXPERTS):
            base = jnp.where(e == ei, start_ref[ei], base)
        dest_ref[k:k + 1, :] = base + r[2 + k:3 + k, :].astype(jnp.int32)


def slot_of_assignment(pad_start, route, t=1024):
    n = route.shape[1]
    t = min(t, n)
    grid_spec = pltpu.PrefetchScalarGridSpec(
        num_scalar_prefetch=1,
        grid=(n // t,),
        in_specs=[pl.BlockSpec((ROUTE_ROWS, t), lambda i, ps: (0, i))],
        out_specs=pl.BlockSpec((2, t), lambda i, ps: (0, i)),
    )
    return pl.pallas_call(
        _dest_kernel,
        grid_spec=grid_spec,
        out_shape=jax.ShapeDtypeStruct((2, n), jnp.int32),
        compiler_params=_cparams(("parallel",)),
        name="slot_of_assignment",
    )(pad_start, route)


def _row_copy(src, src_row, dst, dst_row, sem):
    return pltpu.make_async_copy(src.at[pl.ds(src_row, 1)], dst.at[pl.ds(dst_row, 1)], sem)


SUBLANES = 8
ISSUE_UNROLL = SUBLANES


def _dispatch_kernel(pad_lo_ref, pad_n_ref, d0_ref, d1_ref, x_ref, slots_out, zrow, sem, zsem, *, t):
    @pl.when(pl.program_id(0) == 0)
    def _():
        zrow[...] = jnp.zeros(zrow.shape, zrow.dtype)
        for e in range(N_EXPERTS):
            lo = pad_lo_ref[e]

            def zero_start(r, carry):
                _row_copy(zrow, 0, slots_out, lo + r, zsem).start()
                return carry

            def zero_wait(r, carry):
                _row_copy(zrow, 0, slots_out, lo + r, zsem).wait()
                return carry

            lax.fori_loop(0, pad_n_ref[e], zero_start, 0)
            lax.fori_loop(0, pad_n_ref[e], zero_wait, 0)

        tail_lo = pad_lo_ref[N_EXPERTS]
        group = zrow.shape[0]

        def tail_copy(r):
            dst = slots_out.at[pl.ds(pl.multiple_of(tail_lo + r * group, group), group)]
            return pltpu.make_async_copy(zrow, dst, zsem)

        def tail_start(r, carry):
            tail_copy(r).start()
            return carry

        def tail_wait(r, carry):
            tail_copy(r).wait()
            return carry

        lax.fori_loop(0, pad_n_ref[N_EXPERTS], tail_start, 0)
        lax.fori_loop(0, pad_n_ref[N_EXPERTS], tail_wait, 0)

    def issue(j, carry):
        base = pl.multiple_of(j * ISSUE_UNROLL, ISSUE_UNROLL)
        group = x_ref.at[pl.ds(base, ISSUE_UNROLL)]
        for u in range(ISSUE_UNROLL):
            for k, d_ref in enumerate((d0_ref, d1_ref)):
                _row_copy(group, u, slots_out, d_ref[base + u], sem).start()
        return carry

    lax.fori_loop(0, t // ISSUE_UNROLL, issue, 0)
    for k in range(2):
        pltpu.make_async_copy(x_ref, slots_out.at[pl.ds(0, t)], sem).wait()


def dispatch_rows(pad_lo, pad_n, dest, x2d, n_slots, t=1024):
    n, dm = x2d.shape
    t = min(t, n)
    grid_spec = pltpu.PrefetchScalarGridSpec(
        num_scalar_prefetch=2,
        grid=(n // t,),
        in_specs=[pl.BlockSpec((t,), lambda i, lo, cnt: (i,), memory_space=pltpu.SMEM),
                  pl.BlockSpec((t,), lambda i, lo, cnt: (i,), memory_space=pltpu.SMEM),
                  pl.BlockSpec((t, dm), lambda i, lo, cnt: (i, 0))],
        out_specs=pl.BlockSpec(memory_space=pl.ANY),
        scratch_shapes=[pltpu.VMEM((SUBLANES, dm), x2d.dtype), pltpu.SemaphoreType.DMA,
                        pltpu.SemaphoreType.DMA],
    )
    return pl.pallas_call(
        functools.partial(_dispatch_kernel, t=t),
        grid_spec=grid_spec,
        out_shape=jax.ShapeDtypeStruct((n_slots, dm), x2d.dtype),
        compiler_params=_cparams(("arbitrary",)),
        name="dispatch_rows",
    )(pad_lo, pad_n, dest[0], dest[1], x2d)


def _expert_kernel(be_ref, nv_ref, x_ref, wg_ref, wu_ref, wd_ref, o_ref, xb):
    blk = pl.program_id(0)
    f = pl.program_id(1)
    valid = blk < nv_ref[0]

    @pl.when(f == 0)
    def _():
        o_ref[...] = jnp.zeros(o_ref.shape, o_ref.dtype)

    @pl.when(valid & (f == 0))
    def _():
        xb[...] = x_ref[...].astype(BF16)

    @pl.when(valid)
    def _():
        hidden = (_silu(_dot(xb[...], wg_ref[0].astype(BF16)))
                  * _dot(xb[...], wu_ref[0].astype(BF16)))
        o_ref[...] += _dot(hidden.astype(BF16), wd_ref[0].astype(BF16))


def expert_swiglu(block_e, n_valid, slots, wg, wu, wd, bm, tf=512):
    n_slots, dm = slots.shape
    edim = wg.shape[2]
    grid_spec = pltpu.PrefetchScalarGridSpec(
        num_scalar_prefetch=2,
        grid=(n_slots // bm, edim // tf),
        in_specs=[pl.BlockSpec((bm, dm), lambda b, f, be, nv: (jnp.minimum(b, nv[0] - 1), 0)),
                  pl.BlockSpec((1, dm, tf), lambda b, f, be, nv: (be[b], 0, f)),
                  pl.BlockSpec((1, dm, tf), lambda b, f, be, nv: (be[b], 0, f)),
                  pl.BlockSpec((1, tf, dm), lambda b, f, be, nv: (be[b], f, 0))],
        out_specs=pl.BlockSpec((bm, dm), lambda b, f, be, nv: (b, 0)),
        scratch_shapes=[pltpu.VMEM((bm, dm), BF16)],
    )
    return pl.pallas_call(
        _expert_kernel,
        grid_spec=grid_spec,
        out_shape=jax.ShapeDtypeStruct((n_slots, dm), F32),
        compiler_params=_cparams(("parallel", "arbitrary")),
        name="expert_swiglu",
    )(block_e, n_valid, slots, wg, wu, wd)


def _combine_kernel(d0_ref, d1_ref, d0_next_ref, d1_next_ref, rt_ref, x_ref, y_hbm, g_ref, b_ref, o_ref,
                    rows, sem, *, t):
    i = pl.program_id(0)
    cur = i % 2

    def gather(d_refs, buf):
        def issue(j, carry):
            for u in range(ISSUE_UNROLL):
                for k in range(2):
                    _row_copy(y_hbm, d_refs[k][j * ISSUE_UNROLL + u], rows.at[buf, k, j], u, sem.at[buf]).start()
            return carry

        lax.fori_loop(0, t // ISSUE_UNROLL, issue, 0)

    @pl.when(i == 0)
    def _():
        gather((d0_ref, d1_ref), 0)

    @pl.when(i + 1 < pl.num_programs(0))
    def _():
        gather((d0_next_ref, d1_next_ref), 1 - cur)

    for k in range(2):
        pltpu.make_async_copy(y_hbm.at[pl.ds(0, t)], o_ref, sem.at[cur]).wait()

    rt = rt_ref[...]
    dm = o_ref.shape[1]
    mixed = rt[:, 4:5] * rows[cur, 0].reshape(t, dm) + rt[:, 5:6] * rows[cur, 1].reshape(t, dm)
    o_ref[...] = _layer_norm(ALPHA * x_ref[...] + mixed, g_ref[...], b_ref[...])


def combine_ln(dest, route_t, x2d, y, g, b, t=1024):
    n, dm = x2d.shape
    t = min(t, n)
    last = n // t - 1
    return pl.pallas_call(
        functools.partial(_combine_kernel, t=t),
        grid=(n // t,),
        in_specs=[pl.BlockSpec((t,), lambda i: (i,), memory_space=pltpu.SMEM),
                  pl.BlockSpec((t,), lambda i: (i,), memory_space=pltpu.SMEM),
                  pl.BlockSpec((t,), lambda i: (jnp.minimum(i + 1, last),), memory_space=pltpu.SMEM),
                  pl.BlockSpec((t,), lambda i: (jnp.minimum(i + 1, last),), memory_space=pltpu.SMEM),
                  pl.BlockSpec((t, LANES), lambda i: (i, 0)),
                  pl.BlockSpec((t, dm), lambda i: (i, 0)),
                  pl.BlockSpec(memory_space=pl.ANY),
                  pl.BlockSpec((1, dm), lambda i: (0, 0)),
                  pl.BlockSpec((1, dm), lambda i: (0, 0))],
        out_specs=pl.BlockSpec((t, dm), lambda i: (i, 0)),
        out_shape=jax.ShapeDtypeStruct((n, dm), F32),
        scratch_shapes=[pltpu.VMEM((2, 2, t // ISSUE_UNROLL, ISSUE_UNROLL, dm), F32),
                        pltpu.SemaphoreType.DMA((2,))],
        compiler_params=_cparams(("arbitrary",)),
        name="combine_ln",
    )(dest[0], dest[1], dest[0], dest[1], route_t, x2d, y, g.reshape(1, dm), b.reshape(1, dm))


def moe_ln(x2d, router_w, wg, wu, wd, g, b, bm=1024):
    n, _ = x2d.shape
    bm = min(bm, n)
    route, route_t, cnt = route_tokens(x2d, router_w)
    counts = cnt[:, 0].astype(jnp.int32)
    padded = (counts + bm - 1) // bm * bm
    pad_end = jnp.cumsum(padded)
    pad_start = (pad_end - padded).astype(jnp.int32)
    n_blocks = 2 * n // bm + N_EXPERTS
    n_slots = n_blocks * bm
    block_start = jnp.arange(n_blocks, dtype=jnp.int32) * bm
    block_e = jnp.minimum(jnp.sum(pad_end[None, :] <= block_start[:, None], axis=1),
                          N_EXPERTS - 1).astype(jnp.int32)
    n_valid = (pad_end[-1:] // bm).astype(jnp.int32)
    dest = slot_of_assignment(pad_start, route)
    pad_lo = jnp.concatenate([pad_start + counts, pad_end[-1:]]).astype(jnp.int32)
    pad_n = jnp.concatenate([padded - counts, (n_slots - pad_end[-1:]) // SUBLANES]).astype(jnp.int32)
    slots = dispatch_rows(pad_lo, pad_n, dest, x2d, n_slots)
    y = expert_swiglu(block_e, n_valid, slots, wg, wu, wd, bm)
    return combine_ln(dest, route_t, x2d, y, g, b)


def kernel(x, rel_bias, w_in, diff_lambda, diff_subln_g, pool_w, pool_scale, ret_gn_g, w_out,
           ln1_g, ln1_b, ln2_g, ln2_b, ffn_w_gate, ffn_w_up, ffn_w_down,
           router_w, moe_w_gate, moe_w_up, moe_w_down):
    bsz, seq, dm = x.shape
    x2d = x.reshape(bsz * seq, dm)
    bias_tiles = attn_bias_tiles(rel_bias, min(ATTN_BLOCK, seq))
    col_scale = jnp.where(jnp.arange(w_in.shape[2]) < DA_WIDTH, Q_SCALE, 1.0).astype(F32)
    for l in range(DEPTH):
        lam_init = 0.8 - 0.6 * math.exp(-0.3 * l)
        proj = in_proj(x2d, (w_in[l] * col_scale).astype(BF16)).reshape(bsz, seq, -1)
        y_da = diff_attention(proj, bias_tiles, diff_lambda[l], diff_subln_g[l], lam_init)
        y_pool = pool_mixer(proj, pool_w[l], pool_scale[l])
        y_ret = retention(proj, ret_gn_g[l])
        flat = lambda a: a.reshape(bsz * seq, -1)
        mixed = (flat(y_da), flat(y_pool), flat(y_ret), x2d, w_out[l].astype(BF16), ln1_g[l], ln1_b[l])
        j = l // 2
        if l % 2 == 0:
            x2d = out_proj_ffn_ln(*mixed, ffn_w_gate[j].astype(BF16), ffn_w_up[j].astype(BF16),
                                  ffn_w_down[j].astype(BF16), ln2_g[l], ln2_b[l])
        else:
            x2d = out_proj_ln(*mixed)
            x2d = moe_ln(x2d, router_w[j], moe_w_gate[j], moe_w_up[j], moe_w_down[j], ln2_g[l], ln2_b[l])
    return x2d.reshape(bsz, seq, dm)
```

```python
import functools
import math

import jax
import jax.numpy as jnp
from jax import lax
from jax.experimental import pallas as pl
from jax.experimental.pallas import tpu as pltpu

F32 = jnp.float32
BF16 = jnp.bfloat16

DEPTH = 2
DA_HEAD_DIM = 64
DA_V_DIM = 128
DA_HEADS = 4
DA_WIDTH = 512
POOL_WIDTH = 256
POOL_WINDOWS = (2, 4, 8, 16)
POOL_GROUP_DIM = 64
POOL_HALO = 16
RET_WIDTH = 256
RET_HEAD_DIM = 64
RET_HEADS = 4
RET_CHUNK = 128
REL_BUCKETS = 32
REL_MAX_DIST = 128
N_EXPERTS = 8
ALPHA = (2 * DEPTH) ** 0.25
LN_EPS = 1e-5
NORM_EPS = 1e-6
NEG_BIG = -1e30

POOL_COL_BLOCK = 3 * DA_WIDTH // POOL_WIDTH
RET_COL_BLOCK = POOL_COL_BLOCK + 1

VMEM_LIMIT = 56 * 1024 * 1024


def _cparams(sem, vmem=VMEM_LIMIT):
    return pltpu.CompilerParams(dimension_semantics=sem, vmem_limit_bytes=vmem)


def _nt_dot(a, b):
    return lax.dot_general(a, b, (((1,), (1,)), ((), ())), preferred_element_type=F32)


def _dot(a, b):
    return jnp.dot(a, b, preferred_element_type=F32)


def _layer_norm(z, g, b):
    mu = jnp.mean(z, axis=-1, keepdims=True)
    zc = z - mu
    var = jnp.mean(zc * zc, axis=-1, keepdims=True)
    return zc * lax.rsqrt(var + LN_EPS) * g + b


def _silu(x):
    return x / (1.0 + jnp.exp(-x))


def _inproj_kernel(x_ref, w_ref, o_ref, *, tn):
    xb = x_ref[...].astype(BF16)
    for j in range(0, w_ref.shape[1], tn):
        o_ref[:, j:j + tn] = _dot(xb, w_ref[:, j:j + tn]).astype(o_ref.dtype)


def in_proj(x2d, w, tm=512, tn=256):
    n, k = x2d.shape
    m = w.shape[1]
    tm = min(tm, n)
    return pl.pallas_call(
        functools.partial(_inproj_kernel, tn=tn),
        grid=(n // tm,),
        in_specs=[pl.BlockSpec((tm, k), lambda i: (i, 0)),
                  pl.BlockSpec((k, m), lambda i: (0, 0))],
        out_specs=pl.BlockSpec((tm, m), lambda i: (i, 0)),
        out_shape=jax.ShapeDtypeStruct((n, m), BF16),
        compiler_params=_cparams(("parallel",)),
        name="in_proj",
    )(x2d, w)


def _t5_bucket(dist):
    n = jnp.maximum(dist, 0)
    max_exact = REL_BUCKETS // 2
    nf = jnp.maximum(n, 1).astype(F32)
    large = max_exact + (jnp.log(nf / max_exact) / math.log(REL_MAX_DIST / max_exact)
                         * (REL_BUCKETS - max_exact)).astype(jnp.int32)
    large = jnp.minimum(large, REL_BUCKETS - 1)
    return jnp.where(n < max_exact, n, large)


LOG2E = math.log2(math.e)
Q_SCALE = DA_HEAD_DIM ** -0.5 * LOG2E
ATTN_BLOCK = 512
ATTN_CHUNK = 512
ONES_ROWS = 16


def attn_bias_tiles(rel_bias, t):
    table = rel_bias.astype(F32).reshape(REL_BUCKETS, DA_HEADS * 2)
    vec = table[_t5_bucket(jnp.arange(2 * t))]
    far = table[_t5_bucket(jnp.array(2 * t))]
    vec = ((vec - far[None, :]) * LOG2E).T
    masked = jnp.full((DA_HEADS * 2, t), NEG_BIG, F32)
    u_diag = jnp.concatenate([vec[:, :t], masked], axis=1)
    u_prev = jnp.concatenate([vec[:, t:], vec[:, :t]], axis=1)

    def toeplitz(u):
        skew = jnp.tile(u, (1, t))[:, :t * (2 * t - 1)].reshape(-1, t, 2 * t - 1)
        return skew[:, :, :t]

    def per_head(a):
        return a.reshape(DA_HEADS, 2, t, t).transpose(0, 2, 1, 3).reshape(DA_HEADS, t, 2 * t)

    return jnp.stack([per_head(toeplitz(u_diag)), per_head(toeplitz(u_prev))], axis=1)


def _attn_kernel(q_ref, k_ref, v_ref, bias_ref, lam_ref, g_ref, o_ref, vt, m_s, acc, s_a, s_b, *, t, lam_init):
    nk = vt.shape[0]
    ones = jnp.ones((ONES_ROWS, t), BF16)
    for ki in range(nk):
        v_t = v_ref[0, ki * t:(ki + 1) * t, :].astype(F32).T.astype(BF16)
        vt[ki] = jnp.concatenate([v_t, ones], axis=0)

    def query_tile(qi, carry):
        _attn_query_tile(qi, q_ref, k_ref, bias_ref, lam_ref, g_ref, o_ref, vt, m_s, acc, s_a, s_b,
                         t=t, lam_init=lam_init)
        return carry

    lax.fori_loop(0, nk // 2, query_tile, 0)


def _attn_query_tile(qi, q_ref, k_ref, bias_ref, lam_ref, g_ref, o_ref, vt, m_s, acc, s_a, s_b, *, t, lam_init):
    tq = 2 * t
    q_rows = pl.ds(pl.multiple_of(qi * tq, tq), tq)
    q = q_ref[0, q_rows, :]
    lane = lax.broadcasted_iota(jnp.int32, q.shape, 1)
    zero = jnp.zeros_like(q)
    qcat = jnp.concatenate([jnp.where(lane < DA_HEAD_DIM, q, zero),
                            jnp.where(lane >= DA_HEAD_DIM, q, zero)], axis=0)
    m_s[...] = jnp.full(m_s.shape, NEG_BIG, F32)
    acc[...] = jnp.zeros(acc.shape, F32)

    chunk = min(ATTN_CHUNK, t)
    DIAG, PREV, FAR, SKIP = 0, 1, None, "skip"

    def step(nxt, cur):
        kb = None
        if nxt is not None:
            kb = k_ref[0, pl.ds(pl.multiple_of(nxt[0] * t, t), t), :]
        for c in range(0, 2 * tq, chunk):
            cols = slice(c, c + chunk)
            mp, qoff = divmod(c, tq)
            half, ioff = divmod(qoff, t)
            ci = c // chunk
            if nxt is not None and not (len(nxt) > 2 and nxt[2][half] == SKIP):
                nxt[1][ci] = _nt_dot(kb, qcat[cols, :])
            if cur is not None:
                ki, s_ref, kinds = cur
                kind = kinds[half]
                if kind == SKIP:
                    continue
                s = s_ref[ci]
                if kind is not FAR:
                    s = s + bias_ref[0, kind, :, mp * t + ioff:mp * t + ioff + chunk]
                m_prev = m_s[ci]
                m_new = jnp.maximum(m_prev, jnp.max(s, axis=0, keepdims=True))
                alpha = jnp.exp2(m_prev - m_new)
                p = jnp.exp2(s - m_new).astype(BF16)
                acc[ci] = alpha * acc[ci] + _dot(vt[ki], p)
                m_s[ci] = m_new

    upper = (SKIP, DIAG)
    lower = (DIAG, PREV)
    before = (PREV, FAR)
    plain = (FAR, FAR)
    step((2 * qi + 1, s_a, upper), None)
    step((2 * qi, s_b), (2 * qi + 1, s_a, upper))

    @pl.when(qi == 0)
    def _():
        step(None, (0, s_b, lower))

    @pl.when(qi >= 1)
    def _():
        n_far = 2 * qi - 1
        step((2 * qi - 1, s_a), (2 * qi, s_b, lower))
        step((0, s_b), (2 * qi - 1, s_a, before))

        def pair(jj, carry):
            j = 2 * jj
            step((j + 1, s_a), (j, s_b, plain))
            step((j + 2, s_b), (j + 1, s_a, plain))
            return carry

        lax.fori_loop(0, n_far // 2, pair, 0)
        step(None, (n_far - 1, s_b, plain))

    lm = lam_ref[...]
    lam = (jnp.exp(jnp.sum(lm[0:1] * lm[1:2], keepdims=True))
           - jnp.exp(jnp.sum(lm[2:3] * lm[3:4], keepdims=True)) + lam_init)
    a = jnp.concatenate([acc[ci] for ci in range(acc.shape[0])], axis=1)
    o = a[:DA_V_DIM, :] / a[DA_V_DIM:DA_V_DIM + 1, :]
    out = o[:, :tq] - lam * o[:, tq:]
    out = out * lax.rsqrt(jnp.mean(out * out, axis=0, keepdims=True) + NORM_EPS) * g_ref[...]
    o_ref[0, q_rows, :] = (out * (1.0 - lam_init)).T.astype(o_ref.dtype)


def diff_attention(proj3, bias_tiles, lam_params, subln_g, lam_init):
    b, s, _ = proj3.shape
    t = bias_tiles.shape[2]
    chunk = min(ATTN_CHUNK, t)
    nch = 4 * t // chunk
    seq = lambda col0: pl.BlockSpec((1, s, DA_V_DIM), lambda bi, h: (bi, 0, col0 + h))
    return pl.pallas_call(
        functools.partial(_attn_kernel, t=t, lam_init=lam_init),
        grid=(b, DA_HEADS),
        in_specs=[
            seq(0), seq(DA_HEADS), seq(2 * DA_HEADS),
            pl.BlockSpec((1, 2, t, 2 * t), lambda bi, h: (h, 0, 0, 0)),
            pl.BlockSpec((4, DA_HEAD_DIM), lambda bi, h: (0, 0)),
            pl.BlockSpec((DA_V_DIM, 1), lambda bi, h: (0, 0)),
        ],
        out_specs=pl.BlockSpec((1, s, DA_V_DIM), lambda bi, h: (bi, 0, h)),
        out_shape=jax.ShapeDtypeStruct((b, s, DA_WIDTH), BF16),
        scratch_shapes=[pltpu.VMEM((s // t, DA_V_DIM + ONES_ROWS, t), BF16),
                        pltpu.VMEM((nch, 1, chunk), F32),
                        pltpu.VMEM((nch, DA_V_DIM + ONES_ROWS, chunk), F32),
                        pltpu.VMEM((nch, t, chunk), F32), pltpu.VMEM((nch, t, chunk), F32)],
        compiler_params=_cparams(("parallel", "parallel")),
        name="diff_attention",
    )(proj3, proj3, proj3, bias_tiles, lam_params, subln_g.reshape(DA_V_DIM, 1))


def _pool_kernel(p_ref, w_ref, scale_ref, o_ref, halo, *, t):
    si = pl.program_id(1)

    @pl.when(si == 0)
    def _():
        halo[...] = jnp.zeros(halo.shape, F32)

    p = p_ref[0].astype(F32)
    ext = jnp.concatenate([halo[...], p], axis=0)
    halo[...] = p[t - POOL_HALO:, :]
    sums = {1: ext}
    w = 1
    while w < POOL_WINDOWS[-1]:
        sums[2 * w] = sums[w] + pltpu.roll(sums[w], w, 0)
        w *= 2
    lane = lax.broadcasted_iota(jnp.int32, (t, POOL_WIDTH), 1)
    pos = (si * t + lax.broadcasted_iota(jnp.int32, (t, POOL_WIDTH), 0) + 1).astype(F32)
    wsum = sums[POOL_WINDOWS[-1]][POOL_HALO:, :]
    cnt = jnp.minimum(pos, float(POOL_WINDOWS[-1]))
    for gi in range(len(POOL_WINDOWS) - 2, -1, -1):
        in_group = lane < (gi + 1) * POOL_GROUP_DIM
        wsum = jnp.where(in_group, sums[POOL_WINDOWS[gi]][POOL_HALO:, :], wsum)
        cnt = jnp.where(in_group, jnp.minimum(pos, float(POOL_WINDOWS[gi])), cnt)
    pooled = wsum / cnt - p
    mixed = _dot(pooled.astype(BF16), w_ref[...])
    o_ref[0] = (mixed * scale_ref[...]).astype(o_ref.dtype)


def pool_mixer(proj3, pool_w, pool_scale, t=512):
    b, s, _ = proj3.shape
    t = min(t, s)
    g = len(POOL_WINDOWS)
    wbd = (jnp.eye(g, dtype=F32)[:, None, :, None] * pool_w.astype(F32)[:, :, None, :]).reshape(
        POOL_WIDTH, POOL_WIDTH).astype(BF16)
    return pl.pallas_call(
        functools.partial(_pool_kernel, t=t),
        grid=(b, s // t),
        in_specs=[pl.BlockSpec((1, t, POOL_WIDTH), lambda bi, si: (bi, si, POOL_COL_BLOCK)),
                  pl.BlockSpec((POOL_WIDTH, POOL_WIDTH), lambda bi, si: (0, 0)),
                  pl.BlockSpec((1, POOL_WIDTH), lambda bi, si: (0, 0))],
        out_specs=pl.BlockSpec((1, t, POOL_WIDTH), lambda bi, si: (bi, si, 0)),
        out_shape=jax.ShapeDtypeStruct((b, s, POOL_WIDTH), BF16),
        scratch_shapes=[pltpu.VMEM((POOL_HALO, POOL_WIDTH), F32)],
        compiler_params=_cparams(("parallel", "arbitrary")),
        name="pool_mixer",
    )(proj3, wbd, pool_scale.reshape(1, POOL_WIDTH))


def _retention_tables(s, t):
    d, hn, c = RET_HEAD_DIM, RET_HEADS, RET_CHUNK
    half = d // 2
    inv = 10000.0 ** (-jnp.linspace(0.0, 1.0, half, dtype=F32))
    ang = jnp.arange(s)[:, None].astype(F32) * inv[None, :]
    cos, sin = jnp.cos(ang), jnp.sin(ang)
    cos_t = jnp.tile(jnp.concatenate([cos, cos], axis=-1), (1, hn))
    sin_t = jnp.tile(jnp.concatenate([-sin, sin], axis=-1), (1, hn))
    log_gamma = jnp.log(1.0 - 2.0 ** (-5.0 - jnp.arange(hn, dtype=F32)))
    idx = jnp.arange(c, dtype=F32)
    rel = idx[:, None] - idx[None, :]
    intra = jnp.where(rel >= 0, jnp.exp(log_gamma[:, None, None] * jnp.maximum(rel, 0.0)), 0.0)
    q_decay = jnp.exp(log_gamma[:, None] * (idx + 1.0))
    k_decay = jnp.exp(log_gamma[:, None] * (c - 1.0 - idx))
    chunk_decay = jnp.exp(log_gamma * c)
    lanes = lambda a: jnp.tile(jnp.repeat(a.T, d, axis=1), (t // c, 1))
    l = jnp.arange(hn * d)
    partner = jnp.where(l % d < half, l + half, l - half)
    swap = (l[:, None] == partner[None, :]).astype(BF16)
    same = (jnp.arange(2 * d)[:, None] // d) == (jnp.arange(2 * d)[None, :] // d)
    decay_bd = jnp.where(same[None], jnp.repeat(chunk_decay, d).reshape(hn // 2, 2 * d, 1), 0.0)
    avg = jnp.where(same, 1.0 / d, 0.0).astype(BF16)
    return cos_t, sin_t, intra, lanes(q_decay), lanes(k_decay), swap, decay_bd.astype(F32), avg


def _ret_kernel(q_ref, k_ref, v_ref, g_ref, cos_ref, sin_ref, intra_ref, qd_ref, kd_ref, swap_ref, decay_ref,
                avg_ref, gn_ref, o_ref, state, *, t):
    si = pl.program_id(1)
    d, c = RET_HEAD_DIM, RET_CHUNK
    pw = 2 * d

    @pl.when(si == 0)
    def _():
        state[...] = jnp.zeros(state.shape, F32)

    def rotate(x_ref):
        x = x_ref[0]
        return x.astype(F32) * cos_ref[...] + _dot(x, swap_ref[...]) * sin_ref[...]

    def lane_mean(a):
        hi = a.astype(BF16)
        lo = (a - hi.astype(F32)).astype(BF16)
        return _dot(hi, avg_ref[...]) + _dot(lo, avg_ref[...])

    q = rotate(q_ref)
    k = rotate(k_ref) * (d ** -0.5)
    qs = (q * qd_ref[...]).astype(BF16)
    ks = (k * kd_ref[...]).astype(BF16)
    qb = q.astype(BF16)
    kb = k.astype(BF16)
    v = v_ref[0]
    first = lax.broadcasted_iota(jnp.int32, (c, pw), 1) < d
    same = ((lax.broadcasted_iota(jnp.int32, (pw, pw), 0) < d)
            == (lax.broadcasted_iota(jnp.int32, (pw, pw), 1) < d))
    zero = jnp.zeros((c, pw), BF16)

    chunks = [slice(ci * c, (ci + 1) * c) for ci in range(t // c)]
    pairs = [slice(p * pw, (p + 1) * pw) for p in range(RET_HEADS // 2)]
    scores = {}
    for p, cols in enumerate(pairs):
        for ci, rows in enumerate(chunks):
            qp, kp = qb[rows, cols], kb[rows, cols]
            scores[p, ci] = ((_nt_dot(jnp.where(first, qp, zero), kp) * intra_ref[2 * p]).astype(BF16),
                             (_nt_dot(jnp.where(first, zero, qp), kp) * intra_ref[2 * p + 1]).astype(BF16))
    intra_out, increment = {}, {}
    for p, cols in enumerate(pairs):
        for ci, rows in enumerate(chunks):
            vp = v[rows, cols]
            intra_out[p, ci] = jnp.where(first, _dot(scores[p, ci][0], vp), _dot(scores[p, ci][1], vp))
            increment[p, ci] = jnp.where(same, _dot(ks[rows, cols].T, vp), 0.0)
    ys = {}
    for p, cols in enumerate(pairs):
        st = state[p]
        for ci, rows in enumerate(chunks):
            ys[p, ci] = intra_out[p, ci] + _dot(qs[rows, cols], st.astype(BF16))
            st = st * decay_ref[p] + increment[p, ci]
        state[p] = st
    y = jnp.concatenate([jnp.concatenate([ys[p, ci] for ci in range(len(chunks))], axis=0)
                         for p in range(len(pairs))], axis=1)
    mean = jnp.concatenate([lane_mean(y[:, cols]) for cols in pairs], axis=1)
    yc = y - mean
    var = jnp.concatenate([lane_mean((yc * yc)[:, cols]) for cols in pairs], axis=1)
    yn = yc * lax.rsqrt(var + NORM_EPS)
    gate = _silu(g_ref[0].astype(F32))
    o_ref[0] = (gate * (yn * gn_ref[...])).astype(o_ref.dtype)


def retention(proj3, gn_g, t=512):
    b, s, _ = proj3.shape
    t = min(t, s)
    cos_t, sin_t, intra, qd, kd, swap, decay_bd, avg = _retention_tables(s, t)
    pw = 2 * RET_HEAD_DIM
    col = lambda j: pl.BlockSpec((1, t, RET_WIDTH), lambda bi, si: (bi, si, RET_COL_BLOCK + j))
    const2 = lambda shape: pl.BlockSpec(shape, lambda bi, si: (0, 0))
    const3 = lambda shape: pl.BlockSpec(shape, lambda bi, si: (0, 0, 0))
    return pl.pallas_call(
        functools.partial(_ret_kernel, t=t),
        grid=(b, s // t),
        in_specs=[col(0), col(1), col(2), col(3),
                  pl.BlockSpec((t, RET_WIDTH), lambda bi, si: (si, 0)),
                  pl.BlockSpec((t, RET_WIDTH), lambda bi, si: (si, 0)),
                  const3((RET_HEADS, RET_CHUNK, RET_CHUNK)),
                  const2((t, RET_WIDTH)), const2((t, RET_WIDTH)), const2((RET_WIDTH, RET_WIDTH)),
                  const3((RET_HEADS // 2, pw, pw)), const2((pw, pw)), const2((1, RET_WIDTH))],
        out_specs=pl.BlockSpec((1, t, RET_WIDTH), lambda bi, si: (bi, si, 0)),
        out_shape=jax.ShapeDtypeStruct((b, s, RET_WIDTH), BF16),
        scratch_shapes=[pltpu.VMEM((RET_HEADS // 2, pw, pw), F32)],
        compiler_params=_cparams(("parallel", "arbitrary")),
        name="retention",
    )(proj3, proj3, proj3, proj3, cos_t, sin_t, intra, qd, kd, swap, decay_bd, avg, gn_g.reshape(1, RET_WIDTH))


def _mix_ln(da_ref, pool_ref, ret_ref, x_ref, w_ref, g_ref, b_ref):
    e0, e1 = DA_WIDTH, DA_WIDTH + POOL_WIDTH
    mix = (_dot(da_ref[...], w_ref[0:e0, :]) + _dot(pool_ref[...], w_ref[e0:e1, :])
           + _dot(ret_ref[...], w_ref[e1:, :]))
    return _layer_norm(ALPHA * x_ref[...] + mix, g_ref[...], b_ref[...])


def _outproj_kernel(da_ref, pool_ref, ret_ref, x_ref, w_ref, g_ref, b_ref, o_ref):
    o_ref[...] = _mix_ln(da_ref, pool_ref, ret_ref, x_ref, w_ref, g_ref, b_ref)


def out_proj_ln(y_da, y_pool, y_ret, x2d, w, g, b, tm=512):
    n, dm = x2d.shape
    tm = min(tm, n)
    row = lambda width: pl.BlockSpec((tm, width), lambda i: (i, 0))
    const = lambda shape: pl.BlockSpec(shape, lambda i: (0, 0))
    return pl.pallas_call(
        _outproj_kernel,
        grid=(n // tm,),
        in_specs=[row(DA_WIDTH), row(POOL_WIDTH), row(RET_WIDTH), row(dm),
                  const(w.shape), const((1, dm)), const((1, dm))],
        out_specs=row(dm),
        out_shape=jax.ShapeDtypeStruct((n, dm), F32),
        compiler_params=_cparams(("parallel",)),
        name="out_proj_ln",
    )(y_da, y_pool, y_ret, x2d, w, g.reshape(1, dm), b.reshape(1, dm))


def _mix_ffn_kernel(da_ref, pool_ref, ret_ref, x_ref, wo_ref, g1_ref, b1_ref, wg_ref, wu_ref, wd_ref,
                    g2_ref, b2_ref, o_ref, *, tf):
    x = _mix_ln(da_ref, pool_ref, ret_ref, x_ref, wo_ref, g1_ref, b1_ref)
    xb = x.astype(BF16)
    acc = jnp.zeros(x.shape, F32)
    for f0 in range(0, wg_ref.shape[1], tf):
        hidden = _silu(_dot(xb, wg_ref[:, f0:f0 + tf])) * _dot(xb, wu_ref[:, f0:f0 + tf])
        acc = acc + _dot(hidden.astype(BF16), wd_ref[f0:f0 + tf, :])
    o_ref[...] = _layer_norm(ALPHA * x + acc, g2_ref[...], b2_ref[...])


def out_proj_ffn_ln(y_da, y_pool, y_ret, x2d, wo, g1, b1, wg, wu, wd, g2, b2, tm=512, tf=256):
    n, dm = x2d.shape
    tm = min(tm, n)
    row = lambda width: pl.BlockSpec((tm, width), lambda i: (i, 0))
    resident = lambda shape: pl.BlockSpec(shape, lambda i: (0, 0), pipeline_mode=pl.Buffered(1))
    vec = lambda a: a.reshape(1, dm)
    return pl.pallas_call(
        functools.partial(_mix_ffn_kernel, tf=tf),
        grid=(n // tm,),
        in_specs=[row(DA_WIDTH), row(POOL_WIDTH), row(RET_WIDTH), row(dm),
                  resident(wo.shape), resident((1, dm)), resident((1, dm)),
                  resident(wg.shape), resident(wu.shape), resident(wd.shape),
                  resident((1, dm)), resident((1, dm))],
        out_specs=row(dm),
        out_shape=jax.ShapeDtypeStruct((n, dm), F32),
        compiler_params=_cparams(("parallel",)),
        name="out_proj_ffn_ln",
    )(y_da, y_pool, y_ret, x2d, wo, vec(g1), vec(b1), wg, wu, wd, vec(g2), vec(b2))


ROUTE_ROWS = 8
LANES = 128


def _router_kernel(x_ref, w_ref, tri_ref, route_ref, route_t_ref, cnt_ref, carry, *, t):
    i = pl.program_id(0)

    @pl.when(i == 0)
    def _():
        carry[...] = jnp.zeros(carry.shape, F32)

    x = x_ref[...]
    xh = x.astype(BF16)
    xl = (x - xh.astype(F32)).astype(BF16)
    w = w_ref[...]
    wh = w.astype(BF16)
    wl = (w - wh.astype(F32)).astype(BF16)
    logits = _nt_dot(wh, xh) + _nt_dot(wl, xh) + _nt_dot(wh, xl)

    row = lax.broadcasted_iota(jnp.int32, logits.shape, 0)
    v0 = jnp.max(logits, axis=0, keepdims=True)
    i0 = jnp.min(jnp.where(logits == v0, row, N_EXPERTS), axis=0, keepdims=True)
    rest = jnp.where(row == i0, -jnp.inf, logits)
    v1 = jnp.max(rest, axis=0, keepdims=True)
    i1 = jnp.min(jnp.where(rest == v1, row, N_EXPERTS), axis=0, keepdims=True)
    ex = jnp.exp(v1 - v0)
    gate0 = 1.0 / (1.0 + ex)
    gate1 = ex / (1.0 + ex)

    oh0 = row == i0
    oh1 = row == i1
    member = jnp.where(oh0 | oh1, 1.0, 0.0)
    before = _dot(member.astype(BF16), tri_ref[...]) + carry[:, 0:1]
    rank0 = jnp.sum(jnp.where(oh0, before, 0.0), axis=0, keepdims=True)
    rank1 = jnp.sum(jnp.where(oh1, before, 0.0), axis=0, keepdims=True)
    carry[...] = carry[...] + jnp.sum(member, axis=1, keepdims=True)
    cnt_ref[...] = carry[...]

    route = jnp.concatenate([i0.astype(F32), i1.astype(F32), rank0, rank1, gate0, gate1,
                             jnp.zeros((2, t), F32)], axis=0)
    route_ref[...] = route
    padded = jnp.concatenate([route, jnp.zeros((LANES - ROUTE_ROWS, t), F32)], axis=0)
    route_t_ref[...] = padded.T


def route_tokens(x2d, router_w, t=1024):
    n, dm = x2d.shape
    t = min(t, n)
    tri = (jnp.arange(t)[:, None] < jnp.arange(t)[None, :]).astype(BF16)
    return pl.pallas_call(
        functools.partial(_router_kernel, t=t),
        grid=(n // t,),
        in_specs=[pl.BlockSpec((t, dm), lambda i: (i, 0)),
                  pl.BlockSpec((N_EXPERTS, dm), lambda i: (0, 0)),
                  pl.BlockSpec((t, t), lambda i: (0, 0))],
        out_specs=[pl.BlockSpec((ROUTE_ROWS, t), lambda i: (0, i)),
                   pl.BlockSpec((t, LANES), lambda i: (i, 0)),
                   pl.BlockSpec((N_EXPERTS, LANES), lambda i: (0, 0))],
        out_shape=[jax.ShapeDtypeStruct((ROUTE_ROWS, n), F32),
                   jax.ShapeDtypeStruct((n, LANES), F32),
                   jax.ShapeDtypeStruct((N_EXPERTS, LANES), F32)],
        scratch_shapes=[pltpu.VMEM((N_EXPERTS, LANES), F32)],
        compiler_params=_cparams(("arbitrary",)),
        name="route_tokens",
    )(x2d, router_w.T, tri)


def _dest_kernel(start_ref, route_ref, dest_ref):
    r = route_ref[...]
    for k in range(2):
        e = r[k:k + 1, :].astype(jnp.int32)
        base = jnp.zeros(e.shape, jnp.int32)
        for ei in range(N_EXPERTS):
            base = jnp.where(e == ei, start_ref[ei], base)
        dest_ref[k:k + 1, :] = base + r[2 + k:3 + k, :].astype(jnp.int32)


def slot_of_assignment(pad_start, route, t=1024):
    n = route.shape[1]
    t = min(t, n)
    grid_spec = pltpu.PrefetchScalarGridSpec(
        num_scalar_prefetch=1,
        grid=(n // t,),
        in_specs=[pl.BlockSpec((ROUTE_ROWS, t), lambda i, ps: (0, i))],
        out_specs=pl.BlockSpec((2, t), lambda i, ps: (0, i)),
    )
    return pl.pallas_call(
        _dest_kernel,
        grid_spec=grid_spec,
        out_shape=jax.ShapeDtypeStruct((2, n), jnp.int32),
        compiler_params=_cparams(("parallel",)),
        name="slot_of_assignment",
    )(pad_start, route)


def _row_copy(src, src_row, dst, dst_row, sem):
    return pltpu.make_async_copy(src.at[pl.ds(src_row, 1)], dst.at[pl.ds(dst_row, 1)], sem)


SUBLANES = 8
ISSUE_UNROLL = SUBLANES


def _dispatch_kernel(pad_lo_ref, pad_n_ref, d0_ref, d1_ref, x_ref, slots_out, zrow, sem, zsem, *, t):
    @pl.when(pl.program_id(0) == 0)
    def _():
        zrow[...] = jnp.zeros(zrow.shape, zrow.dtype)
        for e in range(N_EXPERTS):
            lo = pad_lo_ref[e]

            def zero_start(r, carry):
                _row_copy(zrow, 0, slots_out, lo + r, zsem).start()
                return carry

            def zero_wait(r, carry):
                _row_copy(zrow, 0, slots_out, lo + r, zsem).wait()
                return carry

            lax.fori_loop(0, pad_n_ref[e], zero_start, 0)
            lax.fori_loop(0, pad_n_ref[e], zero_wait, 0)

        tail_lo = pad_lo_ref[N_EXPERTS]
        group = zrow.shape[0]

        def tail_copy(r):
            dst = slots_out.at[pl.ds(pl.multiple_of(tail_lo + r * group, group), group)]
            return pltpu.make_async_copy(zrow, dst, zsem)

        def tail_start(r, carry):
            tail_copy(r).start()
            return carry

        def tail_wait(r, carry):
            tail_copy(r).wait()
            return carry

        lax.fori_loop(0, pad_n_ref[N_EXPERTS], tail_start, 0)
        lax.fori_loop(0, pad_n_ref[N_EXPERTS], tail_wait, 0)

    def issue(j, carry):
        base = pl.multiple_of(j * ISSUE_UNROLL, ISSUE_UNROLL)
        group = x_ref.at[pl.ds(base, ISSUE_UNROLL)]
        for u in range(ISSUE_UNROLL):
            for k, d_ref in enumerate((d0_ref, d1_ref)):
                _row_copy(group, u, slots_out, d_ref[base + u], sem).start()
        return carry

    lax.fori_loop(0, t // ISSUE_UNROLL, issue, 0)
    for k in range(2):
        pltpu.make_async_copy(x_ref, slots_out.at[pl.ds(0, t)], sem).wait()


def dispatch_rows(pad_lo, pad_n, dest, x2d, n_slots, t=512):
    n, dm = x2d.shape
    t = min(t, n)
    grid_spec = pltpu.PrefetchScalarGridSpec(
        num_scalar_prefetch=2,
        grid=(n // t,),
        in_specs=[pl.BlockSpec((t,), lambda i, lo, cnt: (i,), memory_space=pltpu.SMEM),
                  pl.BlockSpec((t,), lambda i, lo, cnt: (i,), memory_space=pltpu.SMEM),
                  pl.BlockSpec((t, dm), lambda i, lo, cnt: (i, 0))],
        out_specs=pl.BlockSpec(memory_space=pl.ANY),
        scratch_shapes=[pltpu.VMEM((SUBLANES, dm), x2d.dtype), pltpu.SemaphoreType.DMA,
                        pltpu.SemaphoreType.DMA],
    )
    return pl.pallas_call(
        functools.partial(_dispatch_kernel, t=t),
        grid_spec=grid_spec,
        out_shape=jax.ShapeDtypeStruct((n_slots, dm), x2d.dtype),
        compiler_params=_cparams(("arbitrary",)),
        name="dispatch_rows",
    )(pad_lo, pad_n, dest[0], dest[1], x2d)


def _expert_kernel(be_ref, nv_ref, x_ref, wg_ref, wu_ref, wd_ref, o_ref, xb):
    blk = pl.program_id(0)
    f = pl.program_id(1)
    valid = blk < nv_ref[0]

    @pl.when(f == 0)
    def _():
        o_ref[...] = jnp.zeros(o_ref.shape, o_ref.dtype)

    @pl.when(valid & (f == 0))
    def _():
        xb[...] = x_ref[...].astype(BF16)

    @pl.when(valid)
    def _():
        hidden = (_silu(_dot(xb[...], wg_ref[0].astype(BF16)))
                  * _dot(xb[...], wu_ref[0].astype(BF16)))
        o_ref[...] += _dot(hidden.astype(BF16), wd_ref[0].astype(BF16))


def expert_swiglu(block_e, n_valid, slots, wg, wu, wd, bm, tf=512):
    n_slots, dm = slots.shape
    edim = wg.shape[2]
    grid_spec = pltpu.PrefetchScalarGridSpec(
        num_scalar_prefetch=2,
        grid=(n_slots // bm, edim // tf),
        in_specs=[pl.BlockSpec((bm, dm), lambda b, f, be, nv: (jnp.minimum(b, nv[0] - 1), 0)),
                  pl.BlockSpec((1, dm, tf), lambda b, f, be, nv: (be[b], 0, f)),
                  pl.BlockSpec((1, dm, tf), lambda b, f, be, nv: (be[b], 0, f)),
                  pl.BlockSpec((1, tf, dm), lambda b, f, be, nv: (be[b], f, 0))],
        out_specs=pl.BlockSpec((bm, dm), lambda b, f, be, nv: (b, 0)),
        scratch_shapes=[pltpu.VMEM((bm, dm), BF16)],
    )
    return pl.pallas_call(
        _expert_kernel,
        grid_spec=grid_spec,
        out_shape=jax.ShapeDtypeStruct((n_slots, dm), F32),
        compiler_params=_cparams(("parallel", "arbitrary")),
        name="expert_swiglu",
    )(block_e, n_valid, slots, wg, wu, wd)


def _combine_kernel(d0_ref, d1_ref, d0_next_ref, d1_next_ref, rt_ref, x_ref, y_hbm, g_ref, b_ref, o_ref,
                    rows, sem, *, t):
    i = pl.program_id(0)
    cur = i % 2

    def gather(d_refs, buf):
        def issue(j, carry):
            for u in range(ISSUE_UNROLL):
                for k in range(2):
                    _row_copy(y_hbm, d_refs[k][j * ISSUE_UNROLL + u], rows.at[buf, k, j], u, sem.at[buf]).start()
            return carry

        lax.fori_loop(0, t // ISSUE_UNROLL, issue, 0)

    @pl.when(i == 0)
    def _():
        gather((d0_ref, d1_ref), 0)

    @pl.when(i + 1 < pl.num_programs(0))
    def _():
        gather((d0_next_ref, d1_next_ref), 1 - cur)

    for k in range(2):
        pltpu.make_async_copy(y_hbm.at[pl.ds(0, t)], o_ref, sem.at[cur]).wait()

    rt = rt_ref[...]
    dm = o_ref.shape[1]
    mixed = rt[:, 4:5] * rows[cur, 0].reshape(t, dm) + rt[:, 5:6] * rows[cur, 1].reshape(t, dm)
    o_ref[...] = _layer_norm(ALPHA * x_ref[...] + mixed, g_ref[...], b_ref[...])


def combine_ln(dest, route_t, x2d, y, g, b, t=512):
    n, dm = x2d.shape
    t = min(t, n)
    last = n // t - 1
    return pl.pallas_call(
        functools.partial(_combine_kernel, t=t),
        grid=(n // t,),
        in_specs=[pl.BlockSpec((t,), lambda i: (i,), memory_space=pltpu.SMEM),
                  pl.BlockSpec((t,), lambda i: (i,), memory_space=pltpu.SMEM),
                  pl.BlockSpec((t,), lambda i: (jnp.minimum(i + 1, last),), memory_space=pltpu.SMEM),
                  pl.BlockSpec((t,), lambda i: (jnp.minimum(i + 1, last),), memory_space=pltpu.SMEM),
                  pl.BlockSpec((t, LANES), lambda i: (i, 0)),
                  pl.BlockSpec((t, dm), lambda i: (i, 0)),
                  pl.BlockSpec(memory_space=pl.ANY),
                  pl.BlockSpec((1, dm), lambda i: (0, 0)),
                  pl.BlockSpec((1, dm), lambda i: (0, 0))],
        out_specs=pl.BlockSpec((t, dm), lambda i: (i, 0)),
        out_shape=jax.ShapeDtypeStruct((n, dm), F32),
        scratch_shapes=[pltpu.VMEM((2, 2, t // ISSUE_UNROLL, ISSUE_UNROLL, dm), F32),
                        pltpu.SemaphoreType.DMA((2,))],
        compiler_params=_cparams(("arbitrary",)),
        name="combine_ln",
    )(dest[0], dest[1], dest[0], dest[1], route_t, x2d, y, g.reshape(1, dm), b.reshape(1, dm))


def moe_ln(x2d, router_w, wg, wu, wd, g, b, bm=1024):
    n, _ = x2d.shape
    bm = min(bm, n)
    route, route_t, cnt = route_tokens(x2d, router_w)
    counts = cnt[:, 0].astype(jnp.int32)
    padded = (counts + bm - 1) // bm * bm
    pad_end = jnp.cumsum(padded)
    pad_start = (pad_end - padded).astype(jnp.int32)
    n_blocks = 2 * n // bm + N_EXPERTS
    n_slots = n_blocks * bm
    block_start = jnp.arange(n_blocks, dtype=jnp.int32) * bm
    block_e = jnp.minimum(jnp.sum(pad_end[None, :] <= block_start[:, None], axis=1),
                          N_EXPERTS - 1).astype(jnp.int32)
    n_valid = (pad_end[-1:] // bm).astype(jnp.int32)
    dest = slot_of_assignment(pad_start, route)
    pad_lo = jnp.concatenate([pad_start + counts, pad_end[-1:]]).astype(jnp.int32)
    pad_n = jnp.concatenate([padded - counts, (n_slots - pad_end[-1:]) // SUBLANES]).astype(jnp.int32)
    slots = dispatch_rows(pad_lo, pad_n, dest, x2d, n_slots)
    y = expert_swiglu(block_e, n_valid, slots, wg, wu, wd, bm)
    return combine_ln(dest, route_t, x2d, y, g, b)


def kernel(x, rel_bias, w_in, diff_lambda, diff_subln_g, pool_w, pool_scale, ret_gn_g, w_out,
           ln1_g, ln1_b, ln2_g, ln2_b, ffn_w_gate, ffn_w_up, ffn_w_down,
           router_w, moe_w_gate, moe_w_up, moe_w_down):
    bsz, seq, dm = x.shape
    x2d = x.reshape(bsz * seq, dm)
    bias_tiles = attn_bias_tiles(rel_bias, min(ATTN_BLOCK, seq))
    col_scale = jnp.where(jnp.arange(w_in.shape[2]) < DA_WIDTH, Q_SCALE, 1.0).astype(F32)
    for l in range(DEPTH):
        lam_init = 0.8 - 0.6 * math.exp(-0.3 * l)
        proj = in_proj(x2d, (w_in[l] * col_scale).astype(BF16)).reshape(bsz, seq, -1)
        y_da = diff_attention(proj, bias_tiles, diff_lambda[l], diff_subln_g[l], lam_init)
        y_pool = pool_mixer(proj, pool_w[l], pool_scale[l])
        y_ret = retention(proj, ret_gn_g[l])
        flat = lambda a: a.reshape(bsz * seq, -1)
        mixed = (flat(y_da), flat(y_pool), flat(y_ret), x2d, w_out[l].astype(BF16), ln1_g[l], ln1_b[l])
        j = l // 2
        if l % 2 == 0:
            x2d = out_proj_ffn_ln(*mixed, ffn_w_gate[j].astype(BF16), ffn_w_up[j].astype(BF16),
                                  ffn_w_down[j].astype(BF16), ln2_g[l], ln2_b[l])
        else:
            x2d = out_proj_ln(*mixed)
            x2d = moe_ln(x2d, router_w[j], moe_w_gate[j], moe_w_up[j], moe_w_down[j], ln2_g[l], ln2_b[l])
    return x2d.reshape(bsz, seq, dm)
```

```python
import functools
import math

import jax
import jax.numpy as jnp
from jax import lax
from jax.experimental import pallas as pl
from jax.experimental.pallas import tpu as pltpu

F32 = jnp.float32
BF16 = jnp.bfloat16

DEPTH = 2
DA_HEAD_DIM = 64
DA_V_DIM = 128
DA_HEADS = 4
DA_WIDTH = 512
POOL_WIDTH = 256
POOL_WINDOWS = (2, 4, 8, 16)
POOL_GROUP_DIM = 64
POOL_HALO = 16
RET_WIDTH = 256
RET_HEAD_DIM = 64
RET_HEADS = 4
RET_CHUNK = 128
REL_BUCKETS = 32
REL_MAX_DIST = 128
N_EXPERTS = 8
ALPHA = (2 * DEPTH) ** 0.25
LN_EPS = 1e-5
NORM_EPS = 1e-6
NEG_BIG = -1e30

POOL_COL_BLOCK = 3 * DA_WIDTH // POOL_WIDTH
RET_COL_BLOCK = POOL_COL_BLOCK + 1

VMEM_LIMIT = 56 * 1024 * 1024


def _cparams(sem, vmem=VMEM_LIMIT):
    return pltpu.CompilerParams(dimension_semantics=sem, vmem_limit_bytes=vmem)


def _nt_dot(a, b):
    return lax.dot_general(a, b, (((1,), (1,)), ((), ())), preferred_element_type=F32)


def _dot(a, b):
    return jnp.dot(a, b, preferred_element_type=F32)


def _layer_norm(z, g, b):
    mu = jnp.mean(z, axis=-1, keepdims=True)
    zc = z - mu
    var = jnp.mean(zc * zc, axis=-1, keepdims=True)
    return zc * lax.rsqrt(var + LN_EPS) * g + b


def _silu(x):
    return x / (1.0 + jnp.exp(-x))


def _inproj_kernel(x_ref, w_ref, o_ref, *, tn):
    xb = x_ref[...].astype(BF16)
    for j in range(0, w_ref.shape[1], tn):
        o_ref[:, j:j + tn] = _dot(xb, w_ref[:, j:j + tn]).astype(o_ref.dtype)


def in_proj(x2d, w, tm=1024, tn=256):
    n, k = x2d.shape
    m = w.shape[1]
    tm = min(tm, n)
    return pl.pallas_call(
        functools.partial(_inproj_kernel, tn=tn),
        grid=(n // tm,),
        in_specs=[pl.BlockSpec((tm, k), lambda i: (i, 0)),
                  pl.BlockSpec((k, m), lambda i: (0, 0))],
        out_specs=pl.BlockSpec((tm, m), lambda i: (i, 0)),
        out_shape=jax.ShapeDtypeStruct((n, m), BF16),
        compiler_params=_cparams(("parallel",)),
        name="in_proj",
    )(x2d, w)


def _t5_bucket(dist):
    n = jnp.maximum(dist, 0)
    max_exact = REL_BUCKETS // 2
    nf = jnp.maximum(n, 1).astype(F32)
    large = max_exact + (jnp.log(nf / max_exact) / math.log(REL_MAX_DIST / max_exact)
                         * (REL_BUCKETS - max_exact)).astype(jnp.int32)
    large = jnp.minimum(large, REL_BUCKETS - 1)
    return jnp.where(n < max_exact, n, large)


LOG2E = math.log2(math.e)
Q_SCALE = DA_HEAD_DIM ** -0.5 * LOG2E
ATTN_BLOCK = 512
ATTN_CHUNK = 512
ONES_ROWS = 16


def attn_bias_tiles(rel_bias, t):
    table = rel_bias.astype(F32).reshape(REL_BUCKETS, DA_HEADS * 2)
    vec = table[_t5_bucket(jnp.arange(2 * t))]
    far = table[_t5_bucket(jnp.array(2 * t))]
    vec = ((vec - far[None, :]) * LOG2E).T
    masked = jnp.full((DA_HEADS * 2, t), NEG_BIG, F32)
    u_diag = jnp.concatenate([vec[:, :t], masked], axis=1)
    u_prev = jnp.concatenate([vec[:, t:], vec[:, :t]], axis=1)

    def toeplitz(u):
        skew = jnp.tile(u, (1, t))[:, :t * (2 * t - 1)].reshape(-1, t, 2 * t - 1)
        return skew[:, :, :t]

    def per_head(a):
        return a.reshape(DA_HEADS, 2, t, t).transpose(0, 2, 1, 3).reshape(DA_HEADS, t, 2 * t)

    return jnp.stack([per_head(toeplitz(u_diag)), per_head(toeplitz(u_prev))], axis=1)


def _attn_kernel(q_ref, k_ref, v_ref, bias_ref, lam_ref, g_ref, o_ref, vt, m_s, acc, s_a, s_b, *, t, lam_init):
    nk = vt.shape[0]
    ones = jnp.ones((ONES_ROWS, t), BF16)
    for ki in range(nk):
        v_t = v_ref[0, ki * t:(ki + 1) * t, :].astype(F32).T.astype(BF16)
        vt[ki] = jnp.concatenate([v_t, ones], axis=0)

    def query_tile(qi, carry):
        _attn_query_tile(qi, q_ref, k_ref, bias_ref, lam_ref, g_ref, o_ref, vt, m_s, acc, s_a, s_b,
                         t=t, lam_init=lam_init)
        return carry

    lax.fori_loop(0, nk // 2, query_tile, 0)


def _attn_query_tile(qi, q_ref, k_ref, bias_ref, lam_ref, g_ref, o_ref, vt, m_s, acc, s_a, s_b, *, t, lam_init):
    tq = 2 * t
    q_rows = pl.ds(pl.multiple_of(qi * tq, tq), tq)
    q = q_ref[0, q_rows, :]
    lane = lax.broadcasted_iota(jnp.int32, q.shape, 1)
    zero = jnp.zeros_like(q)
    qcat = jnp.concatenate([jnp.where(lane < DA_HEAD_DIM, q, zero),
                            jnp.where(lane >= DA_HEAD_DIM, q, zero)], axis=0)
    m_s[...] = jnp.full(m_s.shape, NEG_BIG, F32)
    acc[...] = jnp.zeros(acc.shape, F32)

    chunk = min(ATTN_CHUNK, t)
    DIAG, PREV, FAR, SKIP = 0, 1, None, "skip"

    def step(nxt, cur):
        kb = None
        if nxt is not None:
            kb = k_ref[0, pl.ds(pl.multiple_of(nxt[0] * t, t), t), :]
        for c in range(0, 2 * tq, chunk):
            cols = slice(c, c + chunk)
            mp, qoff = divmod(c, tq)
            half, ioff = divmod(qoff, t)
            ci = c // chunk
            if nxt is not None and not (len(nxt) > 2 and nxt[2][half] == SKIP):
                nxt[1][ci] = _nt_dot(kb, qcat[cols, :])
            if cur is not None:
                ki, s_ref, kinds = cur
                kind = kinds[half]
                if kind == SKIP:
                    continue
                s = s_ref[ci]
                if kind is not FAR:
                    s = s + bias_ref[0, kind, :, mp * t + ioff:mp * t + ioff + chunk]
                m_prev = m_s[ci]
                m_new = jnp.maximum(m_prev, jnp.max(s, axis=0, keepdims=True))
                alpha = jnp.exp2(m_prev - m_new)
                p = jnp.exp2(s - m_new).astype(BF16)
                acc[ci] = alpha * acc[ci] + _dot(vt[ki], p)
                m_s[ci] = m_new

    upper = (SKIP, DIAG)
    lower = (DIAG, PREV)
    before = (PREV, FAR)
    plain = (FAR, FAR)
    step((2 * qi + 1, s_a, upper), None)
    step((2 * qi, s_b), (2 * qi + 1, s_a, upper))

    @pl.when(qi == 0)
    def _():
        step(None, (0, s_b, lower))

    @pl.when(qi >= 1)
    def _():
        n_far = 2 * qi - 1
        step((2 * qi - 1, s_a), (2 * qi, s_b, lower))
        step((0, s_b), (2 * qi - 1, s_a, before))

        def pair(jj, carry):
            j = 2 * jj
            step((j + 1, s_a), (j, s_b, plain))
            step((j + 2, s_b), (j + 1, s_a, plain))
            return carry

        lax.fori_loop(0, n_far // 2, pair, 0)
        step(None, (n_far - 1, s_b, plain))

    lm = lam_ref[...]
    lam = (jnp.exp(jnp.sum(lm[0:1] * lm[1:2], keepdims=True))
           - jnp.exp(jnp.sum(lm[2:3] * lm[3:4], keepdims=True)) + lam_init)
    a = jnp.concatenate([acc[ci] for ci in range(acc.shape[0])], axis=1)
    o = a[:DA_V_DIM, :] / a[DA_V_DIM:DA_V_DIM + 1, :]
    out = o[:, :tq] - lam * o[:, tq:]
    out = out * lax.rsqrt(jnp.mean(out * out, axis=0, keepdims=True) + NORM_EPS) * g_ref[...]
    o_ref[0, q_rows, :] = (out * (1.0 - lam_init)).T.astype(o_ref.dtype)


def diff_attention(proj3, bias_tiles, lam_params, subln_g, lam_init):
    b, s, _ = proj3.shape
    t = bias_tiles.shape[2]
    chunk = min(ATTN_CHUNK, t)
    nch = 4 * t // chunk
    seq = lambda col0: pl.BlockSpec((1, s, DA_V_DIM), lambda bi, h: (bi, 0, col0 + h))
    return pl.pallas_call(
        functools.partial(_attn_kernel, t=t, lam_init=lam_init),
        grid=(b, DA_HEADS),
        in_specs=[
            seq(0), seq(DA_HEADS), seq(2 * DA_HEADS),
            pl.BlockSpec((1, 2, t, 2 * t), lambda bi, h: (h, 0, 0, 0)),
            pl.BlockSpec((4, DA_HEAD_DIM), lambda bi, h: (0, 0)),
            pl.BlockSpec((DA_V_DIM, 1), lambda bi, h: (0, 0)),
        ],
        out_specs=pl.BlockSpec((1, s, DA_V_DIM), lambda bi, h: (bi, 0, h)),
        out_shape=jax.ShapeDtypeStruct((b, s, DA_WIDTH), BF16),
        scratch_shapes=[pltpu.VMEM((s // t, DA_V_DIM + ONES_ROWS, t), BF16),
                        pltpu.VMEM((nch, 1, chunk), F32),
                        pltpu.VMEM((nch, DA_V_DIM + ONES_ROWS, chunk), F32),
                        pltpu.VMEM((nch, t, chunk), F32), pltpu.VMEM((nch, t, chunk), F32)],
        compiler_params=_cparams(("parallel", "parallel")),
        name="diff_attention",
    )(proj3, proj3, proj3, bias_tiles, lam_params, subln_g.reshape(DA_V_DIM, 1))


def _pool_kernel(p_ref, w_ref, scale_ref, o_ref, halo, *, t):
    si = pl.program_id(1)

    @pl.when(si == 0)
    def _():
        halo[...] = jnp.zeros(halo.shape, F32)

    p = p_ref[0].astype(F32)
    ext = jnp.concatenate([halo[...], p], axis=0)
    halo[...] = p[t - POOL_HALO:, :]
    sums = {1: ext}
    w = 1
    while w < POOL_WINDOWS[-1]:
        sums[2 * w] = sums[w] + pltpu.roll(sums[w], w, 0)
        w *= 2
    lane = lax.broadcasted_iota(jnp.int32, (t, POOL_WIDTH), 1)
    pos = (si * t + lax.broadcasted_iota(jnp.int32, (t, POOL_WIDTH), 0) + 1).astype(F32)
    wsum = sums[POOL_WINDOWS[-1]][POOL_HALO:, :]
    cnt = jnp.minimum(pos, float(POOL_WINDOWS[-1]))
    for gi in range(len(POOL_WINDOWS) - 2, -1, -1):
        in_group = lane < (gi + 1) * POOL_GROUP_DIM
        wsum = jnp.where(in_group, sums[POOL_WINDOWS[gi]][POOL_HALO:, :], wsum)
        cnt = jnp.where(in_group, jnp.minimum(pos, float(POOL_WINDOWS[gi])), cnt)
    pooled = wsum / cnt - p
    mixed = _dot(pooled.astype(BF16), w_ref[...])
    o_ref[0] = (mixed * scale_ref[...]).astype(o_ref.dtype)


def pool_mixer(proj3, pool_w, pool_scale, t=1024):
    b, s, _ = proj3.shape
    t = min(t, s)
    g = len(POOL_WINDOWS)
    wbd = (jnp.eye(g, dtype=F32)[:, None, :, None] * pool_w.astype(F32)[:, :, None, :]).reshape(
        POOL_WIDTH, POOL_WIDTH).astype(BF16)
    return pl.pallas_call(
        functools.partial(_pool_kernel, t=t),
        grid=(b, s // t),
        in_specs=[pl.BlockSpec((1, t, POOL_WIDTH), lambda bi, si: (bi, si, POOL_COL_BLOCK)),
                  pl.BlockSpec((POOL_WIDTH, POOL_WIDTH), lambda bi, si: (0, 0)),
                  pl.BlockSpec((1, POOL_WIDTH), lambda bi, si: (0, 0))],
        out_specs=pl.BlockSpec((1, t, POOL_WIDTH), lambda bi, si: (bi, si, 0)),
        out_shape=jax.ShapeDtypeStruct((b, s, POOL_WIDTH), BF16),
        scratch_shapes=[pltpu.VMEM((POOL_HALO, POOL_WIDTH), F32)],
        compiler_params=_cparams(("parallel", "arbitrary")),
        name="pool_mixer",
    )(proj3, wbd, pool_scale.reshape(1, POOL_WIDTH))


def _retention_tables(s, t):
    d, hn, c = RET_HEAD_DIM, RET_HEADS, RET_CHUNK
    half = d // 2
    inv = 10000.0 ** (-jnp.linspace(0.0, 1.0, half, dtype=F32))
    ang = jnp.arange(s)[:, None].astype(F32) * inv[None, :]
    cos, sin = jnp.cos(ang), jnp.sin(ang)
    cos_t = jnp.tile(jnp.concatenate([cos, cos], axis=-1), (1, hn))
    sin_t = jnp.tile(jnp.concatenate([-sin, sin], axis=-1), (1, hn))
    log_gamma = jnp.log(1.0 - 2.0 ** (-5.0 - jnp.arange(hn, dtype=F32)))
    idx = jnp.arange(c, dtype=F32)
    rel = idx[:, None] - idx[None, :]
    intra = jnp.where(rel >= 0, jnp.exp(log_gamma[:, None, None] * jnp.maximum(rel, 0.0)), 0.0)
    q_decay = jnp.exp(log_gamma[:, None] * (idx + 1.0))
    k_decay = jnp.exp(log_gamma[:, None] * (c - 1.0 - idx))
    chunk_decay = jnp.exp(log_gamma * c)
    lanes = lambda a: jnp.tile(jnp.repeat(a.T, d, axis=1), (t // c, 1))
    l = jnp.arange(hn * d)
    partner = jnp.where(l % d < half, l + half, l - half)
    swap = (l[:, None] == partner[None, :]).astype(BF16)
    same = (jnp.arange(2 * d)[:, None] // d) == (jnp.arange(2 * d)[None, :] // d)
    decay_bd = jnp.where(same[None], jnp.repeat(chunk_decay, d).reshape(hn // 2, 2 * d, 1), 0.0)
    avg = jnp.where(same, 1.0 / d, 0.0).astype(BF16)
    return cos_t, sin_t, intra, lanes(q_decay), lanes(k_decay), swap, decay_bd.astype(F32), avg


def _ret_kernel(q_ref, k_ref, v_ref, g_ref, cos_ref, sin_ref, intra_ref, qd_ref, kd_ref, swap_ref, decay_ref,
                avg_ref, gn_ref, o_ref, state, *, t):
    si = pl.program_id(1)
    d, c = RET_HEAD_DIM, RET_CHUNK
    pw = 2 * d

    @pl.when(si == 0)
    def _():
        state[...] = jnp.zeros(state.shape, F32)

    def rotate(x_ref):
        x = x_ref[0]
        return x.astype(F32) * cos_ref[...] + _dot(x, swap_ref[...]) * sin_ref[...]

    def lane_mean(a):
        hi = a.astype(BF16)
        lo = (a - hi.astype(F32)).astype(BF16)
        return _dot(hi, avg_ref[...]) + _dot(lo, avg_ref[...])

    q = rotate(q_ref)
    k = rotate(k_ref) * (d ** -0.5)
    qs = (q * qd_ref[...]).astype(BF16)
    ks = (k * kd_ref[...]).astype(BF16)
    qb = q.astype(BF16)
    kb = k.astype(BF16)
    v = v_ref[0]
    first = lax.broadcasted_iota(jnp.int32, (c, pw), 1) < d
    same = ((lax.broadcasted_iota(jnp.int32, (pw, pw), 0) < d)
            == (lax.broadcasted_iota(jnp.int32, (pw, pw), 1) < d))
    zero = jnp.zeros((c, pw), BF16)

    chunks = [slice(ci * c, (ci + 1) * c) for ci in range(t // c)]
    pairs = [slice(p * pw, (p + 1) * pw) for p in range(RET_HEADS // 2)]
    scores = {}
    for p, cols in enumerate(pairs):
        for ci, rows in enumerate(chunks):
            qp, kp = qb[rows, cols], kb[rows, cols]
            scores[p, ci] = ((_nt_dot(jnp.where(first, qp, zero), kp) * intra_ref[2 * p]).astype(BF16),
                             (_nt_dot(jnp.where(first, zero, qp), kp) * intra_ref[2 * p + 1]).astype(BF16))
    intra_out, increment = {}, {}
    for p, cols in enumerate(pairs):
        for ci, rows in enumerate(chunks):
            vp = v[rows, cols]
            intra_out[p, ci] = jnp.where(first, _dot(scores[p, ci][0], vp), _dot(scores[p, ci][1], vp))
            increment[p, ci] = jnp.where(same, _dot(ks[rows, cols].T, vp), 0.0)
    ys = {}
    for p, cols in enumerate(pairs):
        st = state[p]
        for ci, rows in enumerate(chunks):
            ys[p, ci] = intra_out[p, ci] + _dot(qs[rows, cols], st.astype(BF16))
            st = st * decay_ref[p] + increment[p, ci]
        state[p] = st
    y = jnp.concatenate([jnp.concatenate([ys[p, ci] for ci in range(len(chunks))], axis=0)
                         for p in range(len(pairs))], axis=1)
    mean = jnp.concatenate([lane_mean(y[:, cols]) for cols in pairs], axis=1)
    yc = y - mean
    var = jnp.concatenate([lane_mean((yc * yc)[:, cols]) for cols in pairs], axis=1)
    yn = yc * lax.rsqrt(var + NORM_EPS)
    gate = _silu(g_ref[0].astype(F32))
    o_ref[0] = (gate * (yn * gn_ref[...])).astype(o_ref.dtype)


def retention(proj3, gn_g, t=1024):
    b, s, _ = proj3.shape
    t = min(t, s)
    cos_t, sin_t, intra, qd, kd, swap, decay_bd, avg = _retention_tables(s, t)
    pw = 2 * RET_HEAD_DIM
    col = lambda j: pl.BlockSpec((1, t, RET_WIDTH), lambda bi, si: (bi, si, RET_COL_BLOCK + j))
    const2 = lambda shape: pl.BlockSpec(shape, lambda bi, si: (0, 0))
    const3 = lambda shape: pl.BlockSpec(shape, lambda bi, si: (0, 0, 0))
    return pl.pallas_call(
        functools.partial(_ret_kernel, t=t),
        grid=(b, s // t),
        in_specs=[col(0), col(1), col(2), col(3),
                  pl.BlockSpec((t, RET_WIDTH), lambda bi, si: (si, 0)),
                  pl.BlockSpec((t, RET_WIDTH), lambda bi, si: (si, 0)),
                  const3((RET_HEADS, RET_CHUNK, RET_CHUNK)),
                  const2((t, RET_WIDTH)), const2((t, RET_WIDTH)), const2((RET_WIDTH, RET_WIDTH)),
                  const3((RET_HEADS // 2, pw, pw)), const2((pw, pw)), const2((1, RET_WIDTH))],
        out_specs=pl.BlockSpec((1, t, RET_WIDTH), lambda bi, si: (bi, si, 0)),
        out_shape=jax.ShapeDtypeStruct((b, s, RET_WIDTH), BF16),
        scratch_shapes=[pltpu.VMEM((RET_HEADS // 2, pw, pw), F32)],
        compiler_params=_cparams(("parallel", "arbitrary")),
        name="retention",
    )(proj3, proj3, proj3, proj3, cos_t, sin_t, intra, qd, kd, swap, decay_bd, avg, gn_g.reshape(1, RET_WIDTH))


def _mix_ln(da_ref, pool_ref, ret_ref, x_ref, w_ref, g_ref, b_ref):
    e0, e1 = DA_WIDTH, DA_WIDTH + POOL_WIDTH
    mix = (_dot(da_ref[...], w_ref[0:e0, :]) + _dot(pool_ref[...], w_ref[e0:e1, :])
           + _dot(ret_ref[...], w_ref[e1:, :]))
    return _layer_norm(ALPHA * x_ref[...] + mix, g_ref[...], b_ref[...])


def _outproj_kernel(da_ref, pool_ref, ret_ref, x_ref, w_ref, g_ref, b_ref, o_ref):
    o_ref[...] = _mix_ln(da_ref, pool_ref, ret_ref, x_ref, w_ref, g_ref, b_ref)


def out_proj_ln(y_da, y_pool, y_ret, x2d, w, g, b, tm=512):
    n, dm = x2d.shape
    tm = min(tm, n)
    row = lambda width: pl.BlockSpec((tm, width), lambda i: (i, 0))
    const = lambda shape: pl.BlockSpec(shape, lambda i: (0, 0))
    return pl.pallas_call(
        _outproj_kernel,
        grid=(n // tm,),
        in_specs=[row(DA_WIDTH), row(POOL_WIDTH), row(RET_WIDTH), row(dm),
                  const(w.shape), const((1, dm)), const((1, dm))],
        out_specs=row(dm),
        out_shape=jax.ShapeDtypeStruct((n, dm), F32),
        compiler_params=_cparams(("parallel",)),
        name="out_proj_ln",
    )(y_da, y_pool, y_ret, x2d, w, g.reshape(1, dm), b.reshape(1, dm))


def _mix_ffn_kernel(da_ref, pool_ref, ret_ref, x_ref, wo_ref, g1_ref, b1_ref, wg_ref, wu_ref, wd_ref,
                    g2_ref, b2_ref, o_ref, *, tf):
    x = _mix_ln(da_ref, pool_ref, ret_ref, x_ref, wo_ref, g1_ref, b1_ref)
    xb = x.astype(BF16)
    acc = jnp.zeros(x.shape, F32)
    for f0 in range(0, wg_ref.shape[1], tf):
        hidden = _silu(_dot(xb, wg_ref[:, f0:f0 + tf])) * _dot(xb, wu_ref[:, f0:f0 + tf])
        acc = acc + _dot(hidden.astype(BF16), wd_ref[f0:f0 + tf, :])
    o_ref[...] = _layer_norm(ALPHA * x + acc, g2_ref[...], b2_ref[...])


def out_proj_ffn_ln(y_da, y_pool, y_ret, x2d, wo, g1, b1, wg, wu, wd, g2, b2, tm=512, tf=256):
    n, dm = x2d.shape
    tm = min(tm, n)
    row = lambda width: pl.BlockSpec((tm, width), lambda i: (i, 0))
    resident = lambda shape: pl.BlockSpec(shape, lambda i: (0, 0), pipeline_mode=pl.Buffered(1))
    vec = lambda a: a.reshape(1, dm)
    return pl.pallas_call(
        functools.partial(_mix_ffn_kernel, tf=tf),
        grid=(n // tm,),
        in_specs=[row(DA_WIDTH), row(POOL_WIDTH), row(RET_WIDTH), row(dm),
                  resident(wo.shape), resident((1, dm)), resident((1, dm)),
                  resident(wg.shape), resident(wu.shape), resident(wd.shape),
                  resident((1, dm)), resident((1, dm))],
        out_specs=row(dm),
        out_shape=jax.ShapeDtypeStruct((n, dm), F32),
        compiler_params=_cparams(("parallel",)),
        name="out_proj_ffn_ln",
    )(y_da, y_pool, y_ret, x2d, wo, vec(g1), vec(b1), wg, wu, wd, vec(g2), vec(b2))


ROUTE_ROWS = 8
LANES = 128


def _router_kernel(x_ref, w_ref, tri_ref, route_ref, route_t_ref, cnt_ref, carry, *, t):
    i = pl.program_id(0)

    @pl.when(i == 0)
    def _():
        carry[...] = jnp.zeros(carry.shape, F32)

    x = x_ref[...]
    xh = x.astype(BF16)
    xl = (x - xh.astype(F32)).astype(BF16)
    w = w_ref[...]
    wh = w.astype(BF16)
    wl = (w - wh.astype(F32)).astype(BF16)
    logits = _nt_dot(wh, xh) + _nt_dot(wl, xh) + _nt_dot(wh, xl)

    row = lax.broadcasted_iota(jnp.int32, logits.shape, 0)
    v0 = jnp.max(logits, axis=0, keepdims=True)
    i0 = jnp.min(jnp.where(logits == v0, row, N_EXPERTS), axis=0, keepdims=True)
    rest = jnp.where(row == i0, -jnp.inf, logits)
    v1 = jnp.max(rest, axis=0, keepdims=True)
    i1 = jnp.min(jnp.where(rest == v1, row, N_EXPERTS), axis=0, keepdims=True)
    ex = jnp.exp(v1 - v0)
    gate0 = 1.0 / (1.0 + ex)
    gate1 = ex / (1.0 + ex)

    oh0 = row == i0
    oh1 = row == i1
    member = jnp.where(oh0 | oh1, 1.0, 0.0)
    before = _dot(member.astype(BF16), tri_ref[...]) + carry[:, 0:1]
    rank0 = jnp.sum(jnp.where(oh0, before, 0.0), axis=0, keepdims=True)
    rank1 = jnp.sum(jnp.where(oh1, before, 0.0), axis=0, keepdims=True)
    carry[...] = carry[...] + jnp.sum(member, axis=1, keepdims=True)
    cnt_ref[...] = carry[...]

    route = jnp.concatenate([i0.astype(F32), i1.astype(F32), rank0, rank1, gate0, gate1,
                             jnp.zeros((2, t), F32)], axis=0)
    route_ref[...] = route
    padded = jnp.concatenate([route, jnp.zeros((LANES - ROUTE_ROWS, t), F32)], axis=0)
    route_t_ref[...] = padded.T


def route_tokens(x2d, router_w, t=1024):
    n, dm = x2d.shape
    t = min(t, n)
    tri = (jnp.arange(t)[:, None] < jnp.arange(t)[None, :]).astype(BF16)
    return pl.pallas_call(
        functools.partial(_router_kernel, t=t),
        grid=(n // t,),
        in_specs=[pl.BlockSpec((t, dm), lambda i: (i, 0)),
                  pl.BlockSpec((N_EXPERTS, dm), lambda i: (0, 0)),
                  pl.BlockSpec((t, t), lambda i: (0, 0))],
        out_specs=[pl.BlockSpec((ROUTE_ROWS, t), lambda i: (0, i)),
                   pl.BlockSpec((t, LANES), lambda i: (i, 0)),
                   pl.BlockSpec((N_EXPERTS, LANES), lambda i: (0, 0))],
        out_shape=[jax.ShapeDtypeStruct((ROUTE_ROWS, n), F32),
                   jax.ShapeDtypeStruct((n, LANES), F32),
                   jax.ShapeDtypeStruct((N_EXPERTS, LANES), F32)],
        scratch_shapes=[pltpu.VMEM((N_EXPERTS, LANES), F32)],
        compiler_params=_cparams(("arbitrary",)),
        name="route_tokens",
    )(x2d, router_w.T, tri)


def _dest_kernel(start_ref, route_ref, dest_ref):
    r = route_ref[...]
    for k in range(2):
        e = r[k:k + 1, :].astype(jnp.int32)
        base = jnp.zeros(e.shape, jnp.int32)
        for ei in range(N_EXPERTS):
            base = jnp.where(e == ei, start_ref[ei], base)
        dest_ref[k:k + 1, :] = base + r[2 + k:3 + k, :].astype(jnp.int32)


def slot_of_assignment(pad_start, route, t=1024):
    n = route.shape[1]
    t = min(t, n)
    grid_spec = pltpu.PrefetchScalarGridSpec(
        num_scalar_prefetch=1,
        grid=(n // t,),
        in_specs=[pl.BlockSpec((ROUTE_ROWS, t), lambda i, ps: (0, i))],
        out_specs=pl.BlockSpec((2, t), lambda i, ps: (0, i)),
    )
    return pl.pallas_call(
        _dest_kernel,
        grid_spec=grid_spec,
        out_shape=jax.ShapeDtypeStruct((2, n), jnp.int32),
        compiler_params=_cparams(("parallel",)),
        name="slot_of_assignment",
    )(pad_start, route)


def _row_copy(src, src_row, dst, dst_row, sem):
    return pltpu.make_async_copy(src.at[pl.ds(src_row, 1)], dst.at[pl.ds(dst_row, 1)], sem)


SUBLANES = 8
ISSUE_UNROLL = SUBLANES


def _dispatch_kernel(pad_lo_ref, pad_n_ref, d0_ref, d1_ref, x_ref, slots_out, zrow, sem, zsem, *, t):
    @pl.when(pl.program_id(0) == 0)
    def _():
        zrow[...] = jnp.zeros(zrow.shape, zrow.dtype)
        for e in range(N_EXPERTS):
            lo = pad_lo_ref[e]

            def zero_start(r, carry):
                _row_copy(zrow, 0, slots_out, lo + r, zsem).start()
                return carry

            def zero_wait(r, carry):
                _row_copy(zrow, 0, slots_out, lo + r, zsem).wait()
                return carry

            lax.fori_loop(0, pad_n_ref[e], zero_start, 0)
            lax.fori_loop(0, pad_n_ref[e], zero_wait, 0)

        tail_lo = pad_lo_ref[N_EXPERTS]
        group = zrow.shape[0]

        def tail_copy(r):
            dst = slots_out.at[pl.ds(pl.multiple_of(tail_lo + r * group, group), group)]
            return pltpu.make_async_copy(zrow, dst, zsem)

        def tail_start(r, carry):
            tail_copy(r).start()
            return carry

        def tail_wait(r, carry):
            tail_copy(r).wait()
            return carry

        lax.fori_loop(0, pad_n_ref[N_EXPERTS], tail_start, 0)
        lax.fori_loop(0, pad_n_ref[N_EXPERTS], tail_wait, 0)

    def issue(j, carry):
        base = pl.multiple_of(j * ISSUE_UNROLL, ISSUE_UNROLL)
        group = x_ref.at[pl.ds(base, ISSUE_UNROLL)]
        for u in range(ISSUE_UNROLL):
            for k, d_ref in enumerate((d0_ref, d1_ref)):
                _row_copy(group, u, slots_out, d_ref[base + u], sem).start()
        return carry

    lax.fori_loop(0, t // ISSUE_UNROLL, issue, 0)
    for k in range(2):
        pltpu.make_async_copy(x_ref, slots_out.at[pl.ds(0, t)], sem).wait()


def dispatch_rows(pad_lo, pad_n, dest, x2d, n_slots, t=2048):
    n, dm = x2d.shape
    t = min(t, n)
    grid_spec = pltpu.PrefetchScalarGridSpec(
        num_scalar_prefetch=2,
        grid=(n // t,),
        in_specs=[pl.BlockSpec((t,), lambda i, lo, cnt: (i,), memory_space=pltpu.SMEM),
                  pl.BlockSpec((t,), lambda i, lo, cnt: (i,), memory_space=pltpu.SMEM),
                  pl.BlockSpec((t, dm), lambda i, lo, cnt: (i, 0))],
        out_specs=pl.BlockSpec(memory_space=pl.ANY),
        scratch_shapes=[pltpu.VMEM((SUBLANES, dm), x2d.dtype), pltpu.SemaphoreType.DMA,
                        pltpu.SemaphoreType.DMA],
    )
    return pl.pallas_call(
        functools.partial(_dispatch_kernel, t=t),
        grid_spec=grid_spec,
        out_shape=jax.ShapeDtypeStruct((n_slots, dm), x2d.dtype),
        compiler_params=_cparams(("arbitrary",)),
        name="dispatch_rows",
    )(pad_lo, pad_n, dest[0], dest[1], x2d)


def _expert_kernel(be_ref, nv_ref, x_ref, wg_ref, wu_ref, wd_ref, o_ref, xb):
    blk = pl.program_id(0)
    f = pl.program_id(1)
    valid = blk < nv_ref[0]

    @pl.when(f == 0)
    def _():
        o_ref[...] = jnp.zeros(o_ref.shape, o_ref.dtype)

    @pl.when(valid & (f == 0))
    def _():
        xb[...] = x_ref[...].astype(BF16)

    @pl.when(valid)
    def _():
        hidden = (_silu(_dot(xb[...], wg_ref[0].astype(BF16)))
                  * _dot(xb[...], wu_ref[0].astype(BF16)))
        o_ref[...] += _dot(hidden.astype(BF16), wd_ref[0].astype(BF16))


def expert_swiglu(block_e, n_valid, slots, wg, wu, wd, bm, tf=512):
    n_slots, dm = slots.shape
    edim = wg.shape[2]
    grid_spec = pltpu.PrefetchScalarGridSpec(
        num_scalar_prefetch=2,
        grid=(n_slots // bm, edim // tf),
        in_specs=[pl.BlockSpec((bm, dm), lambda b, f, be, nv: (jnp.minimum(b, nv[0] - 1), 0)),
                  pl.BlockSpec((1, dm, tf), lambda b, f, be, nv: (be[b], 0, f)),
                  pl.BlockSpec((1, dm, tf), lambda b, f, be, nv: (be[b], 0, f)),
                  pl.BlockSpec((1, tf, dm), lambda b, f, be, nv: (be[b], f, 0))],
        out_specs=pl.BlockSpec((bm, dm), lambda b, f, be, nv: (b, 0)),
        scratch_shapes=[pltpu.VMEM((bm, dm), BF16)],
    )
    return pl.pallas_call(
        _expert_kernel,
        grid_spec=grid_spec,
        out_shape=jax.ShapeDtypeStruct((n_slots, dm), F32),
        compiler_params=_cparams(("parallel", "arbitrary")),
        name="expert_swiglu",
    )(block_e, n_valid, slots, wg, wu, wd)


def _combine_kernel(d0_ref, d1_ref, d0_next_ref, d1_next_ref, rt_ref, x_ref, y_hbm, g_ref, b_ref, o_ref,
                    rows, sem, *, t):
    i = pl.program_id(0)
    cur = i % 2

    def gather(d_refs, buf):
        def issue(j, carry):
            for u in range(ISSUE_UNROLL):
                for k in range(2):
                    _row_copy(y_hbm, d_refs[k][j * ISSUE_UNROLL + u], rows.at[buf, k, j], u, sem.at[buf]).start()
            return carry

        lax.fori_loop(0, t // ISSUE_UNROLL, issue, 0)

    @pl.when(i == 0)
    def _():
        gather((d0_ref, d1_ref), 0)

    @pl.when(i + 1 < pl.num_programs(0))
    def _():
        gather((d0_next_ref, d1_next_ref), 1 - cur)

    for k in range(2):
        pltpu.make_async_copy(y_hbm.at[pl.ds(0, t)], o_ref, sem.at[cur]).wait()

    rt = rt_ref[...]
    dm = o_ref.shape[1]
    mixed = rt[:, 4:5] * rows[cur, 0].reshape(t, dm) + rt[:, 5:6] * rows[cur, 1].reshape(t, dm)
    o_ref[...] = _layer_norm(ALPHA * x_ref[...] + mixed, g_ref[...], b_ref[...])


def combine_ln(dest, route_t, x2d, y, g, b, t=512):
    n, dm = x2d.shape
    t = min(t, n)
    last = n // t - 1
    return pl.pallas_call(
        functools.partial(_combine_kernel, t=t),
        grid=(n // t,),
        in_specs=[pl.BlockSpec((t,), lambda i: (i,), memory_space=pltpu.SMEM),
                  pl.BlockSpec((t,), lambda i: (i,), memory_space=pltpu.SMEM),
                  pl.BlockSpec((t,), lambda i: (jnp.minimum(i + 1, last),), memory_space=pltpu.SMEM),
                  pl.BlockSpec((t,), lambda i: (jnp.minimum(i + 1, last),), memory_space=pltpu.SMEM),
                  pl.BlockSpec((t, LANES), lambda i: (i, 0)),
                  pl.BlockSpec((t, dm), lambda i: (i, 0)),
                  pl.BlockSpec(memory_space=pl.ANY),
                  pl.BlockSpec((1, dm), lambda i: (0, 0)),
                  pl.BlockSpec((1, dm), lambda i: (0, 0))],
        out_specs=pl.BlockSpec((t, dm), lambda i: (i, 0)),
        out_shape=jax.ShapeDtypeStruct((n, dm), F32),
        scratch_shapes=[pltpu.VMEM((2, 2, t // ISSUE_UNROLL, ISSUE_UNROLL, dm), F32),
                        pltpu.SemaphoreType.DMA((2,))],
        compiler_params=_cparams(("arbitrary",)),
        name="combine_ln",
    )(dest[0], dest[1], dest[0], dest[1], route_t, x2d, y, g.reshape(1, dm), b.reshape(1, dm))


def moe_ln(x2d, router_w, wg, wu, wd, g, b, bm=1024):
    n, _ = x2d.shape
    bm = min(bm, n)
    route, route_t, cnt = route_tokens(x2d, router_w)
    counts = cnt[:, 0].astype(jnp.int32)
    padded = (counts + bm - 1) // bm * bm
    pad_end = jnp.cumsum(padded)
    pad_start = (pad_end - padded).astype(jnp.int32)
    n_blocks = 2 * n // bm + N_EXPERTS
    n_slots = n_blocks * bm
    block_start = jnp.arange(n_blocks, dtype=jnp.int32) * bm
    block_e = jnp.minimum(jnp.sum(pad_end[None, :] <= block_start[:, None], axis=1),
                          N_EXPERTS - 1).astype(jnp.int32)
    n_valid = (pad_end[-1:] // bm).astype(jnp.int32)
    dest = slot_of_assignment(pad_start, route)
    pad_lo = jnp.concatenate([pad_start + counts, pad_end[-1:]]).astype(jnp.int32)
    pad_n = jnp.concatenate([padded - counts, (n_slots - pad_end[-1:]) // SUBLANES]).astype(jnp.int32)
    slots = dispatch_rows(pad_lo, pad_n, dest, x2d, n_slots)
    y = expert_swiglu(block_e, n_valid, slots, wg, wu, wd, bm)
    return combine_ln(dest, route_t, x2d, y, g, b)


def kernel(x, rel_bias, w_in, diff_lambda, diff_subln_g, pool_w, pool_scale, ret_gn_g, w_out,
           ln1_g, ln1_b, ln2_g, ln2_b, ffn_w_gate, ffn_w_up, ffn_w_down,
           router_w, moe_w_gate, moe_w_up, moe_w_down):
    bsz, seq, dm = x.shape
    x2d = x.reshape(bsz * seq, dm)
    bias_tiles = attn_bias_tiles(rel_bias, min(ATTN_BLOCK, seq))
    col_scale = jnp.where(jnp.arange(w_in.shape[2]) < DA_WIDTH, Q_SCALE, 1.0).astype(F32)
    for l in range(DEPTH):
        lam_init = 0.8 - 0.6 * math.exp(-0.3 * l)
        proj = in_proj(x2d, (w_in[l] * col_scale).astype(BF16)).reshape(bsz, seq, -1)
        y_da = diff_attention(proj, bias_tiles, diff_lambda[l], diff_subln_g[l], lam_init)
        y_pool = pool_mixer(proj, pool_w[l], pool_scale[l])
        y_ret = retention(proj, ret_gn_g[l])
        flat = lambda a: a.reshape(bsz * seq, -1)
        mixed = (flat(y_da), flat(y_pool), flat(y_ret), x2d, w_out[l].astype(BF16), ln1_g[l], ln1_b[l])
        j = l // 2
        if l % 2 == 0:
            x2d = out_proj_ffn_ln(*mixed, ffn_w_gate[j].astype(BF16), ffn_w_up[j].astype(BF16),
                                  ffn_w_down[j].astype(BF16), ln2_g[l], ln2_b[l])
        else:
            x2d = out_proj_ln(*mixed)
            x2d = moe_ln(x2d, router_w[j], moe_w_gate[j], moe_w_up[j], moe_w_down[j], ln2_g[l], ln2_b[l])
    return x2d.reshape(bsz, seq, dm)
```

```python
import functools
import math

import jax
import jax.numpy as jnp
from jax import lax
from jax.experimental import pallas as pl
from jax.experimental.pallas import tpu as pltpu

F32 = jnp.float32
BF16 = jnp.bfloat16

DEPTH = 2
DA_HEAD_DIM = 64
DA_V_DIM = 128
DA_HEADS = 4
DA_WIDTH = 512
POOL_WIDTH = 256
POOL_WINDOWS = (2, 4, 8, 16)
POOL_GROUP_DIM = 64
POOL_HALO = 16
RET_WIDTH = 256
RET_HEAD_DIM = 64
RET_HEADS = 4
RET_CHUNK = 128
REL_BUCKETS = 32
REL_MAX_DIST = 128
N_EXPERTS = 8
ALPHA = (2 * DEPTH) ** 0.25
LN_EPS = 1e-5
NORM_EPS = 1e-6
NEG_BIG = -1e30

POOL_COL_BLOCK = 3 * DA_WIDTH // POOL_WIDTH
RET_COL_BLOCK = POOL_COL_BLOCK + 1

VMEM_LIMIT = 56 * 1024 * 1024


def _cparams(sem, vmem=VMEM_LIMIT):
    return pltpu.CompilerParams(dimension_semantics=sem, vmem_limit_bytes=vmem)


def _nt_dot(a, b):
    return lax.dot_general(a, b, (((1,), (1,)), ((), ())), preferred_element_type=F32)


def _dot(a, b):
    return jnp.dot(a, b, preferred_element_type=F32)


def _layer_norm(z, g, b):
    mu = jnp.mean(z, axis=-1, keepdims=True)
    zc = z - mu
    var = jnp.mean(zc * zc, axis=-1, keepdims=True)
    return zc * lax.rsqrt(var + LN_EPS) * g + b


def _silu(x):
    return x / (1.0 + jnp.exp(-x))


def _inproj_kernel(x_ref, w_ref, o_ref, *, tn):
    xb = x_ref[...].astype(BF16)
    for j in range(0, w_ref.shape[1], tn):
        o_ref[:, j:j + tn] = _dot(xb, w_ref[:, j:j + tn]).astype(o_ref.dtype)


def in_proj(x2d, w, tm=1024, tn=256):
    n, k = x2d.shape
    m = w.shape[1]
    tm = min(tm, n)
    return pl.pallas_call(
        functools.partial(_inproj_kernel, tn=tn),
        grid=(n // tm,),
        in_specs=[pl.BlockSpec((tm, k), lambda i: (i, 0)),
                  pl.BlockSpec((k, m), lambda i: (0, 0))],
        out_specs=pl.BlockSpec((tm, m), lambda i: (i, 0)),
        out_shape=jax.ShapeDtypeStruct((n, m), BF16),
        compiler_params=_cparams(("parallel",)),
        name="in_proj",
    )(x2d, w)


def _t5_bucket(dist):
    n = jnp.maximum(dist, 0)
    max_exact = REL_BUCKETS // 2
    nf = jnp.maximum(n, 1).astype(F32)
    large = max_exact + (jnp.log(nf / max_exact) / math.log(REL_MAX_DIST / max_exact)
                         * (REL_BUCKETS - max_exact)).astype(jnp.int32)
    large = jnp.minimum(large, REL_BUCKETS - 1)
    return jnp.where(n < max_exact, n, large)


LOG2E = math.log2(math.e)
Q_SCALE = DA_HEAD_DIM ** -0.5 * LOG2E
ATTN_BLOCK = 512
ATTN_CHUNK = 512
ONES_ROWS = 16


def attn_bias_tiles(rel_bias, t):
    table = rel_bias.astype(F32).reshape(REL_BUCKETS, DA_HEADS * 2)
    vec = table[_t5_bucket(jnp.arange(2 * t))]
    far = table[_t5_bucket(jnp.array(2 * t))]
    vec = ((vec - far[None, :]) * LOG2E).T
    masked = jnp.full((DA_HEADS * 2, t), NEG_BIG, F32)
    u_diag = jnp.concatenate([vec[:, :t], masked], axis=1)
    u_prev = jnp.concatenate([vec[:, t:], vec[:, :t]], axis=1)

    def toeplitz(u):
        skew = jnp.tile(u, (1, t))[:, :t * (2 * t - 1)].reshape(-1, t, 2 * t - 1)
        return skew[:, :, :t]

    def per_head(a):
        return a.reshape(DA_HEADS, 2, t, t).transpose(0, 2, 1, 3).reshape(DA_HEADS, t, 2 * t)

    return jnp.stack([per_head(toeplitz(u_diag)), per_head(toeplitz(u_prev))], axis=1)


def _attn_kernel(q_ref, k_ref, v_ref, bias_ref, lam_ref, g_ref, o_ref, vt, m_s, acc, s_a, s_b, *, t, lam_init):
    nk = vt.shape[0]
    ones = jnp.ones((ONES_ROWS, t), BF16)
    for ki in range(nk):
        v_t = v_ref[0, ki * t:(ki + 1) * t, :].astype(F32).T.astype(BF16)
        vt[ki] = jnp.concatenate([v_t, ones], axis=0)

    def query_tile(qi, carry):
        _attn_query_tile(qi, q_ref, k_ref, bias_ref, lam_ref, g_ref, o_ref, vt, m_s, acc, s_a, s_b,
                         t=t, lam_init=lam_init)
        return carry

    lax.fori_loop(0, nk // 2, query_tile, 0)


def _attn_query_tile(qi, q_ref, k_ref, bias_ref, lam_ref, g_ref, o_ref, vt, m_s, acc, s_a, s_b, *, t, lam_init):
    tq = 2 * t
    q_rows = pl.ds(pl.multiple_of(qi * tq, tq), tq)
    q = q_ref[0, q_rows, :]
    lane = lax.broadcasted_iota(jnp.int32, q.shape, 1)
    zero = jnp.zeros_like(q)
    qcat = jnp.concatenate([jnp.where(lane < DA_HEAD_DIM, q, zero),
                            jnp.where(lane >= DA_HEAD_DIM, q, zero)], axis=0)
    m_s[...] = jnp.full(m_s.shape, NEG_BIG, F32)
    acc[...] = jnp.zeros(acc.shape, F32)

    chunk = min(ATTN_CHUNK, t)
    DIAG, PREV, FAR, SKIP = 0, 1, None, "skip"

    def step(nxt, cur):
        kb = None
        if nxt is not None:
            kb = k_ref[0, pl.ds(pl.multiple_of(nxt[0] * t, t), t), :]
        for c in range(0, 2 * tq, chunk):
            cols = slice(c, c + chunk)
            mp, qoff = divmod(c, tq)
            half, ioff = divmod(qoff, t)
            ci = c // chunk
            if nxt is not None and not (len(nxt) > 2 and nxt[2][half] == SKIP):
                nxt[1][ci] = _nt_dot(kb, qcat[cols, :])
            if cur is not None:
                ki, s_ref, kinds = cur
                kind = kinds[half]
                if kind == SKIP:
                    continue
                s = s_ref[ci]
                if kind is not FAR:
                    s = s + bias_ref[0, kind, :, mp * t + ioff:mp * t + ioff + chunk]
                m_prev = m_s[ci]
                m_new = jnp.maximum(m_prev, jnp.max(s, axis=0, keepdims=True))
                alpha = jnp.exp2(m_prev - m_new)
                p = jnp.exp2(s - m_new).astype(BF16)
                acc[ci] = alpha * acc[ci] + _dot(vt[ki], p)
                m_s[ci] = m_new

    upper = (SKIP, DIAG)
    lower = (DIAG, PREV)
    before = (PREV, FAR)
    plain = (FAR, FAR)
    step((2 * qi + 1, s_a, upper), None)
    step((2 * qi, s_b), (2 * qi + 1, s_a, upper))

    @pl.when(qi == 0)
    def _():
        step(None, (0, s_b, lower))

    @pl.when(qi >= 1)
    def _():
        n_far = 2 * qi - 1
        step((2 * qi - 1, s_a), (2 * qi, s_b, lower))
        step((0, s_b), (2 * qi - 1, s_a, before))

        def pair(jj, carry):
            j = 2 * jj
            step((j + 1, s_a), (j, s_b, plain))
            step((j + 2, s_b), (j + 1, s_a, plain))
            return carry

        lax.fori_loop(0, n_far // 2, pair, 0)
        step(None, (n_far - 1, s_b, plain))

    lm = lam_ref[...]
    lam = (jnp.exp(jnp.sum(lm[0:1] * lm[1:2], keepdims=True))
           - jnp.exp(jnp.sum(lm[2:3] * lm[3:4], keepdims=True)) + lam_init)
    a = jnp.concatenate([acc[ci] for ci in range(acc.shape[0])], axis=1)
    o = a[:DA_V_DIM, :] / a[DA_V_DIM:DA_V_DIM + 1, :]
    out = o[:, :tq] - lam * o[:, tq:]
    out = out * lax.rsqrt(jnp.mean(out * out, axis=0, keepdims=True) + NORM_EPS) * g_ref[...]
    o_ref[0, q_rows, :] = (out * (1.0 - lam_init)).T.astype(o_ref.dtype)


def diff_attention(proj3, bias_tiles, lam_params, subln_g, lam_init):
    b, s, _ = proj3.shape
    t = bias_tiles.shape[2]
    chunk = min(ATTN_CHUNK, t)
    nch = 4 * t // chunk
    seq = lambda col0: pl.BlockSpec((1, s, DA_V_DIM), lambda bi, h: (bi, 0, col0 + h))
    return pl.pallas_call(
        functools.partial(_attn_kernel, t=t, lam_init=lam_init),
        grid=(b, DA_HEADS),
        in_specs=[
            seq(0), seq(DA_HEADS), seq(2 * DA_HEADS),
            pl.BlockSpec((1, 2, t, 2 * t), lambda bi, h: (h, 0, 0, 0)),
            pl.BlockSpec((4, DA_HEAD_DIM), lambda bi, h: (0, 0)),
            pl.BlockSpec((DA_V_DIM, 1), lambda bi, h: (0, 0)),
        ],
        out_specs=pl.BlockSpec((1, s, DA_V_DIM), lambda bi, h: (bi, 0, h)),
        out_shape=jax.ShapeDtypeStruct((b, s, DA_WIDTH), BF16),
        scratch_shapes=[pltpu.VMEM((s // t, DA_V_DIM + ONES_ROWS, t), BF16),
                        pltpu.VMEM((nch, 1, chunk), F32),
                        pltpu.VMEM((nch, DA_V_DIM + ONES_ROWS, chunk), F32),
                        pltpu.VMEM((nch, t, chunk), F32), pltpu.VMEM((nch, t, chunk), F32)],
        compiler_params=_cparams(("parallel", "parallel")),
        name="diff_attention",
    )(proj3, proj3, proj3, bias_tiles, lam_params, subln_g.reshape(DA_V_DIM, 1))


def _pool_kernel(p_ref, w_ref, scale_ref, o_ref, halo, *, t):
    si = pl.program_id(1)

    @pl.when(si == 0)
    def _():
        halo[...] = jnp.zeros(halo.shape, F32)

    p = p_ref[0].astype(F32)
    ext = jnp.concatenate([halo[...], p], axis=0)
    halo[...] = p[t - POOL_HALO:, :]
    sums = {1: ext}
    w = 1
    while w < POOL_WINDOWS[-1]:
        sums[2 * w] = sums[w] + pltpu.roll(sums[w], w, 0)
        w *= 2
    lane = lax.broadcasted_iota(jnp.int32, (t, POOL_WIDTH), 1)
    pos = (si * t + lax.broadcasted_iota(jnp.int32, (t, POOL_WIDTH), 0) + 1).astype(F32)
    wsum = sums[POOL_WINDOWS[-1]][POOL_HALO:, :]
    cnt = jnp.minimum(pos, float(POOL_WINDOWS[-1]))
    for gi in range(len(POOL_WINDOWS) - 2, -1, -1):
        in_group = lane < (gi + 1) * POOL_GROUP_DIM
        wsum = jnp.where(in_group, sums[POOL_WINDOWS[gi]][POOL_HALO:, :], wsum)
        cnt = jnp.where(in_group, jnp.minimum(pos, float(POOL_WINDOWS[gi])), cnt)
    pooled = wsum / cnt - p
    mixed = _dot(pooled.astype(BF16), w_ref[...])
    o_ref[0] = (mixed * scale_ref[...]).astype(o_ref.dtype)


def pool_mixer(proj3, pool_w, pool_scale, t=2048):
    b, s, _ = proj3.shape
    t = min(t, s)
    g = len(POOL_WINDOWS)
    wbd = (jnp.eye(g, dtype=F32)[:, None, :, None] * pool_w.astype(F32)[:, :, None, :]).reshape(
        POOL_WIDTH, POOL_WIDTH).astype(BF16)
    return pl.pallas_call(
        functools.partial(_pool_kernel, t=t),
        grid=(b, s // t),
        in_specs=[pl.BlockSpec((1, t, POOL_WIDTH), lambda bi, si: (bi, si, POOL_COL_BLOCK)),
                  pl.BlockSpec((POOL_WIDTH, POOL_WIDTH), lambda bi, si: (0, 0)),
                  pl.BlockSpec((1, POOL_WIDTH), lambda bi, si: (0, 0))],
        out_specs=pl.BlockSpec((1, t, POOL_WIDTH), lambda bi, si: (bi, si, 0)),
        out_shape=jax.ShapeDtypeStruct((b, s, POOL_WIDTH), BF16),
        scratch_shapes=[pltpu.VMEM((POOL_HALO, POOL_WIDTH), F32)],
        compiler_params=_cparams(("parallel", "arbitrary")),
        name="pool_mixer",
    )(proj3, wbd, pool_scale.reshape(1, POOL_WIDTH))


def _retention_tables(s, t):
    d, hn, c = RET_HEAD_DIM, RET_HEADS, RET_CHUNK
    half = d // 2
    inv = 10000.0 ** (-jnp.linspace(0.0, 1.0, half, dtype=F32))
    ang = jnp.arange(s)[:, None].astype(F32) * inv[None, :]
    cos, sin = jnp.cos(ang), jnp.sin(ang)
    cos_t = jnp.tile(jnp.concatenate([cos, cos], axis=-1), (1, hn))
    sin_t = jnp.tile(jnp.concatenate([-sin, sin], axis=-1), (1, hn))
    log_gamma = jnp.log(1.0 - 2.0 ** (-5.0 - jnp.arange(hn, dtype=F32)))
    idx = jnp.arange(c, dtype=F32)
    rel = idx[:, None] - idx[None, :]
    intra = jnp.where(rel >= 0, jnp.exp(log_gamma[:, None, None] * jnp.maximum(rel, 0.0)), 0.0)
    q_decay = jnp.exp(log_gamma[:, None] * (idx + 1.0))
    k_decay = jnp.exp(log_gamma[:, None] * (c - 1.0 - idx))
    chunk_decay = jnp.exp(log_gamma * c)
    lanes = lambda a: jnp.tile(jnp.repeat(a.T, d, axis=1), (t // c, 1))
    l = jnp.arange(hn * d)
    partner = jnp.where(l % d < half, l + half, l - half)
    swap = (l[:, None] == partner[None, :]).astype(BF16)
    same = (jnp.arange(2 * d)[:, None] // d) == (jnp.arange(2 * d)[None, :] // d)
    decay_bd = jnp.where(same[None], jnp.repeat(chunk_decay, d).reshape(hn // 2, 2 * d, 1), 0.0)
    avg = jnp.where(same, 1.0 / d, 0.0).astype(BF16)
    return cos_t, sin_t, intra, lanes(q_decay), lanes(k_decay), swap, decay_bd.astype(F32), avg


def _ret_kernel(q_ref, k_ref, v_ref, g_ref, cos_ref, sin_ref, intra_ref, qd_ref, kd_ref, swap_ref, decay_ref,
                avg_ref, gn_ref, o_ref, state, *, t):
    si = pl.program_id(1)
    d, c = RET_HEAD_DIM, RET_CHUNK
    pw = 2 * d

    @pl.when(si == 0)
    def _():
        state[...] = jnp.zeros(state.shape, F32)

    def rotate(x_ref):
        x = x_ref[0]
        return x.astype(F32) * cos_ref[...] + _dot(x, swap_ref[...]) * sin_ref[...]

    def lane_mean(a):
        hi = a.astype(BF16)
        lo = (a - hi.astype(F32)).astype(BF16)
        return _dot(hi, avg_ref[...]) + _dot(lo, avg_ref[...])

    q = rotate(q_ref)
    k = rotate(k_ref) * (d ** -0.5)
    qs = (q * qd_ref[...]).astype(BF16)
    ks = (k * kd_ref[...]).astype(BF16)
    qb = q.astype(BF16)
    kb = k.astype(BF16)
    v = v_ref[0]
    first = lax.broadcasted_iota(jnp.int32, (c, pw), 1) < d
    same = ((lax.broadcasted_iota(jnp.int32, (pw, pw), 0) < d)
            == (lax.broadcasted_iota(jnp.int32, (pw, pw), 1) < d))
    zero = jnp.zeros((c, pw), BF16)

    chunks = [slice(ci * c, (ci + 1) * c) for ci in range(t // c)]
    pairs = [slice(p * pw, (p + 1) * pw) for p in range(RET_HEADS // 2)]
    scores = {}
    for p, cols in enumerate(pairs):
        for ci, rows in enumerate(chunks):
            qp, kp = qb[rows, cols], kb[rows, cols]
            scores[p, ci] = ((_nt_dot(jnp.where(first, qp, zero), kp) * intra_ref[2 * p]).astype(BF16),
                             (_nt_dot(jnp.where(first, zero, qp), kp) * intra_ref[2 * p + 1]).astype(BF16))
    intra_out, increment = {}, {}
    for p, cols in enumerate(pairs):
        for ci, rows in enumerate(chunks):
            vp = v[rows, cols]
            intra_out[p, ci] = jnp.where(first, _dot(scores[p, ci][0], vp), _dot(scores[p, ci][1], vp))
            increment[p, ci] = jnp.where(same, _dot(ks[rows, cols].T, vp), 0.0)
    ys = {}
    for p, cols in enumerate(pairs):
        st = state[p]
        for ci, rows in enumerate(chunks):
            ys[p, ci] = intra_out[p, ci] + _dot(qs[rows, cols], st.astype(BF16))
            st = st * decay_ref[p] + increment[p, ci]
        state[p] = st
    y = jnp.concatenate([jnp.concatenate([ys[p, ci] for ci in range(len(chunks))], axis=0)
                         for p in range(len(pairs))], axis=1)
    mean = jnp.concatenate([lane_mean(y[:, cols]) for cols in pairs], axis=1)
    yc = y - mean
    var = jnp.concatenate([lane_mean((yc * yc)[:, cols]) for cols in pairs], axis=1)
    yn = yc * lax.rsqrt(var + NORM_EPS)
    gate = _silu(g_ref[0].astype(F32))
    o_ref[0] = (gate * (yn * gn_ref[...])).astype(o_ref.dtype)


def retention(proj3, gn_g, t=2048):
    b, s, _ = proj3.shape
    t = min(t, s)
    cos_t, sin_t, intra, qd, kd, swap, decay_bd, avg = _retention_tables(s, t)
    pw = 2 * RET_HEAD_DIM
    col = lambda j: pl.BlockSpec((1, t, RET_WIDTH), lambda bi, si: (bi, si, RET_COL_BLOCK + j))
    const2 = lambda shape: pl.BlockSpec(shape, lambda bi, si: (0, 0))
    const3 = lambda shape: pl.BlockSpec(shape, lambda bi, si: (0, 0, 0))
    return pl.pallas_call(
        functools.partial(_ret_kernel, t=t),
        grid=(b, s // t),
        in_specs=[col(0), col(1), col(2), col(3),
                  pl.BlockSpec((t, RET_WIDTH), lambda bi, si: (si, 0)),
                  pl.BlockSpec((t, RET_WIDTH), lambda bi, si: (si, 0)),
                  const3((RET_HEADS, RET_CHUNK, RET_CHUNK)),
                  const2((t, RET_WIDTH)), const2((t, RET_WIDTH)), const2((RET_WIDTH, RET_WIDTH)),
                  const3((RET_HEADS // 2, pw, pw)), const2((pw, pw)), const2((1, RET_WIDTH))],
        out_specs=pl.BlockSpec((1, t, RET_WIDTH), lambda bi, si: (bi, si, 0)),
        out_shape=jax.ShapeDtypeStruct((b, s, RET_WIDTH), BF16),
        scratch_shapes=[pltpu.VMEM((RET_HEADS // 2, pw, pw), F32)],
        compiler_params=_cparams(("parallel", "arbitrary")),
        name="retention",
    )(proj3, proj3, proj3, proj3, cos_t, sin_t, intra, qd, kd, swap, decay_bd, avg, gn_g.reshape(1, RET_WIDTH))


def _mix_ln(da_ref, pool_ref, ret_ref, x_ref, w_ref, g_ref, b_ref):
    e0, e1 = DA_WIDTH, DA_WIDTH + POOL_WIDTH
    mix = (_dot(da_ref[...], w_ref[0:e0, :]) + _dot(pool_ref[...], w_ref[e0:e1, :])
           + _dot(ret_ref[...], w_ref[e1:, :]))
    return _layer_norm(ALPHA * x_ref[...] + mix, g_ref[...], b_ref[...])


def _outproj_kernel(da_ref, pool_ref, ret_ref, x_ref, w_ref, g_ref, b_ref, o_ref):
    o_ref[...] = _mix_ln(da_ref, pool_ref, ret_ref, x_ref, w_ref, g_ref, b_ref)


def out_proj_ln(y_da, y_pool, y_ret, x2d, w, g, b, tm=1024):
    n, dm = x2d.shape
    tm = min(tm, n)
    row = lambda width: pl.BlockSpec((tm, width), lambda i: (i, 0))
    const = lambda shape: pl.BlockSpec(shape, lambda i: (0, 0))
    return pl.pallas_call(
        _outproj_kernel,
        grid=(n // tm,),
        in_specs=[row(DA_WIDTH), row(POOL_WIDTH), row(RET_WIDTH), row(dm),
                  const(w.shape), const((1, dm)), const((1, dm))],
        out_specs=row(dm),
        out_shape=jax.ShapeDtypeStruct((n, dm), F32),
        compiler_params=_cparams(("parallel",)),
        name="out_proj_ln",
    )(y_da, y_pool, y_ret, x2d, w, g.reshape(1, dm), b.reshape(1, dm))


def _mix_ffn_kernel(da_ref, pool_ref, ret_ref, x_ref, wo_ref, g1_ref, b1_ref, wg_ref, wu_ref, wd_ref,
                    g2_ref, b2_ref, o_ref, *, tf):
    x = _mix_ln(da_ref, pool_ref, ret_ref, x_ref, wo_ref, g1_ref, b1_ref)
    xb = x.astype(BF16)
    acc = jnp.zeros(x.shape, F32)
    for f0 in range(0, wg_ref.shape[1], tf):
        hidden = _silu(_dot(xb, wg_ref[:, f0:f0 + tf])) * _dot(xb, wu_ref[:, f0:f0 + tf])
        acc = acc + _dot(hidden.astype(BF16), wd_ref[f0:f0 + tf, :])
    o_ref[...] = _layer_norm(ALPHA * x + acc, g2_ref[...], b2_ref[...])


def out_proj_ffn_ln(y_da, y_pool, y_ret, x2d, wo, g1, b1, wg, wu, wd, g2, b2, tm=512, tf=256):
    n, dm = x2d.shape
    tm = min(tm, n)
    row = lambda width: pl.BlockSpec((tm, width), lambda i: (i, 0))
    resident = lambda shape: pl.BlockSpec(shape, lambda i: (0, 0), pipeline_mode=pl.Buffered(1))
    vec = lambda a: a.reshape(1, dm)
    return pl.pallas_call(
        functools.partial(_mix_ffn_kernel, tf=tf),
        grid=(n // tm,),
        in_specs=[row(DA_WIDTH), row(POOL_WIDTH), row(RET_WIDTH), row(dm),
                  resident(wo.shape), resident((1, dm)), resident((1, dm)),
                  resident(wg.shape), resident(wu.shape), resident(wd.shape),
                  resident((1, dm)), resident((1, dm))],
        out_specs=row(dm),
        out_shape=jax.ShapeDtypeStruct((n, dm), F32),
        compiler_params=_cparams(("parallel",)),
        name="out_proj_ffn_ln",
    )(y_da, y_pool, y_ret, x2d, wo, vec(g1), vec(b1), wg, wu, wd, vec(g2), vec(b2))


ROUTE_ROWS = 8
LANES = 128


def _router_kernel(x_ref, w_ref, tri_ref, route_ref, route_t_ref, cnt_ref, carry, *, t):
    i = pl.program_id(0)

    @pl.when(i == 0)
    def _():
        carry[...] = jnp.zeros(carry.shape, F32)

    x = x_ref[...]
    xh = x.astype(BF16)
    xl = (x - xh.astype(F32)).astype(BF16)
    w = w_ref[...]
    wh = w.astype(BF16)
    wl = (w - wh.astype(F32)).astype(BF16)
    logits = _nt_dot(wh, xh) + _nt_dot(wl, xh) + _nt_dot(wh, xl)

    row = lax.broadcasted_iota(jnp.int32, logits.shape, 0)
    v0 = jnp.max(logits, axis=0, keepdims=True)
    i0 = jnp.min(jnp.where(logits == v0, row, N_EXPERTS), axis=0, keepdims=True)
    rest = jnp.where(row == i0, -jnp.inf, logits)
    v1 = jnp.max(rest, axis=0, keepdims=True)
    i1 = jnp.min(jnp.where(rest == v1, row, N_EXPERTS), axis=0, keepdims=True)
    ex = jnp.exp(v1 - v0)
    gate0 = 1.0 / (1.0 + ex)
    gate1 = ex / (1.0 + ex)

    oh0 = row == i0
    oh1 = row == i1
    member = jnp.where(oh0 | oh1, 1.0, 0.0)
    before = _dot(member.astype(BF16), tri_ref[...]) + carry[:, 0:1]
    rank0 = jnp.sum(jnp.where(oh0, before, 0.0), axis=0, keepdims=True)
    rank1 = jnp.sum(jnp.where(oh1, before, 0.0), axis=0, keepdims=True)
    carry[...] = carry[...] + jnp.sum(member, axis=1, keepdims=True)
    cnt_ref[...] = carry[...]

    route = jnp.concatenate([i0.astype(F32), i1.astype(F32), rank0, rank1, gate0, gate1,
                             jnp.zeros((2, t), F32)], axis=0)
    route_ref[...] = route
    padded = jnp.concatenate([route, jnp.zeros((LANES - ROUTE_ROWS, t), F32)], axis=0)
    route_t_ref[...] = padded.T


def route_tokens(x2d, router_w, t=512):
    n, dm = x2d.shape
    t = min(t, n)
    tri = (jnp.arange(t)[:, None] < jnp.arange(t)[None, :]).astype(BF16)
    return pl.pallas_call(
        functools.partial(_router_kernel, t=t),
        grid=(n // t,),
        in_specs=[pl.BlockSpec((t, dm), lambda i: (i, 0)),
                  pl.BlockSpec((N_EXPERTS, dm), lambda i: (0, 0)),
                  pl.BlockSpec((t, t), lambda i: (0, 0))],
        out_specs=[pl.BlockSpec((ROUTE_ROWS, t), lambda i: (0, i)),
                   pl.BlockSpec((t, LANES), lambda i: (i, 0)),
                   pl.BlockSpec((N_EXPERTS, LANES), lambda i: (0, 0))],
        out_shape=[jax.ShapeDtypeStruct((ROUTE_ROWS, n), F32),
                   jax.ShapeDtypeStruct((n, LANES), F32),
                   jax.ShapeDtypeStruct((N_EXPERTS, LANES), F32)],
        scratch_shapes=[pltpu.VMEM((N_EXPERTS, LANES), F32)],
        compiler_params=_cparams(("arbitrary",)),
        name="route_tokens",
    )(x2d, router_w.T, tri)


def _dest_kernel(start_ref, route_ref, dest_ref):
    r = route_ref[...]
    for k in range(2):
        e = r[k:k + 1, :].astype(jnp.int32)
        base = jnp.zeros(e.shape, jnp.int32)
        for ei in range(N_EXPERTS):
            base = jnp.where(e == ei, start_ref[ei], base)
        dest_ref[k:k + 1, :] = base + r[2 + k:3 + k, :].astype(jnp.int32)


def slot_of_assignment(pad_start, route, t=4096):
    n = route.shape[1]
    t = min(t, n)
    grid_spec = pltpu.PrefetchScalarGridSpec(
        num_scalar_prefetch=1,
        grid=(n // t,),
        in_specs=[pl.BlockSpec((ROUTE_ROWS, t), lambda i, ps: (0, i))],
        out_specs=pl.BlockSpec((2, t), lambda i, ps: (0, i)),
    )
    return pl.pallas_call(
        _dest_kernel,
        grid_spec=grid_spec,
        out_shape=jax.ShapeDtypeStruct((2, n), jnp.int32),
        compiler_params=_cparams(("parallel",)),
        name="slot_of_assignment",
    )(pad_start, route)


def _row_copy(src, src_row, dst, dst_row, sem):
    return pltpu.make_async_copy(src.at[pl.ds(src_row, 1)], dst.at[pl.ds(dst_row, 1)], sem)


SUBLANES = 8
ISSUE_UNROLL = SUBLANES


def _dispatch_kernel(pad_lo_ref, pad_n_ref, d0_ref, d1_ref, x_ref, slots_out, zrow, sem, zsem, *, t):
    @pl.when(pl.program_id(0) == 0)
    def _():
        zrow[...] = jnp.zeros(zrow.shape, zrow.dtype)
        for e in range(N_EXPERTS):
            lo = pad_lo_ref[e]

            def zero_start(r, carry):
                _row_copy(zrow, 0, slots_out, lo + r, zsem).start()
                return carry

            def zero_wait(r, carry):
                _row_copy(zrow, 0, slots_out, lo + r, zsem).wait()
                return carry

            lax.fori_loop(0, pad_n_ref[e], zero_start, 0)
            lax.fori_loop(0, pad_n_ref[e], zero_wait, 0)

        tail_lo = pad_lo_ref[N_EXPERTS]
        group = zrow.shape[0]

        def tail_copy(r):
            dst = slots_out.at[pl.ds(pl.multiple_of(tail_lo + r * group, group), group)]
            return pltpu.make_async_copy(zrow, dst, zsem)

        def tail_start(r, carry):
            tail_copy(r).start()
            return carry

        def tail_wait(r, carry):
            tail_copy(r).wait()
            return carry

        lax.fori_loop(0, pad_n_ref[N_EXPERTS], tail_start, 0)
        lax.fori_loop(0, pad_n_ref[N_EXPERTS], tail_wait, 0)

    def issue(j, carry):
        base = pl.multiple_of(j * ISSUE_UNROLL, ISSUE_UNROLL)
        group = x_ref.at[pl.ds(base, ISSUE_UNROLL)]
        for u in range(ISSUE_UNROLL):
            for k, d_ref in enumerate((d0_ref, d1_ref)):
                _row_copy(group, u, slots_out, d_ref[base + u], sem).start()
        return carry

    lax.fori_loop(0, t // ISSUE_UNROLL, issue, 0)
    for k in range(2):
        pltpu.make_async_copy(x_ref, slots_out.at[pl.ds(0, t)], sem).wait()


def dispatch_rows(pad_lo, pad_n, dest, x2d, n_slots, t=4096):
    n, dm = x2d.shape
    t = min(t, n)
    grid_spec = pltpu.PrefetchScalarGridSpec(
        num_scalar_prefetch=2,
        grid=(n // t,),
        in_specs=[pl.BlockSpec((t,), lambda i, lo, cnt: (i,), memory_space=pltpu.SMEM),
                  pl.BlockSpec((t,), lambda i, lo, cnt: (i,), memory_space=pltpu.SMEM),
                  pl.BlockSpec((t, dm), lambda i, lo, cnt: (i, 0))],
        out_specs=pl.BlockSpec(memory_space=pl.ANY),
        scratch_shapes=[pltpu.VMEM((SUBLANES, dm), x2d.dtype), pltpu.SemaphoreType.DMA,
                        pltpu.SemaphoreType.DMA],
    )
    return pl.pallas_call(
        functools.partial(_dispatch_kernel, t=t),
        grid_spec=grid_spec,
        out_shape=jax.ShapeDtypeStruct((n_slots, dm), x2d.dtype),
        compiler_params=_cparams(("arbitrary",)),
        name="dispatch_rows",
    )(pad_lo, pad_n, dest[0], dest[1], x2d)


def _expert_kernel(be_ref, nv_ref, x_ref, wg_ref, wu_ref, wd_ref, o_ref, xb):
    blk = pl.program_id(0)
    f = pl.program_id(1)
    valid = blk < nv_ref[0]

    @pl.when(f == 0)
    def _():
        o_ref[...] = jnp.zeros(o_ref.shape, o_ref.dtype)

    @pl.when(valid & (f == 0))
    def _():
        xb[...] = x_ref[...].astype(BF16)

    @pl.when(valid)
    def _():
        hidden = (_silu(_dot(xb[...], wg_ref[0].astype(BF16)))
                  * _dot(xb[...], wu_ref[0].astype(BF16)))
        o_ref[...] += _dot(hidden.astype(BF16), wd_ref[0].astype(BF16))


def expert_swiglu(block_e, n_valid, slots, wg, wu, wd, bm, tf=512):
    n_slots, dm = slots.shape
    edim = wg.shape[2]
    grid_spec = pltpu.PrefetchScalarGridSpec(
        num_scalar_prefetch=2,
        grid=(n_slots // bm, edim // tf),
        in_specs=[pl.BlockSpec((bm, dm), lambda b, f, be, nv: (jnp.minimum(b, nv[0] - 1), 0)),
                  pl.BlockSpec((1, dm, tf), lambda b, f, be, nv: (be[b], 0, f)),
                  pl.BlockSpec((1, dm, tf), lambda b, f, be, nv: (be[b], 0, f)),
                  pl.BlockSpec((1, tf, dm), lambda b, f, be, nv: (be[b], f, 0))],
        out_specs=pl.BlockSpec((bm, dm), lambda b, f, be, nv: (b, 0)),
        scratch_shapes=[pltpu.VMEM((bm, dm), BF16)],
    )
    return pl.pallas_call(
        _expert_kernel,
        grid_spec=grid_spec,
        out_shape=jax.ShapeDtypeStruct((n_slots, dm), F32),
        compiler_params=_cparams(("parallel", "arbitrary")),
        name="expert_swiglu",
    )(block_e, n_valid, slots, wg, wu, wd)


def _combine_kernel(d0_ref, d1_ref, d0_next_ref, d1_next_ref, rt_ref, x_ref, y_hbm, g_ref, b_ref, o_ref,
                    rows, sem, *, t):
    i = pl.program_id(0)
    cur = i % 2

    def gather(d_refs, buf):
        def issue(j, carry):
            for u in range(ISSUE_UNROLL):
                for k in range(2):
                    _row_copy(y_hbm, d_refs[k][j * ISSUE_UNROLL + u], rows.at[buf, k, j], u, sem.at[buf]).start()
            return carry

        lax.fori_loop(0, t // ISSUE_UNROLL, issue, 0)

    @pl.when(i == 0)
    def _():
        gather((d0_ref, d1_ref), 0)

    @pl.when(i + 1 < pl.num_programs(0))
    def _():
        gather((d0_next_ref, d1_next_ref), 1 - cur)

    for k in range(2):
        pltpu.make_async_copy(y_hbm.at[pl.ds(0, t)], o_ref, sem.at[cur]).wait()

    rt = rt_ref[...]
    dm = o_ref.shape[1]
    mixed = rt[:, 4:5] * rows[cur, 0].reshape(t, dm) + rt[:, 5:6] * rows[cur, 1].reshape(t, dm)
    o_ref[...] = _layer_norm(ALPHA * x_ref[...] + mixed, g_ref[...], b_ref[...])


def combine_ln(dest, route_t, x2d, y, g, b, t=512):
    n, dm = x2d.shape
    t = min(t, n)
    last = n // t - 1
    return pl.pallas_call(
        functools.partial(_combine_kernel, t=t),
        grid=(n // t,),
        in_specs=[pl.BlockSpec((t,), lambda i: (i,), memory_space=pltpu.SMEM),
                  pl.BlockSpec((t,), lambda i: (i,), memory_space=pltpu.SMEM),
                  pl.BlockSpec((t,), lambda i: (jnp.minimum(i + 1, last),), memory_space=pltpu.SMEM),
                  pl.BlockSpec((t,), lambda i: (jnp.minimum(i + 1, last),), memory_space=pltpu.SMEM),
                  pl.BlockSpec((t, LANES), lambda i: (i, 0)),
                  pl.BlockSpec((t, dm), lambda i: (i, 0)),
                  pl.BlockSpec(memory_space=pl.ANY),
                  pl.BlockSpec((1, dm), lambda i: (0, 0)),
                  pl.BlockSpec((1, dm), lambda i: (0, 0))],
        out_specs=pl.BlockSpec((t, dm), lambda i: (i, 0)),
        out_shape=jax.ShapeDtypeStruct((n, dm), F32),
        scratch_shapes=[pltpu.VMEM((2, 2, t // ISSUE_UNROLL, ISSUE_UNROLL, dm), F32),
                        pltpu.SemaphoreType.DMA((2,))],
        compiler_params=_cparams(("arbitrary",)),
        name="combine_ln",
    )(dest[0], dest[1], dest[0], dest[1], route_t, x2d, y, g.reshape(1, dm), b.reshape(1, dm))


def moe_ln(x2d, router_w, wg, wu, wd, g, b, bm=1024):
    n, _ = x2d.shape
    bm = min(bm, n)
    route, route_t, cnt = route_tokens(x2d, router_w)
    counts = cnt[:, 0].astype(jnp.int32)
    padded = (counts + bm - 1) // bm * bm
    pad_end = jnp.cumsum(padded)
    pad_start = (pad_end - padded).astype(jnp.int32)
    n_blocks = 2 * n // bm + N_EXPERTS
    n_slots = n_blocks * bm
    block_start = jnp.arange(n_blocks, dtype=jnp.int32) * bm
    block_e = jnp.minimum(jnp.sum(pad_end[None, :] <= block_start[:, None], axis=1),
                          N_EXPERTS - 1).astype(jnp.int32)
    n_valid = (pad_end[-1:] // bm).astype(jnp.int32)
    dest = slot_of_assignment(pad_start, route)
    pad_lo = jnp.concatenate([pad_start + counts, pad_end[-1:]]).astype(jnp.int32)
    pad_n = jnp.concatenate([padded - counts, (n_slots - pad_end[-1:]) // SUBLANES]).astype(jnp.int32)
    slots = dispatch_rows(pad_lo, pad_n, dest, x2d, n_slots)
    y = expert_swiglu(block_e, n_valid, slots, wg, wu, wd, bm)
    return combine_ln(dest, route_t, x2d, y, g, b)


def kernel(x, rel_bias, w_in, diff_lambda, diff_subln_g, pool_w, pool_scale, ret_gn_g, w_out,
           ln1_g, ln1_b, ln2_g, ln2_b, ffn_w_gate, ffn_w_up, ffn_w_down,
           router_w, moe_w_gate, moe_w_up, moe_w_down):
    bsz, seq, dm = x.shape
    x2d = x.reshape(bsz * seq, dm)
    bias_tiles = attn_bias_tiles(rel_bias, min(ATTN_BLOCK, seq))
    col_scale = jnp.where(jnp.arange(w_in.shape[2]) < DA_WIDTH, Q_SCALE, 1.0).astype(F32)
    for l in range(DEPTH):
        lam_init = 0.8 - 0.6 * math.exp(-0.3 * l)
        proj = in_proj(x2d, (w_in[l] * col_scale).astype(BF16)).reshape(bsz, seq, -1)
        y_da = diff_attention(proj, bias_tiles, diff_lambda[l], diff_subln_g[l], lam_init)
        y_pool = pool_mixer(proj, pool_w[l], pool_scale[l])
        y_ret = retention(proj, ret_gn_g[l])
        flat = lambda a: a.reshape(bsz * seq, -1)
        mixed = (flat(y_da), flat(y_pool), flat(y_ret), x2d, w_out[l].astype(BF16), ln1_g[l], ln1_b[l])
        j = l // 2
        if l % 2 == 0:
            x2d = out_proj_ffn_ln(*mixed, ffn_w_gate[j].astype(BF16), ffn_w_up[j].astype(BF16),
                                  ffn_w_down[j].astype(BF16), ln2_g[l], ln2_b[l])
        else:
            x2d = out_proj_ln(*mixed)
            x2d = moe_ln(x2d, router_w[j], moe_w_gate[j], moe_w_up[j], moe_w_down[j], ln2_g[l], ln2_b[l])
    return x2d.reshape(bsz, seq, dm)
```

```python
import functools
import math

import jax
import jax.numpy as jnp
from jax import lax
from jax.experimental import pallas as pl
from jax.experimental.pallas import tpu as pltpu

F32 = jnp.float32
BF16 = jnp.bfloat16

DEPTH = 2
DA_HEAD_DIM = 64
DA_V_DIM = 128
DA_HEADS = 4
DA_WIDTH = 512
POOL_WIDTH = 256
POOL_WINDOWS = (2, 4, 8, 16)
POOL_GROUP_DIM = 64
POOL_HALO = 16
RET_WIDTH = 256
RET_HEAD_DIM = 64
RET_HEADS = 4
RET_CHUNK = 128
REL_BUCKETS = 32
REL_MAX_DIST = 128
N_EXPERTS = 8
ALPHA = (2 * DEPTH) ** 0.25
LN_EPS = 1e-5
NORM_EPS = 1e-6
NEG_BIG = -1e30

POOL_COL_BLOCK = 3 * DA_WIDTH // POOL_WIDTH
RET_COL_BLOCK = POOL_COL_BLOCK + 1

VMEM_LIMIT = 56 * 1024 * 1024


def _cparams(sem, vmem=VMEM_LIMIT):
    return pltpu.CompilerParams(dimension_semantics=sem, vmem_limit_bytes=vmem)


def _nt_dot(a, b):
    return lax.dot_general(a, b, (((1,), (1,)), ((), ())), preferred_element_type=F32)


def _dot(a, b):
    return jnp.dot(a, b, preferred_element_type=F32)


def _layer_norm(z, g, b):
    mu = jnp.mean(z, axis=-1, keepdims=True)
    zc = z - mu
    var = jnp.mean(zc * zc, axis=-1, keepdims=True)
    return zc * lax.rsqrt(var + LN_EPS) * g + b


def _silu(x):
    return x / (1.0 + jnp.exp(-x))


def _inproj_kernel(x_ref, w_ref, o_ref, *, tn):
    xb = x_ref[...].astype(BF16)
    for j in range(0, w_ref.shape[1], tn):
        o_ref[:, j:j + tn] = _dot(xb, w_ref[:, j:j + tn]).astype(o_ref.dtype)


def in_proj(x2d, w, tm=1024, tn=256):
    n, k = x2d.shape
    m = w.shape[1]
    tm = min(tm, n)
    return pl.pallas_call(
        functools.partial(_inproj_kernel, tn=tn),
        grid=(n // tm,),
        in_specs=[pl.BlockSpec((tm, k), lambda i: (i, 0)),
                  pl.BlockSpec((k, m), lambda i: (0, 0))],
        out_specs=pl.BlockSpec((tm, m), lambda i: (i, 0)),
        out_shape=jax.ShapeDtypeStruct((n, m), BF16),
        compiler_params=_cparams(("parallel",)),
        name="in_proj",
    )(x2d, w)


def _t5_bucket(dist):
    n = jnp.maximum(dist, 0)
    max_exact = REL_BUCKETS // 2
    nf = jnp.maximum(n, 1).astype(F32)
    large = max_exact + (jnp.log(nf / max_exact) / math.log(REL_MAX_DIST / max_exact)
                         * (REL_BUCKETS - max_exact)).astype(jnp.int32)
    large = jnp.minimum(large, REL_BUCKETS - 1)
    return jnp.where(n < max_exact, n, large)


LOG2E = math.log2(math.e)
Q_SCALE = DA_HEAD_DIM ** -0.5 * LOG2E
ATTN_BLOCK = 512
ATTN_CHUNK = 512
ONES_ROWS = 16


def attn_bias_tiles(rel_bias, t):
    table = rel_bias.astype(F32).reshape(REL_BUCKETS, DA_HEADS * 2)
    vec = table[_t5_bucket(jnp.arange(2 * t))]
    far = table[_t5_bucket(jnp.array(2 * t))]
    vec = ((vec - far[None, :]) * LOG2E).T
    masked = jnp.full((DA_HEADS * 2, t), NEG_BIG, F32)
    u_diag = jnp.concatenate([vec[:, :t], masked], axis=1)
    u_prev = jnp.concatenate([vec[:, t:], vec[:, :t]], axis=1)

    def toeplitz(u):
        skew = jnp.tile(u, (1, t))[:, :t * (2 * t - 1)].reshape(-1, t, 2 * t - 1)
        return skew[:, :, :t]

    def per_head(a):
        return a.reshape(DA_HEADS, 2, t, t).transpose(0, 2, 1, 3).reshape(DA_HEADS, t, 2 * t)

    return jnp.stack([per_head(toeplitz(u_diag)), per_head(toeplitz(u_prev))], axis=1)


def _attn_kernel(q_ref, k_ref, v_ref, bias_ref, lam_ref, g_ref, o_ref, vt, m_s, acc, s_a, s_b, *, t, lam_init):
    nk = vt.shape[0]
    ones = jnp.ones((ONES_ROWS, t), BF16)
    for ki in range(nk):
        v_t = v_ref[0, ki * t:(ki + 1) * t, :].astype(F32).T.astype(BF16)
        vt[ki] = jnp.concatenate([v_t, ones], axis=0)

    def query_tile(qi, carry):
        _attn_query_tile(qi, q_ref, k_ref, bias_ref, lam_ref, g_ref, o_ref, vt, m_s, acc, s_a, s_b,
                         t=t, lam_init=lam_init)
        return carry

    lax.fori_loop(0, nk // 2, query_tile, 0)


def _attn_query_tile(qi, q_ref, k_ref, bias_ref, lam_ref, g_ref, o_ref, vt, m_s, acc, s_a, s_b, *, t, lam_init):
    tq = 2 * t
    q_rows = pl.ds(pl.multiple_of(qi * tq, tq), tq)
    q = q_ref[0, q_rows, :]
    lane = lax.broadcasted_iota(jnp.int32, q.shape, 1)
    zero = jnp.zeros_like(q)
    qcat = jnp.concatenate([jnp.where(lane < DA_HEAD_DIM, q, zero),
                            jnp.where(lane >= DA_HEAD_DIM, q, zero)], axis=0)
    m_s[...] = jnp.full(m_s.shape, NEG_BIG, F32)
    acc[...] = jnp.zeros(acc.shape, F32)

    chunk = min(ATTN_CHUNK, t)
    DIAG, PREV, FAR, SKIP = 0, 1, None, "skip"

    def step(nxt, cur):
        kb = None
        if nxt is not None:
            kb = k_ref[0, pl.ds(pl.multiple_of(nxt[0] * t, t), t), :]
        for c in range(0, 2 * tq, chunk):
            cols = slice(c, c + chunk)
            mp, qoff = divmod(c, tq)
            half, ioff = divmod(qoff, t)
            ci = c // chunk
            if nxt is not None and not (len(nxt) > 2 and nxt[2][half] == SKIP):
                nxt[1][ci] = _nt_dot(kb, qcat[cols, :])
            if cur is not None:
                ki, s_ref, kinds = cur
                kind = kinds[half]
                if kind == SKIP:
                    continue
                s = s_ref[ci]
                if kind is not FAR:
                    s = s + bias_ref[0, kind, :, mp * t + ioff:mp * t + ioff + chunk]
                m_prev = m_s[ci]
                m_new = jnp.maximum(m_prev, jnp.max(s, axis=0, keepdims=True))
                alpha = jnp.exp2(m_prev - m_new)
                p = jnp.exp2(s - m_new).astype(BF16)
                acc[ci] = alpha * acc[ci] + _dot(vt[ki], p)
                m_s[ci] = m_new

    upper = (SKIP, DIAG)
    lower = (DIAG, PREV)
    before = (PREV, FAR)
    plain = (FAR, FAR)
    step((2 * qi + 1, s_a, upper), None)
    step((2 * qi, s_b), (2 * qi + 1, s_a, upper))

    @pl.when(qi == 0)
    def _():
        step(None, (0, s_b, lower))

    @pl.when(qi >= 1)
    def _():
        n_far = 2 * qi - 1
        step((2 * qi - 1, s_a), (2 * qi, s_b, lower))
        step((0, s_b), (2 * qi - 1, s_a, before))

        def pair(jj, carry):
            j = 2 * jj
            step((j + 1, s_a), (j, s_b, plain))
            step((j + 2, s_b), (j + 1, s_a, plain))
            return carry

        lax.fori_loop(0, n_far // 2, pair, 0)
        step(None, (n_far - 1, s_b, plain))

    lm = lam_ref[...]
    lam = (jnp.exp(jnp.sum(lm[0:1] * lm[1:2], keepdims=True))
           - jnp.exp(jnp.sum(lm[2:3] * lm[3:4], keepdims=True)) + lam_init)
    a = jnp.concatenate([acc[ci] for ci in range(acc.shape[0])], axis=1)
    o = a[:DA_V_DIM, :] / a[DA_V_DIM:DA_V_DIM + 1, :]
    out = o[:, :tq] - lam * o[:, tq:]
    out = out * lax.rsqrt(jnp.mean(out * out, axis=0, keepdims=True) + NORM_EPS) * g_ref[...]
    o_ref[0, q_rows, :] = (out * (1.0 - lam_init)).T.astype(o_ref.dtype)


def diff_attention(proj3, bias_tiles, lam_params, subln_g, lam_init):
    b, s, _ = proj3.shape
    t = bias_tiles.shape[2]
    chunk = min(ATTN_CHUNK, t)
    nch = 4 * t // chunk
    seq = lambda col0: pl.BlockSpec((1, s, DA_V_DIM), lambda bi, h: (bi, 0, col0 + h))
    return pl.pallas_call(
        functools.partial(_attn_kernel, t=t, lam_init=lam_init),
        grid=(b, DA_HEADS),
        in_specs=[
            seq(0), seq(DA_HEADS), seq(2 * DA_HEADS),
            pl.BlockSpec((1, 2, t, 2 * t), lambda bi, h: (h, 0, 0, 0)),
            pl.BlockSpec((4, DA_HEAD_DIM), lambda bi, h: (0, 0)),
            pl.BlockSpec((DA_V_DIM, 1), lambda bi, h: (0, 0)),
        ],
        out_specs=pl.BlockSpec((1, s, DA_V_DIM), lambda bi, h: (bi, 0, h)),
        out_shape=jax.ShapeDtypeStruct((b, s, DA_WIDTH), BF16),
        scratch_shapes=[pltpu.VMEM((s // t, DA_V_DIM + ONES_ROWS, t), BF16),
                        pltpu.VMEM((nch, 1, chunk), F32),
                        pltpu.VMEM((nch, DA_V_DIM + ONES_ROWS, chunk), F32),
                        pltpu.VMEM((nch, t, chunk), F32), pltpu.VMEM((nch, t, chunk), F32)],
        compiler_params=_cparams(("parallel", "parallel")),
        name="diff_attention",
    )(proj3, proj3, proj3, bias_tiles, lam_params, subln_g.reshape(DA_V_DIM, 1))


def _pool_kernel(p_ref, w_ref, scale_ref, o_ref, halo, *, t):
    si = pl.program_id(1)

    @pl.when(si == 0)
    def _():
        halo[...] = jnp.zeros(halo.shape, F32)

    p = p_ref[0].astype(F32)
    ext = jnp.concatenate([halo[...], p], axis=0)
    halo[...] = p[t - POOL_HALO:, :]
    sums = {1: ext}
    w = 1
    while w < POOL_WINDOWS[-1]:
        sums[2 * w] = sums[w] + pltpu.roll(sums[w], w, 0)
        w *= 2
    lane = lax.broadcasted_iota(jnp.int32, (t, POOL_WIDTH), 1)
    pos = (si * t + lax.broadcasted_iota(jnp.int32, (t, POOL_WIDTH), 0) + 1).astype(F32)
    wsum = sums[POOL_WINDOWS[-1]][POOL_HALO:, :]
    cnt = jnp.minimum(pos, float(POOL_WINDOWS[-1]))
    for gi in range(len(POOL_WINDOWS) - 2, -1, -1):
        in_group = lane < (gi + 1) * POOL_GROUP_DIM
        wsum = jnp.where(in_group, sums[POOL_WINDOWS[gi]][POOL_HALO:, :], wsum)
        cnt = jnp.where(in_group, jnp.minimum(pos, float(POOL_WINDOWS[gi])), cnt)
    pooled = wsum / cnt - p
    mixed = _dot(pooled.astype(BF16), w_ref[...])
    o_ref[0] = (mixed * scale_ref[...]).astype(o_ref.dtype)


def pool_mixer(proj3, pool_w, pool_scale, t=2048):
    b, s, _ = proj3.shape
    t = min(t, s)
    g = len(POOL_WINDOWS)
    wbd = (jnp.eye(g, dtype=F32)[:, None, :, None] * pool_w.astype(F32)[:, :, None, :]).reshape(
        POOL_WIDTH, POOL_WIDTH).astype(BF16)
    return pl.pallas_call(
        functools.partial(_pool_kernel, t=t),
        grid=(b, s // t),
        in_specs=[pl.BlockSpec((1, t, POOL_WIDTH), lambda bi, si: (bi, si, POOL_COL_BLOCK)),
                  pl.BlockSpec((POOL_WIDTH, POOL_WIDTH), lambda bi, si: (0, 0)),
                  pl.BlockSpec((1, POOL_WIDTH), lambda bi, si: (0, 0))],
        out_specs=pl.BlockSpec((1, t, POOL_WIDTH), lambda bi, si: (bi, si, 0)),
        out_shape=jax.ShapeDtypeStruct((b, s, POOL_WIDTH), BF16),
        scratch_shapes=[pltpu.VMEM((POOL_HALO, POOL_WIDTH), F32)],
        compiler_params=_cparams(("parallel", "arbitrary")),
        name="pool_mixer",
    )(proj3, wbd, pool_scale.reshape(1, POOL_WIDTH))


def _retention_tables(s, t):
    d, hn, c = RET_HEAD_DIM, RET_HEADS, RET_CHUNK
    half = d // 2
    inv = 10000.0 ** (-jnp.linspace(0.0, 1.0, half, dtype=F32))
    ang = jnp.arange(s)[:, None].astype(F32) * inv[None, :]
    cos, sin = jnp.cos(ang), jnp.sin(ang)
    cos_t = jnp.tile(jnp.concatenate([cos, cos], axis=-1), (1, hn))
    sin_t = jnp.tile(jnp.concatenate([-sin, sin], axis=-1), (1, hn))
    log_gamma = jnp.log(1.0 - 2.0 ** (-5.0 - jnp.arange(hn, dtype=F32)))
    idx = jnp.arange(c, dtype=F32)
    rel = idx[:, None] - idx[None, :]
    intra = jnp.where(rel >= 0, jnp.exp(log_gamma[:, None, None] * jnp.maximum(rel, 0.0)), 0.0)
    q_decay = jnp.exp(log_gamma[:, None] * (idx + 1.0))
    k_decay = jnp.exp(log_gamma[:, None] * (c - 1.0 - idx))
    chunk_decay = jnp.exp(log_gamma * c)
    lanes = lambda a: jnp.tile(jnp.repeat(a.T, d, axis=1), (t // c, 1))
    l = jnp.arange(hn * d)
    partner = jnp.where(l % d < half, l + half, l - half)
    swap = (l[:, None] == partner[None, :]).astype(BF16)
    same = (jnp.arange(2 * d)[:, None] // d) == (jnp.arange(2 * d)[None, :] // d)
    decay_bd = jnp.where(same[None], jnp.repeat(chunk_decay, d).reshape(hn // 2, 2 * d, 1), 0.0)
    avg = jnp.where(same, 1.0 / d, 0.0).astype(BF16)
    return cos_t, sin_t, intra, lanes(q_decay), lanes(k_decay), swap, decay_bd.astype(F32), avg


def _ret_kernel(q_ref, k_ref, v_ref, g_ref, cos_ref, sin_ref, intra_ref, qd_ref, kd_ref, swap_ref, decay_ref,
                avg_ref, gn_ref, o_ref, state, *, t):
    si = pl.program_id(1)
    d, c = RET_HEAD_DIM, RET_CHUNK
    pw = 2 * d

    @pl.when(si == 0)
    def _():
        state[...] = jnp.zeros(state.shape, F32)

    def rotate(x_ref):
        x = x_ref[0]
        return x.astype(F32) * cos_ref[...] + _dot(x, swap_ref[...]) * sin_ref[...]

    def lane_mean(a):
        hi = a.astype(BF16)
        lo = (a - hi.astype(F32)).astype(BF16)
        return _dot(hi, avg_ref[...]) + _dot(lo, avg_ref[...])

    q = rotate(q_ref)
    k = rotate(k_ref) * (d ** -0.5)
    qs = (q * qd_ref[...]).astype(BF16)
    ks = (k * kd_ref[...]).astype(BF16)
    qb = q.astype(BF16)
    kb = k.astype(BF16)
    v = v_ref[0]
    first = lax.broadcasted_iota(jnp.int32, (c, pw), 1) < d
    same = ((lax.broadcasted_iota(jnp.int32, (pw, pw), 0) < d)
            == (lax.broadcasted_iota(jnp.int32, (pw, pw), 1) < d))
    zero = jnp.zeros((c, pw), BF16)

    chunks = [slice(ci * c, (ci + 1) * c) for ci in range(t // c)]
    pairs = [slice(p * pw, (p + 1) * pw) for p in range(RET_HEADS // 2)]
    scores = {}
    for p, cols in enumerate(pairs):
        for ci, rows in enumerate(chunks):
            qp, kp = qb[rows, cols], kb[rows, cols]
            scores[p, ci] = ((_nt_dot(jnp.where(first, qp, zero), kp) * intra_ref[2 * p]).astype(BF16),
                             (_nt_dot(jnp.where(first, zero, qp), kp) * intra_ref[2 * p + 1]).astype(BF16))
    intra_out, increment = {}, {}
    for p, cols in enumerate(pairs):
        for ci, rows in enumerate(chunks):
            vp = v[rows, cols]
            intra_out[p, ci] = jnp.where(first, _dot(scores[p, ci][0], vp), _dot(scores[p, ci][1], vp))
            increment[p, ci] = jnp.where(same, _dot(ks[rows, cols].T, vp), 0.0)
    ys = {}
    for p, cols in enumerate(pairs):
        st = state[p]
        for ci, rows in enumerate(chunks):
            ys[p, ci] = intra_out[p, ci] + _dot(qs[rows, cols], st.astype(BF16))
            st = st * decay_ref[p] + increment[p, ci]
        state[p] = st
    y = jnp.concatenate([jnp.concatenate([ys[p, ci] for ci in range(len(chunks))], axis=0)
                         for p in range(len(pairs))], axis=1)
    mean = jnp.concatenate([lane_mean(y[:, cols]) for cols in pairs], axis=1)
    yc = y - mean
    var = jnp.concatenate([lane_mean((yc * yc)[:, cols]) for cols in pairs], axis=1)
    yn = yc * lax.rsqrt(var + NORM_EPS)
    gate = _silu(g_ref[0].astype(F32))
    o_ref[0] = (gate * (yn * gn_ref[...])).astype(o_ref.dtype)


def retention(proj3, gn_g, t=2048):
    b, s, _ = proj3.shape
    t = min(t, s)
    cos_t, sin_t, intra, qd, kd, swap, decay_bd, avg = _retention_tables(s, t)
    pw = 2 * RET_HEAD_DIM
    col = lambda j: pl.BlockSpec((1, t, RET_WIDTH), lambda bi, si: (bi, si, RET_COL_BLOCK + j))
    const2 = lambda shape: pl.BlockSpec(shape, lambda bi, si: (0, 0))
    const3 = lambda shape: pl.BlockSpec(shape, lambda bi, si: (0, 0, 0))
    return pl.pallas_call(
        functools.partial(_ret_kernel, t=t),
        grid=(b, s // t),
        in_specs=[col(0), col(1), col(2), col(3),
                  pl.BlockSpec((t, RET_WIDTH), lambda bi, si: (si, 0)),
                  pl.BlockSpec((t, RET_WIDTH), lambda bi, si: (si, 0)),
                  const3((RET_HEADS, RET_CHUNK, RET_CHUNK)),
                  const2((t, RET_WIDTH)), const2((t, RET_WIDTH)), const2((RET_WIDTH, RET_WIDTH)),
                  const3((RET_HEADS // 2, pw, pw)), const2((pw, pw)), const2((1, RET_WIDTH))],
        out_specs=pl.BlockSpec((1, t, RET_WIDTH), lambda bi, si: (bi, si, 0)),
        out_shape=jax.ShapeDtypeStruct((b, s, RET_WIDTH), BF16),
        scratch_shapes=[pltpu.VMEM((RET_HEADS // 2, pw, pw), F32)],
        compiler_params=_cparams(("parallel", "arbitrary")),
        name="retention",
    )(proj3, proj3, proj3, proj3, cos_t, sin_t, intra, qd, kd, swap, decay_bd, avg, gn_g.reshape(1, RET_WIDTH))


def _mix_ln(da_ref, pool_ref, ret_ref, x_ref, w_ref, g_ref, b_ref):
    e0, e1 = DA_WIDTH, DA_WIDTH + POOL_WIDTH
    mix = (_dot(da_ref[...], w_ref[0:e0, :]) + _dot(pool_ref[...], w_ref[e0:e1, :])
           + _dot(ret_ref[...], w_ref[e1:, :]))
    return _layer_norm(ALPHA * x_ref[...] + mix, g_ref[...], b_ref[...])


def _outproj_kernel(da_ref, pool_ref, ret_ref, x_ref, w_ref, g_ref, b_ref, o_ref):
    o_ref[...] = _mix_ln(da_ref, pool_ref, ret_ref, x_ref, w_ref, g_ref, b_ref)


def out_proj_ln(y_da, y_pool, y_ret, x2d, w, g, b, tm=1024):
    n, dm = x2d.shape
    tm = min(tm, n)
    row = lambda width: pl.BlockSpec((tm, width), lambda i: (i, 0))
    const = lambda shape: pl.BlockSpec(shape, lambda i: (0, 0))
    return pl.pallas_call(
        _outproj_kernel,
        grid=(n // tm,),
        in_specs=[row(DA_WIDTH), row(POOL_WIDTH), row(RET_WIDTH), row(dm),
                  const(w.shape), const((1, dm)), const((1, dm))],
        out_specs=row(dm),
        out_shape=jax.ShapeDtypeStruct((n, dm), F32),
        compiler_params=_cparams(("parallel",)),
        name="out_proj_ln",
    )(y_da, y_pool, y_ret, x2d, w, g.reshape(1, dm), b.reshape(1, dm))


def _mix_ffn_kernel(da_ref, pool_ref, ret_ref, x_ref, wo_ref, g1_ref, b1_ref, wg_ref, wu_ref, wd_ref,
                    g2_ref, b2_ref, o_ref, *, tf):
    x = _mix_ln(da_ref, pool_ref, ret_ref, x_ref, wo_ref, g1_ref, b1_ref)
    xb = x.astype(BF16)
    acc = jnp.zeros(x.shape, F32)
    for f0 in range(0, wg_ref.shape[1], tf):
        hidden = _silu(_dot(xb, wg_ref[:, f0:f0 + tf])) * _dot(xb, wu_ref[:, f0:f0 + tf])
        acc = acc + _dot(hidden.astype(BF16), wd_ref[f0:f0 + tf, :])
    o_ref[...] = _layer_norm(ALPHA * x + acc, g2_ref[...], b2_ref[...])


def out_proj_ffn_ln(y_da, y_pool, y_ret, x2d, wo, g1, b1, wg, wu, wd, g2, b2, tm=512, tf=256):
    n, dm = x2d.shape
    tm = min(tm, n)
    row = lambda width: pl.BlockSpec((tm, width), lambda i: (i, 0))
    resident = lambda shape: pl.BlockSpec(shape, lambda i: (0, 0), pipeline_mode=pl.Buffered(1))
    vec = lambda a: a.reshape(1, dm)
    return pl.pallas_call(
        functools.partial(_mix_ffn_kernel, tf=tf),
        grid=(n // tm,),
        in_specs=[row(DA_WIDTH), row(POOL_WIDTH), row(RET_WIDTH), row(dm),
                  resident(wo.shape), resident((1, dm)), resident((1, dm)),
                  resident(wg.shape), resident(wu.shape), resident(wd.shape),
                  resident((1, dm)), resident((1, dm))],
        out_specs=row(dm),
        out_shape=jax.ShapeDtypeStruct((n, dm), F32),
        compiler_params=_cparams(("parallel",)),
        name="out_proj_ffn_ln",
    )(y_da, y_pool, y_ret, x2d, wo, vec(g1), vec(b1), wg, wu, wd, vec(g2), vec(b2))


ROUTE_ROWS = 8
LANES = 128


def _router_kernel(x_ref, w_ref, tri_ref, route_ref, route_t_ref, cnt_ref, carry, *, t):
    i = pl.program_id(0)

    @pl.when(i == 0)
    def _():
        carry[...] = jnp.zeros(carry.shape, F32)

    x = x_ref[...]
    xh = x.astype(BF16)
    xl = (x - xh.astype(F32)).astype(BF16)
    w = w_ref[...]
    wh = w.astype(BF16)
    wl = (w - wh.astype(F32)).astype(BF16)
    logits = _nt_dot(wh, xh) + _nt_dot(wl, xh) + _nt_dot(wh, xl)

    row = lax.broadcasted_iota(jnp.int32, logits.shape, 0)
    v0 = jnp.max(logits, axis=0, keepdims=True)
    i0 = jnp.min(jnp.where(logits == v0, row, N_EXPERTS), axis=0, keepdims=True)
    rest = jnp.where(row == i0, -jnp.inf, logits)
    v1 = jnp.max(rest, axis=0, keepdims=True)
    i1 = jnp.min(jnp.where(rest == v1, row, N_EXPERTS), axis=0, keepdims=True)
    ex = jnp.exp(v1 - v0)
    gate0 = 1.0 / (1.0 + ex)
    gate1 = ex / (1.0 + ex)

    oh0 = row == i0
    oh1 = row == i1
    member = jnp.where(oh0 | oh1, 1.0, 0.0)
    before = _dot(member.astype(BF16), tri_ref[...]) + carry[:, 0:1]
    rank0 = jnp.sum(jnp.where(oh0, before, 0.0), axis=0, keepdims=True)
    rank1 = jnp.sum(jnp.where(oh1, before, 0.0), axis=0, keepdims=True)
    carry[...] = carry[...] + jnp.sum(member, axis=1, keepdims=True)
    cnt_ref[...] = carry[...]

    route = jnp.concatenate([i0.astype(F32), i1.astype(F32), rank0, rank1, gate0, gate1,
                             jnp.zeros((2, t), F32)], axis=0)
    route_ref[...] = route
    padded = jnp.concatenate([route, jnp.zeros((LANES - ROUTE_ROWS, t), F32)], axis=0)
    route_t_ref[...] = padded.T


def route_tokens(x2d, router_w, t=2048):
    n, dm = x2d.shape
    t = min(t, n)
    tri = (jnp.arange(t)[:, None] < jnp.arange(t)[None, :]).astype(BF16)
    return pl.pallas_call(
        functools.partial(_router_kernel, t=t),
        grid=(n // t,),
        in_specs=[pl.BlockSpec((t, dm), lambda i: (i, 0)),
                  pl.BlockSpec((N_EXPERTS, dm), lambda i: (0, 0)),
                  pl.BlockSpec((t, t), lambda i: (0, 0))],
        out_specs=[pl.BlockSpec((ROUTE_ROWS, t), lambda i: (0, i)),
                   pl.BlockSpec((t, LANES), lambda i: (i, 0)),
                   pl.BlockSpec((N_EXPERTS, LANES), lambda i: (0, 0))],
        out_shape=[jax.ShapeDtypeStruct((ROUTE_ROWS, n), F32),
                   jax.ShapeDtypeStruct((n, LANES), F32),
                   jax.ShapeDtypeStruct((N_EXPERTS, LANES), F32)],
        scratch_shapes=[pltpu.VMEM((N_EXPERTS, LANES), F32)],
        compiler_params=_cparams(("arbitrary",)),
        name="route_tokens",
    )(x2d, router_w.T, tri)


def _dest_kernel(start_ref, route_ref, dest_ref):
    r = route_ref[...]
    for k in range(2):
        e = r[k:k + 1, :].astype(jnp.int32)
        base = jnp.zeros(e.shape, jnp.int32)
        for ei in range(N_EXPERTS):
            base = jnp.where(e == ei, start_ref[ei], base)
        dest_ref[k:k + 1, :] = base + r[2 + k:3 + k, :].astype(jnp.int32)


def slot_of_assignment(pad_start, route, t=4096):
    n = route.shape[1]
    t = min(t, n)
    grid_spec = pltpu.PrefetchScalarGridSpec(
        num_scalar_prefetch=1,
        grid=(n // t,),
        in_specs=[pl.BlockSpec((ROUTE_ROWS, t), lambda i, ps: (0, i))],
        out_specs=pl.BlockSpec((2, t), lambda i, ps: (0, i)),
    )
    return pl.pallas_call(
        _dest_kernel,
        grid_spec=grid_spec,
        out_shape=jax.ShapeDtypeStruct((2, n), jnp.int32),
        compiler_params=_cparams(("parallel",)),
        name="slot_of_assignment",
    )(pad_start, route)


def _row_copy(src, src_row, dst, dst_row, sem):
    return pltpu.make_async_copy(src.at[pl.ds(src_row, 1)], dst.at[pl.ds(dst_row, 1)], sem)


SUBLANES = 8
ISSUE_UNROLL = SUBLANES


def _dispatch_kernel(pad_lo_ref, pad_n_ref, d0_ref, d1_ref, x_ref, slots_out, zrow, sem, zsem, *, t):
    @pl.when(pl.program_id(0) == 0)
    def _():
        zrow[...] = jnp.zeros(zrow.shape, zrow.dtype)
        for e in range(N_EXPERTS):
            lo = pad_lo_ref[e]

            def zero_start(r, carry):
                _row_copy(zrow, 0, slots_out, lo + r, zsem).start()
                return carry

            def zero_wait(r, carry):
                _row_copy(zrow, 0, slots_out, lo + r, zsem).wait()
                return carry

            lax.fori_loop(0, pad_n_ref[e], zero_start, 0)
            lax.fori_loop(0, pad_n_ref[e], zero_wait, 0)

        tail_lo = pad_lo_ref[N_EXPERTS]
        group = zrow.shape[0]

        def tail_copy(r):
            dst = slots_out.at[pl.ds(pl.multiple_of(tail_lo + r * group, group), group)]
            return pltpu.make_async_copy(zrow, dst, zsem)

        def tail_start(r, carry):
            tail_copy(r).start()
            return carry

        def tail_wait(r, carry):
            tail_copy(r).wait()
            return carry

        lax.fori_loop(0, pad_n_ref[N_EXPERTS], tail_start, 0)
        lax.fori_loop(0, pad_n_ref[N_EXPERTS], tail_wait, 0)

    def issue(j, carry):
        base = pl.multiple_of(j * ISSUE_UNROLL, ISSUE_UNROLL)
        group = x_ref.at[pl.ds(base, ISSUE_UNROLL)]
        for u in range(ISSUE_UNROLL):
            for k, d_ref in enumerate((d0_ref, d1_ref)):
                _row_copy(group, u, slots_out, d_ref[base + u], sem).start()
        return carry

    lax.fori_loop(0, t // ISSUE_UNROLL, issue, 0)
    for k in range(2):
        pltpu.make_async_copy(x_ref, slots_out.at[pl.ds(0, t)], sem).wait()


def dispatch_rows(pad_lo, pad_n, dest, x2d, n_slots, t=4096):
    n, dm = x2d.shape
    t = min(t, n)
    grid_spec = pltpu.PrefetchScalarGridSpec(
        num_scalar_prefetch=2,
        grid=(n // t,),
        in_specs=[pl.BlockSpec((t,), lambda i, lo, cnt: (i,), memory_space=pltpu.SMEM),
                  pl.BlockSpec((t,), lambda i, lo, cnt: (i,), memory_space=pltpu.SMEM),
                  pl.BlockSpec((t, dm), lambda i, lo, cnt: (i, 0))],
        out_specs=pl.BlockSpec(memory_space=pl.ANY),
        scratch_shapes=[pltpu.VMEM((SUBLANES, dm), x2d.dtype), pltpu.SemaphoreType.DMA,
                        pltpu.SemaphoreType.DMA],
    )
    return pl.pallas_call(
        functools.partial(_dispatch_kernel, t=t),
        grid_spec=grid_spec,
        out_shape=jax.ShapeDtypeStruct((n_slots, dm), x2d.dtype),
        compiler_params=_cparams(("arbitrary",)),
        name="dispatch_rows",
    )(pad_lo, pad_n, dest[0], dest[1], x2d)


def _expert_kernel(be_ref, nv_ref, x_ref, wg_ref, wu_ref, wd_ref, o_ref, xb):
    blk = pl.program_id(0)
    f = pl.program_id(1)
    valid = blk < nv_ref[0]

    @pl.when(f == 0)
    def _():
        o_ref[...] = jnp.zeros(o_ref.shape, o_ref.dtype)

    @pl.when(valid & (f == 0))
    def _():
        xb[...] = x_ref[...].astype(BF16)

    @pl.when(valid)
    def _():
        hidden = (_silu(_dot(xb[...], wg_ref[0].astype(BF16)))
                  * _dot(xb[...], wu_ref[0].astype(BF16)))
        o_ref[...] += _dot(hidden.astype(BF16), wd_ref[0].astype(BF16))


def expert_swiglu(block_e, n_valid, slots, wg, wu, wd, bm, tf=512):
    n_slots, dm = slots.shape
    edim = wg.shape[2]
    grid_spec = pltpu.PrefetchScalarGridSpec(
        num_scalar_prefetch=2,
        grid=(n_slots // bm, edim // tf),
        in_specs=[pl.BlockSpec((bm, dm), lambda b, f, be, nv: (jnp.minimum(b, nv[0] - 1), 0)),
                  pl.BlockSpec((1, dm, tf), lambda b, f, be, nv: (be[b], 0, f)),
                  pl.BlockSpec((1, dm, tf), lambda b, f, be, nv: (be[b], 0, f)),
                  pl.BlockSpec((1, tf, dm), lambda b, f, be, nv: (be[b], f, 0))],
        out_specs=pl.BlockSpec((bm, dm), lambda b, f, be, nv: (b, 0)),
        scratch_shapes=[pltpu.VMEM((bm, dm), BF16)],
    )
    return pl.pallas_call(
        _expert_kernel,
        grid_spec=grid_spec,
        out_shape=jax.ShapeDtypeStruct((n_slots, dm), F32),
        compiler_params=_cparams(("parallel", "arbitrary")),
        name="expert_swiglu",
    )(block_e, n_valid, slots, wg, wu, wd)


def _combine_kernel(d0_ref, d1_ref, d0_next_ref, d1_next_ref, rt_ref, x_ref, y_hbm, g_ref, b_ref, o_ref,
                    rows, sem, *, t):
    i = pl.program_id(0)
    cur = i % 2

    def gather(d_refs, buf):
        def issue(j, carry):
            for u in range(ISSUE_UNROLL):
                for k in range(2):
                    _row_copy(y_hbm, d_refs[k][j * ISSUE_UNROLL + u], rows.at[buf, k, j], u, sem.at[buf]).start()
            return carry

        lax.fori_loop(0, t // ISSUE_UNROLL, issue, 0)

    @pl.when(i == 0)
    def _():
        gather((d0_ref, d1_ref), 0)

    @pl.when(i + 1 < pl.num_programs(0))
    def _():
        gather((d0_next_ref, d1_next_ref), 1 - cur)

    for k in range(2):
        pltpu.make_async_copy(y_hbm.at[pl.ds(0, t)], o_ref, sem.at[cur]).wait()

    rt = rt_ref[...]
    dm = o_ref.shape[1]
    mixed = rt[:, 4:5] * rows[cur, 0].reshape(t, dm) + rt[:, 5:6] * rows[cur, 1].reshape(t, dm)
    o_ref[...] = _layer_norm(ALPHA * x_ref[...] + mixed, g_ref[...], b_ref[...])


def combine_ln(dest, route_t, x2d, y, g, b, t=512):
    n, dm = x2d.shape
    t = min(t, n)
    last = n // t - 1
    return pl.pallas_call(
        functools.partial(_combine_kernel, t=t),
        grid=(n // t,),
        in_specs=[pl.BlockSpec((t,), lambda i: (i,), memory_space=pltpu.SMEM),
                  pl.BlockSpec((t,), lambda i: (i,), memory_space=pltpu.SMEM),
                  pl.BlockSpec((t,), lambda i: (jnp.minimum(i + 1, last),), memory_space=pltpu.SMEM),
                  pl.BlockSpec((t,), lambda i: (jnp.minimum(i + 1, last),), memory_space=pltpu.SMEM),
                  pl.BlockSpec((t, LANES), lambda i: (i, 0)),
                  pl.BlockSpec((t, dm), lambda i: (i, 0)),
                  pl.BlockSpec(memory_space=pl.ANY),
                  pl.BlockSpec((1, dm), lambda i: (0, 0)),
                  pl.BlockSpec((1, dm), lambda i: (0, 0))],
        out_specs=pl.BlockSpec((t, dm), lambda i: (i, 0)),
        out_shape=jax.ShapeDtypeStruct((n, dm), F32),
        scratch_shapes=[pltpu.VMEM((2, 2, t // ISSUE_UNROLL, ISSUE_UNROLL, dm), F32),
                        pltpu.SemaphoreType.DMA((2,))],
        compiler_params=_cparams(("arbitrary",)),
        name="combine_ln",
    )(dest[0], dest[1], dest[0], dest[1], route_t, x2d, y, g.reshape(1, dm), b.reshape(1, dm))


def moe_ln(x2d, router_w, wg, wu, wd, g, b, bm=1024):
    n, _ = x2d.shape
    bm = min(bm, n)
    route, route_t, cnt = route_tokens(x2d, router_w)
    counts = cnt[:, 0].astype(jnp.int32)
    padded = (counts + bm - 1) // bm * bm
    pad_end = jnp.cumsum(padded)
    pad_start = (pad_end - padded).astype(jnp.int32)
    n_blocks = 2 * n // bm + N_EXPERTS
    n_slots = n_blocks * bm
    block_start = jnp.arange(n_blocks, dtype=jnp.int32) * bm
    block_e = jnp.minimum(jnp.sum(pad_end[None, :] <= block_start[:, None], axis=1),
                          N_EXPERTS - 1).astype(jnp.int32)
    n_valid = (pad_end[-1:] // bm).astype(jnp.int32)
    dest = slot_of_assignment(pad_start, route)
    pad_lo = jnp.concatenate([pad_start + counts, pad_end[-1:]]).astype(jnp.int32)
    pad_n = jnp.concatenate([padded - counts, (n_slots - pad_end[-1:]) // SUBLANES]).astype(jnp.int32)
    slots = dispatch_rows(pad_lo, pad_n, dest, x2d, n_slots)
    y = expert_swiglu(block_e, n_valid, slots, wg, wu, wd, bm)
    return combine_ln(dest, route_t, x2d, y, g, b)


def kernel(x, rel_bias, w_in, diff_lambda, diff_subln_g, pool_w, pool_scale, ret_gn_g, w_out,
           ln1_g, ln1_b, ln2_g, ln2_b, ffn_w_gate, ffn_w_up, ffn_w_down,
           router_w, moe_w_gate, moe_w_up, moe_w_down):
    bsz, seq, dm = x.shape
    x2d = x.reshape(bsz * seq, dm)
    bias_tiles = attn_bias_tiles(rel_bias, min(ATTN_BLOCK, seq))
    col_scale = jnp.where(jnp.arange(w_in.shape[2]) < DA_WIDTH, Q_SCALE, 1.0).astype(F32)
    for l in range(DEPTH):
        lam_init = 0.8 - 0.6 * math.exp(-0.3 * l)
        proj = in_proj(x2d, (w_in[l] * col_scale).astype(BF16)).reshape(bsz, seq, -1)
        y_da = diff_attention(proj, bias_tiles, diff_lambda[l], diff_subln_g[l], lam_init)
        y_pool = pool_mixer(proj, pool_w[l], pool_scale[l])
        y_ret = retention(proj, ret_gn_g[l])
        flat = lambda a: a.reshape(bsz * seq, -1)
        mixed = (flat(y_da), flat(y_pool), flat(y_ret), x2d, w_out[l].astype(BF16), ln1_g[l], ln1_b[l])
        j = l // 2
        if l % 2 == 0:
            x2d = out_proj_ffn_ln(*mixed, ffn_w_gate[j].astype(BF16), ffn_w_up[j].astype(BF16),
                                  ffn_w_down[j].astype(BF16), ln2_g[l], ln2_b[l])
        else:
            x2d = out_proj_ln(*mixed)
            x2d = moe_ln(x2d, router_w[j], moe_w_gate[j], moe_w_up[j], moe_w_down[j], ln2_g[l], ln2_b[l])
    return x2d.reshape(bsz, seq, dm)
```

```python
import functools
import math

import jax
import jax.numpy as jnp
from jax import lax
from jax.experimental import pallas as pl
from jax.experimental.pallas import tpu as pltpu

F32 = jnp.float32
BF16 = jnp.bfloat16

DEPTH = 2
DA_HEAD_DIM = 64
DA_V_DIM = 128
DA_HEADS = 4
DA_WIDTH = 512
POOL_WIDTH = 256
POOL_WINDOWS = (2, 4, 8, 16)
POOL_GROUP_DIM = 64
POOL_HALO = 16
RET_WIDTH = 256
RET_HEAD_DIM = 64
RET_HEADS = 4
RET_CHUNK = 128
REL_BUCKETS = 32
REL_MAX_DIST = 128
N_EXPERTS = 8
ALPHA = (2 * DEPTH) ** 0.25
LN_EPS = 1e-5
NORM_EPS = 1e-6
NEG_BIG = -1e30

POOL_COL_BLOCK = 3 * DA_WIDTH // POOL_WIDTH
RET_COL_BLOCK = POOL_COL_BLOCK + 1

VMEM_LIMIT = 56 * 1024 * 1024


def _cparams(sem, vmem=VMEM_LIMIT):
    return pltpu.CompilerParams(dimension_semantics=sem, vmem_limit_bytes=vmem)


def _nt_dot(a, b):
    return lax.dot_general(a, b, (((1,), (1,)), ((), ())), preferred_element_type=F32)


def _dot(a, b):
    return jnp.dot(a, b, preferred_element_type=F32)


def _layer_norm(z, g, b):
    mu = jnp.mean(z, axis=-1, keepdims=True)
    zc = z - mu
    var = jnp.mean(zc * zc, axis=-1, keepdims=True)
    return zc * lax.rsqrt(var + LN_EPS) * g + b


def _silu(x):
    return x / (1.0 + jnp.exp(-x))


def _inproj_kernel(x_ref, w_ref, o_ref, *, tn):
    xb = x_ref[...].astype(BF16)
    for j in range(0, w_ref.shape[1], tn):
        o_ref[:, j:j + tn] = _dot(xb, w_ref[:, j:j + tn]).astype(o_ref.dtype)


def in_proj(x2d, w, tm=1024, tn=256):
    n, k = x2d.shape
    m = w.shape[1]
    tm = min(tm, n)
    return pl.pallas_call(
        functools.partial(_inproj_kernel, tn=tn),
        grid=(n // tm,),
        in_specs=[pl.BlockSpec((tm, k), lambda i: (i, 0)),
                  pl.BlockSpec((k, m), lambda i: (0, 0))],
        out_specs=pl.BlockSpec((tm, m), lambda i: (i, 0)),
        out_shape=jax.ShapeDtypeStruct((n, m), BF16),
        compiler_params=_cparams(("parallel",)),
        name="in_proj",
    )(x2d, w)


def _t5_bucket(dist):
    n = jnp.maximum(dist, 0)
    max_exact = REL_BUCKETS // 2
    nf = jnp.maximum(n, 1).astype(F32)
    large = max_exact + (jnp.log(nf / max_exact) / math.log(REL_MAX_DIST / max_exact)
                         * (REL_BUCKETS - max_exact)).astype(jnp.int32)
    large = jnp.minimum(large, REL_BUCKETS - 1)
    return jnp.where(n < max_exact, n, large)


LOG2E = math.log2(math.e)
Q_SCALE = DA_HEAD_DIM ** -0.5 * LOG2E
ATTN_BLOCK = 512
ATTN_CHUNK = 512
ONES_ROWS = 16


def attn_bias_tiles(rel_bias, t):
    table = rel_bias.astype(F32).reshape(REL_BUCKETS, DA_HEADS * 2)
    vec = table[_t5_bucket(jnp.arange(2 * t))]
    far = table[_t5_bucket(jnp.array(2 * t))]
    vec = ((vec - far[None, :]) * LOG2E).T
    masked = jnp.full((DA_HEADS * 2, t), NEG_BIG, F32)
    u_diag = jnp.concatenate([vec[:, :t], masked], axis=1)
    u_prev = jnp.concatenate([vec[:, t:], vec[:, :t]], axis=1)

    def toeplitz(u):
        skew = jnp.tile(u, (1, t))[:, :t * (2 * t - 1)].reshape(-1, t, 2 * t - 1)
        return skew[:, :, :t]

    def per_head(a):
        return a.reshape(DA_HEADS, 2, t, t).transpose(0, 2, 1, 3).reshape(DA_HEADS, t, 2 * t)

    return jnp.stack([per_head(toeplitz(u_diag)), per_head(toeplitz(u_prev))], axis=1)


def _attn_kernel(q_ref, k_ref, v_ref, bias_ref, lam_ref, g_ref, o_ref, vt, m_s, acc, s_a, s_b, *, t, lam_init):
    nk = vt.shape[0]
    ones = jnp.ones((ONES_ROWS, t), BF16)
    for ki in range(nk):
        v_t = v_ref[0, ki * t:(ki + 1) * t, :].astype(F32).T.astype(BF16)
        vt[ki] = jnp.concatenate([v_t, ones], axis=0)

    def query_tile(qi, carry):
        _attn_query_tile(qi, q_ref, k_ref, bias_ref, lam_ref, g_ref, o_ref, vt, m_s, acc, s_a, s_b,
                         t=t, lam_init=lam_init)
        return carry

    lax.fori_loop(0, nk // 2, query_tile, 0)


def _attn_query_tile(qi, q_ref, k_ref, bias_ref, lam_ref, g_ref, o_ref, vt, m_s, acc, s_a, s_b, *, t, lam_init):
    tq = 2 * t
    q_rows = pl.ds(pl.multiple_of(qi * tq, tq), tq)
    q = q_ref[0, q_rows, :]
    lane = lax.broadcasted_iota(jnp.int32, q.shape, 1)
    zero = jnp.zeros_like(q)
    qcat = jnp.concatenate([jnp.where(lane < DA_HEAD_DIM, q, zero),
                            jnp.where(lane >= DA_HEAD_DIM, q, zero)], axis=0)
    m_s[...] = jnp.full(m_s.shape, NEG_BIG, F32)
    acc[...] = jnp.zeros(acc.shape, F32)

    chunk = min(ATTN_CHUNK, t)
    DIAG, PREV, FAR, SKIP = 0, 1, None, "skip"

    def step(nxt, cur):
        kb = None
        if nxt is not None:
            kb = k_ref[0, pl.ds(pl.multiple_of(nxt[0] * t, t), t), :]
        for c in range(0, 2 * tq, chunk):
            cols = slice(c, c + chunk)
            mp, qoff = divmod(c, tq)
            half, ioff = divmod(qoff, t)
            ci = c // chunk
            if nxt is not None and not (len(nxt) > 2 and nxt[2][half] == SKIP):
                nxt[1][ci] = _nt_dot(kb, qcat[cols, :])
            if cur is not None:
                ki, s_ref, kinds = cur
                kind = kinds[half]
                if kind == SKIP:
                    continue
                s = s_ref[ci]
                if kind is not FAR:
                    s = s + bias_ref[0, kind, :, mp * t + ioff:mp * t + ioff + chunk]
                m_prev = m_s[ci]
                m_new = jnp.maximum(m_prev, jnp.max(s, axis=0, keepdims=True))
                alpha = jnp.exp2(m_prev - m_new)
                p = jnp.exp2(s - m_new).astype(BF16)
                acc[ci] = alpha * acc[ci] + _dot(vt[ki], p)
                m_s[ci] = m_new

    upper = (SKIP, DIAG)
    lower = (DIAG, PREV)
    before = (PREV, FAR)
    plain = (FAR, FAR)
    step((2 * qi + 1, s_a, upper), None)
    step((2 * qi, s_b), (2 * qi + 1, s_a, upper))

    @pl.when(qi == 0)
    def _():
        step(None, (0, s_b, lower))

    @pl.when(qi >= 1)
    def _():
        n_far = 2 * qi - 1
        step((2 * qi - 1, s_a), (2 * qi, s_b, lower))
        step((0, s_b), (2 * qi - 1, s_a, before))

        def pair(jj, carry):
            j = 2 * jj
            step((j + 1, s_a), (j, s_b, plain))
            step((j + 2, s_b), (j + 1, s_a, plain))
            return carry

        lax.fori_loop(0, n_far // 2, pair, 0)
        step(None, (n_far - 1, s_b, plain))

    lm = lam_ref[...]
    lam = (jnp.exp(jnp.sum(lm[0:1] * lm[1:2], keepdims=True))
           - jnp.exp(jnp.sum(lm[2:3] * lm[3:4], keepdims=True)) + lam_init)
    a = jnp.concatenate([acc[ci] for ci in range(acc.shape[0])], axis=1)
    o = a[:DA_V_DIM, :] / a[DA_V_DIM:DA_V_DIM + 1, :]
    out = o[:, :tq] - lam * o[:, tq:]
    out = out * lax.rsqrt(jnp.mean(out * out, axis=0, keepdims=True) + NORM_EPS) * g_ref[...]
    o_ref[0, q_rows, :] = (out * (1.0 - lam_init)).T.astype(o_ref.dtype)


def diff_attention(proj3, bias_tiles, lam_params, subln_g, lam_init):
    b, s, _ = proj3.shape
    t = bias_tiles.shape[2]
    chunk = min(ATTN_CHUNK, t)
    nch = 4 * t // chunk
    seq = lambda col0: pl.BlockSpec((1, s, DA_V_DIM), lambda bi, h: (bi, 0, col0 + h))
    return pl.pallas_call(
        functools.partial(_attn_kernel, t=t, lam_init=lam_init),
        grid=(b, DA_HEADS),
        in_specs=[
            seq(0), seq(DA_HEADS), seq(2 * DA_HEADS),
            pl.BlockSpec((1, 2, t, 2 * t), lambda bi, h: (h, 0, 0, 0)),
            pl.BlockSpec((4, DA_HEAD_DIM), lambda bi, h: (0, 0)),
            pl.BlockSpec((DA_V_DIM, 1), lambda bi, h: (0, 0)),
        ],
        out_specs=pl.BlockSpec((1, s, DA_V_DIM), lambda bi, h: (bi, 0, h)),
        out_shape=jax.ShapeDtypeStruct((b, s, DA_WIDTH), BF16),
        scratch_shapes=[pltpu.VMEM((s // t, DA_V_DIM + ONES_ROWS, t), BF16),
                        pltpu.VMEM((nch, 1, chunk), F32),
                        pltpu.VMEM((nch, DA_V_DIM + ONES_ROWS, chunk), F32),
                        pltpu.VMEM((nch, t, chunk), F32), pltpu.VMEM((nch, t, chunk), F32)],
        compiler_params=_cparams(("parallel", "parallel")),
        name="diff_attention",
    )(proj3, proj3, proj3, bias_tiles, lam_params, subln_g.reshape(DA_V_DIM, 1))


def _pool_kernel(p_ref, w_ref, scale_ref, o_ref, halo, *, t):
    si = pl.program_id(1)

    @pl.when(si == 0)
    def _():
        halo[...] = jnp.zeros(halo.shape, F32)

    p = p_ref[0].astype(F32)
    ext = jnp.concatenate([halo[...], p], axis=0)
    halo[...] = p[t - POOL_HALO:, :]
    sums = {1: ext}
    w = 1
    while w < POOL_WINDOWS[-1]:
        sums[2 * w] = sums[w] + pltpu.roll(sums[w], w, 0)
        w *= 2
    lane = lax.broadcasted_iota(jnp.int32, (t, POOL_WIDTH), 1)
    pos = (si * t + lax.broadcasted_iota(jnp.int32, (t, POOL_WIDTH), 0) + 1).astype(F32)
    wsum = sums[POOL_WINDOWS[-1]][POOL_HALO:, :]
    cnt = jnp.minimum(pos, float(POOL_WINDOWS[-1]))
    for gi in range(len(POOL_WINDOWS) - 2, -1, -1):
        in_group = lane < (gi + 1) * POOL_GROUP_DIM
        wsum = jnp.where(in_group, sums[POOL_WINDOWS[gi]][POOL_HALO:, :], wsum)
        cnt = jnp.where(in_group, jnp.minimum(pos, float(POOL_WINDOWS[gi])), cnt)
    pooled = wsum / cnt - p
    mixed = _dot(pooled.astype(BF16), w_ref[...])
    o_ref[0] = (mixed * scale_ref[...]).astype(o_ref.dtype)


def pool_mixer(proj3, pool_w, pool_scale, t=4096):
    b, s, _ = proj3.shape
    t = min(t, s)
    g = len(POOL_WINDOWS)
    wbd = (jnp.eye(g, dtype=F32)[:, None, :, None] * pool_w.astype(F32)[:, :, None, :]).reshape(
        POOL_WIDTH, POOL_WIDTH).astype(BF16)
    return pl.pallas_call(
        functools.partial(_pool_kernel, t=t),
        grid=(b, s // t),
        in_specs=[pl.BlockSpec((1, t, POOL_WIDTH), lambda bi, si: (bi, si, POOL_COL_BLOCK)),
                  pl.BlockSpec((POOL_WIDTH, POOL_WIDTH), lambda bi, si: (0, 0)),
                  pl.BlockSpec((1, POOL_WIDTH), lambda bi, si: (0, 0))],
        out_specs=pl.BlockSpec((1, t, POOL_WIDTH), lambda bi, si: (bi, si, 0)),
        out_shape=jax.ShapeDtypeStruct((b, s, POOL_WIDTH), BF16),
        scratch_shapes=[pltpu.VMEM((POOL_HALO, POOL_WIDTH), F32)],
        compiler_params=_cparams(("parallel", "arbitrary")),
        name="pool_mixer",
    )(proj3, wbd, pool_scale.reshape(1, POOL_WIDTH))


def _retention_tables(s, t):
    d, hn, c = RET_HEAD_DIM, RET_HEADS, RET_CHUNK
    half = d // 2
    inv = 10000.0 ** (-jnp.linspace(0.0, 1.0, half, dtype=F32))
    ang = jnp.arange(s)[:, None].astype(F32) * inv[None, :]
    cos, sin = jnp.cos(ang), jnp.sin(ang)
    cos_t = jnp.tile(jnp.concatenate([cos, cos], axis=-1), (1, hn))
    sin_t = jnp.tile(jnp.concatenate([-sin, sin], axis=-1), (1, hn))
    log_gamma = jnp.log(1.0 - 2.0 ** (-5.0 - jnp.arange(hn, dtype=F32)))
    idx = jnp.arange(c, dtype=F32)
    rel = idx[:, None] - idx[None, :]
    intra = jnp.where(rel >= 0, jnp.exp(log_gamma[:, None, None] * jnp.maximum(rel, 0.0)), 0.0)
    q_decay = jnp.exp(log_gamma[:, None] * (idx + 1.0))
    k_decay = jnp.exp(log_gamma[:, None] * (c - 1.0 - idx))
    chunk_decay = jnp.exp(log_gamma * c)
    lanes = lambda a: jnp.tile(jnp.repeat(a.T, d, axis=1), (t // c, 1))
    l = jnp.arange(hn * d)
    partner = jnp.where(l % d < half, l + half, l - half)
    swap = (l[:, None] == partner[None, :]).astype(BF16)
    same = (jnp.arange(2 * d)[:, None] // d) == (jnp.arange(2 * d)[None, :] // d)
    decay_bd = jnp.where(same[None], jnp.repeat(chunk_decay, d).reshape(hn // 2, 2 * d, 1), 0.0)
    avg = jnp.where(same, 1.0 / d, 0.0).astype(BF16)
    return cos_t, sin_t, intra, lanes(q_decay), lanes(k_decay), swap, decay_bd.astype(F32), avg


def _ret_kernel(q_ref, k_ref, v_ref, g_ref, cos_ref, sin_ref, intra_ref, qd_ref, kd_ref, swap_ref, decay_ref,
                avg_ref, gn_ref, o_ref, state, *, t):
    si = pl.program_id(1)
    d, c = RET_HEAD_DIM, RET_CHUNK
    pw = 2 * d

    @pl.when(si == 0)
    def _():
        state[...] = jnp.zeros(state.shape, F32)

    def rotate(x_ref):
        x = x_ref[0]
        return x.astype(F32) * cos_ref[...] + _dot(x, swap_ref[...]) * sin_ref[...]

    def lane_mean(a):
        hi = a.astype(BF16)
        lo = (a - hi.astype(F32)).astype(BF16)
        return _dot(hi, avg_ref[...]) + _dot(lo, avg_ref[...])

    q = rotate(q_ref)
    k = rotate(k_ref) * (d ** -0.5)
    qs = (q * qd_ref[...]).astype(BF16)
    ks = (k * kd_ref[...]).astype(BF16)
    qb = q.astype(BF16)
    kb = k.astype(BF16)
    v = v_ref[0]
    first = lax.broadcasted_iota(jnp.int32, (c, pw), 1) < d
    same = ((lax.broadcasted_iota(jnp.int32, (pw, pw), 0) < d)
            == (lax.broadcasted_iota(jnp.int32, (pw, pw), 1) < d))
    zero = jnp.zeros((c, pw), BF16)

    chunks = [slice(ci * c, (ci + 1) * c) for ci in range(t // c)]
    pairs = [slice(p * pw, (p + 1) * pw) for p in range(RET_HEADS // 2)]
    scores = {}
    for p, cols in enumerate(pairs):
        for ci, rows in enumerate(chunks):
            qp, kp = qb[rows, cols], kb[rows, cols]
            scores[p, ci] = ((_nt_dot(jnp.where(first, qp, zero), kp) * intra_ref[2 * p]).astype(BF16),
                             (_nt_dot(jnp.where(first, zero, qp), kp) * intra_ref[2 * p + 1]).astype(BF16))
    intra_out, increment = {}, {}
    for p, cols in enumerate(pairs):
        for ci, rows in enumerate(chunks):
            vp = v[rows, cols]
            intra_out[p, ci] = jnp.where(first, _dot(scores[p, ci][0], vp), _dot(scores[p, ci][1], vp))
            increment[p, ci] = jnp.where(same, _dot(ks[rows, cols].T, vp), 0.0)
    ys = {}
    for p, cols in enumerate(pairs):
        st = state[p]
        for ci, rows in enumerate(chunks):
            ys[p, ci] = intra_out[p, ci] + _dot(qs[rows, cols], st.astype(BF16))
            st = st * decay_ref[p] + increment[p, ci]
        state[p] = st
    y = jnp.concatenate([jnp.concatenate([ys[p, ci] for ci in range(len(chunks))], axis=0)
                         for p in range(len(pairs))], axis=1)
    mean = jnp.concatenate([lane_mean(y[:, cols]) for cols in pairs], axis=1)
    yc = y - mean
    var = jnp.concatenate([lane_mean((yc * yc)[:, cols]) for cols in pairs], axis=1)
    yn = yc * lax.rsqrt(var + NORM_EPS)
    gate = _silu(g_ref[0].astype(F32))
    o_ref[0] = (gate * (yn * gn_ref[...])).astype(o_ref.dtype)


def retention(proj3, gn_g, t=2048):
    b, s, _ = proj3.shape
    t = min(t, s)
    cos_t, sin_t, intra, qd, kd, swap, decay_bd, avg = _retention_tables(s, t)
    pw = 2 * RET_HEAD_DIM
    col = lambda j: pl.BlockSpec((1, t, RET_WIDTH), lambda bi, si: (bi, si, RET_COL_BLOCK + j))
    const2 = lambda shape: pl.BlockSpec(shape, lambda bi, si: (0, 0))
    const3 = lambda shape: pl.BlockSpec(shape, lambda bi, si: (0, 0, 0))
    return pl.pallas_call(
        functools.partial(_ret_kernel, t=t),
        grid=(b, s // t),
        in_specs=[col(0), col(1), col(2), col(3),
                  pl.BlockSpec((t, RET_WIDTH), lambda bi, si: (si, 0)),
                  pl.BlockSpec((t, RET_WIDTH), lambda bi, si: (si, 0)),
                  const3((RET_HEADS, RET_CHUNK, RET_CHUNK)),
                  const2((t, RET_WIDTH)), const2((t, RET_WIDTH)), const2((RET_WIDTH, RET_WIDTH)),
                  const3((RET_HEADS // 2, pw, pw)), const2((pw, pw)), const2((1, RET_WIDTH))],
        out_specs=pl.BlockSpec((1, t, RET_WIDTH), lambda bi, si: (bi, si, 0)),
        out_shape=jax.ShapeDtypeStruct((b, s, RET_WIDTH), BF16),
        scratch_shapes=[pltpu.VMEM((RET_HEADS // 2, pw, pw), F32)],
        compiler_params=_cparams(("parallel", "arbitrary")),
        name="retention",
    )(proj3, proj3, proj3, proj3, cos_t, sin_t, intra, qd, kd, swap, decay_bd, avg, gn_g.reshape(1, RET_WIDTH))


def _mix_ln(da_ref, pool_ref, ret_ref, x_ref, w_ref, g_ref, b_ref):
    e0, e1 = DA_WIDTH, DA_WIDTH + POOL_WIDTH
    mix = (_dot(da_ref[...], w_ref[0:e0, :]) + _dot(pool_ref[...], w_ref[e0:e1, :])
           + _dot(ret_ref[...], w_ref[e1:, :]))
    return _layer_norm(ALPHA * x_ref[...] + mix, g_ref[...], b_ref[...])


def _outproj_kernel(da_ref, pool_ref, ret_ref, x_ref, w_ref, g_ref, b_ref, o_ref):
    o_ref[...] = _mix_ln(da_ref, pool_ref, ret_ref, x_ref, w_ref, g_ref, b_ref)


def out_proj_ln(y_da, y_pool, y_ret, x2d, w, g, b, tm=1024):
    n, dm = x2d.shape
    tm = min(tm, n)
    row = lambda width: pl.BlockSpec((tm, width), lambda i: (i, 0))
    const = lambda shape: pl.BlockSpec(shape, lambda i: (0, 0))
    return pl.pallas_call(
        _outproj_kernel,
        grid=(n // tm,),
        in_specs=[row(DA_WIDTH), row(POOL_WIDTH), row(RET_WIDTH), row(dm),
                  const(w.shape), const((1, dm)), const((1, dm))],
        out_specs=row(dm),
        out_shape=jax.ShapeDtypeStruct((n, dm), F32),
        compiler_params=_cparams(("parallel",)),
        name="out_proj_ln",
    )(y_da, y_pool, y_ret, x2d, w, g.reshape(1, dm), b.reshape(1, dm))


def _mix_ffn_kernel(da_ref, pool_ref, ret_ref, x_ref, wo_ref, g1_ref, b1_ref, wg_ref, wu_ref, wd_ref,
                    g2_ref, b2_ref, o_ref, *, tf):
    x = _mix_ln(da_ref, pool_ref, ret_ref, x_ref, wo_ref, g1_ref, b1_ref)
    xb = x.astype(BF16)
    acc = jnp.zeros(x.shape, F32)
    for f0 in range(0, wg_ref.shape[1], tf):
        hidden = _silu(_dot(xb, wg_ref[:, f0:f0 + tf])) * _dot(xb, wu_ref[:, f0:f0 + tf])
        acc = acc + _dot(hidden.astype(BF16), wd_ref[f0:f0 + tf, :])
    o_ref[...] = _layer_norm(ALPHA * x + acc, g2_ref[...], b2_ref[...])


def out_proj_ffn_ln(y_da, y_pool, y_ret, x2d, wo, g1, b1, wg, wu, wd, g2, b2, tm=512, tf=256):
    n, dm = x2d.shape
    tm = min(tm, n)
    row = lambda width: pl.BlockSpec((tm, width), lambda i: (i, 0))
    resident = lambda shape: pl.BlockSpec(shape, lambda i: (0, 0), pipeline_mode=pl.Buffered(1))
    vec = lambda a: a.reshape(1, dm)
    return pl.pallas_call(
        functools.partial(_mix_ffn_kernel, tf=tf),
        grid=(n // tm,),
        in_specs=[row(DA_WIDTH), row(POOL_WIDTH), row(RET_WIDTH), row(dm),
                  resident(wo.shape), resident((1, dm)), resident((1, dm)),
                  resident(wg.shape), resident(wu.shape), resident(wd.shape),
                  resident((1, dm)), resident((1, dm))],
        out_specs=row(dm),
        out_shape=jax.ShapeDtypeStruct((n, dm), F32),
        compiler_params=_cparams(("parallel",)),
        name="out_proj_ffn_ln",
    )(y_da, y_pool, y_ret, x2d, wo, vec(g1), vec(b1), wg, wu, wd, vec(g2), vec(b2))


ROUTE_ROWS = 8
LANES = 128


def _router_kernel(x_ref, w_ref, tri_ref, route_ref, route_t_ref, cnt_ref, carry, *, t):
    i = pl.program_id(0)

    @pl.when(i == 0)
    def _():
        carry[...] = jnp.zeros(carry.shape, F32)

    x = x_ref[...]
    xh = x.astype(BF16)
    xl = (x - xh.astype(F32)).astype(BF16)
    w = w_ref[...]
    wh = w.astype(BF16)
    wl = (w - wh.astype(F32)).astype(BF16)
    logits = _nt_dot(wh, xh) + _nt_dot(wl, xh) + _nt_dot(wh, xl)

    row = lax.broadcasted_iota(jnp.int32, logits.shape, 0)
    v0 = jnp.max(logits, axis=0, keepdims=True)
    i0 = jnp.min(jnp.where(logits == v0, row, N_EXPERTS), axis=0, keepdims=True)
    rest = jnp.where(row == i0, -jnp.inf, logits)
    v1 = jnp.max(rest, axis=0, keepdims=True)
    i1 = jnp.min(jnp.where(rest == v1, row, N_EXPERTS), axis=0, keepdims=True)
    ex = jnp.exp(v1 - v0)
    gate0 = 1.0 / (1.0 + ex)
    gate1 = ex / (1.0 + ex)

    oh0 = row == i0
    oh1 = row == i1
    member = jnp.where(oh0 | oh1, 1.0, 0.0)
    before = _dot(member.astype(BF16), tri_ref[...]) + carry[:, 0:1]
    rank0 = jnp.sum(jnp.where(oh0, before, 0.0), axis=0, keepdims=True)
    rank1 = jnp.sum(jnp.where(oh1, before, 0.0), axis=0, keepdims=True)
    carry[...] = carry[...] + jnp.sum(member, axis=1, keepdims=True)
    cnt_ref[...] = carry[...]

    route = jnp.concatenate([i0.astype(F32), i1.astype(F32), rank0, rank1, gate0, gate1,
                             jnp.zeros((2, t), F32)], axis=0)
    route_ref[...] = route
    padded = jnp.concatenate([route, jnp.zeros((LANES - ROUTE_ROWS, t), F32)], axis=0)
    route_t_ref[...] = padded.T


def route_tokens(x2d, router_w, t=2048):
    n, dm = x2d.shape
    t = min(t, n)
    tri = (jnp.arange(t)[:, None] < jnp.arange(t)[None, :]).astype(BF16)
    return pl.pallas_call(
        functools.partial(_router_kernel, t=t),
        grid=(n // t,),
        in_specs=[pl.BlockSpec((t, dm), lambda i: (i, 0)),
                  pl.BlockSpec((N_EXPERTS, dm), lambda i: (0, 0)),
                  pl.BlockSpec((t, t), lambda i: (0, 0))],
        out_specs=[pl.BlockSpec((ROUTE_ROWS, t), lambda i: (0, i)),
                   pl.BlockSpec((t, LANES), lambda i: (i, 0)),
                   pl.BlockSpec((N_EXPERTS, LANES), lambda i: (0, 0))],
        out_shape=[jax.ShapeDtypeStruct((ROUTE_ROWS, n), F32),
                   jax.ShapeDtypeStruct((n, LANES), F32),
                   jax.ShapeDtypeStruct((N_EXPERTS, LANES), F32)],
        scratch_shapes=[pltpu.VMEM((N_EXPERTS, LANES), F32)],
        compiler_params=_cparams(("arbitrary",)),
        name="route_tokens",
    )(x2d, router_w.T, tri)


def _dest_kernel(start_ref, route_ref, dest_ref):
    r = route_ref[...]
    for k in range(2):
        e = r[k:k + 1, :].astype(jnp.int32)
        base = jnp.zeros(e.shape, jnp.int32)
        for ei in range(N_EXPERTS):
            base = jnp.where(e == ei, start_ref[ei], base)
        dest_ref[k:k + 1, :] = base + r[2 + k:3 + k, :].astype(jnp.int32)


def slot_of_assignment(pad_start, route, t=4096):
    n = route.shape[1]
    t = min(t, n)
    grid_spec = pltpu.PrefetchScalarGridSpec(
        num_scalar_prefetch=1,
        grid=(n // t,),
        in_specs=[pl.BlockSpec((ROUTE_ROWS, t), lambda i, ps: (0, i))],
        out_specs=pl.BlockSpec((2, t), lambda i, ps: (0, i)),
    )
    return pl.pallas_call(
        _dest_kernel,
        grid_spec=grid_spec,
        out_shape=jax.ShapeDtypeStruct((2, n), jnp.int32),
        compiler_params=_cparams(("parallel",)),
        name="slot_of_assignment",
    )(pad_start, route)


def _row_copy(src, src_row, dst, dst_row, sem):
    return pltpu.make_async_copy(src.at[pl.ds(src_row, 1)], dst.at[pl.ds(dst_row, 1)], sem)


SUBLANES = 8
ISSUE_UNROLL = SUBLANES


def _dispatch_kernel(pad_lo_ref, pad_n_ref, d0_ref, d1_ref, x_ref, slots_out, zrow, sem, zsem, *, t):
    @pl.when(pl.program_id(0) == 0)
    def _():
        zrow[...] = jnp.zeros(zrow.shape, zrow.dtype)
        for e in range(N_EXPERTS):
            lo = pad_lo_ref[e]

            def zero_start(r, carry):
                _row_copy(zrow, 0, slots_out, lo + r, zsem).start()
                return carry

            def zero_wait(r, carry):
                _row_copy(zrow, 0, slots_out, lo + r, zsem).wait()
                return carry

            lax.fori_loop(0, pad_n_ref[e], zero_start, 0)
            lax.fori_loop(0, pad_n_ref[e], zero_wait, 0)

        tail_lo = pad_lo_ref[N_EXPERTS]
        group = zrow.shape[0]

        def tail_copy(r):
            dst = slots_out.at[pl.ds(pl.multiple_of(tail_lo + r * group, group), group)]
            return pltpu.make_async_copy(zrow, dst, zsem)

        def tail_start(r, carry):
            tail_copy(r).start()
            return carry

        def tail_wait(r, carry):
            tail_copy(r).wait()
            return carry

        lax.fori_loop(0, pad_n_ref[N_EXPERTS], tail_start, 0)
        lax.fori_loop(0, pad_n_ref[N_EXPERTS], tail_wait, 0)

    def issue(j, carry):
        base = pl.multiple_of(j * ISSUE_UNROLL, ISSUE_UNROLL)
        group = x_ref.at[pl.ds(base, ISSUE_UNROLL)]
        for u in range(ISSUE_UNROLL):
            for k, d_ref in enumerate((d0_ref, d1_ref)):
                _row_copy(group, u, slots_out, d_ref[base + u], sem).start()
        return carry

    lax.fori_loop(0, t // ISSUE_UNROLL, issue, 0)
    for k in range(2):
        pltpu.make_async_copy(x_ref, slots_out.at[pl.ds(0, t)], sem).wait()


def dispatch_rows(pad_lo, pad_n, dest, x2d, n_slots, t=4096):
    n, dm = x2d.shape
    t = min(t, n)
    grid_spec = pltpu.PrefetchScalarGridSpec(
        num_scalar_prefetch=2,
        grid=(n // t,),
        in_specs=[pl.BlockSpec((t,), lambda i, lo, cnt: (i,), memory_space=pltpu.SMEM),
                  pl.BlockSpec((t,), lambda i, lo, cnt: (i,), memory_space=pltpu.SMEM),
                  pl.BlockSpec((t, dm), lambda i, lo, cnt: (i, 0))],
        out_specs=pl.BlockSpec(memory_space=pl.ANY),
        scratch_shapes=[pltpu.VMEM((SUBLANES, dm), x2d.dtype), pltpu.SemaphoreType.DMA,
                        pltpu.SemaphoreType.DMA],
    )
    return pl.pallas_call(
        functools.partial(_dispatch_kernel, t=t),
        grid_spec=grid_spec,
        out_shape=jax.ShapeDtypeStruct((n_slots, dm), x2d.dtype),
        compiler_params=_cparams(("arbitrary",)),
        name="dispatch_rows",
    )(pad_lo, pad_n, dest[0], dest[1], x2d)


def _expert_kernel(be_ref, nv_ref, x_ref, wg_ref, wu_ref, wd_ref, o_ref, xb):
    blk = pl.program_id(0)
    f = pl.program_id(1)
    valid = blk < nv_ref[0]

    @pl.when(f == 0)
    def _():
        o_ref[...] = jnp.zeros(o_ref.shape, o_ref.dtype)

    @pl.when(valid & (f == 0))
    def _():
        xb[...] = x_ref[...].astype(BF16)

    @pl.when(valid)
    def _():
        hidden = (_silu(_dot(xb[...], wg_ref[0].astype(BF16)))
                  * _dot(xb[...], wu_ref[0].astype(BF16)))
        o_ref[...] += _dot(hidden.astype(BF16), wd_ref[0].astype(BF16))


def expert_swiglu(block_e, n_valid, slots, wg, wu, wd, bm, tf=512):
    n_slots, dm = slots.shape
    edim = wg.shape[2]
    grid_spec = pltpu.PrefetchScalarGridSpec(
        num_scalar_prefetch=2,
        grid=(n_slots // bm, edim // tf),
        in_specs=[pl.BlockSpec((bm, dm), lambda b, f, be, nv: (jnp.minimum(b, nv[0] - 1), 0)),
                  pl.BlockSpec((1, dm, tf), lambda b, f, be, nv: (be[b], 0, f)),
                  pl.BlockSpec((1, dm, tf), lambda b, f, be, nv: (be[b], 0, f)),
                  pl.BlockSpec((1, tf, dm), lambda b, f, be, nv: (be[b], f, 0))],
        out_specs=pl.BlockSpec((bm, dm), lambda b, f, be, nv: (b, 0)),
        scratch_shapes=[pltpu.VMEM((bm, dm), BF16)],
    )
    return pl.pallas_call(
        _expert_kernel,
        grid_spec=grid_spec,
        out_shape=jax.ShapeDtypeStruct((n_slots, dm), F32),
        compiler_params=_cparams(("parallel", "arbitrary")),
        name="expert_swiglu",
    )(block_e, n_valid, slots, wg, wu, wd)


def _combine_kernel(d0_ref, d1_ref, d0_next_ref, d1_next_ref, rt_ref, x_ref, y_hbm, g_ref, b_ref, o_ref,
                    rows, sem, *, t):
    i = pl.program_id(0)
    cur = i % 2

    def gather(d_refs, buf):
        def issue(j, carry):
            for u in range(ISSUE_UNROLL):
                for k in range(2):
                    _row_copy(y_hbm, d_refs[k][j * ISSUE_UNROLL + u], rows.at[buf, k, j], u, sem.at[buf]).start()
            return carry

        lax.fori_loop(0, t // ISSUE_UNROLL, issue, 0)

    @pl.when(i == 0)
    def _():
        gather((d0_ref, d1_ref), 0)

    @pl.when(i + 1 < pl.num_programs(0))
    def _():
        gather((d0_next_ref, d1_next_ref), 1 - cur)

    for k in range(2):
        pltpu.make_async_copy(y_hbm.at[pl.ds(0, t)], o_ref, sem.at[cur]).wait()

    rt = rt_ref[...]
    dm = o_ref.shape[1]
    mixed = rt[:, 4:5] * rows[cur, 0].reshape(t, dm) + rt[:, 5:6] * rows[cur, 1].reshape(t, dm)
    o_ref[...] = _layer_norm(ALPHA * x_ref[...] + mixed, g_ref[...], b_ref[...])


def combine_ln(dest, route_t, x2d, y, g, b, t=256):
    n, dm = x2d.shape
    t = min(t, n)
    last = n // t - 1
    return pl.pallas_call(
        functools.partial(_combine_kernel, t=t),
        grid=(n // t,),
        in_specs=[pl.BlockSpec((t,), lambda i: (i,), memory_space=pltpu.SMEM),
                  pl.BlockSpec((t,), lambda i: (i,), memory_space=pltpu.SMEM),
                  pl.BlockSpec((t,), lambda i: (jnp.minimum(i + 1, last),), memory_space=pltpu.SMEM),
                  pl.BlockSpec((t,), lambda i: (jnp.minimum(i + 1, last),), memory_space=pltpu.SMEM),
                  pl.BlockSpec((t, LANES), lambda i: (i, 0)),
                  pl.BlockSpec((t, dm), lambda i: (i, 0)),
                  pl.BlockSpec(memory_space=pl.ANY),
                  pl.BlockSpec((1, dm), lambda i: (0, 0)),
                  pl.BlockSpec((1, dm), lambda i: (0, 0))],
        out_specs=pl.BlockSpec((t, dm), lambda i: (i, 0)),
        out_shape=jax.ShapeDtypeStruct((n, dm), F32),
        scratch_shapes=[pltpu.VMEM((2, 2, t // ISSUE_UNROLL, ISSUE_UNROLL, dm), F32),
                        pltpu.SemaphoreType.DMA((2,))],
        compiler_params=_cparams(("arbitrary",)),
        name="combine_ln",
    )(dest[0], dest[1], dest[0], dest[1], route_t, x2d, y, g.reshape(1, dm), b.reshape(1, dm))


def moe_ln(x2d, router_w, wg, wu, wd, g, b, bm=1024):
    n, _ = x2d.shape
    bm = min(bm, n)
    route, route_t, cnt = route_tokens(x2d, router_w)
    counts = cnt[:, 0].astype(jnp.int32)
    padded = (counts + bm - 1) // bm * bm
    pad_end = jnp.cumsum(padded)
    pad_start = (pad_end - padded).astype(jnp.int32)
    n_blocks = 2 * n // bm + N_EXPERTS
    n_slots = n_blocks * bm
    block_start = jnp.arange(n_blocks, dtype=jnp.int32) * bm
    block_e = jnp.minimum(jnp.sum(pad_end[None, :] <= block_start[:, None], axis=1),
                          N_EXPERTS - 1).astype(jnp.int32)
    n_valid = (pad_end[-1:] // bm).astype(jnp.int32)
    dest = slot_of_assignment(pad_start, route)
    pad_lo = jnp.concatenate([pad_start + counts, pad_end[-1:]]).astype(jnp.int32)
    pad_n = jnp.concatenate([padded - counts, (n_slots - pad_end[-1:]) // SUBLANES]).astype(jnp.int32)
    slots = dispatch_rows(pad_lo, pad_n, dest, x2d, n_slots)
    y = expert_swiglu(block_e, n_valid, slots, wg, wu, wd, bm)
    return combine_ln(dest, route_t, x2d, y, g, b)


def kernel(x, rel_bias, w_in, diff_lambda, diff_subln_g, pool_w, pool_scale, ret_gn_g, w_out,
           ln1_g, ln1_b, ln2_g, ln2_b, ffn_w_gate, ffn_w_up, ffn_w_down,
           router_w, moe_w_gate, moe_w_up, moe_w_down):
    bsz, seq, dm = x.shape
    x2d = x.reshape(bsz * seq, dm)
    bias_tiles = attn_bias_tiles(rel_bias, min(ATTN_BLOCK, seq))
    col_scale = jnp.where(jnp.arange(w_in.shape[2]) < DA_WIDTH, Q_SCALE, 1.0).astype(F32)
    for l in range(DEPTH):
        lam_init = 0.8 - 0.6 * math.exp(-0.3 * l)
        proj = in_proj(x2d, (w_in[l] * col_scale).astype(BF16)).reshape(bsz, seq, -1)
        y_da = diff_attention(proj, bias_tiles, diff_lambda[l], diff_subln_g[l], lam_init)
        y_pool = pool_mixer(proj, pool_w[l], pool_scale[l])
        y_ret = retention(proj, ret_gn_g[l])
        flat = lambda a: a.reshape(bsz * seq, -1)
        mixed = (flat(y_da), flat(y_pool), flat(y_ret), x2d, w_out[l].astype(BF16), ln1_g[l], ln1_b[l])
        j = l // 2
        if l % 2 == 0:
            x2d = out_proj_ffn_ln(*mixed, ffn_w_gate[j].astype(BF16), ffn_w_up[j].astype(BF16),
                                  ffn_w_down[j].astype(BF16), ln2_g[l], ln2_b[l])
        else:
            x2d = out_proj_ln(*mixed)
            x2d = moe_ln(x2d, router_w[j], moe_w_gate[j], moe_w_up[j], moe_w_down[j], ln2_g[l], ln2_b[l])
    return x2d.reshape(bsz, seq, dm)
```
